```python
import math
import jax
import jax.numpy as jnp
from jax import lax
import numpy as np

D_MODEL = 1024
BATCH = 32
SEQ = 256
DEPTH = 2
DEC_BATCH = 8
DEC_SEQ = 1024
PAST_LEN = 512

GRID_W = 64
CHUNK = 128
Q_BLOCK = 128
SGU_GROUPS = 4
SGU_WIDTH = 256
MLA_HEADS = 4
MLA_Q_LORA = 256
MLA_KV_LORA = 128
MLA_NOPE = 64
MLA_ROPE = 32
MLA_V = 64
NA_HEADS = 4
NA_HEAD_DIM = 64
NA_WIN_ROWS = 8
NA_WIN_COLS = 16
DIFF_HEADS = 4
DIFF_QK_DIM = 64
DIFF_V_DIM = 2 * DIFF_QK_DIM
N_BRANCHES = 4
PEER_HEADS = 8
PEER_N_KEYS = 128
PEER_N_EXPERTS = PEER_N_KEYS * PEER_N_KEYS
PEER_KEY_DIM = 256
PEER_TOPK = 16
PEER_TOKEN_BLOCK = 128

ROPE_THETA = 10000.0
LN_EPS = 1e-6
NEG_BIG = -1e30
DEEPNORM_ALPHA = (2 * DEPTH) ** 0.25
DEEPNORM_BETA = (8 * DEPTH) ** -0.25

SPLITS = (SGU_WIDTH, SGU_WIDTH,
          MLA_Q_LORA, MLA_KV_LORA, MLA_ROPE,
          NA_HEADS * NA_HEAD_DIM, NA_HEADS * NA_HEAD_DIM, NA_HEADS * NA_HEAD_DIM,
          DIFF_HEADS * 2 * DIFF_QK_DIM, DIFF_HEADS * 2 * DIFF_QK_DIM, DIFF_HEADS * DIFF_V_DIM)
IN_WIDTH = sum(SPLITS)
SPLIT_POINTS = tuple(sum(SPLITS[:i + 1]) for i in range(len(SPLITS) - 1))

kernel_name = 'hybrid_diffusion_prefix_trunk_step'


def layer_norm(x):
    xf = x.astype(jnp.float32)
    mu = jnp.mean(xf, axis=-1, keepdims=True)
    var = jnp.mean(jnp.square(xf - mu), axis=-1, keepdims=True)
    return ((xf - mu) * lax.rsqrt(var + LN_EPS)).astype(x.dtype)


def layer_norm_affine(x, g, b):
    return layer_norm(x) * g + b


def rms_norm(x, g):
    xf = x.astype(jnp.float32)
    y = xf * lax.rsqrt(jnp.mean(jnp.square(xf), axis=-1, keepdims=True) + LN_EPS)
    return y.astype(x.dtype) * g


def modulate(x, shift, scale):
    return layer_norm(x) * (1 + scale[:, None]) + shift[:, None]


def axial_rope_angles(n_tokens, rot_dim):
    t = jnp.arange(n_tokens)
    row = (t // GRID_W).astype(jnp.float32)
    col = (t % GRID_W).astype(jnp.float32)
    n_freq = rot_dim // 4
    inv_freq = ROPE_THETA ** (-jnp.arange(n_freq, dtype=jnp.float32) / n_freq)
    return jnp.concatenate([row[:, None] * inv_freq, col[:, None] * inv_freq], axis=-1)


def apply_rope(x, angles):
    d2 = x.shape[-1] // 2
    cos = jnp.cos(angles)[:, None, :].astype(x.dtype)
    sin = jnp.sin(angles)[:, None, :].astype(x.dtype)
    x1, x2 = x[..., :d2], x[..., d2:]
    return jnp.concatenate([x1 * cos - x2 * sin, x2 * cos + x1 * sin], axis=-1)


def rope_pair(x, angles):
    B, T, H, M, d = x.shape
    return apply_rope(x.reshape(B, T, H * M, d), angles).reshape(B, T, H, M, d)


def attend_blocks(q, k, v):
    B, T, H, dq = q.shape
    scale = dq ** -0.5
    qb = q.reshape(B, T // Q_BLOCK, Q_BLOCK, H, dq).swapaxes(0, 1)

    def one(q_blk):
        s = jnp.einsum('bqhd,bkhd->bhqk', q_blk, k).astype(jnp.float32) * scale
        p = jax.nn.softmax(s, axis=-1).astype(v.dtype)
        return jnp.einsum('bhqk,bkhd->bqhd', p, v)

    return lax.map(one, qb).swapaxes(0, 1).reshape(B, T, H, v.shape[-1])


def spatial_gating(u, v, norm_g, w_s, b_s):
    B, T, _ = v.shape
    vn = layer_norm(v) * norm_g
    vg = vn.reshape(B, T // CHUNK, CHUNK, SGU_GROUPS, SGU_WIDTH // SGU_GROUPS)
    mixed = jnp.einsum('gpq,bnqgc->bnpgc', w_s, vg) + b_s.T[None, None, :, :, None]
    return u * mixed.reshape(B, T, SGU_WIDTH)


def mla_keys(ckv_n, k_rope, w_ukv):
    B, L, _ = ckv_n.shape
    kv = (ckv_n @ w_ukv).reshape(B, L, MLA_HEADS, MLA_NOPE + MLA_V)
    k_nope, v = kv[..., :MLA_NOPE], kv[..., MLA_NOPE:]
    k = jnp.concatenate([k_nope, jnp.broadcast_to(k_rope, (B, L, MLA_HEADS, MLA_ROPE))], axis=-1)
    return k, v


def neighborhood_attend(q, k, v, k_ctx, v_ctx, rpb):
    B, T, H, d = q.shape
    rows = T // GRID_W
    wr = min(NA_WIN_ROWS, rows)
    scale = d ** -0.5
    qg = q.reshape(B, rows, GRID_W, H, d)
    kg = k.reshape(B, rows, GRID_W, H, d)
    vg = v.reshape(B, rows, GRID_W, H, d)
    col = jnp.arange(GRID_W)
    col_start = jnp.clip(col - NA_WIN_COLS // 2, 0, GRID_W - NA_WIN_COLS)
    col_in = (col[None, :] >= col_start[:, None]) & (col[None, :] < col_start[:, None] + NA_WIN_COLS)
    dc = jnp.clip(col[None, :] - col[:, None], -(NA_WIN_COLS - 1), NA_WIN_COLS - 1) + NA_WIN_COLS - 1
    rpb_cols = rpb[:, :, dc]

    def one_row(r):
        start = jnp.clip(r - wr // 2, 0, rows - wr)
        q_r = lax.dynamic_index_in_dim(qg, r, axis=1, keepdims=False)
        k_w = lax.dynamic_slice_in_dim(kg, start, wr, axis=1)
        v_w = lax.dynamic_slice_in_dim(vg, start, wr, axis=1)
        dr = start + jnp.arange(wr) - r + NA_WIN_ROWS - 1
        bias = rpb_cols[:, dr].transpose(0, 2, 1, 3)
        s_win = jnp.einsum('bqhd,bjkhd->bhqjk', q_r, k_w).astype(jnp.float32) * scale + bias[None]
        s_win = jnp.where(col_in[None, None, :, None, :], s_win, NEG_BIG)
        s_ctx = jnp.einsum('bqhd,blhd->bhql', q_r, k_ctx).astype(jnp.float32) * scale
        s = jnp.concatenate([s_win.reshape(B, H, GRID_W, wr * GRID_W), s_ctx], axis=-1)
        p = jax.nn.softmax(s, axis=-1).astype(v.dtype)
        p_win = p[..., :wr * GRID_W].reshape(B, H, GRID_W, wr, GRID_W)
        p_ctx = p[..., wr * GRID_W:]
        return (jnp.einsum('bhqjk,bjkhd->bqhd', p_win, v_w)
                + jnp.einsum('bhql,blhd->bqhd', p_ctx, v_ctx))

    out = lax.map(one_row, jnp.arange(rows))
    return out.transpose(1, 0, 2, 3, 4).reshape(B, T, H, d)


def differential_attend(q, k, v, p, lambda_init):
    B, T, H, _, d = q.shape
    lam = (jnp.exp(jnp.sum((p['diff_lambda_q1'] * p['diff_lambda_k1']).astype(jnp.float32)))
           - jnp.exp(jnp.sum((p['diff_lambda_q2'] * p['diff_lambda_k2']).astype(jnp.float32)))
           + lambda_init)
    scale = d ** -0.5
    qb = q.reshape(B, T // Q_BLOCK, Q_BLOCK, H, 2, d).swapaxes(0, 1)

    def one(q_blk):
        s = jnp.einsum('bqhmd,bkhmd->bhmqk', q_blk, k).astype(jnp.float32) * scale
        pr = jax.nn.softmax(s, axis=-1)
        w = (pr[:, :, 0] - lam * pr[:, :, 1]).astype(v.dtype)
        return jnp.einsum('bhqk,bkhd->bqhd', w, v)

    o = lax.map(one, qb).swapaxes(0, 1).reshape(B, T, H, DIFF_V_DIM)
    o = rms_norm(o, p['diff_norm_g']) * (1 - lambda_init)
    return o.reshape(B, T, H * DIFF_V_DIM)


def merge_branches(h, out_a, out_b, out_c, out_d, p):
    B, T, _ = h.shape
    g = jax.nn.sigmoid(h @ p['w_gate'] + p['b_gate']).reshape(B, T, N_BRANCHES, D_MODEL)
    merged = (g[:, :, 0] * (out_a.reshape(B, T, -1) @ p['w_branch_a'])
              + g[:, :, 1] * (out_b.reshape(B, T, -1) @ p['w_branch_b'])
              + g[:, :, 2] * (out_c.reshape(B, T, -1) @ p['w_branch_c'])
              + g[:, :, 3] * (out_d.reshape(B, T, -1) @ p['w_branch_d']))
    return merged @ p['w_out']


def token_mixing(h, p, lambda_init, cache):
    B, T, _ = h.shape
    (a_u, a_v, m_cq, m_ckv, m_krope, n_q, n_k, n_v,
     d_q, d_k, d_v) = jnp.split(h @ p['w_in'], SPLIT_POINTS, axis=-1)
    out_a = spatial_gating(jax.nn.gelu(a_u, approximate=False), jax.nn.gelu(a_v, approximate=False),
                           p['sgu_norm_g'], p['sgu_w'], p['sgu_b'])
    m_q = (rms_norm(m_cq, p['mla_q_norm_g']) @ p['mla_w_uq']).reshape(B, T, MLA_HEADS, MLA_NOPE + MLA_ROPE)
    m_ckv = rms_norm(m_ckv, p['mla_kv_norm_g'])
    m_kr = m_krope[:, :, None, :]
    n_q = n_q.reshape(B, T, NA_HEADS, NA_HEAD_DIM)
    n_k = n_k.reshape(B, T, NA_HEADS, NA_HEAD_DIM)
    n_v = n_v.reshape(B, T, NA_HEADS, NA_HEAD_DIM)
    d_q = d_q.reshape(B, T, DIFF_HEADS, 2, DIFF_QK_DIM)
    d_k = d_k.reshape(B, T, DIFF_HEADS, 2, DIFF_QK_DIM)
    d_v = d_v.reshape(B, T, DIFF_HEADS, DIFF_V_DIM)
    if cache is None:
        k_b, v_b = mla_keys(m_ckv, m_kr, p['mla_w_ukv'])
        out_b = attend_blocks(m_q, k_b, v_b)
        out_c = attend_blocks(n_q, n_k, n_v)
        out_d = differential_attend(d_q, d_k, d_v, p, lambda_init)
        ctx_tensors = (m_ckv, m_krope, n_k, n_v, d_k.reshape(B, T, DIFF_HEADS, 2 * DIFF_QK_DIM), d_v)
    else:
        ckv_c, kr_c, nk_c, nv_c, dk_c, dv_c = cache
        L = ckv_c.shape[1]
        ang_b = axial_rope_angles(T, MLA_ROPE)
        ang_d = axial_rope_angles(T, DIFF_QK_DIM)
        m_q = jnp.concatenate([m_q[..., :MLA_NOPE], apply_rope(m_q[..., MLA_NOPE:], ang_b)], axis=-1)
        k_lat, v_lat = mla_keys(m_ckv, apply_rope(m_kr, ang_b), p['mla_w_ukv'])
        k_ctx, v_ctx = mla_keys(ckv_c, kr_c[:, :, None, :], p['mla_w_ukv'])
        out_b = attend_blocks(m_q, jnp.concatenate([k_lat, k_ctx], axis=1),
                              jnp.concatenate([v_lat, v_ctx], axis=1))
        out_c = neighborhood_attend(n_q, n_k, n_v, nk_c, nv_c, p['na_rpb'])
        d_k_all = jnp.concatenate([rope_pair(d_k, ang_d),
                                   dk_c.reshape(B, L, DIFF_HEADS, 2, DIFF_QK_DIM)], axis=1)
        d_v_all = jnp.concatenate([d_v, dv_c], axis=1)
        out_d = differential_attend(rope_pair(d_q, ang_d), d_k_all, d_v_all, p, lambda_init)
        ctx_tensors = None
    return merge_branches(h, out_a, out_b, out_c, out_d, p), ctx_tensors


def peer_ffn(h, w_q, subkeys, u_tab, v_tab):
    B, T, D = h.shape
    n_tok = B * T
    n_blk = n_tok // PEER_TOKEN_BLOCK
    x = h.reshape(n_tok, D)
    q = (x @ w_q).reshape(n_tok, PEER_HEADS, 2, PEER_KEY_DIM // 2)
    s = jnp.einsum('thpd,hpnd->thpn', q, subkeys).astype(jnp.float32)
    half_s, half_i = lax.top_k(s, PEER_TOPK)
    cand = half_s[:, :, 0, :, None] + half_s[:, :, 1, None, :]
    best_s, best_i = lax.top_k(cand.reshape(n_tok, PEER_HEADS, PEER_TOPK * PEER_TOPK), PEER_TOPK)
    i1 = jnp.take_along_axis(half_i[:, :, 0], best_i // PEER_TOPK, axis=-1)
    i2 = jnp.take_along_axis(half_i[:, :, 1], best_i % PEER_TOPK, axis=-1)
    experts = (i1 * PEER_N_KEYS + i2).reshape(n_blk, PEER_TOKEN_BLOCK, PEER_HEADS * PEER_TOPK)
    gates = jax.nn.softmax(best_s, axis=-1).astype(h.dtype).reshape(n_blk, PEER_TOKEN_BLOCK, PEER_HEADS * PEER_TOPK)
    xb = x.reshape(n_blk, PEER_TOKEN_BLOCK, D)

    def expert_block(args):
        xt, e, g = args
        act = jax.nn.gelu(jnp.einsum('td,tkd->tk', xt, jnp.take(u_tab, e, axis=0)), approximate=False) * g
        return jnp.einsum('tk,tkd->td', act, jnp.take(v_tab, e, axis=0))

    return lax.map(expert_block, (xb, experts, gates)).reshape(B, T, D)


def trunk_layer(x, cond, p, lambda_init, cache):
    sh1, sc1, g1, sh2, sc2, g2 = jnp.split(jax.nn.silu(cond) @ p['w_mod'] + p['b_mod'], 6, axis=-1)
    mix, ctx_tensors = token_mixing(modulate(x, sh1, sc1), p, lambda_init, cache)
    x = layer_norm_affine(DEEPNORM_ALPHA * x + g1[:, None] * mix, p['ln1_g'], p['ln1_b'])
    ffn = peer_ffn(modulate(x, sh2, sc2), p['peer_w_q'], p['peer_subkeys'], p['peer_u'], p['peer_v'])
    x = layer_norm_affine(DEEPNORM_ALPHA * x + g2[:, None] * ffn, p['ln2_g'], p['ln2_b'])
    return x, ctx_tensors


def setup_inputs(seed: int = 0) -> dict:
    key = jax.random.key(seed)
    ks = iter(jax.random.split(key, 64))

    def nrm(shape, scale=1.0):
        return scale * jax.random.normal(next(ks), shape, jnp.float32)

    def gain(shape):
        return 1.0 + nrm(shape, 0.01)

    D = D_MODEL
    return {
        'x_prompt': nrm((BATCH, SEQ, D)),
        'x_sample': nrm((DEC_BATCH, DEC_SEQ, D)),
        'cache_mla_ckv': nrm((DEC_BATCH, DEPTH, PAST_LEN, MLA_KV_LORA)),
        'cache_mla_krope': nrm((DEC_BATCH, DEPTH, PAST_LEN, MLA_ROPE)),
        'cache_na_k': nrm((DEC_BATCH, DEPTH, PAST_LEN, NA_HEADS, NA_HEAD_DIM)),
        'cache_na_v': nrm((DEC_BATCH, DEPTH, PAST_LEN, NA_HEADS, NA_HEAD_DIM)),
        'cache_diff_k': nrm((DEC_BATCH, DEPTH, PAST_LEN, DIFF_HEADS, 2 * DIFF_QK_DIM)),
        'cache_diff_v': nrm((DEC_BATCH, DEPTH, PAST_LEN, DIFF_HEADS, DIFF_V_DIM)),
        'c': nrm((DEC_BATCH, D)),
        'c_ctx': nrm((D,)),
        'w_mod': nrm((DEPTH, D, 6 * D), D ** -0.5),
        'b_mod': nrm((DEPTH, 6 * D), 0.01),
        'w_in': nrm((DEPTH, D, IN_WIDTH), D ** -0.5),
        'sgu_norm_g': gain((DEPTH, SGU_WIDTH)),
        'sgu_w': nrm((DEPTH, SGU_GROUPS, CHUNK, CHUNK), CHUNK ** -0.5),
        'sgu_b': gain((DEPTH, SGU_GROUPS, CHUNK)),
        'mla_q_norm_g': gain((DEPTH, MLA_Q_LORA)),
        'mla_w_uq': nrm((DEPTH, MLA_Q_LORA, MLA_HEADS * (MLA_NOPE + MLA_ROPE)), MLA_Q_LORA ** -0.5),
        'mla_kv_norm_g': gain((DEPTH, MLA_KV_LORA)),
        'mla_w_ukv': nrm((DEPTH, MLA_KV_LORA, MLA_HEADS * (MLA_NOPE + MLA_V)), MLA_KV_LORA ** -0.5),
        'na_rpb': nrm((DEPTH, NA_HEADS, 2 * NA_WIN_ROWS - 1, 2 * NA_WIN_COLS - 1), 0.1),
        'diff_lambda_q1': nrm((DEPTH, DIFF_QK_DIM), 0.1),
        'diff_lambda_k1': nrm((DEPTH, DIFF_QK_DIM), 0.1),
        'diff_lambda_q2': nrm((DEPTH, DIFF_QK_DIM), 0.1),
        'diff_lambda_k2': nrm((DEPTH, DIFF_QK_DIM), 0.1),
        'diff_norm_g': gain((DEPTH, DIFF_V_DIM)),
        'w_branch_a': nrm((DEPTH, SGU_WIDTH, D), SGU_WIDTH ** -0.5),
        'w_branch_b': nrm((DEPTH, MLA_HEADS * MLA_V, D), (MLA_HEADS * MLA_V) ** -0.5),
        'w_branch_c': nrm((DEPTH, NA_HEADS * NA_HEAD_DIM, D), (NA_HEADS * NA_HEAD_DIM) ** -0.5),
        'w_branch_d': nrm((DEPTH, DIFF_HEADS * DIFF_V_DIM, D), (DIFF_HEADS * DIFF_V_DIM) ** -0.5),
        'w_gate': nrm((DEPTH, D, N_BRANCHES * D), D ** -0.5),
        'b_gate': nrm((DEPTH, N_BRANCHES * D), 0.01),
        'w_out': nrm((DEPTH, D, D), DEEPNORM_BETA * D ** -0.5),
        'ln1_g': gain((DEPTH, D)),
        'ln1_b': nrm((DEPTH, D), 0.01),
        'peer_w_q': nrm((DEPTH, D, PEER_HEADS * PEER_KEY_DIM), D ** -0.5),
        'peer_subkeys': nrm((DEPTH, PEER_HEADS, 2, PEER_N_KEYS, PEER_KEY_DIM // 2), (PEER_KEY_DIM // 2) ** -0.5),
        'peer_u': nrm((DEPTH, PEER_N_EXPERTS, D), D ** -0.5),
        'peer_v': nrm((DEPTH, PEER_N_EXPERTS, D), DEEPNORM_BETA),
        'ln2_g': gain((DEPTH, D)),
        'ln2_b': nrm((DEPTH, D), 0.01),
    }


def reference(x_prompt, x_sample, cache_mla_ckv, cache_mla_krope, cache_na_k, cache_na_v,
              cache_diff_k, cache_diff_v, c, c_ctx,
              w_mod, b_mod, w_in, sgu_norm_g, sgu_w, sgu_b,
              mla_q_norm_g, mla_w_uq, mla_kv_norm_g, mla_w_ukv, na_rpb,
              diff_lambda_q1, diff_lambda_k1, diff_lambda_q2, diff_lambda_k2, diff_norm_g,
              w_branch_a, w_branch_b, w_branch_c, w_branch_d, w_gate, b_gate, w_out,
              ln1_g, ln1_b, peer_w_q, peer_subkeys, peer_u, peer_v, ln2_g, ln2_b):
    params = dict(w_mod=w_mod, b_mod=b_mod, w_in=w_in, sgu_norm_g=sgu_norm_g, sgu_w=sgu_w, sgu_b=sgu_b,
                  mla_q_norm_g=mla_q_norm_g, mla_w_uq=mla_w_uq, mla_kv_norm_g=mla_kv_norm_g,
                  mla_w_ukv=mla_w_ukv, na_rpb=na_rpb,
                  diff_lambda_q1=diff_lambda_q1, diff_lambda_k1=diff_lambda_k1,
                  diff_lambda_q2=diff_lambda_q2, diff_lambda_k2=diff_lambda_k2, diff_norm_g=diff_norm_g,
                  w_branch_a=w_branch_a, w_branch_b=w_branch_b, w_branch_c=w_branch_c, w_branch_d=w_branch_d,
                  w_gate=w_gate, b_gate=b_gate, w_out=w_out, ln1_g=ln1_g, ln1_b=ln1_b,
                  peer_w_q=peer_w_q, peer_subkeys=peer_subkeys, peer_u=peer_u, peer_v=peer_v,
                  ln2_g=ln2_g, ln2_b=ln2_b)
    y_prompt = x_prompt
    ctx_layers = []
    for l in range(DEPTH):
        p = {name: arr[l] for name, arr in params.items()}
        lambda_init = 0.8 - 0.6 * math.exp(-0.3 * l)
        y_prompt, ctx_tensors = trunk_layer(y_prompt, c_ctx[None], p, lambda_init, None)
        ctx_layers.append(ctx_tensors)
    new_mla_ckv = jnp.stack([t[0] for t in ctx_layers], axis=1)
    new_mla_krope = jnp.stack([t[1] for t in ctx_layers], axis=1)
    new_na_k = jnp.stack([t[2] for t in ctx_layers], axis=1)
    new_na_v = jnp.stack([t[3] for t in ctx_layers], axis=1)
    new_diff_k = jnp.stack([t[4] for t in ctx_layers], axis=1)
    new_diff_v = jnp.stack([t[5] for t in ctx_layers], axis=1)
    y_sample = x_sample
    for l in range(DEPTH):
        p = {name: arr[l] for name, arr in params.items()}
        lambda_init = 0.8 - 0.6 * math.exp(-0.3 * l)
        cache = (cache_mla_ckv[:, l], cache_mla_krope[:, l], cache_na_k[:, l], cache_na_v[:, l],
                 cache_diff_k[:, l], cache_diff_v[:, l])
        y_sample, _ = trunk_layer(y_sample, c, p, lambda_init, cache)
    return (y_prompt, y_sample, new_mla_ckv, new_mla_krope, new_na_k, new_na_v, new_diff_k, new_diff_v)
```

```python
import functools
import math

import jax
import jax.numpy as jnp
from jax import lax
from jax.experimental import pallas as pl
from jax.experimental.pallas import tpu as pltpu

F32 = jnp.float32
BF16 = jnp.bfloat16

D_MODEL = 1024
BATCH = 32
SEQ = 256
DEPTH = 2
DEC_BATCH = 8
DEC_SEQ = 1024
PAST_LEN = 512
GRID_W = 64
CHUNK = 128
SGU_GROUPS = 4
SGU_WIDTH = 256
MLA_HEADS = 4
MLA_Q_LORA = 256
MLA_KV_LORA = 128
MLA_NOPE = 64
MLA_ROPE = 32
MLA_V = 64
NA_HEADS = 4
NA_HEAD_DIM = 64
NA_WIN_ROWS = 8
NA_WIN_COLS = 16
DIFF_HEADS = 4
DIFF_QK_DIM = 64
DIFF_V_DIM = 128
N_BRANCHES = 4
PEER_HEADS = 8
PEER_N_KEYS = 128
PEER_KEY_DIM = 256
PEER_TOPK = 16
ROPE_THETA = 10000.0
LN_EPS = 1e-6
NEG_BIG = -1e30
DEEPNORM_ALPHA = (2 * DEPTH) ** 0.25

LANES = 128
N_CTX = BATCH * SEQ
N_LAT = DEC_BATCH * DEC_SEQ
N_TOK = N_CTX + N_LAT
N_COND = 16
TM = 512
ROWS = DEC_SEQ // GRID_W
Q_TILE = 256
ROUTER_TILE = 256
PEER_TILE = 512
EXPERT_BLOCK = 512
VMEM_LIMIT = 56 * 1024 * 1024

C_AU, C_AV, C_CQ, C_CKV, C_KR = 0, 256, 512, 768, 896
C_NQ, C_NK, C_NV, C_DQ, C_DK, C_DV, C_END = 1024, 1280, 1536, 1792, 2304, 2816, 3328


def _ln(x):
    mu = jnp.mean(x, axis=-1, keepdims=True)
    xc = x - mu
    var = jnp.mean(xc * xc, axis=-1, keepdims=True)
    return xc * lax.rsqrt(var + LN_EPS)


def _rms(x):
    return x * lax.rsqrt(jnp.mean(x * x, axis=-1, keepdims=True) + LN_EPS)


def _gelu(x):
    return 0.5 * x * (1.0 + lax.erf(x * (1.0 / math.sqrt(2.0))))


def _dot(a, b):
    return jnp.dot(a, b, preferred_element_type=F32)


def _dot_nt(a, b):
    return lax.dot_general(a, b, (((1,), (1,)), ((), ())), preferred_element_type=F32)


def _rope(x, tab_ref, half):
    return (x * tab_ref[0] + pltpu.roll(x, LANES - half, 1) * tab_ref[1] + pltpu.roll(x, half, 1) * tab_ref[2])


def _cparams(*sem):
    return pltpu.CompilerParams(dimension_semantics=sem, vmem_limit_bytes=VMEM_LIMIT)


def _mod_kernel(cond_ref, w_ref, b_ref, o_ref):
    c = cond_ref[...]
    s = c * jax.nn.sigmoid(c)
    o_ref[...] = _dot(s, w_ref[...]) + b_ref[...]


def _mod_call(cond, w_mod, b_mod):
    nb = 1536
    return pl.pallas_call(
        _mod_kernel,
        grid=(DEPTH, 6 * D_MODEL // nb),
        in_specs=[pl.BlockSpec((N_COND, D_MODEL), lambda l, j: (0, 0)),
                  pl.BlockSpec((None, D_MODEL, nb), lambda l, j: (l, 0, j)),
                  pl.BlockSpec((None, 1, nb), lambda l, j: (l, 0, j))],
        out_specs=pl.BlockSpec((None, N_COND, nb), lambda l, j: (l, 0, j)),
        out_shape=jax.ShapeDtypeStruct((DEPTH, N_COND, 6 * D_MODEL), F32),
        compiler_params=_cparams("arbitrary", "arbitrary"),
        name="mod_vectors",
    )(cond, w_mod, b_mod.reshape(DEPTH, 1, 6 * D_MODEL))


def _mod_row(i):
    n_ctx_tiles = N_CTX // TM
    return jnp.where(i < n_ctx_tiles, 0, 1 + (i - n_ctx_tiles) // (DEC_SEQ // TM))


def _pos_block(i):
    n_ctx_tiles = N_CTX // TM
    return jnp.where(i < n_ctx_tiles, DEC_SEQ // TM, (i - n_ctx_tiles) % (DEC_SEQ // TM))


def _inproj_kernel(x_ref, mod_ref, rb_ref, rd_ref, w_in_ref, sgu_g_ref, sgu_w_ref, sgu_bias_ref,
                   qg_ref, kvg_ref, wuq_ref,
                   oa_ref, mq_ref, ckv_ref, kr_ref, nq_ref, nk_ref, nv_ref, dq_ref, dk_ref, dv_ref):
    x = x_ref[...]
    shift = mod_ref[0, :, 0:D_MODEL]
    scale = mod_ref[0, :, D_MODEL:2 * D_MODEL]
    h = (_ln(x) * (1.0 + scale) + shift).astype(BF16)

    ya = _dot(h, w_in_ref[:, C_AU:C_CQ])
    u = _gelu(ya[:, :SGU_WIDTH])
    v = _gelu(ya[:, SGU_WIDTH:])
    vn = (_ln(v) * sgu_g_ref[...]).astype(BF16)
    group = lax.broadcasted_iota(jnp.int32, (CHUNK, SGU_WIDTH), 1) // (SGU_WIDTH // SGU_GROUPS)
    for c in range(TM // CHUNK):
        rows = slice(c * CHUNK, (c + 1) * CHUNK)
        mixed = sgu_bias_ref[...]
        for g in range(SGU_GROUPS):
            mixed = mixed + jnp.where(group == g, _dot(sgu_w_ref[g], vn[rows]), 0.0)
        oa_ref[rows, :] = (u[rows] * mixed).astype(oa_ref.dtype)

    ym = _dot(h, w_in_ref[:, C_CQ:C_NQ])
    cq = (_rms(ym[:, :MLA_Q_LORA]) * qg_ref[...]).astype(BF16)
    mq = _dot(cq, wuq_ref[...])
    for g in range(MLA_HEADS):
        lanes = slice(g * LANES, (g + 1) * LANES)
        mq_ref[:, lanes] = _rope(mq[:, lanes], rb_ref, MLA_ROPE // 2)
    ckv_ref[...] = _rms(ym[:, MLA_Q_LORA:MLA_Q_LORA + MLA_KV_LORA]) * kvg_ref[...]
    kr_ref[...] = _rope(ym[:, MLA_Q_LORA + MLA_KV_LORA:], rb_ref, MLA_ROPE // 2)

    yn = _dot(h, w_in_ref[:, C_NQ:C_DQ])
    nq_ref[...] = yn[:, 0:256]
    nk_ref[...] = yn[:, 256:512]
    nv_ref[...] = yn[:, 512:768]

    yd = _dot(h, w_in_ref[:, C_DQ:C_END])
    for g in range(4):
        lanes = slice(g * LANES, (g + 1) * LANES)
        dq_ref[:, lanes] = _rope(yd[:, g * LANES:(g + 1) * LANES], rd_ref, DIFF_QK_DIM // 2)
        dk_ref[:, lanes] = _rope(yd[:, 512 + g * LANES:512 + (g + 1) * LANES], rd_ref, DIFF_QK_DIM // 2)
    dv_ref[...] = yd[:, 1024:1536]


def _inproj_call(x, mod, rope_b, rope_d, w_in_r, sgu_g, sgu_w, sgu_bias, qg, kvg, wuq):
    tile = lambda w: pl.BlockSpec((TM, w), lambda i: (i, 0))
    full = lambda *s: pl.BlockSpec(s, lambda i: (0,) * len(s))
    widths = (SGU_WIDTH, 512, MLA_KV_LORA, LANES, 256, 256, 256, 512, 512, 512)
    dtypes = (BF16,) + (F32,) * 9
    return pl.pallas_call(
        _inproj_kernel,
        grid=(N_TOK // TM,),
        in_specs=[tile(D_MODEL),
                  pl.BlockSpec((1, 1, 6 * D_MODEL), lambda i: (_mod_row(i), 0, 0)),
                  pl.BlockSpec((3, TM, LANES), lambda i: (0, _pos_block(i), 0)),
                  pl.BlockSpec((3, TM, LANES), lambda i: (0, _pos_block(i), 0)),
                  full(D_MODEL, C_END), full(1, SGU_WIDTH), full(SGU_GROUPS, CHUNK, CHUNK),
                  full(CHUNK, SGU_WIDTH), full(1, MLA_Q_LORA), full(1, MLA_KV_LORA),
                  full(MLA_Q_LORA, MLA_HEADS * LANES)],
        out_specs=[tile(w) for w in widths],
        out_shape=[jax.ShapeDtypeStruct((N_TOK, w), dt) for w, dt in zip(widths, dtypes)],
        compiler_params=_cparams("parallel"),
        name="in_projection",
    )(x, mod, rope_b, rope_d, w_in_r, sgu_g, sgu_w, sgu_bias, qg, kvg, wuq)


def _half_mask(lo):
    lane = lax.broadcasted_iota(jnp.int32, (1, LANES), 1)
    return (lane >= lo) & (lane < lo + 64)


def _softmax_pv(scores, values, lanes):
    m = scores[0].max(axis=-1, keepdims=True)
    for s in scores[1:]:
        m = jnp.maximum(m, s.max(axis=-1, keepdims=True))
    den = None
    o = None
    for s, v in zip(scores, values):
        p = jnp.exp(s - m)
        d = p.sum(axis=-1, keepdims=True)
        den = d if den is None else den + d
        pv = _dot(p.astype(BF16), v[:, lanes])
        o = pv if o is None else o + pv
    return o / den


def _pair_attention(q, keys, vals, scale, bias_fn=None):
    outs = []
    for pair in range(2):
        lanes = slice(pair * LANES, (pair + 1) * LANES)
        qp = q[:, lanes]
        acc = None
        for sub in range(2):
            head = 2 * pair + sub
            mask = _half_mask(64 * sub)
            qm = jnp.where(mask, qp, jnp.zeros_like(qp))
            scores = [_dot_nt(qm, k[:, lanes]) * scale for k in keys]
            if bias_fn is not None:
                scores = bias_fn(head, scores)
            o = jnp.where(mask, _softmax_pv(scores, vals, lanes), 0.0)
            acc = o if acc is None else acc + o
        outs.append(acc)
    return outs


def _mla_attention(q, k_blocks, v_blocks, o_ref, rows):
    scale = (MLA_NOPE + MLA_ROPE) ** -0.5
    for pair in range(2):
        lanes = slice(pair * LANES, (pair + 1) * LANES)
        acc = None
        for sub in range(2):
            head = 2 * pair + sub
            hl = slice(head * LANES, (head + 1) * LANES)
            scores = [_dot_nt(q[:, hl], k[:, hl]) * scale for k in k_blocks]
            o = jnp.where(_half_mask(64 * sub), _softmax_pv(scores, v_blocks, lanes), 0.0)
            acc = o if acc is None else acc + o
        o_ref[rows, lanes] = acc.astype(o_ref.dtype)


def _diff_lambda(lq1, lk1, lq2, lk2, lambda_init):
    a = jnp.sum(lq1[...] * lk1[...], axis=-1, keepdims=True)
    b = jnp.sum(lq2[...] * lk2[...], axis=-1, keepdims=True)
    return jnp.exp(a) - jnp.exp(b) + lambda_init


def _diff_attention(q, k_blocks, v_blocks, lam, norm_g, lambda_init, o_ref, rows):
    scale = DIFF_QK_DIM ** -0.5
    for head in range(DIFF_HEADS):
        hl = slice(head * LANES, (head + 1) * LANES)
        qh = q[:, hl]
        probs = []
        for sub in range(2):
            qm = jnp.where(_half_mask(64 * sub), qh, jnp.zeros_like(qh))
            scores = [_dot_nt(qm, k[:, hl]) * scale for k in k_blocks]
            m = scores[0].max(axis=-1, keepdims=True)
            for s in scores[1:]:
                m = jnp.maximum(m, s.max(axis=-1, keepdims=True))
            ps = [jnp.exp(s - m) for s in scores]
            den = ps[0].sum(axis=-1, keepdims=True)
            for p in ps[1:]:
                den = den + p.sum(axis=-1, keepdims=True)
            probs.append((ps, 1.0 / den))
        o = None
        for i, v in enumerate(v_blocks):
            w = probs[0][0][i] * probs[0][1] - probs[1][0][i] * (lam * probs[1][1])
            pv = _dot(w.astype(BF16), v[:, hl])
            o = pv if o is None else o + pv
        o = _rms(o) * norm_g * (1.0 - lambda_init)
        o_ref[rows, hl] = o.astype(o_ref.dtype)


def _ctx_attn_kernel(lambda_init, mq_ref, ckv_ref, kr_ref, nq_ref, nk_ref, nv_ref, dq_ref, dk_ref, dv_ref,
                     wuk_ref, wuv_ref, lq1, lk1, lq2, lk2, dg_ref, ob_ref, oc_ref, od_ref):
    rows = slice(0, SEQ)
    ckv = ckv_ref[...].astype(BF16)
    kr = kr_ref[...]
    k_b = (_dot(ckv, wuk_ref[...]) + jnp.concatenate([kr] * MLA_HEADS, axis=1)).astype(BF16)
    v_b = _dot(ckv, wuv_ref[...]).astype(BF16)
    _mla_attention(mq_ref[...].astype(BF16), [k_b], [v_b], ob_ref, rows)

    outs = _pair_attention(nq_ref[...].astype(BF16), [nk_ref[...].astype(BF16)], [nv_ref[...].astype(BF16)],
                           NA_HEAD_DIM ** -0.5)
    for pair in range(2):
        oc_ref[:, pair * LANES:(pair + 1) * LANES] = outs[pair].astype(oc_ref.dtype)

    lam = _diff_lambda(lq1, lk1, lq2, lk2, lambda_init)
    _diff_attention(dq_ref[...].astype(BF16), [dk_ref[...].astype(BF16)], [dv_ref[...].astype(BF16)],
                    lam, dg_ref[...], lambda_init, od_ref, rows)


def _ctx_attn_call(lambda_init, acts, wuk, wuv, lams, dg):
    mq, ckv, kr, nq, nk, nv, dq, dk, dv = acts
    seq = lambda w: pl.BlockSpec((SEQ, w), lambda b: (b, 0))
    full = lambda *s: pl.BlockSpec(s, lambda b: (0,) * len(s))
    return pl.pallas_call(
        functools.partial(_ctx_attn_kernel, lambda_init),
        grid=(BATCH,),
        in_specs=[seq(512), seq(128), seq(128), seq(256), seq(256), seq(256), seq(512), seq(512), seq(512),
                  full(MLA_KV_LORA, 512), full(MLA_KV_LORA, 256)] + [full(1, DIFF_QK_DIM)] * 4
                 + [full(1, DIFF_V_DIM)],
        out_specs=[seq(256), seq(256), seq(512)],
        out_shape=[jax.ShapeDtypeStruct((N_CTX, w), BF16) for w in (256, 256, 512)],
        compiler_params=_cparams("parallel"),
        name="context_attention",
    )(mq, ckv, kr, nq, nk, nv, dq, dk, dv, wuk, wuv, *lams, dg)


def _lat_mla_kernel(mq_ref, ckv_ref, kr_ref, cckv_ref, ckr_ref, wuk_ref, wuv_ref, o_ref):
    def expand(ckv_f32, kr):
        ckv = ckv_f32.astype(BF16)
        k = (_dot(ckv, wuk_ref[...]) + jnp.concatenate([kr] * MLA_HEADS, axis=1)).astype(BF16)
        return k, _dot(ckv, wuv_ref[...]).astype(BF16)

    k_lat, v_lat = expand(ckv_ref[...], kr_ref[...])
    k_ctx, v_ctx = expand(cckv_ref[...], ckr_ref[...])
    for t in range(DEC_SEQ // Q_TILE):
        rows = slice(t * Q_TILE, (t + 1) * Q_TILE)
        _mla_attention(mq_ref[rows, :].astype(BF16), [k_lat, k_ctx], [v_lat, v_ctx], o_ref, rows)


def _lat_mla_call(l, mq, ckv, kr, cache_ckv, cache_kr_pad, wuk, wuv):
    off = N_CTX // DEC_SEQ
    seq = lambda w: pl.BlockSpec((DEC_SEQ, w), lambda b: (off + b, 0))
    cache = lambda w: pl.BlockSpec((None, None, PAST_LEN, w), lambda b: (b, l, 0, 0))
    full = lambda *s: pl.BlockSpec(s, lambda b: (0,) * len(s))
    return pl.pallas_call(
        _lat_mla_kernel,
        grid=(DEC_BATCH,),
        in_specs=[seq(512), seq(128), seq(128), cache(MLA_KV_LORA), cache(LANES),
                  full(MLA_KV_LORA, 512), full(MLA_KV_LORA, 256)],
        out_specs=pl.BlockSpec((DEC_SEQ, 256), lambda b: (b, 0)),
        out_shape=jax.ShapeDtypeStruct((N_LAT, 256), BF16),
        compiler_params=_cparams("parallel"),
        name="latent_mla_attention",
    )(mq, ckv, kr, cache_ckv, cache_kr_pad, wuk, wuv)


def _win_start(r):
    return jnp.clip(r - NA_WIN_ROWS // 2, 0, ROWS - NA_WIN_ROWS)


def _lat_na_kernel(nq_ref, nk_ref, nv_ref, ck_ref, cv_ref, bias_ref, o_ref):
    r = pl.program_id(1)
    start = pl.multiple_of(_win_start(r) * GRID_W, GRID_W)
    win = NA_WIN_ROWS * GRID_W
    k_w = nk_ref[pl.ds(start, win), :].astype(BF16)
    v_w = nv_ref[pl.ds(start, win), :].astype(BF16)
    k_c = ck_ref[...].astype(BF16)
    v_c = cv_ref[...].astype(BF16)
    q_col = lax.broadcasted_iota(jnp.int32, (GRID_W, win), 0)
    k_col = lax.broadcasted_iota(jnp.int32, (GRID_W, win), 1) % GRID_W
    c0 = jnp.clip(q_col - NA_WIN_COLS // 2, 0, GRID_W - NA_WIN_COLS)
    col_in = (k_col >= c0) & (k_col < c0 + NA_WIN_COLS)

    def bias_fn(head, scores):
        return [jnp.where(col_in, scores[0] + bias_ref[0, head], NEG_BIG), scores[1]]

    outs = _pair_attention(nq_ref[...].astype(BF16), [k_w, k_c], [v_w, v_c], NA_HEAD_DIM ** -0.5, bias_fn)
    for pair in range(2):
        o_ref[:, pair * LANES:(pair + 1) * LANES] = outs[pair].astype(o_ref.dtype)


def _lat_na_call(l, nq, nk, nv, cache_k, cache_v, bias_tab):
    off = N_CTX // DEC_SEQ
    seq = pl.BlockSpec((DEC_SEQ, 256), lambda b, r: (off + b, 0))
    cache = pl.BlockSpec((None, None, PAST_LEN, 256), lambda b, r: (b, l, 0, 0))
    return pl.pallas_call(
        _lat_na_kernel,
        grid=(DEC_BATCH, ROWS),
        in_specs=[pl.BlockSpec((GRID_W, 256), lambda b, r: (N_CTX // GRID_W + b * ROWS + r, 0)),
                  seq, seq, cache, cache,
                  pl.BlockSpec((1, NA_HEADS, GRID_W, NA_WIN_ROWS * GRID_W),
                               lambda b, r: (_win_start(r) - r + NA_WIN_ROWS - 1, 0, 0, 0))],
        out_specs=pl.BlockSpec((GRID_W, 256), lambda b, r: (b * ROWS + r, 0)),
        out_shape=jax.ShapeDtypeStruct((N_LAT, 256), BF16),
        compiler_params=_cparams("parallel", "arbitrary"),
        name="latent_neighbourhood_attention",
    )(nq, nk, nv, cache_k, cache_v, bias_tab)


def _lat_diff_kernel(lambda_init, dq_ref, dk_ref, dv_ref, ck_ref, cv_ref, lq1, lk1, lq2, lk2, dg_ref, o_ref):
    lam = _diff_lambda(lq1, lk1, lq2, lk2, lambda_init)
    k_blocks = [dk_ref[...].astype(BF16), ck_ref[...].astype(BF16)]
    v_blocks = [dv_ref[...].astype(BF16), cv_ref[...].astype(BF16)]
    for t in range(DEC_SEQ // Q_TILE):
        rows = slice(t * Q_TILE, (t + 1) * Q_TILE)
        _diff_attention(dq_ref[rows, :].astype(BF16), k_blocks, v_blocks, lam, dg_ref[...], lambda_init,
                        o_ref, rows)


def _lat_diff_call(l, lambda_init, dq, dk, dv, cache_k, cache_v, lams, dg):
    off = N_CTX // DEC_SEQ
    seq = pl.BlockSpec((DEC_SEQ, 512), lambda b: (off + b, 0))
    cache = pl.BlockSpec((None, None, PAST_LEN, 512), lambda b: (b, l, 0, 0))
    full = lambda *s: pl.BlockSpec(s, lambda b: (0,) * len(s))
    return pl.pallas_call(
        functools.partial(_lat_diff_kernel, lambda_init),
        grid=(DEC_BATCH,),
        in_specs=[seq, seq, seq, cache, cache] + [full(1, DIFF_QK_DIM)] * 4 + [full(1, DIFF_V_DIM)],
        out_specs=pl.BlockSpec((DEC_SEQ, 512), lambda b: (b, 0)),
        out_shape=jax.ShapeDtypeStruct((N_LAT, 512), BF16),
        compiler_params=_cparams("parallel"),
        name="latent_differential_attention",
    )(dq, dk, dv, cache_k, cache_v, *lams, dg)


def _merge_kernel(x_ref, mod_ref, oa_ref, ob_ref, oc_ref, od_ref, wg_ref, bg_ref,
                  wa_ref, wb_ref, wc_ref, wd_ref, wo_ref, g_ref, b_ref, x1_ref, h2t_ref):
    x = x_ref[...]
    mod = lambda k: mod_ref[0, :, k * D_MODEL:(k + 1) * D_MODEL]
    h = (_ln(x) * (1.0 + mod(1)) + mod(0)).astype(BF16)
    merged = None
    for i, (o_ref, w_ref) in enumerate(((oa_ref, wa_ref), (ob_ref, wb_ref), (oc_ref, wc_ref), (od_ref, wd_ref))):
        cols = slice(i * D_MODEL, (i + 1) * D_MODEL)
        gate = jax.nn.sigmoid(_dot(h, wg_ref[:, cols]) + bg_ref[:, cols])
        term = gate * _dot(o_ref[...], w_ref[...])
        merged = term if merged is None else merged + term
    mix = _dot(merged.astype(BF16), wo_ref[...])
    x1 = _ln(DEEPNORM_ALPHA * x + mod(2) * mix) * g_ref[...] + b_ref[...]
    x1_ref[...] = x1
    h2 = _ln(x1) * (1.0 + mod(4)) + mod(3)
    h2t_ref[...] = h2.T.astype(BF16)


def _merge_call(x, mod, oa, ob, oc, od, wg, bg, wa, wb, wc, wd, wo, g, b):
    tile = lambda w: pl.BlockSpec((TM, w), lambda i: (i, 0))
    full = lambda *s: pl.BlockSpec(s, lambda i: (0,) * len(s))
    return pl.pallas_call(
        _merge_kernel,
        grid=(N_TOK // TM,),
        in_specs=[tile(D_MODEL), pl.BlockSpec((1, 1, 6 * D_MODEL), lambda i: (_mod_row(i), 0, 0)),
                  tile(256), tile(256), tile(256), tile(512),
                  full(D_MODEL, 4 * D_MODEL), full(1, 4 * D_MODEL),
                  full(256, D_MODEL), full(256, D_MODEL), full(256, D_MODEL), full(512, D_MODEL),
                  full(D_MODEL, D_MODEL), full(1, D_MODEL), full(1, D_MODEL)],
        out_specs=[tile(D_MODEL), pl.BlockSpec((D_MODEL, TM), lambda i: (0, i))],
        out_shape=[jax.ShapeDtypeStruct((N_TOK, D_MODEL), F32), jax.ShapeDtypeStruct((D_MODEL, N_TOK), BF16)],
        compiler_params=_cparams("parallel"),
        name="branch_merge",
    )(x, mod, oa, ob, oc, od, wg, bg, wa, wb, wc, wd, wo, g, b)


def _top16(s):
    row = lax.broadcasted_iota(jnp.int32, s.shape, 0).astype(F32)
    krow = lax.broadcasted_iota(jnp.int32, (PEER_TOPK, s.shape[1]), 0)

    def body(k, carry):
        work, rank, vals = carry
        m = jnp.max(work, axis=0, keepdims=True)
        idx = jnp.min(jnp.where(work == m, row, float(PEER_N_KEYS)), axis=0, keepdims=True)
        sel = row == idx
        rank = jnp.where(sel, k.astype(F32), rank)
        work = jnp.where(sel, -jnp.inf, work)
        vals = jnp.where(krow == k, m, vals)
        return work, rank, vals

    init = (s, jnp.full(s.shape, float(PEER_N_KEYS), F32), jnp.zeros((PEER_TOPK, s.shape[1]), F32))
    _, rank, vals = lax.fori_loop(0, PEER_TOPK, body, init)
    return vals, rank


def _row_gather(table, idx):
    out = jnp.zeros(idx.shape, F32)
    for k in range(PEER_TOPK):
        out = jnp.where(idx == float(k), table[k:k + 1, :], out)
    return out


def _router_kernel(h2t_ref, wqt_ref, keys_ref, r2_ref, e2_ref, n1_ref, e1_ref, q_scr):
    q_scr[...] = _dot(wqt_ref[...], h2t_ref[...]).astype(BF16)
    t = h2t_ref.shape[1]
    krow = lax.broadcasted_iota(jnp.int32, (PEER_TOPK, t), 0).astype(F32)

    def head_body(hd, _):
        base = pl.multiple_of(hd * PEER_KEY_DIM, PEER_KEY_DIM)
        s1 = _dot(keys_ref[2 * hd], q_scr[pl.ds(base, LANES), :])
        s2 = _dot(keys_ref[2 * hd + 1], q_scr[pl.ds(base + LANES, LANES), :])
        hs1, rank1 = _top16(s1)
        hs2, rank2 = _top16(s2)

        def merge_body(_, carry):
            cnt, front = carry
            m = jnp.max(front, axis=0, keepdims=True)
            win = jnp.min(jnp.where(front == m, krow, float(PEER_TOPK)), axis=0, keepdims=True)
            sel = krow == win
            cnt = jnp.where(sel, cnt + 1.0, cnt)
            nxt = jnp.where(cnt < float(PEER_TOPK), hs1 + _row_gather(hs2, cnt), -jnp.inf)
            return cnt, jnp.where(sel, nxt, front)

        cnt, _ = lax.fori_loop(0, PEER_TOPK, merge_body, (jnp.zeros((PEER_TOPK, t), F32), hs1 + hs2[0:1, :]))

        e1r = jnp.exp(hs1 - hs1[0:1, :])
        e2r = jnp.exp(hs2 - hs2[0:1, :])
        prefix = jnp.zeros((PEER_TOPK, t), F32)
        for kb in range(PEER_TOPK):
            prefix = prefix + jnp.where(cnt > float(kb), e2r[kb:kb + 1, :], 0.0)
        z = jnp.sum(e1r * prefix, axis=0, keepdims=True)

        r2_ref[hd] = rank2
        e2_ref[hd] = jnp.exp(s2 - hs2[0:1, :]) / z
        n1_ref[hd] = _row_gather(cnt, rank1)
        e1_ref[hd] = jnp.exp(s1 - hs1[0:1, :])
        return 0

    lax.fori_loop(0, PEER_HEADS, head_body, 0)


def _router_call(h2t, wqt, keys):
    t = ROUTER_TILE
    out = pl.BlockSpec((PEER_HEADS, PEER_N_KEYS, t), lambda i: (0, 0, i))
    return pl.pallas_call(
        _router_kernel,
        grid=(N_TOK // t,),
        in_specs=[pl.BlockSpec((D_MODEL, t), lambda i: (0, i)),
                  pl.BlockSpec((PEER_HEADS * PEER_KEY_DIM, D_MODEL), lambda i: (0, 0)),
                  pl.BlockSpec((2 * PEER_HEADS, PEER_N_KEYS, PEER_KEY_DIM // 2), lambda i: (0, 0, 0))],
        out_specs=[out] * 4,
        out_shape=[jax.ShapeDtypeStruct((PEER_HEADS, PEER_N_KEYS, N_TOK), F32)] * 4,
        scratch_shapes=[pltpu.VMEM((PEER_HEADS * PEER_KEY_DIM, t), BF16)],
        compiler_params=_cparams("parallel"),
        name="peer_retrieval",
    )(h2t, wqt, keys)


def _peer_kernel(h2t_ref, u_ref, vt_ref, r2_ref, e2_ref, n1_ref, e1_ref, x1_ref, mod_ref, g_ref, b_ref,
                 o_ref, acc_ref, ht_ref, w_ref):
    e = pl.program_id(1)

    @pl.when(e == 0)
    def _():
        acc_ref[...] = jnp.zeros_like(acc_ref)

    ht_ref[...] = _dot(u_ref[...], h2t_ref[...])
    per_block = EXPERT_BLOCK // PEER_N_KEYS

    def key_body(i, _):
        rows = pl.ds(pl.multiple_of(i * PEER_N_KEYS, PEER_N_KEYS), PEER_N_KEYS)
        for j in range(PEER_TILE // LANES):
            lanes = slice(j * LANES, (j + 1) * LANES)
            gate = jnp.zeros((PEER_N_KEYS, LANES), F32)
            n_rows = n1_ref[i, :, lanes]
            c_rows = e1_ref[i, :, lanes]
            for hd in range(PEER_HEADS):
                n_row = n_rows[hd:hd + 1, :]
                c_row = c_rows[hd:hd + 1, :]
                gate = gate + jnp.where(r2_ref[hd, :, lanes] < n_row, e2_ref[hd, :, lanes], 0.0) * c_row
            w_ref[rows, lanes] = (_gelu(ht_ref[rows, lanes]) * gate).astype(BF16)
        return 0

    lax.fori_loop(0, per_block, key_body, 0)
    acc_ref[...] += _dot(vt_ref[...], w_ref[...])

    @pl.when(e == pl.num_programs(1) - 1)
    def _():
        ffn = acc_ref[...].T
        g2 = mod_ref[0, :, 5 * D_MODEL:6 * D_MODEL]
        o_ref[...] = _ln(DEEPNORM_ALPHA * x1_ref[...] + g2 * ffn) * g_ref[...] + b_ref[...]


def _peer_mod_row(i):
    n_ctx_tiles = N_CTX // PEER_TILE
    return jnp.where(i < n_ctx_tiles, 0, 1 + (i - n_ctx_tiles) // (DEC_SEQ // PEER_TILE))


def _peer_call(h2t, u, vt, r2, e2, n1, e1, x1, mod, g, b):
    t = PEER_TILE
    n_exp = PEER_N_KEYS * PEER_N_KEYS
    gates = pl.BlockSpec((PEER_HEADS, PEER_N_KEYS, t), lambda i, e: (0, 0, i))
    per_key = pl.BlockSpec((EXPERT_BLOCK // PEER_N_KEYS, PEER_HEADS, t), lambda i, e: (e, 0, i))
    return pl.pallas_call(
        _peer_kernel,
        grid=(N_TOK // t, n_exp // EXPERT_BLOCK),
        in_specs=[pl.BlockSpec((D_MODEL, t), lambda i, e: (0, i)),
                  pl.BlockSpec((EXPERT_BLOCK, D_MODEL), lambda i, e: (e, 0)),
                  pl.BlockSpec((D_MODEL, EXPERT_BLOCK), lambda i, e: (0, e)),
                  gates, gates, per_key, per_key,
                  pl.BlockSpec((t, D_MODEL), lambda i, e: (i, 0)),
                  pl.BlockSpec((1, 1, 6 * D_MODEL), lambda i, e: (_peer_mod_row(i), 0, 0)),
                  pl.BlockSpec((1, D_MODEL), lambda i, e: (0, 0)),
                  pl.BlockSpec((1, D_MODEL), lambda i, e: (0, 0))],
        out_specs=pl.BlockSpec((t, D_MODEL), lambda i, e: (i, 0)),
        out_shape=jax.ShapeDtypeStruct((N_TOK, D_MODEL), F32),
        scratch_shapes=[pltpu.VMEM((D_MODEL, t), F32), pltpu.VMEM((EXPERT_BLOCK, t), F32),
                        pltpu.VMEM((EXPERT_BLOCK, t), BF16)],
        compiler_params=_cparams("parallel", "arbitrary"),
        name="peer_dense",
    )(h2t, u, vt, r2, e2, n1, e1, x1, mod, g, b)


def _rope_tables():
    t = jnp.arange(DEC_SEQ)
    row = (t // GRID_W).astype(F32)
    col = (t % GRID_W).astype(F32)

    def angles(rot_dim):
        n_freq = rot_dim // 4
        inv_freq = ROPE_THETA ** (-jnp.arange(n_freq, dtype=F32) / n_freq)
        return jnp.concatenate([row[:, None] * inv_freq, col[:, None] * inv_freq], axis=-1)

    def pack(cos_l, sa_l, sb_l):
        tab = jnp.stack([cos_l, sa_l, sb_l])
        ident = jnp.stack([jnp.ones((TM, LANES), F32), jnp.zeros((TM, LANES), F32), jnp.zeros((TM, LANES), F32)])
        return jnp.concatenate([tab, ident], axis=1)

    ang_b = angles(MLA_ROPE)
    cb, sb = jnp.cos(ang_b), jnp.sin(ang_b)
    one, zero = jnp.ones((DEC_SEQ, 64), F32), jnp.zeros((DEC_SEQ, 64), F32)
    z16, z32 = jnp.zeros((DEC_SEQ, 16), F32), jnp.zeros((DEC_SEQ, 32), F32)
    rope_b = pack(jnp.concatenate([one, cb, cb, jnp.ones((DEC_SEQ, 32), F32)], axis=1),
                  jnp.concatenate([zero, -sb, z16, z32], axis=1),
                  jnp.concatenate([zero, z16, sb, z32], axis=1))
    ang_d = angles(DIFF_QK_DIM)
    cd, sd = jnp.cos(ang_d), jnp.sin(ang_d)
    rope_d = pack(jnp.concatenate([cd, cd, cd, cd], axis=1),
                  jnp.concatenate([-sd, z32, -sd, z32], axis=1),
                  jnp.concatenate([z32, sd, z32, sd], axis=1))
    return rope_b, rope_d


def _na_bias_table(rpb):
    col = jnp.arange(GRID_W)
    dc = jnp.clip(col[None, :] - col[:, None], -(NA_WIN_COLS - 1), NA_WIN_COLS - 1) + NA_WIN_COLS - 1
    rpb_cols = rpb[:, :, dc]
    tabs = [rpb_cols[:, off:off + NA_WIN_ROWS].transpose(0, 2, 1, 3).reshape(NA_HEADS, GRID_W, NA_WIN_ROWS * GRID_W)
            for off in range(NA_WIN_ROWS)]
    return jnp.stack(tabs)


def _pad_cols(w, left, right):
    return jnp.pad(w, ((0, 0), (left, right)))


def kernel(x_prompt, x_sample, cache_mla_ckv, cache_mla_krope, cache_na_k, cache_na_v, cache_diff_k, cache_diff_v, c, c_ctx, w_mod, b_mod, w_in, sgu_norm_g, sgu_w, sgu_b, mla_q_norm_g, mla_w_uq, mla_kv_norm_g, mla_w_ukv, na_rpb, diff_lambda_q1, diff_lambda_k1, diff_lambda_q2, diff_lambda_k2, diff_norm_g, w_branch_a, w_branch_b, w_branch_c, w_branch_d, w_gate, b_gate, w_out, ln1_g, ln1_b, peer_w_q, peer_subkeys, peer_u, peer_v, ln2_g, ln2_b):
    x = jnp.concatenate([x_prompt.reshape(N_CTX, D_MODEL), x_sample.reshape(N_LAT, D_MODEL)], axis=0)
    cond = jnp.concatenate([c_ctx[None], c, jnp.zeros((N_COND - 1 - DEC_BATCH, D_MODEL), F32)], axis=0)
    mod_all = _mod_call(cond, w_mod, b_mod)
    rope_b, rope_d = _rope_tables()
    cache_kr_pad = jnp.pad(cache_mla_krope, ((0, 0), (0, 0), (0, 0), (MLA_NOPE, LANES - MLA_NOPE - MLA_ROPE)))
    cache_na_k2 = cache_na_k.reshape(DEC_BATCH, DEPTH, PAST_LEN, 256)
    cache_na_v2 = cache_na_v.reshape(DEC_BATCH, DEPTH, PAST_LEN, 256)
    cache_diff_k2 = cache_diff_k.reshape(DEC_BATCH, DEPTH, PAST_LEN, 512)
    cache_diff_v2 = cache_diff_v.reshape(DEC_BATCH, DEPTH, PAST_LEN, 512)

    ctx_out = []
    for l in range(DEPTH):
        lambda_init = 0.8 - 0.6 * math.exp(-0.3 * l)
        mod = mod_all[l].reshape(N_COND, 1, 6 * D_MODEL)

        wi = w_in[l]
        kr_cols = _pad_cols(wi[:, C_KR:C_KR + MLA_ROPE], MLA_NOPE, LANES - MLA_NOPE - MLA_ROPE)
        w_in_r = jnp.concatenate([wi[:, :C_KR], kr_cols, wi[:, C_KR + MLA_ROPE:]], axis=1).astype(BF16)
        wuq = mla_w_uq[l].reshape(MLA_Q_LORA, MLA_HEADS, MLA_NOPE + MLA_ROPE)
        wuq = jnp.pad(wuq, ((0, 0), (0, 0), (0, LANES - MLA_NOPE - MLA_ROPE))).reshape(MLA_Q_LORA, -1).astype(BF16)
        wukv = mla_w_ukv[l].reshape(MLA_KV_LORA, MLA_HEADS, MLA_NOPE + MLA_V)
        wuk = jnp.pad(wukv[:, :, :MLA_NOPE], ((0, 0), (0, 0), (0, LANES - MLA_NOPE))).reshape(MLA_KV_LORA, -1)
        wuk = wuk.astype(BF16)
        wuv = wukv[:, :, MLA_NOPE:].reshape(MLA_KV_LORA, -1).astype(BF16)
        sgu_bias = jnp.repeat(sgu_b[l].T, SGU_WIDTH // SGU_GROUPS, axis=1)
        lams = [p[l].reshape(1, DIFF_QK_DIM) for p in (diff_lambda_q1, diff_lambda_k1, diff_lambda_q2, diff_lambda_k2)]
        dg = diff_norm_g[l].reshape(1, DIFF_V_DIM)

        oa, mq, ckv, kr, nq, nk, nv, dq, dk, dv = _inproj_call(
            x, mod, rope_b, rope_d, w_in_r, sgu_norm_g[l].reshape(1, -1), sgu_w[l].astype(BF16), sgu_bias,
            mla_q_norm_g[l].reshape(1, -1), mla_kv_norm_g[l].reshape(1, -1), wuq)

        ob_c, oc_c, od_c = _ctx_attn_call(lambda_init, (mq, ckv, kr, nq, nk, nv, dq, dk, dv), wuk, wuv, lams, dg)
        ob_l = _lat_mla_call(l, mq, ckv, kr, cache_mla_ckv, cache_kr_pad, wuk, wuv)
        oc_l = _lat_na_call(l, nq, nk, nv, cache_na_k2, cache_na_v2, _na_bias_table(na_rpb[l]))
        od_l = _lat_diff_call(l, lambda_init, dq, dk, dv, cache_diff_k2, cache_diff_v2, lams, dg)
        ob = jnp.concatenate([ob_c, ob_l], axis=0)
        oc = jnp.concatenate([oc_c, oc_l], axis=0)
        od = jnp.concatenate([od_c, od_l], axis=0)

        x1, h2t = _merge_call(
            x, mod, oa, ob, oc, od, w_gate[l].astype(BF16), b_gate[l].reshape(1, -1),
            w_branch_a[l].astype(BF16), w_branch_b[l].astype(BF16), w_branch_c[l].astype(BF16),
            w_branch_d[l].astype(BF16), w_out[l].astype(BF16), ln1_g[l].reshape(1, -1), ln1_b[l].reshape(1, -1))

        keys = peer_subkeys[l].reshape(2 * PEER_HEADS, PEER_N_KEYS, PEER_KEY_DIM // 2).astype(BF16)
        r2, e2, n1, e1 = _router_call(h2t, peer_w_q[l].T.astype(BF16), keys)
        n1, e1 = n1.transpose(1, 0, 2), e1.transpose(1, 0, 2)
        x = _peer_call(h2t, peer_u[l].astype(BF16), peer_v[l].T.astype(BF16), r2, e2, n1, e1, x1, mod,
                       ln2_g[l].reshape(1, -1), ln2_b[l].reshape(1, -1))

        ctx_out.append((ckv[:N_CTX].reshape(BATCH, SEQ, MLA_KV_LORA),
                        kr[:N_CTX, MLA_NOPE:MLA_NOPE + MLA_ROPE].reshape(BATCH, SEQ, MLA_ROPE),
                        nk[:N_CTX].reshape(BATCH, SEQ, NA_HEADS, NA_HEAD_DIM),
                        nv[:N_CTX].reshape(BATCH, SEQ, NA_HEADS, NA_HEAD_DIM),
                        dk[:N_CTX].reshape(BATCH, SEQ, DIFF_HEADS, 2 * DIFF_QK_DIM),
                        dv[:N_CTX].reshape(BATCH, SEQ, DIFF_HEADS, DIFF_V_DIM)))

    y_prompt = x[:N_CTX].reshape(BATCH, SEQ, D_MODEL)
    y_sample = x[N_CTX:].reshape(DEC_BATCH, DEC_SEQ, D_MODEL)
    new = [jnp.stack([t[k] for t in ctx_out], axis=1) for k in range(6)]
    return (y_prompt, y_sample, *new)
```

```python
import functools
import math

import jax
import jax.numpy as jnp
from jax import lax
from jax.experimental import pallas as pl
from jax.experimental.pallas import tpu as pltpu

F32 = jnp.float32
BF16 = jnp.bfloat16

D_MODEL = 1024
BATCH = 32
SEQ = 256
DEPTH = 2
DEC_BATCH = 8
DEC_SEQ = 1024
PAST_LEN = 512
GRID_W = 64
CHUNK = 128
SGU_GROUPS = 4
SGU_WIDTH = 256
MLA_HEADS = 4
MLA_Q_LORA = 256
MLA_KV_LORA = 128
MLA_NOPE = 64
MLA_ROPE = 32
MLA_V = 64
NA_HEADS = 4
NA_HEAD_DIM = 64
NA_WIN_ROWS = 8
NA_WIN_COLS = 16
DIFF_HEADS = 4
DIFF_QK_DIM = 64
DIFF_V_DIM = 128
N_BRANCHES = 4
PEER_HEADS = 8
PEER_N_KEYS = 128
PEER_KEY_DIM = 256
PEER_TOPK = 16
ROPE_THETA = 10000.0
LN_EPS = 1e-6
NEG_BIG = -1e30
DEEPNORM_ALPHA = (2 * DEPTH) ** 0.25

LANES = 128
N_CTX = BATCH * SEQ
N_LAT = DEC_BATCH * DEC_SEQ
N_TOK = N_CTX + N_LAT
N_COND = 16
TM = 512
ROWS = DEC_SEQ // GRID_W
Q_TILE = 256
ROUTER_TILE = 256
PEER_TILE = 512
EXPERT_BLOCK = 1024
KEYS_PER_BLOCK = EXPERT_BLOCK // PEER_N_KEYS
GATE_LANES = 256
VMEM_LIMIT = 56 * 1024 * 1024

C_AU, C_AV, C_CQ, C_CKV, C_KR = 0, 256, 512, 768, 896
C_NQ, C_NK, C_NV, C_DQ, C_DK, C_DV, C_END = 1024, 1280, 1536, 1792, 2304, 2816, 3328


def _ln(x):
    mu = jnp.mean(x, axis=-1, keepdims=True)
    xc = x - mu
    var = jnp.mean(xc * xc, axis=-1, keepdims=True)
    return xc * lax.rsqrt(var + LN_EPS)


def _rms(x):
    return x * lax.rsqrt(jnp.mean(x * x, axis=-1, keepdims=True) + LN_EPS)


def _gelu(x):
    return 0.5 * x * (1.0 + lax.erf(x * (1.0 / math.sqrt(2.0))))


def _dot(a, b):
    return jnp.dot(a, b, preferred_element_type=F32)


def _dot_nt(a, b):
    return lax.dot_general(a, b, (((1,), (1,)), ((), ())), preferred_element_type=F32)


def _rope(x, tab_ref, half):
    return (x * tab_ref[0] + pltpu.roll(x, LANES - half, 1) * tab_ref[1] + pltpu.roll(x, half, 1) * tab_ref[2])


def _cparams(*sem):
    return pltpu.CompilerParams(dimension_semantics=sem, vmem_limit_bytes=VMEM_LIMIT)


def _mod_kernel(cond_ref, w_ref, b_ref, o_ref):
    c = cond_ref[...]
    s = c * jax.nn.sigmoid(c)
    o_ref[...] = _dot(s, w_ref[...]) + b_ref[...]


def _mod_call(cond, w_mod, b_mod):
    nb = 1536
    return pl.pallas_call(
        _mod_kernel,
        grid=(DEPTH, 6 * D_MODEL // nb),
        in_specs=[pl.BlockSpec((N_COND, D_MODEL), lambda l, j: (0, 0)),
                  pl.BlockSpec((None, D_MODEL, nb), lambda l, j: (l, 0, j)),
                  pl.BlockSpec((None, 1, nb), lambda l, j: (l, 0, j))],
        out_specs=pl.BlockSpec((None, N_COND, nb), lambda l, j: (l, 0, j)),
        out_shape=jax.ShapeDtypeStruct((DEPTH, N_COND, 6 * D_MODEL), F32),
        compiler_params=_cparams("arbitrary", "arbitrary"),
        name="mod_vectors",
    )(cond, w_mod, b_mod.reshape(DEPTH, 1, 6 * D_MODEL))


def _mod_row(i):
    n_ctx_tiles = N_CTX // TM
    return jnp.where(i < n_ctx_tiles, 0, 1 + (i - n_ctx_tiles) // (DEC_SEQ // TM))


def _pos_block(i):
    n_ctx_tiles = N_CTX // TM
    return jnp.where(i < n_ctx_tiles, DEC_SEQ // TM, (i - n_ctx_tiles) % (DEC_SEQ // TM))


def _inproj_kernel(x_ref, mod_ref, rb_ref, rd_ref, w_in_ref, sgu_g_ref, sgu_w_ref, sgu_bias_ref,
                   qg_ref, kvg_ref, wuq_ref,
                   oa_ref, mq_ref, ckv_ref, kr_ref, nq_ref, nk_ref, nv_ref, dq_ref, dk_ref, dv_ref):
    x = x_ref[...]
    shift = mod_ref[0, :, 0:D_MODEL]
    scale = mod_ref[0, :, D_MODEL:2 * D_MODEL]
    h = (_ln(x) * (1.0 + scale) + shift).astype(BF16)

    ya = _dot(h, w_in_ref[:, C_AU:C_CQ])
    u = _gelu(ya[:, :SGU_WIDTH])
    v = _gelu(ya[:, SGU_WIDTH:])
    vn = (_ln(v) * sgu_g_ref[...]).astype(BF16)
    group = lax.broadcasted_iota(jnp.int32, (CHUNK, SGU_WIDTH), 1) // (SGU_WIDTH // SGU_GROUPS)
    for c in range(TM // CHUNK):
        rows = slice(c * CHUNK, (c + 1) * CHUNK)
        mixed = sgu_bias_ref[...]
        for g in range(SGU_GROUPS):
            mixed = mixed + jnp.where(group == g, _dot(sgu_w_ref[g], vn[rows]), 0.0)
        oa_ref[rows, :] = (u[rows] * mixed).astype(oa_ref.dtype)

    ym = _dot(h, w_in_ref[:, C_CQ:C_NQ])
    cq = (_rms(ym[:, :MLA_Q_LORA]) * qg_ref[...]).astype(BF16)
    mq = _dot(cq, wuq_ref[...])
    for g in range(MLA_HEADS):
        lanes = slice(g * LANES, (g + 1) * LANES)
        mq_ref[:, lanes] = _rope(mq[:, lanes], rb_ref, MLA_ROPE // 2)
    ckv_ref[...] = _rms(ym[:, MLA_Q_LORA:MLA_Q_LORA + MLA_KV_LORA]) * kvg_ref[...]
    kr_ref[...] = _rope(ym[:, MLA_Q_LORA + MLA_KV_LORA:], rb_ref, MLA_ROPE // 2)

    yn = _dot(h, w_in_ref[:, C_NQ:C_DQ])
    nq_ref[...] = yn[:, 0:256]
    nk_ref[...] = yn[:, 256:512]
    nv_ref[...] = yn[:, 512:768]

    yd = _dot(h, w_in_ref[:, C_DQ:C_END])
    for g in range(4):
        lanes = slice(g * LANES, (g + 1) * LANES)
        dq_ref[:, lanes] = _rope(yd[:, g * LANES:(g + 1) * LANES], rd_ref, DIFF_QK_DIM // 2)
        dk_ref[:, lanes] = _rope(yd[:, 512 + g * LANES:512 + (g + 1) * LANES], rd_ref, DIFF_QK_DIM // 2)
    dv_ref[...] = yd[:, 1024:1536]


def _inproj_call(x, mod, rope_b, rope_d, w_in_r, sgu_g, sgu_w, sgu_bias, qg, kvg, wuq):
    tile = lambda w: pl.BlockSpec((TM, w), lambda i: (i, 0))
    full = lambda *s: pl.BlockSpec(s, lambda i: (0,) * len(s))
    widths = (SGU_WIDTH, 512, MLA_KV_LORA, LANES, 256, 256, 256, 512, 512, 512)
    dtypes = (BF16,) + (F32,) * 9
    return pl.pallas_call(
        _inproj_kernel,
        grid=(N_TOK // TM,),
        in_specs=[tile(D_MODEL),
                  pl.BlockSpec((1, 1, 6 * D_MODEL), lambda i: (_mod_row(i), 0, 0)),
                  pl.BlockSpec((3, TM, LANES), lambda i: (0, _pos_block(i), 0)),
                  pl.BlockSpec((3, TM, LANES), lambda i: (0, _pos_block(i), 0)),
                  full(D_MODEL, C_END), full(1, SGU_WIDTH), full(SGU_GROUPS, CHUNK, CHUNK),
                  full(CHUNK, SGU_WIDTH), full(1, MLA_Q_LORA), full(1, MLA_KV_LORA),
                  full(MLA_Q_LORA, MLA_HEADS * LANES)],
        out_specs=[tile(w) for w in widths],
        out_shape=[jax.ShapeDtypeStruct((N_TOK, w), dt) for w, dt in zip(widths, dtypes)],
        compiler_params=_cparams("parallel"),
        name="in_projection",
    )(x, mod, rope_b, rope_d, w_in_r, sgu_g, sgu_w, sgu_bias, qg, kvg, wuq)


def _half_mask(lo):
    lane = lax.broadcasted_iota(jnp.int32, (1, LANES), 1)
    return (lane >= lo) & (lane < lo + 64)


def _softmax_pv(scores, values, lanes):
    m = scores[0].max(axis=-1, keepdims=True)
    for s in scores[1:]:
        m = jnp.maximum(m, s.max(axis=-1, keepdims=True))
    den = None
    o = None
    for s, v in zip(scores, values):
        p = jnp.exp(s - m)
        d = p.sum(axis=-1, keepdims=True)
        den = d if den is None else den + d
        pv = _dot(p.astype(BF16), v[:, lanes])
        o = pv if o is None else o + pv
    return o / den


def _pair_attention(q, keys, vals, scale, bias_fn=None):
    outs = []
    for pair in range(2):
        lanes = slice(pair * LANES, (pair + 1) * LANES)
        qp = q[:, lanes]
        acc = None
        for sub in range(2):
            head = 2 * pair + sub
            mask = _half_mask(64 * sub)
            qm = jnp.where(mask, qp, jnp.zeros_like(qp))
            scores = [_dot_nt(qm, k[:, lanes]) * scale for k in keys]
            if bias_fn is not None:
                scores = bias_fn(head, scores)
            o = jnp.where(mask, _softmax_pv(scores, vals, lanes), 0.0)
            acc = o if acc is None else acc + o
        outs.append(acc)
    return outs


def _mla_attention(q, k_blocks, v_blocks, o_ref, rows):
    scale = (MLA_NOPE + MLA_ROPE) ** -0.5
    for pair in range(2):
        lanes = slice(pair * LANES, (pair + 1) * LANES)
        acc = None
        for sub in range(2):
            head = 2 * pair + sub
            hl = slice(head * LANES, (head + 1) * LANES)
            scores = [_dot_nt(q[:, hl], k[:, hl]) * scale for k in k_blocks]
            o = jnp.where(_half_mask(64 * sub), _softmax_pv(scores, v_blocks, lanes), 0.0)
            acc = o if acc is None else acc + o
        o_ref[rows, lanes] = acc.astype(o_ref.dtype)


def _diff_lambda(lq1, lk1, lq2, lk2, lambda_init):
    a = jnp.sum(lq1[...] * lk1[...], axis=-1, keepdims=True)
    b = jnp.sum(lq2[...] * lk2[...], axis=-1, keepdims=True)
    return jnp.exp(a) - jnp.exp(b) + lambda_init


def _diff_attention(q, k_blocks, v_blocks, lam, norm_g, lambda_init, o_ref, rows):
    scale = DIFF_QK_DIM ** -0.5
    for head in range(DIFF_HEADS):
        hl = slice(head * LANES, (head + 1) * LANES)
        qh = q[:, hl]
        probs = []
        for sub in range(2):
            qm = jnp.where(_half_mask(64 * sub), qh, jnp.zeros_like(qh))
            scores = [_dot_nt(qm, k[:, hl]) * scale for k in k_blocks]
            m = scores[0].max(axis=-1, keepdims=True)
            for s in scores[1:]:
                m = jnp.maximum(m, s.max(axis=-1, keepdims=True))
            ps = [jnp.exp(s - m) for s in scores]
            den = ps[0].sum(axis=-1, keepdims=True)
            for p in ps[1:]:
                den = den + p.sum(axis=-1, keepdims=True)
            probs.append((ps, 1.0 / den))
        o = None
        for i, v in enumerate(v_blocks):
            w = probs[0][0][i] * probs[0][1] - probs[1][0][i] * (lam * probs[1][1])
            pv = _dot(w.astype(BF16), v[:, hl])
            o = pv if o is None else o + pv
        o = _rms(o) * norm_g * (1.0 - lambda_init)
        o_ref[rows, hl] = o.astype(o_ref.dtype)


def _ctx_attn_kernel(lambda_init, mq_ref, ckv_ref, kr_ref, nq_ref, nk_ref, nv_ref, dq_ref, dk_ref, dv_ref,
                     wuk_ref, wuv_ref, lq1, lk1, lq2, lk2, dg_ref, ob_ref, oc_ref, od_ref):
    rows = slice(0, SEQ)
    ckv = ckv_ref[...].astype(BF16)
    kr = kr_ref[...]
    k_b = (_dot(ckv, wuk_ref[...]) + jnp.concatenate([kr] * MLA_HEADS, axis=1)).astype(BF16)
    v_b = _dot(ckv, wuv_ref[...]).astype(BF16)
    _mla_attention(mq_ref[...].astype(BF16), [k_b], [v_b], ob_ref, rows)

    outs = _pair_attention(nq_ref[...].astype(BF16), [nk_ref[...].astype(BF16)], [nv_ref[...].astype(BF16)],
                           NA_HEAD_DIM ** -0.5)
    for pair in range(2):
        oc_ref[:, pair * LANES:(pair + 1) * LANES] = outs[pair].astype(oc_ref.dtype)

    lam = _diff_lambda(lq1, lk1, lq2, lk2, lambda_init)
    _diff_attention(dq_ref[...].astype(BF16), [dk_ref[...].astype(BF16)], [dv_ref[...].astype(BF16)],
                    lam, dg_ref[...], lambda_init, od_ref, rows)


def _ctx_attn_call(lambda_init, acts, wuk, wuv, lams, dg):
    mq, ckv, kr, nq, nk, nv, dq, dk, dv = acts
    seq = lambda w: pl.BlockSpec((SEQ, w), lambda b: (b, 0))
    full = lambda *s: pl.BlockSpec(s, lambda b: (0,) * len(s))
    return pl.pallas_call(
        functools.partial(_ctx_attn_kernel, lambda_init),
        grid=(BATCH,),
        in_specs=[seq(512), seq(128), seq(128), seq(256), seq(256), seq(256), seq(512), seq(512), seq(512),
                  full(MLA_KV_LORA, 512), full(MLA_KV_LORA, 256)] + [full(1, DIFF_QK_DIM)] * 4
                 + [full(1, DIFF_V_DIM)],
        out_specs=[seq(256), seq(256), seq(512)],
        out_shape=[jax.ShapeDtypeStruct((N_CTX, w), BF16) for w in (256, 256, 512)],
        compiler_params=_cparams("parallel"),
        name="context_attention",
    )(mq, ckv, kr, nq, nk, nv, dq, dk, dv, wuk, wuv, *lams, dg)


def _lat_mla_kernel(mq_ref, ckv_ref, kr_ref, cckv_ref, ckr_ref, wuk_ref, wuv_ref, o_ref):
    def expand(ckv_f32, kr):
        ckv = ckv_f32.astype(BF16)
        k = (_dot(ckv, wuk_ref[...]) + jnp.concatenate([kr] * MLA_HEADS, axis=1)).astype(BF16)
        return k, _dot(ckv, wuv_ref[...]).astype(BF16)

    k_lat, v_lat = expand(ckv_ref[...], kr_ref[...])
    k_ctx, v_ctx = expand(cckv_ref[...], ckr_ref[...])
    for t in range(DEC_SEQ // Q_TILE):
        rows = slice(t * Q_TILE, (t + 1) * Q_TILE)
        _mla_attention(mq_ref[rows, :].astype(BF16), [k_lat, k_ctx], [v_lat, v_ctx], o_ref, rows)


def _lat_mla_call(l, mq, ckv, kr, cache_ckv, cache_kr_pad, wuk, wuv):
    off = N_CTX // DEC_SEQ
    seq = lambda w: pl.BlockSpec((DEC_SEQ, w), lambda b: (off + b, 0))
    cache = lambda w: pl.BlockSpec((None, None, PAST_LEN, w), lambda b: (b, l, 0, 0))
    full = lambda *s: pl.BlockSpec(s, lambda b: (0,) * len(s))
    return pl.pallas_call(
        _lat_mla_kernel,
        grid=(DEC_BATCH,),
        in_specs=[seq(512), seq(128), seq(128), cache(MLA_KV_LORA), cache(LANES),
                  full(MLA_KV_LORA, 512), full(MLA_KV_LORA, 256)],
        out_specs=pl.BlockSpec((DEC_SEQ, 256), lambda b: (b, 0)),
        out_shape=jax.ShapeDtypeStruct((N_LAT, 256), BF16),
        compiler_params=_cparams("parallel"),
        name="latent_mla_attention",
    )(mq, ckv, kr, cache_ckv, cache_kr_pad, wuk, wuv)


def _win_start(r):
    return jnp.clip(r - NA_WIN_ROWS // 2, 0, ROWS - NA_WIN_ROWS)


def _lat_na_kernel(nq_ref, nk_ref, nv_ref, ck_ref, cv_ref, bias_ref, o_ref):
    r = pl.program_id(1)
    start = pl.multiple_of(_win_start(r) * GRID_W, GRID_W)
    win = NA_WIN_ROWS * GRID_W
    k_w = nk_ref[pl.ds(start, win), :].astype(BF16)
    v_w = nv_ref[pl.ds(start, win), :].astype(BF16)
    k_c = ck_ref[...].astype(BF16)
    v_c = cv_ref[...].astype(BF16)
    q_col = lax.broadcasted_iota(jnp.int32, (GRID_W, win), 0)
    k_col = lax.broadcasted_iota(jnp.int32, (GRID_W, win), 1) % GRID_W
    c0 = jnp.clip(q_col - NA_WIN_COLS // 2, 0, GRID_W - NA_WIN_COLS)
    col_in = (k_col >= c0) & (k_col < c0 + NA_WIN_COLS)

    def bias_fn(head, scores):
        return [jnp.where(col_in, scores[0] + bias_ref[0, head], NEG_BIG), scores[1]]

    outs = _pair_attention(nq_ref[...].astype(BF16), [k_w, k_c], [v_w, v_c], NA_HEAD_DIM ** -0.5, bias_fn)
    for pair in range(2):
        o_ref[:, pair * LANES:(pair + 1) * LANES] = outs[pair].astype(o_ref.dtype)


def _lat_na_call(l, nq, nk, nv, cache_k, cache_v, bias_tab):
    off = N_CTX // DEC_SEQ
    seq = pl.BlockSpec((DEC_SEQ, 256), lambda b, r: (off + b, 0))
    cache = pl.BlockSpec((None, None, PAST_LEN, 256), lambda b, r: (b, l, 0, 0))
    return pl.pallas_call(
        _lat_na_kernel,
        grid=(DEC_BATCH, ROWS),
        in_specs=[pl.BlockSpec((GRID_W, 256), lambda b, r: (N_CTX // GRID_W + b * ROWS + r, 0)),
                  seq, seq, cache, cache,
                  pl.BlockSpec((1, NA_HEADS, GRID_W, NA_WIN_ROWS * GRID_W),
                               lambda b, r: (_win_start(r) - r + NA_WIN_ROWS - 1, 0, 0, 0))],
        out_specs=pl.BlockSpec((GRID_W, 256), lambda b, r: (b * ROWS + r, 0)),
        out_shape=jax.ShapeDtypeStruct((N_LAT, 256), BF16),
        compiler_params=_cparams("parallel", "arbitrary"),
        name="latent_neighbourhood_attention",
    )(nq, nk, nv, cache_k, cache_v, bias_tab)


def _lat_diff_kernel(lambda_init, dq_ref, dk_ref, dv_ref, ck_ref, cv_ref, lq1, lk1, lq2, lk2, dg_ref, o_ref):
    lam = _diff_lambda(lq1, lk1, lq2, lk2, lambda_init)
    k_blocks = [dk_ref[...].astype(BF16), ck_ref[...].astype(BF16)]
    v_blocks = [dv_ref[...].astype(BF16), cv_ref[...].astype(BF16)]
    for t in range(DEC_SEQ // Q_TILE):
        rows = slice(t * Q_TILE, (t + 1) * Q_TILE)
        _diff_attention(dq_ref[rows, :].astype(BF16), k_blocks, v_blocks, lam, dg_ref[...], lambda_init,
                        o_ref, rows)


def _lat_diff_call(l, lambda_init, dq, dk, dv, cache_k, cache_v, lams, dg):
    off = N_CTX // DEC_SEQ
    seq = pl.BlockSpec((DEC_SEQ, 512), lambda b: (off + b, 0))
    cache = pl.BlockSpec((None, None, PAST_LEN, 512), lambda b: (b, l, 0, 0))
    full = lambda *s: pl.BlockSpec(s, lambda b: (0,) * len(s))
    return pl.pallas_call(
        functools.partial(_lat_diff_kernel, lambda_init),
        grid=(DEC_BATCH,),
        in_specs=[seq, seq, seq, cache, cache] + [full(1, DIFF_QK_DIM)] * 4 + [full(1, DIFF_V_DIM)],
        out_specs=pl.BlockSpec((DEC_SEQ, 512), lambda b: (b, 0)),
        out_shape=jax.ShapeDtypeStruct((N_LAT, 512), BF16),
        compiler_params=_cparams("parallel"),
        name="latent_differential_attention",
    )(dq, dk, dv, cache_k, cache_v, *lams, dg)


def _merge_kernel(x_ref, mod_ref, oa_ref, ob_ref, oc_ref, od_ref, wg_ref, bg_ref,
                  wa_ref, wb_ref, wc_ref, wd_ref, wo_ref, g_ref, b_ref, x1_ref, h2t_ref):
    x = x_ref[...]
    mod = lambda k: mod_ref[0, :, k * D_MODEL:(k + 1) * D_MODEL]
    h = (_ln(x) * (1.0 + mod(1)) + mod(0)).astype(BF16)
    merged = None
    for i, (o_ref, w_ref) in enumerate(((oa_ref, wa_ref), (ob_ref, wb_ref), (oc_ref, wc_ref), (od_ref, wd_ref))):
        cols = slice(i * D_MODEL, (i + 1) * D_MODEL)
        gate = jax.nn.sigmoid(_dot(h, wg_ref[:, cols]) + bg_ref[:, cols])
        term = gate * _dot(o_ref[...], w_ref[...])
        merged = term if merged is None else merged + term
    mix = _dot(merged.astype(BF16), wo_ref[...])
    x1 = _ln(DEEPNORM_ALPHA * x + mod(2) * mix) * g_ref[...] + b_ref[...]
    x1_ref[...] = x1
    h2 = _ln(x1) * (1.0 + mod(4)) + mod(3)
    h2t_ref[...] = h2.T.astype(BF16)


def _merge_call(x, mod, oa, ob, oc, od, wg, bg, wa, wb, wc, wd, wo, g, b):
    tile = lambda w: pl.BlockSpec((TM, w), lambda i: (i, 0))
    full = lambda *s: pl.BlockSpec(s, lambda i: (0,) * len(s))
    return pl.pallas_call(
        _merge_kernel,
        grid=(N_TOK // TM,),
        in_specs=[tile(D_MODEL), pl.BlockSpec((1, 1, 6 * D_MODEL), lambda i: (_mod_row(i), 0, 0)),
                  tile(256), tile(256), tile(256), tile(512),
                  full(D_MODEL, 4 * D_MODEL), full(1, 4 * D_MODEL),
                  full(256, D_MODEL), full(256, D_MODEL), full(256, D_MODEL), full(512, D_MODEL),
                  full(D_MODEL, D_MODEL), full(1, D_MODEL), full(1, D_MODEL)],
        out_specs=[tile(D_MODEL), pl.BlockSpec((D_MODEL, TM), lambda i: (0, i))],
        out_shape=[jax.ShapeDtypeStruct((N_TOK, D_MODEL), F32), jax.ShapeDtypeStruct((D_MODEL, N_TOK), BF16)],
        compiler_params=_cparams("parallel"),
        name="branch_merge",
    )(x, mod, oa, ob, oc, od, wg, bg, wa, wb, wc, wd, wo, g, b)


def _top16(s):
    row = lax.broadcasted_iota(jnp.int32, s.shape, 0).astype(F32)
    krow = lax.broadcasted_iota(jnp.int32, (PEER_TOPK, s.shape[1]), 0)

    def body(k, carry):
        work, rank, vals = carry
        m = jnp.max(work, axis=0, keepdims=True)
        idx = jnp.min(jnp.where(work == m, row, float(PEER_N_KEYS)), axis=0, keepdims=True)
        sel = row == idx
        rank = jnp.where(sel, k.astype(F32), rank)
        work = jnp.where(sel, -jnp.inf, work)
        vals = jnp.where(krow == k, m, vals)
        return work, rank, vals

    init = (s, jnp.full(s.shape, float(PEER_N_KEYS), F32), jnp.zeros((PEER_TOPK, s.shape[1]), F32))
    _, rank, vals = lax.fori_loop(0, PEER_TOPK, body, init)
    return vals, rank


def _row_gather(table, idx):
    out = jnp.zeros(idx.shape, F32)
    for k in range(PEER_TOPK):
        out = jnp.where(idx == float(k), table[k:k + 1, :], out)
    return out


def _route_columns(s1, s2):
    krow = lax.broadcasted_iota(jnp.int32, (PEER_TOPK, LANES), 0).astype(F32)
    hs1, rank1 = _top16(s1)
    hs2, rank2 = _top16(s2)

    def merge_body(_, carry):
        cnt, front = carry
        m = jnp.max(front, axis=0, keepdims=True)
        win = jnp.min(jnp.where(front == m, krow, float(PEER_TOPK)), axis=0, keepdims=True)
        sel = krow == win
        cnt = jnp.where(sel, cnt + 1.0, cnt)
        nxt = jnp.where(cnt < float(PEER_TOPK), hs1 + _row_gather(hs2, cnt), -jnp.inf)
        return cnt, jnp.where(sel, nxt, front)

    cnt, _ = lax.fori_loop(0, PEER_TOPK, merge_body, (jnp.zeros((PEER_TOPK, LANES), F32), hs1 + hs2[0:1, :]))

    e1r = jnp.exp(hs1 - hs1[0:1, :])
    e2r = jnp.exp(hs2 - hs2[0:1, :])
    prefix = jnp.zeros((PEER_TOPK, LANES), F32)
    for kb in range(PEER_TOPK):
        prefix = prefix + jnp.where(cnt > float(kb), e2r[kb:kb + 1, :], 0.0)
    z = jnp.sum(e1r * prefix, axis=0, keepdims=True)
    return (rank2, jnp.exp(s2 - hs2[0:1, :]) / z, _row_gather(cnt, rank1), 0.5 * jnp.exp(s1 - hs1[0:1, :]))


def _router_kernel(h2t_ref, wqt_ref, keys_ref, r2_ref, e2_ref, n1_ref, e1_ref, q_scr, s_scr):
    q_scr[...] = _dot(wqt_ref[...], h2t_ref[...]).astype(BF16)

    def head_body(hd, _):
        base = pl.multiple_of(hd * PEER_KEY_DIM, PEER_KEY_DIM)
        s_scr[0] = _dot(keys_ref[2 * hd], q_scr[pl.ds(base, LANES), :])
        s_scr[1] = _dot(keys_ref[2 * hd + 1], q_scr[pl.ds(base + LANES, LANES), :])
        for j in range(ROUTER_TILE // LANES):
            lanes = slice(j * LANES, (j + 1) * LANES)
            r2, e2, n1, e1 = _route_columns(s_scr[0, :, lanes], s_scr[1, :, lanes])
            r2_ref[hd, :, lanes] = r2.astype(BF16)
            e2_ref[hd, :, lanes] = e2.astype(BF16)
            n1_ref[hd, :, lanes] = n1
            e1_ref[hd, :, lanes] = e1
        return 0

    lax.fori_loop(0, PEER_HEADS, head_body, 0)


def _router_call(h2t, wqt, keys):
    t = ROUTER_TILE
    out = pl.BlockSpec((PEER_HEADS, PEER_N_KEYS, t), lambda i: (0, 0, i))
    shape = (PEER_HEADS, PEER_N_KEYS, N_TOK)
    return pl.pallas_call(
        _router_kernel,
        grid=(N_TOK // t,),
        in_specs=[pl.BlockSpec((D_MODEL, t), lambda i: (0, i)),
                  pl.BlockSpec((PEER_HEADS * PEER_KEY_DIM, D_MODEL), lambda i: (0, 0)),
                  pl.BlockSpec((2 * PEER_HEADS, PEER_N_KEYS, PEER_KEY_DIM // 2), lambda i: (0, 0, 0))],
        out_specs=[out] * 4,
        out_shape=[jax.ShapeDtypeStruct(shape, BF16), jax.ShapeDtypeStruct(shape, BF16),
                   jax.ShapeDtypeStruct(shape, F32), jax.ShapeDtypeStruct(shape, F32)],
        scratch_shapes=[pltpu.VMEM((PEER_HEADS * PEER_KEY_DIM, t), BF16),
                        pltpu.VMEM((2, PEER_N_KEYS, t), F32)],
        compiler_params=_cparams("parallel"),
        name="peer_retrieval",
    )(h2t, wqt, keys)


def _gated_activations(ht_ref, w_ref, r2_ref, e2_ref, n1_ref, e1_ref):
    for i in range(KEYS_PER_BLOCK):
        rows = slice(i * PEER_N_KEYS, (i + 1) * PEER_N_KEYS)
        for j in range(PEER_TILE // GATE_LANES):
            lanes = slice(j * GATE_LANES, (j + 1) * GATE_LANES)
            gate = jnp.zeros((PEER_N_KEYS, GATE_LANES), BF16)
            for hd in range(PEER_HEADS):
                n_row = n1_ref[hd, i:i + 1, lanes].astype(BF16)
                c_row = e1_ref[hd, i:i + 1, lanes].astype(BF16)
                live = jnp.where(r2_ref[hd, :, lanes] < n_row, e2_ref[hd, :, lanes], jnp.zeros((), BF16))
                gate = gate + live * c_row
            x = ht_ref[rows, lanes]
            act = x * (1.0 + lax.erf(x * (1.0 / math.sqrt(2.0))))
            w_ref[rows, lanes] = act.astype(BF16) * gate


def _peer_kernel(h2t_ref, u_ref, vt_ref, r2_ref, e2_ref, n1_ref, e1_ref,
                 x1_ref, mod_ref, g_ref, b_ref, o_ref, acc_ref, ht_a, ht_b, w_a, w_b):
    g = pl.program_id(1)

    @pl.when(g == 0)
    def _():
        acc_ref[...] = jnp.zeros_like(acc_ref)
        ht_b[...] = jnp.zeros_like(ht_b)
        w_b[...] = jnp.zeros_like(w_b)

    def stage(ht_write, ht_read, w_write, w_read):
        ht_write[...] = _dot(u_ref[...], h2t_ref[...])
        _gated_activations(ht_read, w_write, r2_ref, e2_ref, n1_ref, e1_ref)
        acc_ref[...] += _dot(vt_ref[...], w_read[...])

    pl.when(g % 2 == 0)(functools.partial(stage, ht_a, ht_b, w_a, w_b))
    pl.when(g % 2 == 1)(functools.partial(stage, ht_b, ht_a, w_b, w_a))

    @pl.when(g == pl.num_programs(1) - 1)
    def _():
        ffn = acc_ref[...].T
        g2 = mod_ref[0, :, 5 * D_MODEL:6 * D_MODEL]
        o_ref[...] = _ln(DEEPNORM_ALPHA * x1_ref[...] + g2 * ffn) * g_ref[...] + b_ref[...]


def _peer_mod_row(i):
    n_ctx_tiles = N_CTX // PEER_TILE
    return jnp.where(i < n_ctx_tiles, 0, 1 + (i - n_ctx_tiles) // (DEC_SEQ // PEER_TILE))


def _peer_call(h2t, u, vt, r2, e2, n1, e1, x1, mod, g, b):
    t = PEER_TILE
    n_blocks = PEER_N_KEYS * PEER_N_KEYS // EXPERT_BLOCK
    gates = pl.BlockSpec((PEER_HEADS, PEER_N_KEYS, t), lambda i, g: (0, 0, i))
    keys = pl.BlockSpec((PEER_HEADS, KEYS_PER_BLOCK, t), lambda i, g: (0, jnp.clip(g - 1, 0, n_blocks - 1), i))
    return pl.pallas_call(
        _peer_kernel,
        grid=(N_TOK // t, n_blocks + 2),
        in_specs=[pl.BlockSpec((D_MODEL, t), lambda i, g: (0, i)),
                  pl.BlockSpec((EXPERT_BLOCK, D_MODEL), lambda i, g: (jnp.minimum(g, n_blocks - 1), 0)),
                  pl.BlockSpec((D_MODEL, EXPERT_BLOCK), lambda i, g: (0, jnp.clip(g - 2, 0, n_blocks - 1))),
                  gates, gates, keys, keys,
                  pl.BlockSpec((t, D_MODEL), lambda i, g: (i, 0)),
                  pl.BlockSpec((1, 1, 6 * D_MODEL), lambda i, g: (_peer_mod_row(i), 0, 0)),
                  pl.BlockSpec((1, D_MODEL), lambda i, g: (0, 0)),
                  pl.BlockSpec((1, D_MODEL), lambda i, g: (0, 0))],
        out_specs=pl.BlockSpec((t, D_MODEL), lambda i, g: (i, 0)),
        out_shape=jax.ShapeDtypeStruct((N_TOK, D_MODEL), F32),
        scratch_shapes=[pltpu.VMEM((D_MODEL, t), F32),
                        pltpu.VMEM((EXPERT_BLOCK, t), F32), pltpu.VMEM((EXPERT_BLOCK, t), F32),
                        pltpu.VMEM((EXPERT_BLOCK, t), BF16), pltpu.VMEM((EXPERT_BLOCK, t), BF16)],
        compiler_params=_cparams("parallel", "arbitrary"),
        name="peer_dense",
    )(h2t, u, vt, r2, e2, n1, e1, x1, mod, g, b)


def _rope_tables():
    t = jnp.arange(DEC_SEQ)
    row = (t // GRID_W).astype(F32)
    col = (t % GRID_W).astype(F32)

    def angles(rot_dim):
        n_freq = rot_dim // 4
        inv_freq = ROPE_THETA ** (-jnp.arange(n_freq, dtype=F32) / n_freq)
        return jnp.concatenate([row[:, None] * inv_freq, col[:, None] * inv_freq], axis=-1)

    def pack(cos_l, sa_l, sb_l):
        tab = jnp.stack([cos_l, sa_l, sb_l])
        ident = jnp.stack([jnp.ones((TM, LANES), F32), jnp.zeros((TM, LANES), F32), jnp.zeros((TM, LANES), F32)])
        return jnp.concatenate([tab, ident], axis=1)

    ang_b = angles(MLA_ROPE)
    cb, sb = jnp.cos(ang_b), jnp.sin(ang_b)
    one, zero = jnp.ones((DEC_SEQ, 64), F32), jnp.zeros((DEC_SEQ, 64), F32)
    z16, z32 = jnp.zeros((DEC_SEQ, 16), F32), jnp.zeros((DEC_SEQ, 32), F32)
    rope_b = pack(jnp.concatenate([one, cb, cb, jnp.ones((DEC_SEQ, 32), F32)], axis=1),
                  jnp.concatenate([zero, -sb, z16, z32], axis=1),
                  jnp.concatenate([zero, z16, sb, z32], axis=1))
    ang_d = angles(DIFF_QK_DIM)
    cd, sd = jnp.cos(ang_d), jnp.sin(ang_d)
    rope_d = pack(jnp.concatenate([cd, cd, cd, cd], axis=1),
                  jnp.concatenate([-sd, z32, -sd, z32], axis=1),
                  jnp.concatenate([z32, sd, z32, sd], axis=1))
    return rope_b, rope_d


def _na_bias_table(rpb):
    col = jnp.arange(GRID_W)
    dc = jnp.clip(col[None, :] - col[:, None], -(NA_WIN_COLS - 1), NA_WIN_COLS - 1) + NA_WIN_COLS - 1
    rpb_cols = rpb[:, :, dc]
    tabs = [rpb_cols[:, off:off + NA_WIN_ROWS].transpose(0, 2, 1, 3).reshape(NA_HEADS, GRID_W, NA_WIN_ROWS * GRID_W)
            for off in range(NA_WIN_ROWS)]
    return jnp.stack(tabs)


def _pad_cols(w, left, right):
    return jnp.pad(w, ((0, 0), (left, right)))


def kernel(x_prompt, x_sample, cache_mla_ckv, cache_mla_krope, cache_na_k, cache_na_v, cache_diff_k, cache_diff_v, c, c_ctx, w_mod, b_mod, w_in, sgu_norm_g, sgu_w, sgu_b, mla_q_norm_g, mla_w_uq, mla_kv_norm_g, mla_w_ukv, na_rpb, diff_lambda_q1, diff_lambda_k1, diff_lambda_q2, diff_lambda_k2, diff_norm_g, w_branch_a, w_branch_b, w_branch_c, w_branch_d, w_gate, b_gate, w_out, ln1_g, ln1_b, peer_w_q, peer_subkeys, peer_u, peer_v, ln2_g, ln2_b):
    x = jnp.concatenate([x_prompt.reshape(N_CTX, D_MODEL), x_sample.reshape(N_LAT, D_MODEL)], axis=0)
    cond = jnp.concatenate([c_ctx[None], c, jnp.zeros((N_COND - 1 - DEC_BATCH, D_MODEL), F32)], axis=0)
    mod_all = _mod_call(cond, w_mod, b_mod)
    rope_b, rope_d = _rope_tables()
    cache_kr_pad = jnp.pad(cache_mla_krope, ((0, 0), (0, 0), (0, 0), (MLA_NOPE, LANES - MLA_NOPE - MLA_ROPE)))
    cache_na_k2 = cache_na_k.reshape(DEC_BATCH, DEPTH, PAST_LEN, 256)
    cache_na_v2 = cache_na_v.reshape(DEC_BATCH, DEPTH, PAST_LEN, 256)
    cache_diff_k2 = cache_diff_k.reshape(DEC_BATCH, DEPTH, PAST_LEN, 512)
    cache_diff_v2 = cache_diff_v.reshape(DEC_BATCH, DEPTH, PAST_LEN, 512)

    ctx_out = []
    for l in range(DEPTH):
        lambda_init = 0.8 - 0.6 * math.exp(-0.3 * l)
        mod = mod_all[l].reshape(N_COND, 1, 6 * D_MODEL)

        wi = w_in[l]
        kr_cols = _pad_cols(wi[:, C_KR:C_KR + MLA_ROPE], MLA_NOPE, LANES - MLA_NOPE - MLA_ROPE)
        w_in_r = jnp.concatenate([wi[:, :C_KR], kr_cols, wi[:, C_KR + MLA_ROPE:]], axis=1).astype(BF16)
        wuq = mla_w_uq[l].reshape(MLA_Q_LORA, MLA_HEADS, MLA_NOPE + MLA_ROPE)
        wuq = jnp.pad(wuq, ((0, 0), (0, 0), (0, LANES - MLA_NOPE - MLA_ROPE))).reshape(MLA_Q_LORA, -1).astype(BF16)
        wukv = mla_w_ukv[l].reshape(MLA_KV_LORA, MLA_HEADS, MLA_NOPE + MLA_V)
        wuk = jnp.pad(wukv[:, :, :MLA_NOPE], ((0, 0), (0, 0), (0, LANES - MLA_NOPE))).reshape(MLA_KV_LORA, -1)
        wuk = wuk.astype(BF16)
        wuv = wukv[:, :, MLA_NOPE:].reshape(MLA_KV_LORA, -1).astype(BF16)
        sgu_bias = jnp.repeat(sgu_b[l].T, SGU_WIDTH // SGU_GROUPS, axis=1)
        lams = [p[l].reshape(1, DIFF_QK_DIM) for p in (diff_lambda_q1, diff_lambda_k1, diff_lambda_q2, diff_lambda_k2)]
        dg = diff_norm_g[l].reshape(1, DIFF_V_DIM)

        oa, mq, ckv, kr, nq, nk, nv, dq, dk, dv = _inproj_call(
            x, mod, rope_b, rope_d, w_in_r, sgu_norm_g[l].reshape(1, -1), sgu_w[l].astype(BF16), sgu_bias,
            mla_q_norm_g[l].reshape(1, -1), mla_kv_norm_g[l].reshape(1, -1), wuq)

        ob_c, oc_c, od_c = _ctx_attn_call(lambda_init, (mq, ckv, kr, nq, nk, nv, dq, dk, dv), wuk, wuv, lams, dg)
        ob_l = _lat_mla_call(l, mq, ckv, kr, cache_mla_ckv, cache_kr_pad, wuk, wuv)
        oc_l = _lat_na_call(l, nq, nk, nv, cache_na_k2, cache_na_v2, _na_bias_table(na_rpb[l]))
        od_l = _lat_diff_call(l, lambda_init, dq, dk, dv, cache_diff_k2, cache_diff_v2, lams, dg)
        ob = jnp.concatenate([ob_c, ob_l], axis=0)
        oc = jnp.concatenate([oc_c, oc_l], axis=0)
        od = jnp.concatenate([od_c, od_l], axis=0)

        x1, h2t = _merge_call(
            x, mod, oa, ob, oc, od, w_gate[l].astype(BF16), b_gate[l].reshape(1, -1),
            w_branch_a[l].astype(BF16), w_branch_b[l].astype(BF16), w_branch_c[l].astype(BF16),
            w_branch_d[l].astype(BF16), w_out[l].astype(BF16), ln1_g[l].reshape(1, -1), ln1_b[l].reshape(1, -1))

        keys = peer_subkeys[l].reshape(2 * PEER_HEADS, PEER_N_KEYS, PEER_KEY_DIM // 2).astype(BF16)
        r2, e2, n1, e1 = _router_call(h2t, peer_w_q[l].T.astype(BF16), keys)
        x = _peer_call(h2t, peer_u[l].astype(BF16), peer_v[l].T.astype(BF16), r2, e2, n1, e1, x1, mod,
                       ln2_g[l].reshape(1, -1), ln2_b[l].reshape(1, -1))

        ctx_out.append((ckv[:N_CTX].reshape(BATCH, SEQ, MLA_KV_LORA),
                        kr[:N_CTX, MLA_NOPE:MLA_NOPE + MLA_ROPE].reshape(BATCH, SEQ, MLA_ROPE),
                        nk[:N_CTX].reshape(BATCH, SEQ, NA_HEADS, NA_HEAD_DIM),
                        nv[:N_CTX].reshape(BATCH, SEQ, NA_HEADS, NA_HEAD_DIM),
                        dk[:N_CTX].reshape(BATCH, SEQ, DIFF_HEADS, 2 * DIFF_QK_DIM),
                        dv[:N_CTX].reshape(BATCH, SEQ, DIFF_HEADS, DIFF_V_DIM)))

    y_prompt = x[:N_CTX].reshape(BATCH, SEQ, D_MODEL)
    y_sample = x[N_CTX:].reshape(DEC_BATCH, DEC_SEQ, D_MODEL)
    new = [jnp.stack([t[k] for t in ctx_out], axis=1) for k in range(6)]
    return (y_prompt, y_sample, *new)
```

```python
import functools
import math

import jax
import jax.numpy as jnp
from jax import lax
from jax.experimental import pallas as pl
from jax.experimental.pallas import tpu as pltpu

F32 = jnp.float32
BF16 = jnp.bfloat16

D_MODEL = 1024
BATCH = 32
SEQ = 256
DEPTH = 2
DEC_BATCH = 8
DEC_SEQ = 1024
PAST_LEN = 512
GRID_W = 64
CHUNK = 128
SGU_GROUPS = 4
SGU_WIDTH = 256
MLA_HEADS = 4
MLA_Q_LORA = 256
MLA_KV_LORA = 128
MLA_NOPE = 64
MLA_ROPE = 32
MLA_V = 64
NA_HEADS = 4
NA_HEAD_DIM = 64
NA_WIN_ROWS = 8
NA_WIN_COLS = 16
DIFF_HEADS = 4
DIFF_QK_DIM = 64
DIFF_V_DIM = 128
N_BRANCHES = 4
PEER_HEADS = 8
PEER_N_KEYS = 128
PEER_KEY_DIM = 256
PEER_TOPK = 16
ROPE_THETA = 10000.0
LN_EPS = 1e-6
NEG_BIG = -1e30
DEEPNORM_ALPHA = (2 * DEPTH) ** 0.25

LANES = 128
N_CTX = BATCH * SEQ
N_LAT = DEC_BATCH * DEC_SEQ
N_TOK = N_CTX + N_LAT
N_COND = 16
TM = 512
ROWS = DEC_SEQ // GRID_W
Q_TILE = 256
ROUTER_TILE = 256
PEER_TILE = 512
EXPERT_BLOCK = 1024
KEYS_PER_BLOCK = EXPERT_BLOCK // PEER_N_KEYS
GATE_LANES = 256
VMEM_LIMIT = 56 * 1024 * 1024

C_AU, C_AV, C_CQ, C_CKV, C_KR = 0, 256, 512, 768, 896
C_NQ, C_NK, C_NV, C_DQ, C_DK, C_DV, C_END = 1024, 1280, 1536, 1792, 2304, 2816, 3328


def _ln(x):
    mu = jnp.mean(x, axis=-1, keepdims=True)
    xc = x - mu
    var = jnp.mean(xc * xc, axis=-1, keepdims=True)
    return xc * lax.rsqrt(var + LN_EPS)


def _rms(x):
    return x * lax.rsqrt(jnp.mean(x * x, axis=-1, keepdims=True) + LN_EPS)


def _gelu(x):
    return 0.5 * x * (1.0 + lax.erf(x * (1.0 / math.sqrt(2.0))))


def _dot(a, b):
    return jnp.dot(a, b, preferred_element_type=F32)


def _dot_nt(a, b):
    return lax.dot_general(a, b, (((1,), (1,)), ((), ())), preferred_element_type=F32)


def _rope(x, tab_ref, half):
    return (x * tab_ref[0] + pltpu.roll(x, LANES - half, 1) * tab_ref[1] + pltpu.roll(x, half, 1) * tab_ref[2])


def _cparams(*sem):
    return pltpu.CompilerParams(dimension_semantics=sem, vmem_limit_bytes=VMEM_LIMIT)


def _mod_kernel(cond_ref, w_ref, b_ref, o_ref):
    c = cond_ref[...]
    s = c * jax.nn.sigmoid(c)
    o_ref[...] = _dot(s, w_ref[...]) + b_ref[...]


def _mod_call(cond, w_mod, b_mod):
    nb = 1536
    return pl.pallas_call(
        _mod_kernel,
        grid=(DEPTH, 6 * D_MODEL // nb),
        in_specs=[pl.BlockSpec((N_COND, D_MODEL), lambda l, j: (0, 0)),
                  pl.BlockSpec((None, D_MODEL, nb), lambda l, j: (l, 0, j)),
                  pl.BlockSpec((None, 1, nb), lambda l, j: (l, 0, j))],
        out_specs=pl.BlockSpec((None, N_COND, nb), lambda l, j: (l, 0, j)),
        out_shape=jax.ShapeDtypeStruct((DEPTH, N_COND, 6 * D_MODEL), F32),
        compiler_params=_cparams("arbitrary", "arbitrary"),
        name="mod_vectors",
    )(cond, w_mod, b_mod.reshape(DEPTH, 1, 6 * D_MODEL))


def _mod_row(i):
    n_ctx_tiles = N_CTX // TM
    return jnp.where(i < n_ctx_tiles, 0, 1 + (i - n_ctx_tiles) // (DEC_SEQ // TM))


def _pos_block(i):
    n_ctx_tiles = N_CTX // TM
    return jnp.where(i < n_ctx_tiles, DEC_SEQ // TM, (i - n_ctx_tiles) % (DEC_SEQ // TM))


def _inproj_kernel(x_ref, mod_ref, rb_ref, rd_ref, w_in_ref, sgu_g_ref, sgu_w_ref, sgu_bias_ref,
                   qg_ref, kvg_ref, wuq_ref,
                   oa_ref, mq_ref, ckv_ref, kr_ref, nq_ref, nk_ref, nv_ref, dq_ref, dk_ref, dv_ref):
    x = x_ref[...]
    shift = mod_ref[0, :, 0:D_MODEL]
    scale = mod_ref[0, :, D_MODEL:2 * D_MODEL]
    h = (_ln(x) * (1.0 + scale) + shift).astype(BF16)

    ya = _dot(h, w_in_ref[:, C_AU:C_CQ])
    u = _gelu(ya[:, :SGU_WIDTH])
    v = _gelu(ya[:, SGU_WIDTH:])
    vn = (_ln(v) * sgu_g_ref[...]).astype(BF16)
    group = lax.broadcasted_iota(jnp.int32, (CHUNK, SGU_WIDTH), 1) // (SGU_WIDTH // SGU_GROUPS)
    for c in range(TM // CHUNK):
        rows = slice(c * CHUNK, (c + 1) * CHUNK)
        mixed = sgu_bias_ref[...]
        for g in range(SGU_GROUPS):
            mixed = mixed + jnp.where(group == g, _dot(sgu_w_ref[g], vn[rows]), 0.0)
        oa_ref[rows, :] = (u[rows] * mixed).astype(oa_ref.dtype)

    ym = _dot(h, w_in_ref[:, C_CQ:C_NQ])
    cq = (_rms(ym[:, :MLA_Q_LORA]) * qg_ref[...]).astype(BF16)
    mq = _dot(cq, wuq_ref[...])
    for g in range(MLA_HEADS):
        lanes = slice(g * LANES, (g + 1) * LANES)
        mq_ref[:, lanes] = _rope(mq[:, lanes], rb_ref, MLA_ROPE // 2)
    ckv_ref[...] = _rms(ym[:, MLA_Q_LORA:MLA_Q_LORA + MLA_KV_LORA]) * kvg_ref[...]
    kr_ref[...] = _rope(ym[:, MLA_Q_LORA + MLA_KV_LORA:], rb_ref, MLA_ROPE // 2)

    yn = _dot(h, w_in_ref[:, C_NQ:C_DQ])
    nq_ref[...] = yn[:, 0:256]
    nk_ref[...] = yn[:, 256:512]
    nv_ref[...] = yn[:, 512:768]

    yd = _dot(h, w_in_ref[:, C_DQ:C_END])
    for g in range(4):
        lanes = slice(g * LANES, (g + 1) * LANES)
        dq_ref[:, lanes] = _rope(yd[:, g * LANES:(g + 1) * LANES], rd_ref, DIFF_QK_DIM // 2)
        dk_ref[:, lanes] = _rope(yd[:, 512 + g * LANES:512 + (g + 1) * LANES], rd_ref, DIFF_QK_DIM // 2)
    dv_ref[...] = yd[:, 1024:1536]


def _inproj_call(x, mod, rope_b, rope_d, w_in_r, sgu_g, sgu_w, sgu_bias, qg, kvg, wuq):
    tile = lambda w: pl.BlockSpec((TM, w), lambda i: (i, 0))
    full = lambda *s: pl.BlockSpec(s, lambda i: (0,) * len(s))
    widths = (SGU_WIDTH, 512, MLA_KV_LORA, LANES, 256, 256, 256, 512, 512, 512)
    dtypes = (BF16,) + (F32,) * 9
    return pl.pallas_call(
        _inproj_kernel,
        grid=(N_TOK // TM,),
        in_specs=[tile(D_MODEL),
                  pl.BlockSpec((1, 1, 6 * D_MODEL), lambda i: (_mod_row(i), 0, 0)),
                  pl.BlockSpec((3, TM, LANES), lambda i: (0, _pos_block(i), 0)),
                  pl.BlockSpec((3, TM, LANES), lambda i: (0, _pos_block(i), 0)),
                  full(D_MODEL, C_END), full(1, SGU_WIDTH), full(SGU_GROUPS, CHUNK, CHUNK),
                  full(CHUNK, SGU_WIDTH), full(1, MLA_Q_LORA), full(1, MLA_KV_LORA),
                  full(MLA_Q_LORA, MLA_HEADS * LANES)],
        out_specs=[tile(w) for w in widths],
        out_shape=[jax.ShapeDtypeStruct((N_TOK, w), dt) for w, dt in zip(widths, dtypes)],
        compiler_params=_cparams("parallel"),
        name="in_projection",
    )(x, mod, rope_b, rope_d, w_in_r, sgu_g, sgu_w, sgu_bias, qg, kvg, wuq)


def _half_mask(lo):
    lane = lax.broadcasted_iota(jnp.int32, (1, LANES), 1)
    return (lane >= lo) & (lane < lo + 64)


def _softmax_pv(scores, values, lanes):
    m = scores[0].max(axis=-1, keepdims=True)
    for s in scores[1:]:
        m = jnp.maximum(m, s.max(axis=-1, keepdims=True))
    den = None
    o = None
    for s, v in zip(scores, values):
        p = jnp.exp(s - m)
        d = p.sum(axis=-1, keepdims=True)
        den = d if den is None else den + d
        pv = _dot(p.astype(BF16), v[:, lanes])
        o = pv if o is None else o + pv
    return o / den


def _pair_attention(q, keys, vals, scale, bias_fn=None):
    outs = []
    for pair in range(2):
        lanes = slice(pair * LANES, (pair + 1) * LANES)
        qp = q[:, lanes]
        acc = None
        for sub in range(2):
            head = 2 * pair + sub
            mask = _half_mask(64 * sub)
            qm = jnp.where(mask, qp, jnp.zeros_like(qp))
            scores = [_dot_nt(qm, k[:, lanes]) * scale for k in keys]
            if bias_fn is not None:
                scores = bias_fn(head, scores)
            o = jnp.where(mask, _softmax_pv(scores, vals, lanes), 0.0)
            acc = o if acc is None else acc + o
        outs.append(acc)
    return outs


def _mla_attention(q, k_blocks, v_blocks, o_ref, rows):
    scale = (MLA_NOPE + MLA_ROPE) ** -0.5
    for pair in range(2):
        lanes = slice(pair * LANES, (pair + 1) * LANES)
        acc = None
        for sub in range(2):
            head = 2 * pair + sub
            hl = slice(head * LANES, (head + 1) * LANES)
            scores = [_dot_nt(q[:, hl], k[:, hl]) * scale for k in k_blocks]
            o = jnp.where(_half_mask(64 * sub), _softmax_pv(scores, v_blocks, lanes), 0.0)
            acc = o if acc is None else acc + o
        o_ref[rows, lanes] = acc.astype(o_ref.dtype)


def _diff_lambda(lq1, lk1, lq2, lk2, lambda_init):
    a = jnp.sum(lq1[...] * lk1[...], axis=-1, keepdims=True)
    b = jnp.sum(lq2[...] * lk2[...], axis=-1, keepdims=True)
    return jnp.exp(a) - jnp.exp(b) + lambda_init


def _diff_attention(q, k_blocks, v_blocks, lam, norm_g, lambda_init, o_ref, rows):
    scale = DIFF_QK_DIM ** -0.5
    for head in range(DIFF_HEADS):
        hl = slice(head * LANES, (head + 1) * LANES)
        qh = q[:, hl]
        probs = []
        for sub in range(2):
            qm = jnp.where(_half_mask(64 * sub), qh, jnp.zeros_like(qh))
            scores = [_dot_nt(qm, k[:, hl]) * scale for k in k_blocks]
            m = scores[0].max(axis=-1, keepdims=True)
            for s in scores[1:]:
                m = jnp.maximum(m, s.max(axis=-1, keepdims=True))
            ps = [jnp.exp(s - m) for s in scores]
            den = ps[0].sum(axis=-1, keepdims=True)
            for p in ps[1:]:
                den = den + p.sum(axis=-1, keepdims=True)
            probs.append((ps, 1.0 / den))
        o = None
        for i, v in enumerate(v_blocks):
            w = probs[0][0][i] * probs[0][1] - probs[1][0][i] * (lam * probs[1][1])
            pv = _dot(w.astype(BF16), v[:, hl])
            o = pv if o is None else o + pv
        o = _rms(o) * norm_g * (1.0 - lambda_init)
        o_ref[rows, hl] = o.astype(o_ref.dtype)


def _ctx_attn_kernel(lambda_init, mq_ref, ckv_ref, kr_ref, nq_ref, nk_ref, nv_ref, dq_ref, dk_ref, dv_ref,
                     wuk_ref, wuv_ref, lq1, lk1, lq2, lk2, dg_ref, ob_ref, oc_ref, od_ref):
    rows = slice(0, SEQ)
    ckv = ckv_ref[...].astype(BF16)
    kr = kr_ref[...]
    k_b = (_dot(ckv, wuk_ref[...]) + jnp.concatenate([kr] * MLA_HEADS, axis=1)).astype(BF16)
    v_b = _dot(ckv, wuv_ref[...]).astype(BF16)
    _mla_attention(mq_ref[...].astype(BF16), [k_b], [v_b], ob_ref, rows)

    outs = _pair_attention(nq_ref[...].astype(BF16), [nk_ref[...].astype(BF16)], [nv_ref[...].astype(BF16)],
                           NA_HEAD_DIM ** -0.5)
    for pair in range(2):
        oc_ref[:, pair * LANES:(pair + 1) * LANES] = outs[pair].astype(oc_ref.dtype)

    lam = _diff_lambda(lq1, lk1, lq2, lk2, lambda_init)
    _diff_attention(dq_ref[...].astype(BF16), [dk_ref[...].astype(BF16)], [dv_ref[...].astype(BF16)],
                    lam, dg_ref[...], lambda_init, od_ref, rows)


def _ctx_attn_call(lambda_init, acts, wuk, wuv, lams, dg):
    mq, ckv, kr, nq, nk, nv, dq, dk, dv = acts
    seq = lambda w: pl.BlockSpec((SEQ, w), lambda b: (b, 0))
    full = lambda *s: pl.BlockSpec(s, lambda b: (0,) * len(s))
    return pl.pallas_call(
        functools.partial(_ctx_attn_kernel, lambda_init),
        grid=(BATCH,),
        in_specs=[seq(512), seq(128), seq(128), seq(256), seq(256), seq(256), seq(512), seq(512), seq(512),
                  full(MLA_KV_LORA, 512), full(MLA_KV_LORA, 256)] + [full(1, DIFF_QK_DIM)] * 4
                 + [full(1, DIFF_V_DIM)],
        out_specs=[seq(256), seq(256), seq(512)],
        out_shape=[jax.ShapeDtypeStruct((N_CTX, w), BF16) for w in (256, 256, 512)],
        compiler_params=_cparams("parallel"),
        name="context_attention",
    )(mq, ckv, kr, nq, nk, nv, dq, dk, dv, wuk, wuv, *lams, dg)


def _lat_mla_kernel(mq_ref, ckv_ref, kr_ref, cckv_ref, ckr_ref, wuk_ref, wuv_ref, o_ref):
    def expand(ckv_f32, kr):
        ckv = ckv_f32.astype(BF16)
        k = (_dot(ckv, wuk_ref[...]) + jnp.concatenate([kr] * MLA_HEADS, axis=1)).astype(BF16)
        return k, _dot(ckv, wuv_ref[...]).astype(BF16)

    k_lat, v_lat = expand(ckv_ref[...], kr_ref[...])
    k_ctx, v_ctx = expand(cckv_ref[...], ckr_ref[...])
    for t in range(DEC_SEQ // Q_TILE):
        rows = slice(t * Q_TILE, (t + 1) * Q_TILE)
        _mla_attention(mq_ref[rows, :].astype(BF16), [k_lat, k_ctx], [v_lat, v_ctx], o_ref, rows)


def _lat_mla_call(l, mq, ckv, kr, cache_ckv, cache_kr_pad, wuk, wuv):
    off = N_CTX // DEC_SEQ
    seq = lambda w: pl.BlockSpec((DEC_SEQ, w), lambda b: (off + b, 0))
    cache = lambda w: pl.BlockSpec((None, None, PAST_LEN, w), lambda b: (b, l, 0, 0))
    full = lambda *s: pl.BlockSpec(s, lambda b: (0,) * len(s))
    return pl.pallas_call(
        _lat_mla_kernel,
        grid=(DEC_BATCH,),
        in_specs=[seq(512), seq(128), seq(128), cache(MLA_KV_LORA), cache(LANES),
                  full(MLA_KV_LORA, 512), full(MLA_KV_LORA, 256)],
        out_specs=pl.BlockSpec((DEC_SEQ, 256), lambda b: (b, 0)),
        out_shape=jax.ShapeDtypeStruct((N_LAT, 256), BF16),
        compiler_params=_cparams("parallel"),
        name="latent_mla_attention",
    )(mq, ckv, kr, cache_ckv, cache_kr_pad, wuk, wuv)


def _win_start(r):
    return jnp.clip(r - NA_WIN_ROWS // 2, 0, ROWS - NA_WIN_ROWS)


def _lat_na_kernel(nq_ref, nk_ref, nv_ref, ck_ref, cv_ref, bias_ref, o_ref):
    r = pl.program_id(1)
    start = pl.multiple_of(_win_start(r) * GRID_W, GRID_W)
    win = NA_WIN_ROWS * GRID_W
    k_w = nk_ref[pl.ds(start, win), :].astype(BF16)
    v_w = nv_ref[pl.ds(start, win), :].astype(BF16)
    k_c = ck_ref[...].astype(BF16)
    v_c = cv_ref[...].astype(BF16)
    q_col = lax.broadcasted_iota(jnp.int32, (GRID_W, win), 0)
    k_col = lax.broadcasted_iota(jnp.int32, (GRID_W, win), 1) % GRID_W
    c0 = jnp.clip(q_col - NA_WIN_COLS // 2, 0, GRID_W - NA_WIN_COLS)
    col_in = (k_col >= c0) & (k_col < c0 + NA_WIN_COLS)

    def bias_fn(head, scores):
        return [jnp.where(col_in, scores[0] + bias_ref[0, head], NEG_BIG), scores[1]]

    outs = _pair_attention(nq_ref[...].astype(BF16), [k_w, k_c], [v_w, v_c], NA_HEAD_DIM ** -0.5, bias_fn)
    for pair in range(2):
        o_ref[:, pair * LANES:(pair + 1) * LANES] = outs[pair].astype(o_ref.dtype)


def _lat_na_call(l, nq, nk, nv, cache_k, cache_v, bias_tab):
    off = N_CTX // DEC_SEQ
    seq = pl.BlockSpec((DEC_SEQ, 256), lambda b, r: (off + b, 0))
    cache = pl.BlockSpec((None, None, PAST_LEN, 256), lambda b, r: (b, l, 0, 0))
    return pl.pallas_call(
        _lat_na_kernel,
        grid=(DEC_BATCH, ROWS),
        in_specs=[pl.BlockSpec((GRID_W, 256), lambda b, r: (N_CTX // GRID_W + b * ROWS + r, 0)),
                  seq, seq, cache, cache,
                  pl.BlockSpec((1, NA_HEADS, GRID_W, NA_WIN_ROWS * GRID_W),
                               lambda b, r: (_win_start(r) - r + NA_WIN_ROWS - 1, 0, 0, 0))],
        out_specs=pl.BlockSpec((GRID_W, 256), lambda b, r: (b * ROWS + r, 0)),
        out_shape=jax.ShapeDtypeStruct((N_LAT, 256), BF16),
        compiler_params=_cparams("parallel", "arbitrary"),
        name="latent_neighbourhood_attention",
    )(nq, nk, nv, cache_k, cache_v, bias_tab)


def _lat_diff_kernel(lambda_init, dq_ref, dk_ref, dv_ref, ck_ref, cv_ref, lq1, lk1, lq2, lk2, dg_ref, o_ref):
    lam = _diff_lambda(lq1, lk1, lq2, lk2, lambda_init)
    k_blocks = [dk_ref[...].astype(BF16), ck_ref[...].astype(BF16)]
    v_blocks = [dv_ref[...].astype(BF16), cv_ref[...].astype(BF16)]
    for t in range(DEC_SEQ // Q_TILE):
        rows = slice(t * Q_TILE, (t + 1) * Q_TILE)
        _diff_attention(dq_ref[rows, :].astype(BF16), k_blocks, v_blocks, lam, dg_ref[...], lambda_init,
                        o_ref, rows)


def _lat_diff_call(l, lambda_init, dq, dk, dv, cache_k, cache_v, lams, dg):
    off = N_CTX // DEC_SEQ
    seq = pl.BlockSpec((DEC_SEQ, 512), lambda b: (off + b, 0))
    cache = pl.BlockSpec((None, None, PAST_LEN, 512), lambda b: (b, l, 0, 0))
    full = lambda *s: pl.BlockSpec(s, lambda b: (0,) * len(s))
    return pl.pallas_call(
        functools.partial(_lat_diff_kernel, lambda_init),
        grid=(DEC_BATCH,),
        in_specs=[seq, seq, seq, cache, cache] + [full(1, DIFF_QK_DIM)] * 4 + [full(1, DIFF_V_DIM)],
        out_specs=pl.BlockSpec((DEC_SEQ, 512), lambda b: (b, 0)),
        out_shape=jax.ShapeDtypeStruct((N_LAT, 512), BF16),
        compiler_params=_cparams("parallel"),
        name="latent_differential_attention",
    )(dq, dk, dv, cache_k, cache_v, *lams, dg)


def _merge_kernel(x_ref, mod_ref, oa_ref, ob_ref, oc_ref, od_ref, wg_ref, bg_ref,
                  wa_ref, wb_ref, wc_ref, wd_ref, wo_ref, g_ref, b_ref, x1_ref, h2t_ref):
    x = x_ref[...]
    mod = lambda k: mod_ref[0, :, k * D_MODEL:(k + 1) * D_MODEL]
    h = (_ln(x) * (1.0 + mod(1)) + mod(0)).astype(BF16)
    merged = None
    for i, (o_ref, w_ref) in enumerate(((oa_ref, wa_ref), (ob_ref, wb_ref), (oc_ref, wc_ref), (od_ref, wd_ref))):
        cols = slice(i * D_MODEL, (i + 1) * D_MODEL)
        gate = jax.nn.sigmoid(_dot(h, wg_ref[:, cols]) + bg_ref[:, cols])
        term = gate * _dot(o_ref[...], w_ref[...])
        merged = term if merged is None else merged + term
    mix = _dot(merged.astype(BF16), wo_ref[...])
    x1 = _ln(DEEPNORM_ALPHA * x + mod(2) * mix) * g_ref[...] + b_ref[...]
    x1_ref[...] = x1
    h2 = _ln(x1) * (1.0 + mod(4)) + mod(3)
    h2t_ref[...] = h2.T.astype(BF16)


def _merge_call(x, mod, oa, ob, oc, od, wg, bg, wa, wb, wc, wd, wo, g, b):
    tile = lambda w: pl.BlockSpec((TM, w), lambda i: (i, 0))
    full = lambda *s: pl.BlockSpec(s, lambda i: (0,) * len(s))
    return pl.pallas_call(
        _merge_kernel,
        grid=(N_TOK // TM,),
        in_specs=[tile(D_MODEL), pl.BlockSpec((1, 1, 6 * D_MODEL), lambda i: (_mod_row(i), 0, 0)),
                  tile(256), tile(256), tile(256), tile(512),
                  full(D_MODEL, 4 * D_MODEL), full(1, 4 * D_MODEL),
                  full(256, D_MODEL), full(256, D_MODEL), full(256, D_MODEL), full(512, D_MODEL),
                  full(D_MODEL, D_MODEL), full(1, D_MODEL), full(1, D_MODEL)],
        out_specs=[tile(D_MODEL), pl.BlockSpec((D_MODEL, TM), lambda i: (0, i))],
        out_shape=[jax.ShapeDtypeStruct((N_TOK, D_MODEL), F32), jax.ShapeDtypeStruct((D_MODEL, N_TOK), BF16)],
        compiler_params=_cparams("parallel"),
        name="branch_merge",
    )(x, mod, oa, ob, oc, od, wg, bg, wa, wb, wc, wd, wo, g, b)


KEY_MIN = -2 ** 31


def _tree_sum(terms):
    while len(terms) > 1:
        terms = [a + b for a, b in zip(terms[0::2], terms[1::2])] + ([terms[-1]] if len(terms) % 2 else [])
    return terms[0]


def _row_gather(table, idx):
    return _tree_sum([jnp.where(idx == float(k), table[k:k + 1, :], 0.0) for k in range(PEER_TOPK)])


def _sort_key(x):
    b = lax.bitcast_convert_type(x + 0.0, jnp.int32)
    return b ^ ((b >> 31) & 0x7FFFFFFF)


def _key_value(k):
    return lax.bitcast_convert_type(k ^ ((k >> 31) & 0x7FFFFFFF), F32)


def _top16(s):
    row = lax.broadcasted_iota(jnp.int32, s.shape, 0).astype(F32)
    krow = lax.broadcasted_iota(jnp.int32, (PEER_TOPK, s.shape[1]), 0)

    def body(k, carry):
        work, rank, vals = carry
        m = jnp.max(work, axis=0, keepdims=True)
        idx = jnp.min(jnp.where(work == m, row, float(PEER_N_KEYS)), axis=0, keepdims=True)
        sel = row == idx
        rank = jnp.where(sel, jnp.asarray(k, jnp.int32).astype(F32), rank)
        work = jnp.where(sel, -jnp.inf, work)
        vals = jnp.where(krow == k, m, vals)
        return work, rank, vals

    init = (s, jnp.full(s.shape, float(PEER_N_KEYS), F32), jnp.zeros((PEER_TOPK, s.shape[1]), F32))
    _, rank, vals = lax.fori_loop(0, PEER_TOPK, body, init)
    return vals, rank


def _top16_pair(s1, s2):
    krow = lax.broadcasted_iota(jnp.int32, (PEER_TOPK, LANES), 0)

    def body(k, carry):
        w1, w2, v1, v2 = carry
        code = KEY_MIN + jnp.asarray(k, jnp.int32)
        m1 = jnp.max(w1, axis=0, keepdims=True)
        m2 = jnp.max(w2, axis=0, keepdims=True)
        w1 = jnp.where(w1 == m1, code, w1)
        w2 = jnp.where(w2 == m2, code, w2)
        return w1, w2, jnp.where(krow == k, m1, v1), jnp.where(krow == k, m2, v2)

    zeros = jnp.zeros((PEER_TOPK, LANES), jnp.int32)
    w1, w2, v1, v2 = lax.fori_loop(0, PEER_TOPK, body, (_sort_key(s1), _sort_key(s2), zeros, zeros))

    def decode(w):
        taken = w < KEY_MIN + PEER_TOPK
        rank = jnp.where(taken, (w - KEY_MIN).astype(F32), float(PEER_N_KEYS))
        return rank, jnp.sum(taken.astype(F32), axis=0, keepdims=True)

    r1, c1 = decode(w1)
    r2, c2 = decode(w2)
    ties = jnp.max(jnp.maximum(jnp.abs(c1 - PEER_TOPK), jnp.abs(c2 - PEER_TOPK))) > 0.5

    def exact():
        hs1, q1 = _top16(s1)
        hs2, q2 = _top16(s2)
        return hs1, hs2, q1, q2

    return lax.cond(ties, exact, lambda: (_key_value(v1), _key_value(v2), r1, r2))


def _merge_counts(hs1, hs2):
    krow = lax.broadcasted_iota(jnp.int32, hs1.shape, 0).astype(F32)

    def body(_, carry):
        cnt, front = carry
        m = jnp.max(front, axis=0, keepdims=True)
        win = jnp.min(jnp.where(front == m, krow, float(PEER_TOPK)), axis=0, keepdims=True)
        sel = krow == win
        cnt = jnp.where(sel, cnt + 1.0, cnt)
        nxt = jnp.where(cnt < float(PEER_TOPK), hs1 + _row_gather(hs2, cnt), -jnp.inf)
        return cnt, jnp.where(sel, nxt, front)

    cnt, _ = lax.fori_loop(0, PEER_TOPK, body, (jnp.zeros(hs1.shape, F32), hs1 + hs2[0:1, :]))
    return cnt


def _router_kernel(h2t_ref, wqt_ref, keys_ref, r2_ref, e2_ref, n1_ref, e1_ref, q_scr, s_scr, hs_scr, rank1_scr):
    t = ROUTER_TILE
    q_scr[...] = _dot(wqt_ref[...], h2t_ref[...]).astype(BF16)

    def head_body(hd, _):
        base = pl.multiple_of(hd * PEER_KEY_DIM, PEER_KEY_DIM)
        s_scr[hd, 0] = _dot(keys_ref[2 * hd], q_scr[pl.ds(base, LANES), :])
        s_scr[hd, 1] = _dot(keys_ref[2 * hd + 1], q_scr[pl.ds(base + LANES, LANES), :])
        for j in range(t // LANES):
            lanes = slice(j * LANES, (j + 1) * LANES)
            hs1, hs2, rank1, rank2 = _top16_pair(s_scr[hd, 0, :, lanes], s_scr[hd, 1, :, lanes])
            hs_scr[hd, 0, :, lanes] = hs1
            hs_scr[hd, 1, :, lanes] = hs2
            rank1_scr[hd, :, lanes] = rank1
            r2_ref[hd, :, lanes] = rank2.astype(BF16)
        return 0

    lax.fori_loop(0, PEER_HEADS, head_body, 0)

    for pair in range(PEER_HEADS // 2):
        heads = (2 * pair, 2 * pair + 1)
        hs1 = jnp.concatenate([hs_scr[h, 0] for h in heads], axis=1)
        hs2 = jnp.concatenate([hs_scr[h, 1] for h in heads], axis=1)
        cnt = _merge_counts(hs1, hs2)
        e1r = jnp.exp(hs1 - hs1[0:1, :])
        e2r = jnp.exp(hs2 - hs2[0:1, :])
        prefix = _tree_sum([jnp.where(cnt > float(kb), e2r[kb:kb + 1, :], 0.0) for kb in range(PEER_TOPK)])
        inv_z = 1.0 / jnp.sum(e1r * prefix, axis=0, keepdims=True)
        for i, h in enumerate(heads):
            lanes = slice(i * t, (i + 1) * t)
            e2_ref[h] = (jnp.exp(s_scr[h, 1] - hs2[0:1, lanes]) * inv_z[:, lanes]).astype(BF16)
            e1_ref[h] = 0.5 * jnp.exp(s_scr[h, 0] - hs1[0:1, lanes])
            n1_ref[h] = _row_gather(cnt[:, lanes], rank1_scr[h])


def _router_call(h2t, wqt, keys):
    t = ROUTER_TILE
    out = pl.BlockSpec((PEER_HEADS, PEER_N_KEYS, t), lambda i: (0, 0, i))
    shape = (PEER_HEADS, PEER_N_KEYS, N_TOK)
    return pl.pallas_call(
        _router_kernel,
        grid=(N_TOK // t,),
        in_specs=[pl.BlockSpec((D_MODEL, t), lambda i: (0, i)),
                  pl.BlockSpec((PEER_HEADS * PEER_KEY_DIM, D_MODEL), lambda i: (0, 0)),
                  pl.BlockSpec((2 * PEER_HEADS, PEER_N_KEYS, PEER_KEY_DIM // 2), lambda i: (0, 0, 0))],
        out_specs=[out] * 4,
        out_shape=[jax.ShapeDtypeStruct(shape, BF16), jax.ShapeDtypeStruct(shape, BF16),
                   jax.ShapeDtypeStruct(shape, F32), jax.ShapeDtypeStruct(shape, F32)],
        scratch_shapes=[pltpu.VMEM((PEER_HEADS * PEER_KEY_DIM, t), BF16),
                        pltpu.VMEM((PEER_HEADS, 2, PEER_N_KEYS, t), F32),
                        pltpu.VMEM((PEER_HEADS, 2, PEER_TOPK, t), F32),
                        pltpu.VMEM((PEER_HEADS, PEER_N_KEYS, t), F32)],
        compiler_params=_cparams("parallel"),
        name="peer_retrieval",
    )(h2t, wqt, keys)


def _gated_activations(ht_ref, w_ref, r2_ref, e2_ref, n1_ref, e1_ref):
    for i in range(KEYS_PER_BLOCK):
        rows = slice(i * PEER_N_KEYS, (i + 1) * PEER_N_KEYS)
        for j in range(PEER_TILE // GATE_LANES):
            lanes = slice(j * GATE_LANES, (j + 1) * GATE_LANES)
            gate = jnp.zeros((PEER_N_KEYS, GATE_LANES), BF16)
            for hd in range(PEER_HEADS):
                n_row = n1_ref[hd, i:i + 1, lanes].astype(BF16)
                c_row = e1_ref[hd, i:i + 1, lanes].astype(BF16)
                live = jnp.where(r2_ref[hd, :, lanes] < n_row, e2_ref[hd, :, lanes], jnp.zeros((), BF16))
                gate = gate + live * c_row
            x = ht_ref[rows, lanes]
            act = x * (1.0 + lax.erf(x * (1.0 / math.sqrt(2.0))))
            w_ref[rows, lanes] = act.astype(BF16) * gate


def _peer_kernel(h2t_ref, u_ref, vt_ref, r2_ref, e2_ref, n1_ref, e1_ref,
                 x1_ref, mod_ref, g_ref, b_ref, o_ref, acc_ref, ht_a, ht_b, w_a, w_b):
    g = pl.program_id(1)

    @pl.when(g == 0)
    def _():
        acc_ref[...] = jnp.zeros_like(acc_ref)
        ht_b[...] = jnp.zeros_like(ht_b)
        w_b[...] = jnp.zeros_like(w_b)

    def stage(ht_write, ht_read, w_write, w_read):
        ht_write[...] = _dot(u_ref[...], h2t_ref[...])
        _gated_activations(ht_read, w_write, r2_ref, e2_ref, n1_ref, e1_ref)
        acc_ref[...] += _dot(vt_ref[...], w_read[...])

    pl.when(g % 2 == 0)(functools.partial(stage, ht_a, ht_b, w_a, w_b))
    pl.when(g % 2 == 1)(functools.partial(stage, ht_b, ht_a, w_b, w_a))

    @pl.when(g == pl.num_programs(1) - 1)
    def _():
        ffn = acc_ref[...].T
        g2 = mod_ref[0, :, 5 * D_MODEL:6 * D_MODEL]
        o_ref[...] = _ln(DEEPNORM_ALPHA * x1_ref[...] + g2 * ffn) * g_ref[...] + b_ref[...]


def _peer_mod_row(i):
    n_ctx_tiles = N_CTX // PEER_TILE
    return jnp.where(i < n_ctx_tiles, 0, 1 + (i - n_ctx_tiles) // (DEC_SEQ // PEER_TILE))


def _peer_call(h2t, u, vt, r2, e2, n1, e1, x1, mod, g, b):
    t = PEER_TILE
    n_blocks = PEER_N_KEYS * PEER_N_KEYS // EXPERT_BLOCK
    gates = pl.BlockSpec((PEER_HEADS, PEER_N_KEYS, t), lambda i, g: (0, 0, i))
    keys = pl.BlockSpec((PEER_HEADS, KEYS_PER_BLOCK, t), lambda i, g: (0, jnp.clip(g - 1, 0, n_blocks - 1), i))
    return pl.pallas_call(
        _peer_kernel,
        grid=(N_TOK // t, n_blocks + 2),
        in_specs=[pl.BlockSpec((D_MODEL, t), lambda i, g: (0, i)),
                  pl.BlockSpec((EXPERT_BLOCK, D_MODEL), lambda i, g: (jnp.minimum(g, n_blocks - 1), 0)),
                  pl.BlockSpec((D_MODEL, EXPERT_BLOCK), lambda i, g: (0, jnp.clip(g - 2, 0, n_blocks - 1))),
                  gates, gates, keys, keys,
                  pl.BlockSpec((t, D_MODEL), lambda i, g: (i, 0)),
                  pl.BlockSpec((1, 1, 6 * D_MODEL), lambda i, g: (_peer_mod_row(i), 0, 0)),
                  pl.BlockSpec((1, D_MODEL), lambda i, g: (0, 0)),
                  pl.BlockSpec((1, D_MODEL), lambda i, g: (0, 0))],
        out_specs=pl.BlockSpec((t, D_MODEL), lambda i, g: (i, 0)),
        out_shape=jax.ShapeDtypeStruct((N_TOK, D_MODEL), F32),
        scratch_shapes=[pltpu.VMEM((D_MODEL, t), F32),
                        pltpu.VMEM((EXPERT_BLOCK, t), F32), pltpu.VMEM((EXPERT_BLOCK, t), F32),
                        pltpu.VMEM((EXPERT_BLOCK, t), BF16), pltpu.VMEM((EXPERT_BLOCK, t), BF16)],
        compiler_params=_cparams("parallel", "arbitrary"),
        name="peer_dense",
    )(h2t, u, vt, r2, e2, n1, e1, x1, mod, g, b)


def _rope_tables():
    t = jnp.arange(DEC_SEQ)
    row = (t // GRID_W).astype(F32)
    col = (t % GRID_W).astype(F32)

    def angles(rot_dim):
        n_freq = rot_dim // 4
        inv_freq = ROPE_THETA ** (-jnp.arange(n_freq, dtype=F32) / n_freq)
        return jnp.concatenate([row[:, None] * inv_freq, col[:, None] * inv_freq], axis=-1)

    def pack(cos_l, sa_l, sb_l):
        tab = jnp.stack([cos_l, sa_l, sb_l])
        ident = jnp.stack([jnp.ones((TM, LANES), F32), jnp.zeros((TM, LANES), F32), jnp.zeros((TM, LANES), F32)])
        return jnp.concatenate([tab, ident], axis=1)

    ang_b = angles(MLA_ROPE)
    cb, sb = jnp.cos(ang_b), jnp.sin(ang_b)
    one, zero = jnp.ones((DEC_SEQ, 64), F32), jnp.zeros((DEC_SEQ, 64), F32)
    z16, z32 = jnp.zeros((DEC_SEQ, 16), F32), jnp.zeros((DEC_SEQ, 32), F32)
    rope_b = pack(jnp.concatenate([one, cb, cb, jnp.ones((DEC_SEQ, 32), F32)], axis=1),
                  jnp.concatenate([zero, -sb, z16, z32], axis=1),
                  jnp.concatenate([zero, z16, sb, z32], axis=1))
    ang_d = angles(DIFF_QK_DIM)
    cd, sd = jnp.cos(ang_d), jnp.sin(ang_d)
    rope_d = pack(jnp.concatenate([cd, cd, cd, cd], axis=1),
                  jnp.concatenate([-sd, z32, -sd, z32], axis=1),
                  jnp.concatenate([z32, sd, z32, sd], axis=1))
    return rope_b, rope_d


def _na_bias_table(rpb):
    col = jnp.arange(GRID_W)
    dc = jnp.clip(col[None, :] - col[:, None], -(NA_WIN_COLS - 1), NA_WIN_COLS - 1) + NA_WIN_COLS - 1
    rpb_cols = rpb[:, :, dc]
    tabs = [rpb_cols[:, off:off + NA_WIN_ROWS].transpose(0, 2, 1, 3).reshape(NA_HEADS, GRID_W, NA_WIN_ROWS * GRID_W)
            for off in range(NA_WIN_ROWS)]
    return jnp.stack(tabs)


def _pad_cols(w, left, right):
    return jnp.pad(w, ((0, 0), (left, right)))


def kernel(x_prompt, x_sample, cache_mla_ckv, cache_mla_krope, cache_na_k, cache_na_v, cache_diff_k, cache_diff_v, c, c_ctx, w_mod, b_mod, w_in, sgu_norm_g, sgu_w, sgu_b, mla_q_norm_g, mla_w_uq, mla_kv_norm_g, mla_w_ukv, na_rpb, diff_lambda_q1, diff_lambda_k1, diff_lambda_q2, diff_lambda_k2, diff_norm_g, w_branch_a, w_branch_b, w_branch_c, w_branch_d, w_gate, b_gate, w_out, ln1_g, ln1_b, peer_w_q, peer_subkeys, peer_u, peer_v, ln2_g, ln2_b):
    x = jnp.concatenate([x_prompt.reshape(N_CTX, D_MODEL), x_sample.reshape(N_LAT, D_MODEL)], axis=0)
    cond = jnp.concatenate([c_ctx[None], c, jnp.zeros((N_COND - 1 - DEC_BATCH, D_MODEL), F32)], axis=0)
    mod_all = _mod_call(cond, w_mod, b_mod)
    rope_b, rope_d = _rope_tables()
    cache_kr_pad = jnp.pad(cache_mla_krope, ((0, 0), (0, 0), (0, 0), (MLA_NOPE, LANES - MLA_NOPE - MLA_ROPE)))
    cache_na_k2 = cache_na_k.reshape(DEC_BATCH, DEPTH, PAST_LEN, 256)
    cache_na_v2 = cache_na_v.reshape(DEC_BATCH, DEPTH, PAST_LEN, 256)
    cache_diff_k2 = cache_diff_k.reshape(DEC_BATCH, DEPTH, PAST_LEN, 512)
    cache_diff_v2 = cache_diff_v.reshape(DEC_BATCH, DEPTH, PAST_LEN, 512)

    ctx_out = []
    for l in range(DEPTH):
        lambda_init = 0.8 - 0.6 * math.exp(-0.3 * l)
        mod = mod_all[l].reshape(N_COND, 1, 6 * D_MODEL)

        wi = w_in[l]
        kr_cols = _pad_cols(wi[:, C_KR:C_KR + MLA_ROPE], MLA_NOPE, LANES - MLA_NOPE - MLA_ROPE)
        w_in_r = jnp.concatenate([wi[:, :C_KR], kr_cols, wi[:, C_KR + MLA_ROPE:]], axis=1).astype(BF16)
        wuq = mla_w_uq[l].reshape(MLA_Q_LORA, MLA_HEADS, MLA_NOPE + MLA_ROPE)
        wuq = jnp.pad(wuq, ((0, 0), (0, 0), (0, LANES - MLA_NOPE - MLA_ROPE))).reshape(MLA_Q_LORA, -1).astype(BF16)
        wukv = mla_w_ukv[l].reshape(MLA_KV_LORA, MLA_HEADS, MLA_NOPE + MLA_V)
        wuk = jnp.pad(wukv[:, :, :MLA_NOPE], ((0, 0), (0, 0), (0, LANES - MLA_NOPE))).reshape(MLA_KV_LORA, -1)
        wuk = wuk.astype(BF16)
        wuv = wukv[:, :, MLA_NOPE:].reshape(MLA_KV_LORA, -1).astype(BF16)
        sgu_bias = jnp.repeat(sgu_b[l].T, SGU_WIDTH // SGU_GROUPS, axis=1)
        lams = [p[l].reshape(1, DIFF_QK_DIM) for p in (diff_lambda_q1, diff_lambda_k1, diff_lambda_q2, diff_lambda_k2)]
        dg = diff_norm_g[l].reshape(1, DIFF_V_DIM)

        oa, mq, ckv, kr, nq, nk, nv, dq, dk, dv = _inproj_call(
            x, mod, rope_b, rope_d, w_in_r, sgu_norm_g[l].reshape(1, -1), sgu_w[l].astype(BF16), sgu_bias,
            mla_q_norm_g[l].reshape(1, -1), mla_kv_norm_g[l].reshape(1, -1), wuq)

        ob_c, oc_c, od_c = _ctx_attn_call(lambda_init, (mq, ckv, kr, nq, nk, nv, dq, dk, dv), wuk, wuv, lams, dg)
        ob_l = _lat_mla_call(l, mq, ckv, kr, cache_mla_ckv, cache_kr_pad, wuk, wuv)
        oc_l = _lat_na_call(l, nq, nk, nv, cache_na_k2, cache_na_v2, _na_bias_table(na_rpb[l]))
        od_l = _lat_diff_call(l, lambda_init, dq, dk, dv, cache_diff_k2, cache_diff_v2, lams, dg)
        ob = jnp.concatenate([ob_c, ob_l], axis=0)
        oc = jnp.concatenate([oc_c, oc_l], axis=0)
        od = jnp.concatenate([od_c, od_l], axis=0)

        x1, h2t = _merge_call(
            x, mod, oa, ob, oc, od, w_gate[l].astype(BF16), b_gate[l].reshape(1, -1),
            w_branch_a[l].astype(BF16), w_branch_b[l].astype(BF16), w_branch_c[l].astype(BF16),
            w_branch_d[l].astype(BF16), w_out[l].astype(BF16), ln1_g[l].reshape(1, -1), ln1_b[l].reshape(1, -1))

        keys = peer_subkeys[l].reshape(2 * PEER_HEADS, PEER_N_KEYS, PEER_KEY_DIM // 2).astype(BF16)
        r2, e2, n1, e1 = _router_call(h2t, peer_w_q[l].T.astype(BF16), keys)
        x = _peer_call(h2t, peer_u[l].astype(BF16), peer_v[l].T.astype(BF16), r2, e2, n1, e1, x1, mod,
                       ln2_g[l].reshape(1, -1), ln2_b[l].reshape(1, -1))

        ctx_out.append((ckv[:N_CTX].reshape(BATCH, SEQ, MLA_KV_LORA),
                        kr[:N_CTX, MLA_NOPE:MLA_NOPE + MLA_ROPE].reshape(BATCH, SEQ, MLA_ROPE),
                        nk[:N_CTX].reshape(BATCH, SEQ, NA_HEADS, NA_HEAD_DIM),
                        nv[:N_CTX].reshape(BATCH, SEQ, NA_HEADS, NA_HEAD_DIM),
                        dk[:N_CTX].reshape(BATCH, SEQ, DIFF_HEADS, 2 * DIFF_QK_DIM),
                        dv[:N_CTX].reshape(BATCH, SEQ, DIFF_HEADS, DIFF_V_DIM)))

    y_prompt = x[:N_CTX].reshape(BATCH, SEQ, D_MODEL)
    y_sample = x[N_CTX:].reshape(DEC_BATCH, DEC_SEQ, D_MODEL)
    new = [jnp.stack([t[k] for t in ctx_out], axis=1) for k in range(6)]
    return (y_prompt, y_sample, *new)
```

```python
import functools
import math

import jax
import jax.numpy as jnp
from jax import lax
from jax.experimental import pallas as pl
from jax.experimental.pallas import tpu as pltpu

F32 = jnp.float32
BF16 = jnp.bfloat16

D_MODEL = 1024
BATCH = 32
SEQ = 256
DEPTH = 2
DEC_BATCH = 8
DEC_SEQ = 1024
PAST_LEN = 512
GRID_W = 64
CHUNK = 128
SGU_GROUPS = 4
SGU_WIDTH = 256
MLA_HEADS = 4
MLA_Q_LORA = 256
MLA_KV_LORA = 128
MLA_NOPE = 64
MLA_ROPE = 32
MLA_V = 64
NA_HEADS = 4
NA_HEAD_DIM = 64
NA_WIN_ROWS = 8
NA_WIN_COLS = 16
DIFF_HEADS = 4
DIFF_QK_DIM = 64
DIFF_V_DIM = 128
N_BRANCHES = 4
PEER_HEADS = 8
PEER_N_KEYS = 128
PEER_KEY_DIM = 256
PEER_TOPK = 16
ROPE_THETA = 10000.0
LN_EPS = 1e-6
NEG_BIG = -1e30
DEEPNORM_ALPHA = (2 * DEPTH) ** 0.25

LANES = 128
N_CTX = BATCH * SEQ
N_LAT = DEC_BATCH * DEC_SEQ
N_TOK = N_CTX + N_LAT
N_COND = 16
TM = 512
ROWS = DEC_SEQ // GRID_W
Q_TILE = 256
ROUTER_TILE = 256
PEER_TILE = 512
EXPERT_BLOCK = 1024
KEYS_PER_BLOCK = EXPERT_BLOCK // PEER_N_KEYS
GATE_LANES = 256
VMEM_LIMIT = 56 * 1024 * 1024

C_AU, C_AV, C_CQ, C_CKV, C_KR = 0, 256, 512, 768, 896
C_NQ, C_NK, C_NV, C_DQ, C_DK, C_DV, C_END = 1024, 1280, 1536, 1792, 2304, 2816, 3328


def _ln(x):
    mu = jnp.mean(x, axis=-1, keepdims=True)
    xc = x - mu
    var = jnp.mean(xc * xc, axis=-1, keepdims=True)
    return xc * lax.rsqrt(var + LN_EPS)


def _rms(x):
    return x * lax.rsqrt(jnp.mean(x * x, axis=-1, keepdims=True) + LN_EPS)


def _gelu(x):
    return 0.5 * x * (1.0 + lax.erf(x * (1.0 / math.sqrt(2.0))))


def _dot(a, b):
    return jnp.dot(a, b, preferred_element_type=F32)


def _dot_nt(a, b):
    return lax.dot_general(a, b, (((1,), (1,)), ((), ())), preferred_element_type=F32)


def _rope(x, tab_ref, half):
    return (x * tab_ref[0] + pltpu.roll(x, LANES - half, 1) * tab_ref[1] + pltpu.roll(x, half, 1) * tab_ref[2])


def _cparams(*sem):
    return pltpu.CompilerParams(dimension_semantics=sem, vmem_limit_bytes=VMEM_LIMIT)


def _mod_kernel(cond_ref, w_ref, b_ref, o_ref):
    c = cond_ref[...]
    s = c * jax.nn.sigmoid(c)
    o_ref[...] = _dot(s, w_ref[...]) + b_ref[...]


def _mod_call(cond, w_mod, b_mod):
    nb = 1536
    return pl.pallas_call(
        _mod_kernel,
        grid=(DEPTH, 6 * D_MODEL // nb),
        in_specs=[pl.BlockSpec((N_COND, D_MODEL), lambda l, j: (0, 0)),
                  pl.BlockSpec((None, D_MODEL, nb), lambda l, j: (l, 0, j)),
                  pl.BlockSpec((None, 1, nb), lambda l, j: (l, 0, j))],
        out_specs=pl.BlockSpec((None, N_COND, nb), lambda l, j: (l, 0, j)),
        out_shape=jax.ShapeDtypeStruct((DEPTH, N_COND, 6 * D_MODEL), F32),
        compiler_params=_cparams("arbitrary", "arbitrary"),
        name="mod_vectors",
    )(cond, w_mod, b_mod.reshape(DEPTH, 1, 6 * D_MODEL))


def _mod_row(i):
    n_ctx_tiles = N_CTX // TM
    return jnp.where(i < n_ctx_tiles, 0, 1 + (i - n_ctx_tiles) // (DEC_SEQ // TM))


def _pos_block(i):
    n_ctx_tiles = N_CTX // TM
    return jnp.where(i < n_ctx_tiles, DEC_SEQ // TM, (i - n_ctx_tiles) % (DEC_SEQ // TM))


def _inproj_kernel(x_ref, mod_ref, rb_ref, rd_ref, w_in_ref, sgu_g_ref, sgu_w_ref, sgu_bias_ref,
                   qg_ref, kvg_ref, wuq_ref,
                   oa_ref, mq_ref, ckv_ref, kr_ref, nq_ref, nk_ref, nv_ref, dq_ref, dk_ref, dv_ref):
    x = x_ref[...]
    shift = mod_ref[0, :, 0:D_MODEL]
    scale = mod_ref[0, :, D_MODEL:2 * D_MODEL]
    h = (_ln(x) * (1.0 + scale) + shift).astype(BF16)

    ya = _dot(h, w_in_ref[:, C_AU:C_CQ])
    u = _gelu(ya[:, :SGU_WIDTH])
    v = _gelu(ya[:, SGU_WIDTH:])
    vn = (_ln(v) * sgu_g_ref[...]).astype(BF16)
    group = lax.broadcasted_iota(jnp.int32, (CHUNK, SGU_WIDTH), 1) // (SGU_WIDTH // SGU_GROUPS)
    for c in range(TM // CHUNK):
        rows = slice(c * CHUNK, (c + 1) * CHUNK)
        mixed = sgu_bias_ref[...]
        for g in range(SGU_GROUPS):
            mixed = mixed + jnp.where(group == g, _dot(sgu_w_ref[g], vn[rows]), 0.0)
        oa_ref[rows, :] = (u[rows] * mixed).astype(oa_ref.dtype)

    ym = _dot(h, w_in_ref[:, C_CQ:C_NQ])
    cq = (_rms(ym[:, :MLA_Q_LORA]) * qg_ref[...]).astype(BF16)
    mq = _dot(cq, wuq_ref[...])
    for g in range(MLA_HEADS):
        lanes = slice(g * LANES, (g + 1) * LANES)
        mq_ref[:, lanes] = _rope(mq[:, lanes], rb_ref, MLA_ROPE // 2)
    ckv_ref[...] = _rms(ym[:, MLA_Q_LORA:MLA_Q_LORA + MLA_KV_LORA]) * kvg_ref[...]
    kr_ref[...] = _rope(ym[:, MLA_Q_LORA + MLA_KV_LORA:], rb_ref, MLA_ROPE // 2)

    yn = _dot(h, w_in_ref[:, C_NQ:C_DQ])
    nq_ref[...] = yn[:, 0:256]
    nk_ref[...] = yn[:, 256:512]
    nv_ref[...] = yn[:, 512:768]

    yd = _dot(h, w_in_ref[:, C_DQ:C_END])
    for g in range(4):
        lanes = slice(g * LANES, (g + 1) * LANES)
        dq_ref[:, lanes] = _rope(yd[:, g * LANES:(g + 1) * LANES], rd_ref, DIFF_QK_DIM // 2)
        dk_ref[:, lanes] = _rope(yd[:, 512 + g * LANES:512 + (g + 1) * LANES], rd_ref, DIFF_QK_DIM // 2)
    dv_ref[...] = yd[:, 1024:1536]


def _inproj_call(x, mod, rope_b, rope_d, w_in_r, sgu_g, sgu_w, sgu_bias, qg, kvg, wuq):
    tile = lambda w: pl.BlockSpec((TM, w), lambda i: (i, 0))
    full = lambda *s: pl.BlockSpec(s, lambda i: (0,) * len(s))
    widths = (SGU_WIDTH, 512, MLA_KV_LORA, LANES, 256, 256, 256, 512, 512, 512)
    dtypes = (BF16,) + (F32,) * 9
    return pl.pallas_call(
        _inproj_kernel,
        grid=(N_TOK // TM,),
        in_specs=[tile(D_MODEL),
                  pl.BlockSpec((1, 1, 6 * D_MODEL), lambda i: (_mod_row(i), 0, 0)),
                  pl.BlockSpec((3, TM, LANES), lambda i: (0, _pos_block(i), 0)),
                  pl.BlockSpec((3, TM, LANES), lambda i: (0, _pos_block(i), 0)),
                  full(D_MODEL, C_END), full(1, SGU_WIDTH), full(SGU_GROUPS, CHUNK, CHUNK),
                  full(CHUNK, SGU_WIDTH), full(1, MLA_Q_LORA), full(1, MLA_KV_LORA),
                  full(MLA_Q_LORA, MLA_HEADS * LANES)],
        out_specs=[tile(w) for w in widths],
        out_shape=[jax.ShapeDtypeStruct((N_TOK, w), dt) for w, dt in zip(widths, dtypes)],
        compiler_params=_cparams("parallel"),
        name="in_projection",
    )(x, mod, rope_b, rope_d, w_in_r, sgu_g, sgu_w, sgu_bias, qg, kvg, wuq)


def _half_mask(lo):
    lane = lax.broadcasted_iota(jnp.int32, (1, LANES), 1)
    return (lane >= lo) & (lane < lo + 64)


def _softmax_pv(scores, values, lanes):
    m = scores[0].max(axis=-1, keepdims=True)
    for s in scores[1:]:
        m = jnp.maximum(m, s.max(axis=-1, keepdims=True))
    den = None
    o = None
    for s, v in zip(scores, values):
        p = jnp.exp(s - m)
        d = p.sum(axis=-1, keepdims=True)
        den = d if den is None else den + d
        pv = _dot(p.astype(BF16), v[:, lanes])
        o = pv if o is None else o + pv
    return o / den


def _pair_attention(q, keys, vals, scale, bias_fn=None):
    outs = []
    for pair in range(2):
        lanes = slice(pair * LANES, (pair + 1) * LANES)
        qp = q[:, lanes]
        acc = None
        for sub in range(2):
            head = 2 * pair + sub
            mask = _half_mask(64 * sub)
            qm = jnp.where(mask, qp, jnp.zeros_like(qp))
            scores = [_dot_nt(qm, k[:, lanes]) * scale for k in keys]
            if bias_fn is not None:
                scores = bias_fn(head, scores)
            o = jnp.where(mask, _softmax_pv(scores, vals, lanes), 0.0)
            acc = o if acc is None else acc + o
        outs.append(acc)
    return outs


def _mla_attention(q, k_blocks, v_blocks, o_ref, rows):
    scale = (MLA_NOPE + MLA_ROPE) ** -0.5
    for pair in range(2):
        lanes = slice(pair * LANES, (pair + 1) * LANES)
        acc = None
        for sub in range(2):
            head = 2 * pair + sub
            hl = slice(head * LANES, (head + 1) * LANES)
            scores = [_dot_nt(q[:, hl], k[:, hl]) * scale for k in k_blocks]
            o = jnp.where(_half_mask(64 * sub), _softmax_pv(scores, v_blocks, lanes), 0.0)
            acc = o if acc is None else acc + o
        o_ref[rows, lanes] = acc.astype(o_ref.dtype)


def _diff_lambda(lq1, lk1, lq2, lk2, lambda_init):
    a = jnp.sum(lq1[...] * lk1[...], axis=-1, keepdims=True)
    b = jnp.sum(lq2[...] * lk2[...], axis=-1, keepdims=True)
    return jnp.exp(a) - jnp.exp(b) + lambda_init


def _diff_attention(q, k_blocks, v_blocks, lam, norm_g, lambda_init, o_ref, rows):
    scale = DIFF_QK_DIM ** -0.5
    for head in range(DIFF_HEADS):
        hl = slice(head * LANES, (head + 1) * LANES)
        qh = q[:, hl]
        probs = []
        for sub in range(2):
            qm = jnp.where(_half_mask(64 * sub), qh, jnp.zeros_like(qh))
            scores = [_dot_nt(qm, k[:, hl]) * scale for k in k_blocks]
            m = scores[0].max(axis=-1, keepdims=True)
            for s in scores[1:]:
                m = jnp.maximum(m, s.max(axis=-1, keepdims=True))
            ps = [jnp.exp(s - m) for s in scores]
            den = ps[0].sum(axis=-1, keepdims=True)
            for p in ps[1:]:
                den = den + p.sum(axis=-1, keepdims=True)
            probs.append((ps, 1.0 / den))
        o = None
        for i, v in enumerate(v_blocks):
            w = probs[0][0][i] * probs[0][1] - probs[1][0][i] * (lam * probs[1][1])
            pv = _dot(w.astype(BF16), v[:, hl])
            o = pv if o is None else o + pv
        o = _rms(o) * norm_g * (1.0 - lambda_init)
        o_ref[rows, hl] = o.astype(o_ref.dtype)


def _ctx_attn_kernel(lambda_init, mq_ref, ckv_ref, kr_ref, nq_ref, nk_ref, nv_ref, dq_ref, dk_ref, dv_ref,
                     wuk_ref, wuv_ref, lq1, lk1, lq2, lk2, dg_ref, ob_ref, oc_ref, od_ref):
    rows = slice(0, SEQ)
    ckv = ckv_ref[...].astype(BF16)
    kr = kr_ref[...]
    k_b = (_dot(ckv, wuk_ref[...]) + jnp.concatenate([kr] * MLA_HEADS, axis=1)).astype(BF16)
    v_b = _dot(ckv, wuv_ref[...]).astype(BF16)
    _mla_attention(mq_ref[...].astype(BF16), [k_b], [v_b], ob_ref, rows)

    outs = _pair_attention(nq_ref[...].astype(BF16), [nk_ref[...].astype(BF16)], [nv_ref[...].astype(BF16)],
                           NA_HEAD_DIM ** -0.5)
    for pair in range(2):
        oc_ref[:, pair * LANES:(pair + 1) * LANES] = outs[pair].astype(oc_ref.dtype)

    lam = _diff_lambda(lq1, lk1, lq2, lk2, lambda_init)
    _diff_attention(dq_ref[...].astype(BF16), [dk_ref[...].astype(BF16)], [dv_ref[...].astype(BF16)],
                    lam, dg_ref[...], lambda_init, od_ref, rows)


def _ctx_attn_call(lambda_init, acts, wuk, wuv, lams, dg):
    mq, ckv, kr, nq, nk, nv, dq, dk, dv = acts
    seq = lambda w: pl.BlockSpec((SEQ, w), lambda b: (b, 0))
    full = lambda *s: pl.BlockSpec(s, lambda b: (0,) * len(s))
    return pl.pallas_call(
        functools.partial(_ctx_attn_kernel, lambda_init),
        grid=(BATCH,),
        in_specs=[seq(512), seq(128), seq(128), seq(256), seq(256), seq(256), seq(512), seq(512), seq(512),
                  full(MLA_KV_LORA, 512), full(MLA_KV_LORA, 256)] + [full(1, DIFF_QK_DIM)] * 4
                 + [full(1, DIFF_V_DIM)],
        out_specs=[seq(256), seq(256), seq(512)],
        out_shape=[jax.ShapeDtypeStruct((N_CTX, w), BF16) for w in (256, 256, 512)],
        compiler_params=_cparams("parallel"),
        name="context_attention",
    )(mq, ckv, kr, nq, nk, nv, dq, dk, dv, wuk, wuv, *lams, dg)


def _lat_mla_kernel(mq_ref, ckv_ref, kr_ref, cckv_ref, ckr_ref, wuk_ref, wuv_ref, o_ref):
    def expand(ckv_f32, kr):
        ckv = ckv_f32.astype(BF16)
        k = (_dot(ckv, wuk_ref[...]) + jnp.concatenate([kr] * MLA_HEADS, axis=1)).astype(BF16)
        return k, _dot(ckv, wuv_ref[...]).astype(BF16)

    k_lat, v_lat = expand(ckv_ref[...], kr_ref[...])
    k_ctx, v_ctx = expand(cckv_ref[...], ckr_ref[...])
    for t in range(DEC_SEQ // Q_TILE):
        rows = slice(t * Q_TILE, (t + 1) * Q_TILE)
        _mla_attention(mq_ref[rows, :].astype(BF16), [k_lat, k_ctx], [v_lat, v_ctx], o_ref, rows)


def _lat_mla_call(l, mq, ckv, kr, cache_ckv, cache_kr_pad, wuk, wuv):
    off = N_CTX // DEC_SEQ
    seq = lambda w: pl.BlockSpec((DEC_SEQ, w), lambda b: (off + b, 0))
    cache = lambda w: pl.BlockSpec((None, None, PAST_LEN, w), lambda b: (b, l, 0, 0))
    full = lambda *s: pl.BlockSpec(s, lambda b: (0,) * len(s))
    return pl.pallas_call(
        _lat_mla_kernel,
        grid=(DEC_BATCH,),
        in_specs=[seq(512), seq(128), seq(128), cache(MLA_KV_LORA), cache(LANES),
                  full(MLA_KV_LORA, 512), full(MLA_KV_LORA, 256)],
        out_specs=pl.BlockSpec((DEC_SEQ, 256), lambda b: (b, 0)),
        out_shape=jax.ShapeDtypeStruct((N_LAT, 256), BF16),
        compiler_params=_cparams("parallel"),
        name="latent_mla_attention",
    )(mq, ckv, kr, cache_ckv, cache_kr_pad, wuk, wuv)


def _win_start(r):
    return jnp.clip(r - NA_WIN_ROWS // 2, 0, ROWS - NA_WIN_ROWS)


def _lat_na_kernel(nq_ref, nk_ref, nv_ref, ck_ref, cv_ref, bias_ref, o_ref):
    r = pl.program_id(1)
    start = pl.multiple_of(_win_start(r) * GRID_W, GRID_W)
    win = NA_WIN_ROWS * GRID_W
    k_w = nk_ref[pl.ds(start, win), :].astype(BF16)
    v_w = nv_ref[pl.ds(start, win), :].astype(BF16)
    k_c = ck_ref[...].astype(BF16)
    v_c = cv_ref[...].astype(BF16)
    q_col = lax.broadcasted_iota(jnp.int32, (GRID_W, win), 0)
    k_col = lax.broadcasted_iota(jnp.int32, (GRID_W, win), 1) % GRID_W
    c0 = jnp.clip(q_col - NA_WIN_COLS // 2, 0, GRID_W - NA_WIN_COLS)
    col_in = (k_col >= c0) & (k_col < c0 + NA_WIN_COLS)

    def bias_fn(head, scores):
        return [jnp.where(col_in, scores[0] + bias_ref[0, head], NEG_BIG), scores[1]]

    outs = _pair_attention(nq_ref[...].astype(BF16), [k_w, k_c], [v_w, v_c], NA_HEAD_DIM ** -0.5, bias_fn)
    for pair in range(2):
        o_ref[:, pair * LANES:(pair + 1) * LANES] = outs[pair].astype(o_ref.dtype)


def _lat_na_call(l, nq, nk, nv, cache_k, cache_v, bias_tab):
    off = N_CTX // DEC_SEQ
    seq = pl.BlockSpec((DEC_SEQ, 256), lambda b, r: (off + b, 0))
    cache = pl.BlockSpec((None, None, PAST_LEN, 256), lambda b, r: (b, l, 0, 0))
    return pl.pallas_call(
        _lat_na_kernel,
        grid=(DEC_BATCH, ROWS),
        in_specs=[pl.BlockSpec((GRID_W, 256), lambda b, r: (N_CTX // GRID_W + b * ROWS + r, 0)),
                  seq, seq, cache, cache,
                  pl.BlockSpec((1, NA_HEADS, GRID_W, NA_WIN_ROWS * GRID_W),
                               lambda b, r: (_win_start(r) - r + NA_WIN_ROWS - 1, 0, 0, 0))],
        out_specs=pl.BlockSpec((GRID_W, 256), lambda b, r: (b * ROWS + r, 0)),
        out_shape=jax.ShapeDtypeStruct((N_LAT, 256), BF16),
        compiler_params=_cparams("parallel", "arbitrary"),
        name="latent_neighbourhood_attention",
    )(nq, nk, nv, cache_k, cache_v, bias_tab)


def _lat_diff_kernel(lambda_init, dq_ref, dk_ref, dv_ref, ck_ref, cv_ref, lq1, lk1, lq2, lk2, dg_ref, o_ref):
    lam = _diff_lambda(lq1, lk1, lq2, lk2, lambda_init)
    k_blocks = [dk_ref[...].astype(BF16), ck_ref[...].astype(BF16)]
    v_blocks = [dv_ref[...].astype(BF16), cv_ref[...].astype(BF16)]
    for t in range(DEC_SEQ // Q_TILE):
        rows = slice(t * Q_TILE, (t + 1) * Q_TILE)
        _diff_attention(dq_ref[rows, :].astype(BF16), k_blocks, v_blocks, lam, dg_ref[...], lambda_init,
                        o_ref, rows)


def _lat_diff_call(l, lambda_init, dq, dk, dv, cache_k, cache_v, lams, dg):
    off = N_CTX // DEC_SEQ
    seq = pl.BlockSpec((DEC_SEQ, 512), lambda b: (off + b, 0))
    cache = pl.BlockSpec((None, None, PAST_LEN, 512), lambda b: (b, l, 0, 0))
    full = lambda *s: pl.BlockSpec(s, lambda b: (0,) * len(s))
    return pl.pallas_call(
        functools.partial(_lat_diff_kernel, lambda_init),
        grid=(DEC_BATCH,),
        in_specs=[seq, seq, seq, cache, cache] + [full(1, DIFF_QK_DIM)] * 4 + [full(1, DIFF_V_DIM)],
        out_specs=pl.BlockSpec((DEC_SEQ, 512), lambda b: (b, 0)),
        out_shape=jax.ShapeDtypeStruct((N_LAT, 512), BF16),
        compiler_params=_cparams("parallel"),
        name="latent_differential_attention",
    )(dq, dk, dv, cache_k, cache_v, *lams, dg)


def _merge_kernel(x_ref, mod_ref, oa_ref, ob_ref, oc_ref, od_ref, wg_ref, bg_ref,
                  wa_ref, wb_ref, wc_ref, wd_ref, wo_ref, g_ref, b_ref, x1_ref, h2t_ref):
    x = x_ref[...]
    mod = lambda k: mod_ref[0, :, k * D_MODEL:(k + 1) * D_MODEL]
    h = (_ln(x) * (1.0 + mod(1)) + mod(0)).astype(BF16)
    merged = None
    for i, (o_ref, w_ref) in enumerate(((oa_ref, wa_ref), (ob_ref, wb_ref), (oc_ref, wc_ref), (od_ref, wd_ref))):
        cols = slice(i * D_MODEL, (i + 1) * D_MODEL)
        gate = jax.nn.sigmoid(_dot(h, wg_ref[:, cols]) + bg_ref[:, cols])
        term = gate * _dot(o_ref[...], w_ref[...])
        merged = term if merged is None else merged + term
    mix = _dot(merged.astype(BF16), wo_ref[...])
    x1 = _ln(DEEPNORM_ALPHA * x + mod(2) * mix) * g_ref[...] + b_ref[...]
    x1_ref[...] = x1
    h2 = _ln(x1) * (1.0 + mod(4)) + mod(3)
    h2t_ref[...] = h2.T.astype(BF16)


def _merge_call(x, mod, oa, ob, oc, od, wg, bg, wa, wb, wc, wd, wo, g, b):
    tile = lambda w: pl.BlockSpec((TM, w), lambda i: (i, 0))
    full = lambda *s: pl.BlockSpec(s, lambda i: (0,) * len(s))
    return pl.pallas_call(
        _merge_kernel,
        grid=(N_TOK // TM,),
        in_specs=[tile(D_MODEL), pl.BlockSpec((1, 1, 6 * D_MODEL), lambda i: (_mod_row(i), 0, 0)),
                  tile(256), tile(256), tile(256), tile(512),
                  full(D_MODEL, 4 * D_MODEL), full(1, 4 * D_MODEL),
                  full(256, D_MODEL), full(256, D_MODEL), full(256, D_MODEL), full(512, D_MODEL),
                  full(D_MODEL, D_MODEL), full(1, D_MODEL), full(1, D_MODEL)],
        out_specs=[tile(D_MODEL), pl.BlockSpec((D_MODEL, TM), lambda i: (0, i))],
        out_shape=[jax.ShapeDtypeStruct((N_TOK, D_MODEL), F32), jax.ShapeDtypeStruct((D_MODEL, N_TOK), BF16)],
        compiler_params=_cparams("parallel"),
        name="branch_merge",
    )(x, mod, oa, ob, oc, od, wg, bg, wa, wb, wc, wd, wo, g, b)


KEY_MIN = -2 ** 31


def _tree_sum(terms):
    while len(terms) > 1:
        terms = [a + b for a, b in zip(terms[0::2], terms[1::2])] + ([terms[-1]] if len(terms) % 2 else [])
    return terms[0]


def _row_gather(table, idx):
    return _tree_sum([jnp.where(idx == float(k), table[k:k + 1, :], 0.0) for k in range(PEER_TOPK)])


def _sort_key(x):
    b = lax.bitcast_convert_type(x + 0.0, jnp.int32)
    return b ^ ((b >> 31) & 0x7FFFFFFF)


def _key_value(k):
    return lax.bitcast_convert_type(k ^ ((k >> 31) & 0x7FFFFFFF), F32)


def _top16(s):
    row = lax.broadcasted_iota(jnp.int32, s.shape, 0).astype(F32)
    krow = lax.broadcasted_iota(jnp.int32, (PEER_TOPK, s.shape[1]), 0)

    def body(k, carry):
        work, rank, vals = carry
        m = jnp.max(work, axis=0, keepdims=True)
        idx = jnp.min(jnp.where(work == m, row, float(PEER_N_KEYS)), axis=0, keepdims=True)
        sel = row == idx
        rank = jnp.where(sel, jnp.asarray(k, jnp.int32).astype(F32), rank)
        work = jnp.where(sel, -jnp.inf, work)
        vals = jnp.where(krow == k, m, vals)
        return work, rank, vals

    init = (s, jnp.full(s.shape, float(PEER_N_KEYS), F32), jnp.zeros((PEER_TOPK, s.shape[1]), F32))
    _, rank, vals = lax.fori_loop(0, PEER_TOPK, body, init)
    return vals, rank


def _top16_pair(s1, s2):
    krow = lax.broadcasted_iota(jnp.int32, (PEER_TOPK, LANES), 0)

    def body(k, carry):
        w1, w2, v1, v2 = carry
        code = KEY_MIN + jnp.asarray(k, jnp.int32)
        m1 = jnp.max(w1, axis=0, keepdims=True)
        m2 = jnp.max(w2, axis=0, keepdims=True)
        w1 = jnp.where(w1 == m1, code, w1)
        w2 = jnp.where(w2 == m2, code, w2)
        return w1, w2, jnp.where(krow == k, m1, v1), jnp.where(krow == k, m2, v2)

    zeros = jnp.zeros((PEER_TOPK, LANES), jnp.int32)
    w1, w2, v1, v2 = lax.fori_loop(0, PEER_TOPK, body, (_sort_key(s1), _sort_key(s2), zeros, zeros))

    def decode(w):
        taken = w < KEY_MIN + PEER_TOPK
        rank = jnp.where(taken, (w - KEY_MIN).astype(F32), float(PEER_N_KEYS))
        return rank, jnp.sum(taken.astype(F32), axis=0, keepdims=True)

    r1, c1 = decode(w1)
    r2, c2 = decode(w2)
    ties = jnp.max(jnp.maximum(jnp.abs(c1 - PEER_TOPK), jnp.abs(c2 - PEER_TOPK))) > 0.5

    def exact():
        hs1, q1 = _top16(s1)
        hs2, q2 = _top16(s2)
        return hs1, hs2, q1, q2

    return lax.cond(ties, exact, lambda: (_key_value(v1), _key_value(v2), r1, r2))


def _merge_counts(hs1, hs2):
    krow = lax.broadcasted_iota(jnp.int32, hs1.shape, 0).astype(F32)

    def body(_, carry):
        cnt, front = carry
        m = jnp.max(front, axis=0, keepdims=True)
        win = jnp.min(jnp.where(front == m, krow, float(PEER_TOPK)), axis=0, keepdims=True)
        sel = krow == win
        cnt = jnp.where(sel, cnt + 1.0, cnt)
        nxt = jnp.where(cnt < float(PEER_TOPK), hs1 + _row_gather(hs2, cnt), -jnp.inf)
        return cnt, jnp.where(sel, nxt, front)

    cnt, _ = lax.fori_loop(0, PEER_TOPK, body, (jnp.zeros(hs1.shape, F32), hs1 + hs2[0:1, :]))
    return cnt


def _router_kernel(h2t_ref, wqt_ref, keys_ref, r2_ref, e2_ref, n1_ref, e1_ref, q_scr, s_scr, hs_scr, rank1_scr):
    t = ROUTER_TILE
    q_scr[...] = _dot(wqt_ref[...], h2t_ref[...]).astype(BF16)

    def head_body(hd, _):
        base = pl.multiple_of(hd * PEER_KEY_DIM, PEER_KEY_DIM)
        s_scr[hd, 0] = _dot(keys_ref[2 * hd], q_scr[pl.ds(base, LANES), :])
        s_scr[hd, 1] = _dot(keys_ref[2 * hd + 1], q_scr[pl.ds(base + LANES, LANES), :])
        for j in range(t // LANES):
            lanes = slice(j * LANES, (j + 1) * LANES)
            hs1, hs2, rank1, rank2 = _top16_pair(s_scr[hd, 0, :, lanes], s_scr[hd, 1, :, lanes])
            hs_scr[hd, 0, :, lanes] = hs1
            hs_scr[hd, 1, :, lanes] = hs2
            rank1_scr[hd, :, lanes] = rank1
            r2_ref[hd, :, lanes] = rank2.astype(BF16)
        return 0

    lax.fori_loop(0, PEER_HEADS, head_body, 0)

    for pair in range(PEER_HEADS // 2):
        heads = (2 * pair, 2 * pair + 1)
        hs1 = jnp.concatenate([hs_scr[h, 0] for h in heads], axis=1)
        hs2 = jnp.concatenate([hs_scr[h, 1] for h in heads], axis=1)
        cnt = _merge_counts(hs1, hs2)
        e1r = jnp.exp(hs1 - hs1[0:1, :])
        e2r = jnp.exp(hs2 - hs2[0:1, :])
        prefix = _tree_sum([jnp.where(cnt > float(kb), e2r[kb:kb + 1, :], 0.0) for kb in range(PEER_TOPK)])
        inv_z = 1.0 / jnp.sum(e1r * prefix, axis=0, keepdims=True)
        for i, h in enumerate(heads):
            lanes = slice(i * t, (i + 1) * t)
            e2_ref[h] = (jnp.exp(s_scr[h, 1] - hs2[0:1, lanes]) * inv_z[:, lanes]).astype(BF16)
            e1_ref[h] = 0.5 * jnp.exp(s_scr[h, 0] - hs1[0:1, lanes])
            n1_ref[h] = _row_gather(cnt[:, lanes], rank1_scr[h])


def _router_call(h2t, wqt, keys):
    t = ROUTER_TILE
    out = pl.BlockSpec((PEER_HEADS, PEER_N_KEYS, t), lambda i: (0, 0, i))
    shape = (PEER_HEADS, PEER_N_KEYS, N_TOK)
    return pl.pallas_call(
        _router_kernel,
        grid=(N_TOK // t,),
        in_specs=[pl.BlockSpec((D_MODEL, t), lambda i: (0, i)),
                  pl.BlockSpec((PEER_HEADS * PEER_KEY_DIM, D_MODEL), lambda i: (0, 0)),
                  pl.BlockSpec((2 * PEER_HEADS, PEER_N_KEYS, PEER_KEY_DIM // 2), lambda i: (0, 0, 0))],
        out_specs=[out] * 4,
        out_shape=[jax.ShapeDtypeStruct(shape, BF16), jax.ShapeDtypeStruct(shape, BF16),
                   jax.ShapeDtypeStruct(shape, F32), jax.ShapeDtypeStruct(shape, F32)],
        scratch_shapes=[pltpu.VMEM((PEER_HEADS * PEER_KEY_DIM, t), BF16),
                        pltpu.VMEM((PEER_HEADS, 2, PEER_N_KEYS, t), F32),
                        pltpu.VMEM((PEER_HEADS, 2, PEER_TOPK, t), F32),
                        pltpu.VMEM((PEER_HEADS, PEER_N_KEYS, t), F32)],
        compiler_params=_cparams("parallel"),
        name="peer_retrieval",
    )(h2t, wqt, keys)


def _gated_activations(ht_ref, w_ref, r2_ref, e2_ref, n1_ref, e1_ref):
    for i in range(KEYS_PER_BLOCK):
        rows = slice(i * PEER_N_KEYS, (i + 1) * PEER_N_KEYS)
        for j in range(PEER_TILE // GATE_LANES):
            lanes = slice(j * GATE_LANES, (j + 1) * GATE_LANES)
            gate = jnp.zeros((PEER_N_KEYS, GATE_LANES), BF16)
            for hd in range(PEER_HEADS):
                n_row = n1_ref[hd, i:i + 1, lanes].astype(BF16)
                c_row = e1_ref[hd, i:i + 1, lanes].astype(BF16)
                live = jnp.where(r2_ref[hd, :, lanes] < n_row, e2_ref[hd, :, lanes], jnp.zeros((), BF16))
                gate = gate + live * c_row
            x = ht_ref[rows, lanes]
            act = x * (1.0 + lax.erf(x * (1.0 / math.sqrt(2.0))))
            w_ref[rows, lanes] = act.astype(BF16) * gate


def _peer_kernel(h2t_ref, u_ref, vt_ref, r2_ref, e2_ref, n1_ref, e1_ref,
                 x1_ref, mod_ref, g_ref, b_ref, o_ref, acc_ref, ht_ref, w_ref):
    e = pl.program_id(1)

    @pl.when(e == 0)
    def _():
        acc_ref[...] = jnp.zeros_like(acc_ref)

    ht_ref[...] = _dot(u_ref[...], h2t_ref[...])
    _gated_activations(ht_ref, w_ref, r2_ref, e2_ref, n1_ref, e1_ref)
    acc_ref[...] += _dot(vt_ref[...], w_ref[...])

    @pl.when(e == pl.num_programs(1) - 1)
    def _():
        ffn = acc_ref[...].T
        g2 = mod_ref[0, :, 5 * D_MODEL:6 * D_MODEL]
        o_ref[...] = _ln(DEEPNORM_ALPHA * x1_ref[...] + g2 * ffn) * g_ref[...] + b_ref[...]


def _peer_mod_row(i):
    n_ctx_tiles = N_CTX // PEER_TILE
    return jnp.where(i < n_ctx_tiles, 0, 1 + (i - n_ctx_tiles) // (DEC_SEQ // PEER_TILE))


def _peer_call(h2t, u, vt, r2, e2, n1, e1, x1, mod, g, b):
    t = PEER_TILE
    n_blocks = PEER_N_KEYS * PEER_N_KEYS // EXPERT_BLOCK
    gates = pl.BlockSpec((PEER_HEADS, PEER_N_KEYS, t), lambda i, g: (0, 0, i))
    keys = pl.BlockSpec((PEER_HEADS, KEYS_PER_BLOCK, t), lambda i, g: (0, g, i))
    return pl.pallas_call(
        _peer_kernel,
        grid=(N_TOK // t, n_blocks),
        in_specs=[pl.BlockSpec((D_MODEL, t), lambda i, g: (0, i)),
                  pl.BlockSpec((EXPERT_BLOCK, D_MODEL), lambda i, g: (g, 0)),
                  pl.BlockSpec((D_MODEL, EXPERT_BLOCK), lambda i, g: (0, g)),
                  gates, gates, keys, keys,
                  pl.BlockSpec((t, D_MODEL), lambda i, g: (i, 0)),
                  pl.BlockSpec((1, 1, 6 * D_MODEL), lambda i, g: (_peer_mod_row(i), 0, 0)),
                  pl.BlockSpec((1, D_MODEL), lambda i, g: (0, 0)),
                  pl.BlockSpec((1, D_MODEL), lambda i, g: (0, 0))],
        out_specs=pl.BlockSpec((t, D_MODEL), lambda i, g: (i, 0)),
        out_shape=jax.ShapeDtypeStruct((N_TOK, D_MODEL), F32),
        scratch_shapes=[pltpu.VMEM((D_MODEL, t), F32), pltpu.VMEM((EXPERT_BLOCK, t), F32),
                        pltpu.VMEM((EXPERT_BLOCK, t), BF16)],
        compiler_params=_cparams("parallel", "arbitrary"),
        name="peer_dense",
    )(h2t, u, vt, r2, e2, n1, e1, x1, mod, g, b)


def _rope_tables():
    t = jnp.arange(DEC_SEQ)
    row = (t // GRID_W).astype(F32)
    col = (t % GRID_W).astype(F32)

    def angles(rot_dim):
        n_freq = rot_dim // 4
        inv_freq = ROPE_THETA ** (-jnp.arange(n_freq, dtype=F32) / n_freq)
        return jnp.concatenate([row[:, None] * inv_freq, col[:, None] * inv_freq], axis=-1)

    def pack(cos_l, sa_l, sb_l):
        tab = jnp.stack([cos_l, sa_l, sb_l])
        ident = jnp.stack([jnp.ones((TM, LANES), F32), jnp.zeros((TM, LANES), F32), jnp.zeros((TM, LANES), F32)])
        return jnp.concatenate([tab, ident], axis=1)

    ang_b = angles(MLA_ROPE)
    cb, sb = jnp.cos(ang_b), jnp.sin(ang_b)
    one, zero = jnp.ones((DEC_SEQ, 64), F32), jnp.zeros((DEC_SEQ, 64), F32)
    z16, z32 = jnp.zeros((DEC_SEQ, 16), F32), jnp.zeros((DEC_SEQ, 32), F32)
    rope_b = pack(jnp.concatenate([one, cb, cb, jnp.ones((DEC_SEQ, 32), F32)], axis=1),
                  jnp.concatenate([zero, -sb, z16, z32], axis=1),
                  jnp.concatenate([zero, z16, sb, z32], axis=1))
    ang_d = angles(DIFF_QK_DIM)
    cd, sd = jnp.cos(ang_d), jnp.sin(ang_d)
    rope_d = pack(jnp.concatenate([cd, cd, cd, cd], axis=1),
                  jnp.concatenate([-sd, z32, -sd, z32], axis=1),
                  jnp.concatenate([z32, sd, z32, sd], axis=1))
    return rope_b, rope_d


def _na_bias_table(rpb):
    col = jnp.arange(GRID_W)
    dc = jnp.clip(col[None, :] - col[:, None], -(NA_WIN_COLS - 1), NA_WIN_COLS - 1) + NA_WIN_COLS - 1
    rpb_cols = rpb[:, :, dc]
    tabs = [rpb_cols[:, off:off + NA_WIN_ROWS].transpose(0, 2, 1, 3).reshape(NA_HEADS, GRID_W, NA_WIN_ROWS * GRID_W)
            for off in range(NA_WIN_ROWS)]
    return jnp.stack(tabs)


def _pad_cols(w, left, right):
    return jnp.pad(w, ((0, 0), (left, right)))


def kernel(x_prompt, x_sample, cache_mla_ckv, cache_mla_krope, cache_na_k, cache_na_v, cache_diff_k, cache_diff_v, c, c_ctx, w_mod, b_mod, w_in, sgu_norm_g, sgu_w, sgu_b, mla_q_norm_g, mla_w_uq, mla_kv_norm_g, mla_w_ukv, na_rpb, diff_lambda_q1, diff_lambda_k1, diff_lambda_q2, diff_lambda_k2, diff_norm_g, w_branch_a, w_branch_b, w_branch_c, w_branch_d, w_gate, b_gate, w_out, ln1_g, ln1_b, peer_w_q, peer_subkeys, peer_u, peer_v, ln2_g, ln2_b):
    x = jnp.concatenate([x_prompt.reshape(N_CTX, D_MODEL), x_sample.reshape(N_LAT, D_MODEL)], axis=0)
    cond = jnp.concatenate([c_ctx[None], c, jnp.zeros((N_COND - 1 - DEC_BATCH, D_MODEL), F32)], axis=0)
    mod_all = _mod_call(cond, w_mod, b_mod)
    rope_b, rope_d = _rope_tables()
    cache_kr_pad = jnp.pad(cache_mla_krope, ((0, 0), (0, 0), (0, 0), (MLA_NOPE, LANES - MLA_NOPE - MLA_ROPE)))
    cache_na_k2 = cache_na_k.reshape(DEC_BATCH, DEPTH, PAST_LEN, 256)
    cache_na_v2 = cache_na_v.reshape(DEC_BATCH, DEPTH, PAST_LEN, 256)
    cache_diff_k2 = cache_diff_k.reshape(DEC_BATCH, DEPTH, PAST_LEN, 512)
    cache_diff_v2 = cache_diff_v.reshape(DEC_BATCH, DEPTH, PAST_LEN, 512)

    ctx_out = []
    for l in range(DEPTH):
        lambda_init = 0.8 - 0.6 * math.exp(-0.3 * l)
        mod = mod_all[l].reshape(N_COND, 1, 6 * D_MODEL)

        wi = w_in[l]
        kr_cols = _pad_cols(wi[:, C_KR:C_KR + MLA_ROPE], MLA_NOPE, LANES - MLA_NOPE - MLA_ROPE)
        w_in_r = jnp.concatenate([wi[:, :C_KR], kr_cols, wi[:, C_KR + MLA_ROPE:]], axis=1).astype(BF16)
        wuq = mla_w_uq[l].reshape(MLA_Q_LORA, MLA_HEADS, MLA_NOPE + MLA_ROPE)
        wuq = jnp.pad(wuq, ((0, 0), (0, 0), (0, LANES - MLA_NOPE - MLA_ROPE))).reshape(MLA_Q_LORA, -1).astype(BF16)
        wukv = mla_w_ukv[l].reshape(MLA_KV_LORA, MLA_HEADS, MLA_NOPE + MLA_V)
        wuk = jnp.pad(wukv[:, :, :MLA_NOPE], ((0, 0), (0, 0), (0, LANES - MLA_NOPE))).reshape(MLA_KV_LORA, -1)
        wuk = wuk.astype(BF16)
        wuv = wukv[:, :, MLA_NOPE:].reshape(MLA_KV_LORA, -1).astype(BF16)
        sgu_bias = jnp.repeat(sgu_b[l].T, SGU_WIDTH // SGU_GROUPS, axis=1)
        lams = [p[l].reshape(1, DIFF_QK_DIM) for p in (diff_lambda_q1, diff_lambda_k1, diff_lambda_q2, diff_lambda_k2)]
        dg = diff_norm_g[l].reshape(1, DIFF_V_DIM)

        oa, mq, ckv, kr, nq, nk, nv, dq, dk, dv = _inproj_call(
            x, mod, rope_b, rope_d, w_in_r, sgu_norm_g[l].reshape(1, -1), sgu_w[l].astype(BF16), sgu_bias,
            mla_q_norm_g[l].reshape(1, -1), mla_kv_norm_g[l].reshape(1, -1), wuq)

        ob_c, oc_c, od_c = _ctx_attn_call(lambda_init, (mq, ckv, kr, nq, nk, nv, dq, dk, dv), wuk, wuv, lams, dg)
        ob_l = _lat_mla_call(l, mq, ckv, kr, cache_mla_ckv, cache_kr_pad, wuk, wuv)
        oc_l = _lat_na_call(l, nq, nk, nv, cache_na_k2, cache_na_v2, _na_bias_table(na_rpb[l]))
        od_l = _lat_diff_call(l, lambda_init, dq, dk, dv, cache_diff_k2, cache_diff_v2, lams, dg)
        ob = jnp.concatenate([ob_c, ob_l], axis=0)
        oc = jnp.concatenate([oc_c, oc_l], axis=0)
        od = jnp.concatenate([od_c, od_l], axis=0)

        x1, h2t = _merge_call(
            x, mod, oa, ob, oc, od, w_gate[l].astype(BF16), b_gate[l].reshape(1, -1),
            w_branch_a[l].astype(BF16), w_branch_b[l].astype(BF16), w_branch_c[l].astype(BF16),
            w_branch_d[l].astype(BF16), w_out[l].astype(BF16), ln1_g[l].reshape(1, -1), ln1_b[l].reshape(1, -1))

        keys = peer_subkeys[l].reshape(2 * PEER_HEADS, PEER_N_KEYS, PEER_KEY_DIM // 2).astype(BF16)
        r2, e2, n1, e1 = _router_call(h2t, peer_w_q[l].T.astype(BF16), keys)
        x = _peer_call(h2t, peer_u[l].astype(BF16), peer_v[l].T.astype(BF16), r2, e2, n1, e1, x1, mod,
                       ln2_g[l].reshape(1, -1), ln2_b[l].reshape(1, -1))

        ctx_out.append((ckv[:N_CTX].reshape(BATCH, SEQ, MLA_KV_LORA),
                        kr[:N_CTX, MLA_NOPE:MLA_NOPE + MLA_ROPE].reshape(BATCH, SEQ, MLA_ROPE),
                        nk[:N_CTX].reshape(BATCH, SEQ, NA_HEADS, NA_HEAD_DIM),
                        nv[:N_CTX].reshape(BATCH, SEQ, NA_HEADS, NA_HEAD_DIM),
                        dk[:N_CTX].reshape(BATCH, SEQ, DIFF_HEADS, 2 * DIFF_QK_DIM),
                        dv[:N_CTX].reshape(BATCH, SEQ, DIFF_HEADS, DIFF_V_DIM)))

    y_prompt = x[:N_CTX].reshape(BATCH, SEQ, D_MODEL)
    y_sample = x[N_CTX:].reshape(DEC_BATCH, DEC_SEQ, D_MODEL)
    new = [jnp.stack([t[k] for t in ctx_out], axis=1) for k in range(6)]
    return (y_prompt, y_sample, *new)
```

```python
import functools
import math

import jax
import jax.numpy as jnp
from jax import lax
from jax.experimental import pallas as pl
from jax.experimental.pallas import tpu as pltpu

F32 = jnp.float32
BF16 = jnp.bfloat16

D_MODEL = 1024
BATCH = 32
SEQ = 256
DEPTH = 2
DEC_BATCH = 8
DEC_SEQ = 1024
PAST_LEN = 512
GRID_W = 64
CHUNK = 128
SGU_GROUPS = 4
SGU_WIDTH = 256
MLA_HEADS = 4
MLA_Q_LORA = 256
MLA_KV_LORA = 128
MLA_NOPE = 64
MLA_ROPE = 32
MLA_V = 64
NA_HEADS = 4
NA_HEAD_DIM = 64
NA_WIN_ROWS = 8
NA_WIN_COLS = 16
DIFF_HEADS = 4
DIFF_QK_DIM = 64
DIFF_V_DIM = 128
N_BRANCHES = 4
PEER_HEADS = 8
PEER_N_KEYS = 128
PEER_KEY_DIM = 256
PEER_TOPK = 16
ROPE_THETA = 10000.0
LN_EPS = 1e-6
NEG_BIG = -1e30
DEEPNORM_ALPHA = (2 * DEPTH) ** 0.25

LANES = 128
SUBLANES = 8
N_CTX = BATCH * SEQ
N_LAT = DEC_BATCH * DEC_SEQ
N_TOK = N_CTX + N_LAT
N_COND = 16
TM = 512
ROWS = DEC_SEQ // GRID_W
Q_TILE = 256
ROUTER_TILE = 256
PEER_TILE = 512
EXPERT_BLOCK = 1024
KEYS_PER_BLOCK = EXPERT_BLOCK // PEER_N_KEYS
GATE_LANES = 256
VMEM_LIMIT = 56 * 1024 * 1024

C_AU, C_AV, C_CQ, C_CKV, C_KR = 0, 256, 512, 768, 896
C_NQ, C_NK, C_NV, C_DQ, C_DK, C_DV, C_END = 1024, 1280, 1536, 1792, 2304, 2816, 3328


def _ln(x):
    mu = jnp.mean(x, axis=-1, keepdims=True)
    xc = x - mu
    var = jnp.mean(xc * xc, axis=-1, keepdims=True)
    return xc * lax.rsqrt(var + LN_EPS)


def _rms(x):
    return x * lax.rsqrt(jnp.mean(x * x, axis=-1, keepdims=True) + LN_EPS)


def _gelu(x):
    return 0.5 * x * (1.0 + lax.erf(x * (1.0 / math.sqrt(2.0))))


def _dot(a, b):
    return jnp.dot(a, b, preferred_element_type=F32)


def _dot_nt(a, b):
    return lax.dot_general(a, b, (((1,), (1,)), ((), ())), preferred_element_type=F32)


def _rope(x, tab_ref, half):
    return (x * tab_ref[0] + pltpu.roll(x, LANES - half, 1) * tab_ref[1] + pltpu.roll(x, half, 1) * tab_ref[2])


def _cparams(*sem):
    return pltpu.CompilerParams(dimension_semantics=sem, vmem_limit_bytes=VMEM_LIMIT)


def _mod_kernel(cond_ref, w_ref, b_ref, o_ref):
    c = cond_ref[...]
    s = c * jax.nn.sigmoid(c)
    o_ref[...] = _dot(s, w_ref[...]) + b_ref[...]


def _mod_call(cond, w_mod, b_mod):
    nb = 1536
    return pl.pallas_call(
        _mod_kernel,
        grid=(DEPTH, 6 * D_MODEL // nb),
        in_specs=[pl.BlockSpec((N_COND, D_MODEL), lambda l, j: (0, 0)),
                  pl.BlockSpec((None, D_MODEL, nb), lambda l, j: (l, 0, j)),
                  pl.BlockSpec((None, 1, nb), lambda l, j: (l, 0, j))],
        out_specs=pl.BlockSpec((None, N_COND, nb), lambda l, j: (l, 0, j)),
        out_shape=jax.ShapeDtypeStruct((DEPTH, N_COND, 6 * D_MODEL), F32),
        compiler_params=_cparams("arbitrary", "arbitrary"),
        name="mod_vectors",
    )(cond, w_mod, b_mod.reshape(DEPTH, 1, 6 * D_MODEL))


def _mod_row(i):
    n_ctx_tiles = N_CTX // TM
    return jnp.where(i < n_ctx_tiles, 0, 1 + (i - n_ctx_tiles) // (DEC_SEQ // TM))


def _pos_block(i):
    n_ctx_tiles = N_CTX // TM
    return jnp.where(i < n_ctx_tiles, DEC_SEQ // TM, (i - n_ctx_tiles) % (DEC_SEQ // TM))


def _inproj_kernel(x_ref, mod_ref, rb_ref, rd_ref, w_in_ref, sgu_g_ref, sgu_w_ref, sgu_bias_ref,
                   qg_ref, kvg_ref, wuq_ref,
                   oa_ref, mq_ref, ckv_ref, kr_ref, nq_ref, nk_ref, nv_ref, dq_ref, dk_ref, dv_ref):
    x = x_ref[...]
    shift = mod_ref[0, :, 0:D_MODEL]
    scale = mod_ref[0, :, D_MODEL:2 * D_MODEL]
    h = (_ln(x) * (1.0 + scale) + shift).astype(BF16)

    ya = _dot(h, w_in_ref[:, C_AU:C_CQ])
    u = _gelu(ya[:, :SGU_WIDTH])
    v = _gelu(ya[:, SGU_WIDTH:])
    vn = (_ln(v) * sgu_g_ref[...]).astype(BF16)
    group = lax.broadcasted_iota(jnp.int32, (CHUNK, SGU_WIDTH), 1) // (SGU_WIDTH // SGU_GROUPS)
    for c in range(TM // CHUNK):
        rows = slice(c * CHUNK, (c + 1) * CHUNK)
        mixed = sgu_bias_ref[...]
        for g in range(SGU_GROUPS):
            mixed = mixed + jnp.where(group == g, _dot(sgu_w_ref[g], vn[rows]), 0.0)
        oa_ref[rows, :] = (u[rows] * mixed).astype(oa_ref.dtype)

    ym = _dot(h, w_in_ref[:, C_CQ:C_NQ])
    cq = (_rms(ym[:, :MLA_Q_LORA]) * qg_ref[...]).astype(BF16)
    mq = _dot(cq, wuq_ref[...])
    for g in range(MLA_HEADS):
        lanes = slice(g * LANES, (g + 1) * LANES)
        mq_ref[:, lanes] = _rope(mq[:, lanes], rb_ref, MLA_ROPE // 2)
    ckv_ref[...] = _rms(ym[:, MLA_Q_LORA:MLA_Q_LORA + MLA_KV_LORA]) * kvg_ref[...]
    kr_ref[...] = _rope(ym[:, MLA_Q_LORA + MLA_KV_LORA:], rb_ref, MLA_ROPE // 2)

    yn = _dot(h, w_in_ref[:, C_NQ:C_DQ])
    nq_ref[...] = yn[:, 0:256]
    nk_ref[...] = yn[:, 256:512]
    nv_ref[...] = yn[:, 512:768]

    yd = _dot(h, w_in_ref[:, C_DQ:C_END])
    for g in range(4):
        lanes = slice(g * LANES, (g + 1) * LANES)
        dq_ref[:, lanes] = _rope(yd[:, g * LANES:(g + 1) * LANES], rd_ref, DIFF_QK_DIM // 2)
        dk_ref[:, lanes] = _rope(yd[:, 512 + g * LANES:512 + (g + 1) * LANES], rd_ref, DIFF_QK_DIM // 2)
    dv_ref[...] = yd[:, 1024:1536]


def _inproj_call(x, mod, rope_b, rope_d, w_in_r, sgu_g, sgu_w, sgu_bias, qg, kvg, wuq):
    tile = lambda w: pl.BlockSpec((TM, w), lambda i: (i, 0))
    full = lambda *s: pl.BlockSpec(s, lambda i: (0,) * len(s))
    widths = (SGU_WIDTH, 512, MLA_KV_LORA, LANES, 256, 256, 256, 512, 512, 512)
    dtypes = (BF16,) + (F32,) * 9
    return pl.pallas_call(
        _inproj_kernel,
        grid=(N_TOK // TM,),
        in_specs=[tile(D_MODEL),
                  pl.BlockSpec((1, 1, 6 * D_MODEL), lambda i: (_mod_row(i), 0, 0)),
                  pl.BlockSpec((3, TM, LANES), lambda i: (0, _pos_block(i), 0)),
                  pl.BlockSpec((3, TM, LANES), lambda i: (0, _pos_block(i), 0)),
                  full(D_MODEL, C_END), full(1, SGU_WIDTH), full(SGU_GROUPS, CHUNK, CHUNK),
                  full(CHUNK, SGU_WIDTH), full(1, MLA_Q_LORA), full(1, MLA_KV_LORA),
                  full(MLA_Q_LORA, MLA_HEADS * LANES)],
        out_specs=[tile(w) for w in widths],
        out_shape=[jax.ShapeDtypeStruct((N_TOK, w), dt) for w, dt in zip(widths, dtypes)],
        compiler_params=_cparams("parallel"),
        name="in_projection",
    )(x, mod, rope_b, rope_d, w_in_r, sgu_g, sgu_w, sgu_bias, qg, kvg, wuq)


def _half_mask(lo):
    lane = lax.broadcasted_iota(jnp.int32, (1, LANES), 1)
    return (lane >= lo) & (lane < lo + 64)


def _softmax_pv(scores, values, lanes):
    m = scores[0].max(axis=-1, keepdims=True)
    for s in scores[1:]:
        m = jnp.maximum(m, s.max(axis=-1, keepdims=True))
    den = None
    o = None
    for s, v in zip(scores, values):
        p = jnp.exp(s - m)
        d = p.sum(axis=-1, keepdims=True)
        den = d if den is None else den + d
        pv = _dot(p.astype(BF16), v[:, lanes])
        o = pv if o is None else o + pv
    return o / den


def _pair_attention(q, keys, vals, scale, bias_fn=None):
    outs = []
    for pair in range(2):
        lanes = slice(pair * LANES, (pair + 1) * LANES)
        qp = q[:, lanes]
        acc = None
        for sub in range(2):
            head = 2 * pair + sub
            mask = _half_mask(64 * sub)
            qm = jnp.where(mask, qp, jnp.zeros_like(qp))
            scores = [_dot_nt(qm, k[:, lanes]) * scale for k in keys]
            if bias_fn is not None:
                scores = bias_fn(head, scores)
            o = jnp.where(mask, _softmax_pv(scores, vals, lanes), 0.0)
            acc = o if acc is None else acc + o
        outs.append(acc)
    return outs


def _mla_attention(q, k_blocks, v_blocks, o_ref, rows):
    scale = (MLA_NOPE + MLA_ROPE) ** -0.5
    for pair in range(2):
        lanes = slice(pair * LANES, (pair + 1) * LANES)
        acc = None
        for sub in range(2):
            head = 2 * pair + sub
            hl = slice(head * LANES, (head + 1) * LANES)
            scores = [_dot_nt(q[:, hl], k[:, hl]) * scale for k in k_blocks]
            o = jnp.where(_half_mask(64 * sub), _softmax_pv(scores, v_blocks, lanes), 0.0)
            acc = o if acc is None else acc + o
        o_ref[rows, lanes] = acc.astype(o_ref.dtype)


def _diff_lambda(lq1, lk1, lq2, lk2, lambda_init):
    a = jnp.sum(lq1[...] * lk1[...], axis=-1, keepdims=True)
    b = jnp.sum(lq2[...] * lk2[...], axis=-1, keepdims=True)
    return jnp.exp(a) - jnp.exp(b) + lambda_init


def _diff_attention(q, k_blocks, v_blocks, lam, norm_g, lambda_init, o_ref, rows):
    scale = DIFF_QK_DIM ** -0.5
    for head in range(DIFF_HEADS):
        hl = slice(head * LANES, (head + 1) * LANES)
        qh = q[:, hl]
        probs = []
        for sub in range(2):
            qm = jnp.where(_half_mask(64 * sub), qh, jnp.zeros_like(qh))
            scores = [_dot_nt(qm, k[:, hl]) * scale for k in k_blocks]
            m = scores[0].max(axis=-1, keepdims=True)
            for s in scores[1:]:
                m = jnp.maximum(m, s.max(axis=-1, keepdims=True))
            ps = [jnp.exp(s - m) for s in scores]
            den = ps[0].sum(axis=-1, keepdims=True)
            for p in ps[1:]:
                den = den + p.sum(axis=-1, keepdims=True)
            probs.append((ps, 1.0 / den))
        o = None
        for i, v in enumerate(v_blocks):
            w = probs[0][0][i] * probs[0][1] - probs[1][0][i] * (lam * probs[1][1])
            pv = _dot(w.astype(BF16), v[:, hl])
            o = pv if o is None else o + pv
        o = _rms(o) * norm_g * (1.0 - lambda_init)
        o_ref[rows, hl] = o.astype(o_ref.dtype)


def _ctx_attn_kernel(lambda_init, mq_ref, ckv_ref, kr_ref, nq_ref, nk_ref, nv_ref, dq_ref, dk_ref, dv_ref,
                     wuk_ref, wuv_ref, lq1, lk1, lq2, lk2, dg_ref, ob_ref, oc_ref, od_ref):
    rows = slice(0, SEQ)
    ckv = ckv_ref[...].astype(BF16)
    kr = kr_ref[...]
    k_b = (_dot(ckv, wuk_ref[...]) + jnp.concatenate([kr] * MLA_HEADS, axis=1)).astype(BF16)
    v_b = _dot(ckv, wuv_ref[...]).astype(BF16)
    _mla_attention(mq_ref[...].astype(BF16), [k_b], [v_b], ob_ref, rows)

    outs = _pair_attention(nq_ref[...].astype(BF16), [nk_ref[...].astype(BF16)], [nv_ref[...].astype(BF16)],
                           NA_HEAD_DIM ** -0.5)
    for pair in range(2):
        oc_ref[:, pair * LANES:(pair + 1) * LANES] = outs[pair].astype(oc_ref.dtype)

    lam = _diff_lambda(lq1, lk1, lq2, lk2, lambda_init)
    _diff_attention(dq_ref[...].astype(BF16), [dk_ref[...].astype(BF16)], [dv_ref[...].astype(BF16)],
                    lam, dg_ref[...], lambda_init, od_ref, rows)


def _ctx_attn_call(lambda_init, acts, wuk, wuv, lams, dg):
    mq, ckv, kr, nq, nk, nv, dq, dk, dv = acts
    seq = lambda w: pl.BlockSpec((SEQ, w), lambda b: (b, 0))
    full = lambda *s: pl.BlockSpec(s, lambda b: (0,) * len(s))
    return pl.pallas_call(
        functools.partial(_ctx_attn_kernel, lambda_init),
        grid=(BATCH,),
        in_specs=[seq(512), seq(128), seq(128), seq(256), seq(256), seq(256), seq(512), seq(512), seq(512),
                  full(MLA_KV_LORA, 512), full(MLA_KV_LORA, 256)] + [full(1, DIFF_QK_DIM)] * 4
                 + [full(1, DIFF_V_DIM)],
        out_specs=[seq(256), seq(256), seq(512)],
        out_shape=[jax.ShapeDtypeStruct((N_CTX, w), BF16) for w in (256, 256, 512)],
        compiler_params=_cparams("parallel"),
        name="context_attention",
    )(mq, ckv, kr, nq, nk, nv, dq, dk, dv, wuk, wuv, *lams, dg)


def _lat_mla_kernel(mq_ref, ckv_ref, kr_ref, cckv_ref, ckr_ref, wuk_ref, wuv_ref, o_ref):
    def expand(ckv_f32, kr):
        ckv = ckv_f32.astype(BF16)
        k = (_dot(ckv, wuk_ref[...]) + jnp.concatenate([kr] * MLA_HEADS, axis=1)).astype(BF16)
        return k, _dot(ckv, wuv_ref[...]).astype(BF16)

    k_lat, v_lat = expand(ckv_ref[...], kr_ref[...])
    k_ctx, v_ctx = expand(cckv_ref[...], ckr_ref[...])
    for t in range(DEC_SEQ // Q_TILE):
        rows = slice(t * Q_TILE, (t + 1) * Q_TILE)
        _mla_attention(mq_ref[rows, :].astype(BF16), [k_lat, k_ctx], [v_lat, v_ctx], o_ref, rows)


def _lat_mla_call(l, mq, ckv, kr, cache_ckv, cache_kr_pad, wuk, wuv):
    off = N_CTX // DEC_SEQ
    seq = lambda w: pl.BlockSpec((DEC_SEQ, w), lambda b: (off + b, 0))
    cache = lambda w: pl.BlockSpec((None, None, PAST_LEN, w), lambda b: (b, l, 0, 0))
    full = lambda *s: pl.BlockSpec(s, lambda b: (0,) * len(s))
    return pl.pallas_call(
        _lat_mla_kernel,
        grid=(DEC_BATCH,),
        in_specs=[seq(512), seq(128), seq(128), cache(MLA_KV_LORA), cache(LANES),
                  full(MLA_KV_LORA, 512), full(MLA_KV_LORA, 256)],
        out_specs=pl.BlockSpec((DEC_SEQ, 256), lambda b: (b, 0)),
        out_shape=jax.ShapeDtypeStruct((N_LAT, 256), BF16),
        compiler_params=_cparams("parallel"),
        name="latent_mla_attention",
    )(mq, ckv, kr, cache_ckv, cache_kr_pad, wuk, wuv)


def _win_start(r):
    return jnp.clip(r - NA_WIN_ROWS // 2, 0, ROWS - NA_WIN_ROWS)


def _lat_na_kernel(nq_ref, nk_ref, nv_ref, ck_ref, cv_ref, bias_ref, o_ref):
    r = pl.program_id(1)
    start = pl.multiple_of(_win_start(r) * GRID_W, GRID_W)
    win = NA_WIN_ROWS * GRID_W
    k_w = nk_ref[pl.ds(start, win), :].astype(BF16)
    v_w = nv_ref[pl.ds(start, win), :].astype(BF16)
    k_c = ck_ref[...].astype(BF16)
    v_c = cv_ref[...].astype(BF16)
    q_col = lax.broadcasted_iota(jnp.int32, (GRID_W, win), 0)
    k_col = lax.broadcasted_iota(jnp.int32, (GRID_W, win), 1) % GRID_W
    c0 = jnp.clip(q_col - NA_WIN_COLS // 2, 0, GRID_W - NA_WIN_COLS)
    col_in = (k_col >= c0) & (k_col < c0 + NA_WIN_COLS)

    def bias_fn(head, scores):
        return [jnp.where(col_in, scores[0] + bias_ref[0, head], NEG_BIG), scores[1]]

    outs = _pair_attention(nq_ref[...].astype(BF16), [k_w, k_c], [v_w, v_c], NA_HEAD_DIM ** -0.5, bias_fn)
    for pair in range(2):
        o_ref[:, pair * LANES:(pair + 1) * LANES] = outs[pair].astype(o_ref.dtype)


def _lat_na_call(l, nq, nk, nv, cache_k, cache_v, bias_tab):
    off = N_CTX // DEC_SEQ
    seq = pl.BlockSpec((DEC_SEQ, 256), lambda b, r: (off + b, 0))
    cache = pl.BlockSpec((None, None, PAST_LEN, 256), lambda b, r: (b, l, 0, 0))
    return pl.pallas_call(
        _lat_na_kernel,
        grid=(DEC_BATCH, ROWS),
        in_specs=[pl.BlockSpec((GRID_W, 256), lambda b, r: (N_CTX // GRID_W + b * ROWS + r, 0)),
                  seq, seq, cache, cache,
                  pl.BlockSpec((1, NA_HEADS, GRID_W, NA_WIN_ROWS * GRID_W),
                               lambda b, r: (_win_start(r) - r + NA_WIN_ROWS - 1, 0, 0, 0))],
        out_specs=pl.BlockSpec((GRID_W, 256), lambda b, r: (b * ROWS + r, 0)),
        out_shape=jax.ShapeDtypeStruct((N_LAT, 256), BF16),
        compiler_params=_cparams("parallel", "arbitrary"),
        name="latent_neighbourhood_attention",
    )(nq, nk, nv, cache_k, cache_v, bias_tab)


def _lat_diff_kernel(lambda_init, dq_ref, dk_ref, dv_ref, ck_ref, cv_ref, lq1, lk1, lq2, lk2, dg_ref, o_ref):
    lam = _diff_lambda(lq1, lk1, lq2, lk2, lambda_init)
    k_blocks = [dk_ref[...].astype(BF16), ck_ref[...].astype(BF16)]
    v_blocks = [dv_ref[...].astype(BF16), cv_ref[...].astype(BF16)]
    for t in range(DEC_SEQ // Q_TILE):
        rows = slice(t * Q_TILE, (t + 1) * Q_TILE)
        _diff_attention(dq_ref[rows, :].astype(BF16), k_blocks, v_blocks, lam, dg_ref[...], lambda_init,
                        o_ref, rows)


def _lat_diff_call(l, lambda_init, dq, dk, dv, cache_k, cache_v, lams, dg):
    off = N_CTX // DEC_SEQ
    seq = pl.BlockSpec((DEC_SEQ, 512), lambda b: (off + b, 0))
    cache = pl.BlockSpec((None, None, PAST_LEN, 512), lambda b: (b, l, 0, 0))
    full = lambda *s: pl.BlockSpec(s, lambda b: (0,) * len(s))
    return pl.pallas_call(
        functools.partial(_lat_diff_kernel, lambda_init),
        grid=(DEC_BATCH,),
        in_specs=[seq, seq, seq, cache, cache] + [full(1, DIFF_QK_DIM)] * 4 + [full(1, DIFF_V_DIM)],
        out_specs=pl.BlockSpec((DEC_SEQ, 512), lambda b: (b, 0)),
        out_shape=jax.ShapeDtypeStruct((N_LAT, 512), BF16),
        compiler_params=_cparams("parallel"),
        name="latent_differential_attention",
    )(dq, dk, dv, cache_k, cache_v, *lams, dg)


def _merge_kernel(x_ref, mod_ref, oa_ref, ob_ref, oc_ref, od_ref, wg_ref, bg_ref,
                  wa_ref, wb_ref, wc_ref, wd_ref, wo_ref, g_ref, b_ref, x1_ref, h2t_ref):
    x = x_ref[...]
    mod = lambda k: mod_ref[0, :, k * D_MODEL:(k + 1) * D_MODEL]
    h = (_ln(x) * (1.0 + mod(1)) + mod(0)).astype(BF16)
    merged = None
    for i, (o_ref, w_ref) in enumerate(((oa_ref, wa_ref), (ob_ref, wb_ref), (oc_ref, wc_ref), (od_ref, wd_ref))):
        cols = slice(i * D_MODEL, (i + 1) * D_MODEL)
        gate = jax.nn.sigmoid(_dot(h, wg_ref[:, cols]) + bg_ref[:, cols])
        term = gate * _dot(o_ref[...], w_ref[...])
        merged = term if merged is None else merged + term
    mix = _dot(merged.astype(BF16), wo_ref[...])
    x1 = _ln(DEEPNORM_ALPHA * x + mod(2) * mix) * g_ref[...] + b_ref[...]
    x1_ref[...] = x1
    h2 = _ln(x1) * (1.0 + mod(4)) + mod(3)
    h2t_ref[...] = h2.T.astype(BF16)


def _merge_call(x, mod, oa, ob, oc, od, wg, bg, wa, wb, wc, wd, wo, g, b):
    tile = lambda w: pl.BlockSpec((TM, w), lambda i: (i, 0))
    full = lambda *s: pl.BlockSpec(s, lambda i: (0,) * len(s))
    return pl.pallas_call(
        _merge_kernel,
        grid=(N_TOK // TM,),
        in_specs=[tile(D_MODEL), pl.BlockSpec((1, 1, 6 * D_MODEL), lambda i: (_mod_row(i), 0, 0)),
                  tile(256), tile(256), tile(256), tile(512),
                  full(D_MODEL, 4 * D_MODEL), full(1, 4 * D_MODEL),
                  full(256, D_MODEL), full(256, D_MODEL), full(256, D_MODEL), full(512, D_MODEL),
                  full(D_MODEL, D_MODEL), full(1, D_MODEL), full(1, D_MODEL)],
        out_specs=[tile(D_MODEL), pl.BlockSpec((D_MODEL, TM), lambda i: (0, i))],
        out_shape=[jax.ShapeDtypeStruct((N_TOK, D_MODEL), F32), jax.ShapeDtypeStruct((D_MODEL, N_TOK), BF16)],
        compiler_params=_cparams("parallel"),
        name="branch_merge",
    )(x, mod, oa, ob, oc, od, wg, bg, wa, wb, wc, wd, wo, g, b)


KEY_MIN = -2 ** 31


def _tree_sum(terms):
    while len(terms) > 1:
        terms = [a + b for a, b in zip(terms[0::2], terms[1::2])] + ([terms[-1]] if len(terms) % 2 else [])
    return terms[0]


def _row_gather(table, idx):
    ii = idx.astype(jnp.int32)
    low = ii & (SUBLANES - 1)
    outs = []
    for c in range(idx.shape[0] // SUBLANES):
        rows = slice(c * SUBLANES, (c + 1) * SUBLANES)
        lo = jnp.take_along_axis(table[0:SUBLANES], low[rows], axis=0)
        hi = jnp.take_along_axis(table[SUBLANES:PEER_TOPK], low[rows], axis=0)
        outs.append(jnp.where(ii[rows] < SUBLANES, lo, jnp.where(ii[rows] < PEER_TOPK, hi, 0.0)))
    return jnp.concatenate(outs, axis=0)


def _sort_key(x):
    b = lax.bitcast_convert_type(x + 0.0, jnp.int32)
    return b ^ ((b >> 31) & 0x7FFFFFFF)


def _key_value(k):
    return lax.bitcast_convert_type(k ^ ((k >> 31) & 0x7FFFFFFF), F32)


def _top16(s):
    row = lax.broadcasted_iota(jnp.int32, s.shape, 0).astype(F32)
    krow = lax.broadcasted_iota(jnp.int32, (PEER_TOPK, s.shape[1]), 0)

    def body(k, carry):
        work, rank, vals = carry
        m = jnp.max(work, axis=0, keepdims=True)
        idx = jnp.min(jnp.where(work == m, row, float(PEER_N_KEYS)), axis=0, keepdims=True)
        sel = row == idx
        rank = jnp.where(sel, jnp.asarray(k, jnp.int32).astype(F32), rank)
        work = jnp.where(sel, -jnp.inf, work)
        vals = jnp.where(krow == k, m, vals)
        return work, rank, vals

    init = (s, jnp.full(s.shape, float(PEER_N_KEYS), F32), jnp.zeros((PEER_TOPK, s.shape[1]), F32))
    _, rank, vals = lax.fori_loop(0, PEER_TOPK, body, init)
    return vals, rank


def _top16_pair(s1, s2):
    krow = lax.broadcasted_iota(jnp.int32, (PEER_TOPK, LANES), 0)

    def body(k, carry):
        w1, w2, v1, v2 = carry
        code = KEY_MIN + jnp.asarray(k, jnp.int32)
        m1 = jnp.max(w1, axis=0, keepdims=True)
        m2 = jnp.max(w2, axis=0, keepdims=True)
        w1 = jnp.where(w1 == m1, code, w1)
        w2 = jnp.where(w2 == m2, code, w2)
        return w1, w2, jnp.where(krow == k, m1, v1), jnp.where(krow == k, m2, v2)

    zeros = jnp.zeros((PEER_TOPK, LANES), jnp.int32)
    w1, w2, v1, v2 = lax.fori_loop(0, PEER_TOPK, body, (_sort_key(s1), _sort_key(s2), zeros, zeros))

    def decode(w):
        taken = w < KEY_MIN + PEER_TOPK
        rank = jnp.where(taken, (w - KEY_MIN).astype(F32), float(PEER_N_KEYS))
        return rank, jnp.sum(taken.astype(F32), axis=0, keepdims=True)

    r1, c1 = decode(w1)
    r2, c2 = decode(w2)
    ties = jnp.max(jnp.maximum(jnp.abs(c1 - PEER_TOPK), jnp.abs(c2 - PEER_TOPK))) > 0.5

    def exact():
        hs1, q1 = _top16(s1)
        hs2, q2 = _top16(s2)
        return hs1, hs2, q1, q2

    return lax.cond(ties, exact, lambda: (_key_value(v1), _key_value(v2), r1, r2))


def _merge_counts(hs1, hs2):
    krow = lax.broadcasted_iota(jnp.int32, hs1.shape, 0).astype(F32)

    def body(_, carry):
        cnt, front = carry
        m = jnp.max(front, axis=0, keepdims=True)
        win = jnp.min(jnp.where(front == m, krow, float(PEER_TOPK)), axis=0, keepdims=True)
        sel = krow == win
        cnt = jnp.where(sel, cnt + 1.0, cnt)
        nxt = jnp.where(cnt < float(PEER_TOPK), hs1 + _row_gather(hs2, cnt), -jnp.inf)
        return cnt, jnp.where(sel, nxt, front)

    cnt, _ = lax.fori_loop(0, PEER_TOPK, body, (jnp.zeros(hs1.shape, F32), hs1 + hs2[0:1, :]))
    return cnt


def _router_kernel(h2t_ref, wqt_ref, keys_ref, r2_ref, e2_ref, n1_ref, e1_ref, q_scr, s_scr, hs_scr, rank1_scr):
    t = ROUTER_TILE
    q_scr[...] = _dot(wqt_ref[...], h2t_ref[...]).astype(BF16)

    def head_body(hd, _):
        base = pl.multiple_of(hd * PEER_KEY_DIM, PEER_KEY_DIM)
        s_scr[hd, 0] = _dot(keys_ref[2 * hd], q_scr[pl.ds(base, LANES), :])
        s_scr[hd, 1] = _dot(keys_ref[2 * hd + 1], q_scr[pl.ds(base + LANES, LANES), :])
        for j in range(t // LANES):
            lanes = slice(j * LANES, (j + 1) * LANES)
            hs1, hs2, rank1, rank2 = _top16_pair(s_scr[hd, 0, :, lanes], s_scr[hd, 1, :, lanes])
            hs_scr[hd, 0, :, lanes] = hs1
            hs_scr[hd, 1, :, lanes] = hs2
            rank1_scr[hd, :, lanes] = rank1
            r2_ref[hd, :, lanes] = rank2.astype(BF16)
        return 0

    lax.fori_loop(0, PEER_HEADS, head_body, 0)

    for pair in range(PEER_HEADS // 2):
        heads = (2 * pair, 2 * pair + 1)
        hs1 = jnp.concatenate([hs_scr[h, 0] for h in heads], axis=1)
        hs2 = jnp.concatenate([hs_scr[h, 1] for h in heads], axis=1)
        cnt = _merge_counts(hs1, hs2)
        e1r = jnp.exp(hs1 - hs1[0:1, :])
        e2r = jnp.exp(hs2 - hs2[0:1, :])
        prefix = _tree_sum([jnp.where(cnt > float(kb), e2r[kb:kb + 1, :], 0.0) for kb in range(PEER_TOPK)])
        inv_z = 1.0 / jnp.sum(e1r * prefix, axis=0, keepdims=True)
        for i, h in enumerate(heads):
            lanes = slice(i * t, (i + 1) * t)
            e2_ref[h] = (jnp.exp(s_scr[h, 1] - hs2[0:1, lanes]) * inv_z[:, lanes]).astype(BF16)
            e1_ref[h] = 0.5 * jnp.exp(s_scr[h, 0] - hs1[0:1, lanes])
            n1_ref[h] = _row_gather(cnt[:, lanes], rank1_scr[h])


def _router_call(h2t, wqt, keys):
    t = ROUTER_TILE
    out = pl.BlockSpec((PEER_HEADS, PEER_N_KEYS, t), lambda i: (0, 0, i))
    shape = (PEER_HEADS, PEER_N_KEYS, N_TOK)
    return pl.pallas_call(
        _router_kernel,
        grid=(N_TOK // t,),
        in_specs=[pl.BlockSpec((D_MODEL, t), lambda i: (0, i)),
                  pl.BlockSpec((PEER_HEADS * PEER_KEY_DIM, D_MODEL), lambda i: (0, 0)),
                  pl.BlockSpec((2 * PEER_HEADS, PEER_N_KEYS, PEER_KEY_DIM // 2), lambda i: (0, 0, 0))],
        out_specs=[out] * 4,
        out_shape=[jax.ShapeDtypeStruct(shape, BF16), jax.ShapeDtypeStruct(shape, BF16),
                   jax.ShapeDtypeStruct(shape, F32), jax.ShapeDtypeStruct(shape, F32)],
        scratch_shapes=[pltpu.VMEM((PEER_HEADS * PEER_KEY_DIM, t), BF16),
                        pltpu.VMEM((PEER_HEADS, 2, PEER_N_KEYS, t), F32),
                        pltpu.VMEM((PEER_HEADS, 2, PEER_TOPK, t), F32),
                        pltpu.VMEM((PEER_HEADS, PEER_N_KEYS, t), F32)],
        compiler_params=_cparams("parallel"),
        name="peer_retrieval",
    )(h2t, wqt, keys)


def _gated_activations(ht_ref, w_ref, r2_ref, e2_ref, n1_ref, e1_ref):
    for i in range(KEYS_PER_BLOCK):
        rows = slice(i * PEER_N_KEYS, (i + 1) * PEER_N_KEYS)
        for j in range(PEER_TILE // GATE_LANES):
            lanes = slice(j * GATE_LANES, (j + 1) * GATE_LANES)
            gate = jnp.zeros((PEER_N_KEYS, GATE_LANES), BF16)
            for hd in range(PEER_HEADS):
                n_row = n1_ref[hd, i:i + 1, lanes].astype(BF16)
                c_row = e1_ref[hd, i:i + 1, lanes].astype(BF16)
                live = jnp.where(r2_ref[hd, :, lanes] < n_row, e2_ref[hd, :, lanes], jnp.zeros((), BF16))
                gate = gate + live * c_row
            x = ht_ref[rows, lanes]
            act = x * (1.0 + lax.erf(x * (1.0 / math.sqrt(2.0))))
            w_ref[rows, lanes] = act.astype(BF16) * gate


def _peer_kernel(h2t_ref, u_ref, vt_ref, r2_ref, e2_ref, n1_ref, e1_ref,
                 x1_ref, mod_ref, g_ref, b_ref, o_ref, acc_ref, ht_ref, w_ref):
    e = pl.program_id(1)

    @pl.when(e == 0)
    def _():
        acc_ref[...] = jnp.zeros_like(acc_ref)

    ht_ref[...] = _dot(u_ref[...], h2t_ref[...])
    _gated_activations(ht_ref, w_ref, r2_ref, e2_ref, n1_ref, e1_ref)
    acc_ref[...] += _dot(vt_ref[...], w_ref[...])

    @pl.when(e == pl.num_programs(1) - 1)
    def _():
        ffn = acc_ref[...].T
        g2 = mod_ref[0, :, 5 * D_MODEL:6 * D_MODEL]
        o_ref[...] = _ln(DEEPNORM_ALPHA * x1_ref[...] + g2 * ffn) * g_ref[...] + b_ref[...]


def _peer_mod_row(i):
    n_ctx_tiles = N_CTX // PEER_TILE
    return jnp.where(i < n_ctx_tiles, 0, 1 + (i - n_ctx_tiles) // (DEC_SEQ // PEER_TILE))


def _peer_call(h2t, u, vt, r2, e2, n1, e1, x1, mod, g, b):
    t = PEER_TILE
    n_blocks = PEER_N_KEYS * PEER_N_KEYS // EXPERT_BLOCK
    gates = pl.BlockSpec((PEER_HEADS, PEER_N_KEYS, t), lambda i, g: (0, 0, i))
    keys = pl.BlockSpec((PEER_HEADS, KEYS_PER_BLOCK, t), lambda i, g: (0, g, i))
    return pl.pallas_call(
        _peer_kernel,
        grid=(N_TOK // t, n_blocks),
        in_specs=[pl.BlockSpec((D_MODEL, t), lambda i, g: (0, i)),
                  pl.BlockSpec((EXPERT_BLOCK, D_MODEL), lambda i, g: (g, 0)),
                  pl.BlockSpec((D_MODEL, EXPERT_BLOCK), lambda i, g: (0, g)),
                  gates, gates, keys, keys,
                  pl.BlockSpec((t, D_MODEL), lambda i, g: (i, 0)),
                  pl.BlockSpec((1, 1, 6 * D_MODEL), lambda i, g: (_peer_mod_row(i), 0, 0)),
                  pl.BlockSpec((1, D_MODEL), lambda i, g: (0, 0)),
                  pl.BlockSpec((1, D_MODEL), lambda i, g: (0, 0))],
        out_specs=pl.BlockSpec((t, D_MODEL), lambda i, g: (i, 0)),
        out_shape=jax.ShapeDtypeStruct((N_TOK, D_MODEL), F32),
        scratch_shapes=[pltpu.VMEM((D_MODEL, t), F32), pltpu.VMEM((EXPERT_BLOCK, t), F32),
                        pltpu.VMEM((EXPERT_BLOCK, t), BF16)],
        compiler_params=_cparams("parallel", "arbitrary"),
        name="peer_dense",
    )(h2t, u, vt, r2, e2, n1, e1, x1, mod, g, b)


def _rope_tables():
    t = jnp.arange(DEC_SEQ)
    row = (t // GRID_W).astype(F32)
    col = (t % GRID_W).astype(F32)

    def angles(rot_dim):
        n_freq = rot_dim // 4
        inv_freq = ROPE_THETA ** (-jnp.arange(n_freq, dtype=F32) / n_freq)
        return jnp.concatenate([row[:, None] * inv_freq, col[:, None] * inv_freq], axis=-1)

    def pack(cos_l, sa_l, sb_l):
        tab = jnp.stack([cos_l, sa_l, sb_l])
        ident = jnp.stack([jnp.ones((TM, LANES), F32), jnp.zeros((TM, LANES), F32), jnp.zeros((TM, LANES), F32)])
        return jnp.concatenate([tab, ident], axis=1)

    ang_b = angles(MLA_ROPE)
    cb, sb = jnp.cos(ang_b), jnp.sin(ang_b)
    one, zero = jnp.ones((DEC_SEQ, 64), F32), jnp.zeros((DEC_SEQ, 64), F32)
    z16, z32 = jnp.zeros((DEC_SEQ, 16), F32), jnp.zeros((DEC_SEQ, 32), F32)
    rope_b = pack(jnp.concatenate([one, cb, cb, jnp.ones((DEC_SEQ, 32), F32)], axis=1),
                  jnp.concatenate([zero, -sb, z16, z32], axis=1),
                  jnp.concatenate([zero, z16, sb, z32], axis=1))
    ang_d = angles(DIFF_QK_DIM)
    cd, sd = jnp.cos(ang_d), jnp.sin(ang_d)
    rope_d = pack(jnp.concatenate([cd, cd, cd, cd], axis=1),
                  jnp.concatenate([-sd, z32, -sd, z32], axis=1),
                  jnp.concatenate([z32, sd, z32, sd], axis=1))
    return rope_b, rope_d


def _na_bias_table(rpb):
    col = jnp.arange(GRID_W)
    dc = jnp.clip(col[None, :] - col[:, None], -(NA_WIN_COLS - 1), NA_WIN_COLS - 1) + NA_WIN_COLS - 1
    rpb_cols = rpb[:, :, dc]
    tabs = [rpb_cols[:, off:off + NA_WIN_ROWS].transpose(0, 2, 1, 3).reshape(NA_HEADS, GRID_W, NA_WIN_ROWS * GRID_W)
            for off in range(NA_WIN_ROWS)]
    return jnp.stack(tabs)


def _pad_cols(w, left, right):
    return jnp.pad(w, ((0, 0), (left, right)))


def kernel(x_prompt, x_sample, cache_mla_ckv, cache_mla_krope, cache_na_k, cache_na_v, cache_diff_k, cache_diff_v, c, c_ctx, w_mod, b_mod, w_in, sgu_norm_g, sgu_w, sgu_b, mla_q_norm_g, mla_w_uq, mla_kv_norm_g, mla_w_ukv, na_rpb, diff_lambda_q1, diff_lambda_k1, diff_lambda_q2, diff_lambda_k2, diff_norm_g, w_branch_a, w_branch_b, w_branch_c, w_branch_d, w_gate, b_gate, w_out, ln1_g, ln1_b, peer_w_q, peer_subkeys, peer_u, peer_v, ln2_g, ln2_b):
    x = jnp.concatenate([x_prompt.reshape(N_CTX, D_MODEL), x_sample.reshape(N_LAT, D_MODEL)], axis=0)
    cond = jnp.concatenate([c_ctx[None], c, jnp.zeros((N_COND - 1 - DEC_BATCH, D_MODEL), F32)], axis=0)
    mod_all = _mod_call(cond, w_mod, b_mod)
    rope_b, rope_d = _rope_tables()
    cache_kr_pad = jnp.pad(cache_mla_krope, ((0, 0), (0, 0), (0, 0), (MLA_NOPE, LANES - MLA_NOPE - MLA_ROPE)))
    cache_na_k2 = cache_na_k.reshape(DEC_BATCH, DEPTH, PAST_LEN, 256)
    cache_na_v2 = cache_na_v.reshape(DEC_BATCH, DEPTH, PAST_LEN, 256)
    cache_diff_k2 = cache_diff_k.reshape(DEC_BATCH, DEPTH, PAST_LEN, 512)
    cache_diff_v2 = cache_diff_v.reshape(DEC_BATCH, DEPTH, PAST_LEN, 512)

    ctx_out = []
    for l in range(DEPTH):
        lambda_init = 0.8 - 0.6 * math.exp(-0.3 * l)
        mod = mod_all[l].reshape(N_COND, 1, 6 * D_MODEL)

        wi = w_in[l]
        kr_cols = _pad_cols(wi[:, C_KR:C_KR + MLA_ROPE], MLA_NOPE, LANES - MLA_NOPE - MLA_ROPE)
        w_in_r = jnp.concatenate([wi[:, :C_KR], kr_cols, wi[:, C_KR + MLA_ROPE:]], axis=1).astype(BF16)
        wuq = mla_w_uq[l].reshape(MLA_Q_LORA, MLA_HEADS, MLA_NOPE + MLA_ROPE)
        wuq = jnp.pad(wuq, ((0, 0), (0, 0), (0, LANES - MLA_NOPE - MLA_ROPE))).reshape(MLA_Q_LORA, -1).astype(BF16)
        wukv = mla_w_ukv[l].reshape(MLA_KV_LORA, MLA_HEADS, MLA_NOPE + MLA_V)
        wuk = jnp.pad(wukv[:, :, :MLA_NOPE], ((0, 0), (0, 0), (0, LANES - MLA_NOPE))).reshape(MLA_KV_LORA, -1)
        wuk = wuk.astype(BF16)
        wuv = wukv[:, :, MLA_NOPE:].reshape(MLA_KV_LORA, -1).astype(BF16)
        sgu_bias = jnp.repeat(sgu_b[l].T, SGU_WIDTH // SGU_GROUPS, axis=1)
        lams = [p[l].reshape(1, DIFF_QK_DIM) for p in (diff_lambda_q1, diff_lambda_k1, diff_lambda_q2, diff_lambda_k2)]
        dg = diff_norm_g[l].reshape(1, DIFF_V_DIM)

        oa, mq, ckv, kr, nq, nk, nv, dq, dk, dv = _inproj_call(
            x, mod, rope_b, rope_d, w_in_r, sgu_norm_g[l].reshape(1, -1), sgu_w[l].astype(BF16), sgu_bias,
            mla_q_norm_g[l].reshape(1, -1), mla_kv_norm_g[l].reshape(1, -1), wuq)

        ob_c, oc_c, od_c = _ctx_attn_call(lambda_init, (mq, ckv, kr, nq, nk, nv, dq, dk, dv), wuk, wuv, lams, dg)
        ob_l = _lat_mla_call(l, mq, ckv, kr, cache_mla_ckv, cache_kr_pad, wuk, wuv)
        oc_l = _lat_na_call(l, nq, nk, nv, cache_na_k2, cache_na_v2, _na_bias_table(na_rpb[l]))
        od_l = _lat_diff_call(l, lambda_init, dq, dk, dv, cache_diff_k2, cache_diff_v2, lams, dg)
        ob = jnp.concatenate([ob_c, ob_l], axis=0)
        oc = jnp.concatenate([oc_c, oc_l], axis=0)
        od = jnp.concatenate([od_c, od_l], axis=0)

        x1, h2t = _merge_call(
            x, mod, oa, ob, oc, od, w_gate[l].astype(BF16), b_gate[l].reshape(1, -1),
            w_branch_a[l].astype(BF16), w_branch_b[l].astype(BF16), w_branch_c[l].astype(BF16),
            w_branch_d[l].astype(BF16), w_out[l].astype(BF16), ln1_g[l].reshape(1, -1), ln1_b[l].reshape(1, -1))

        keys = peer_subkeys[l].reshape(2 * PEER_HEADS, PEER_N_KEYS, PEER_KEY_DIM // 2).astype(BF16)
        r2, e2, n1, e1 = _router_call(h2t, peer_w_q[l].T.astype(BF16), keys)
        x = _peer_call(h2t, peer_u[l].astype(BF16), peer_v[l].T.astype(BF16), r2, e2, n1, e1, x1, mod,
                       ln2_g[l].reshape(1, -1), ln2_b[l].reshape(1, -1))

        ctx_out.append((ckv[:N_CTX].reshape(BATCH, SEQ, MLA_KV_LORA),
                        kr[:N_CTX, MLA_NOPE:MLA_NOPE + MLA_ROPE].reshape(BATCH, SEQ, MLA_ROPE),
                        nk[:N_CTX].reshape(BATCH, SEQ, NA_HEADS, NA_HEAD_DIM),
                        nv[:N_CTX].reshape(BATCH, SEQ, NA_HEADS, NA_HEAD_DIM),
                        dk[:N_CTX].reshape(BATCH, SEQ, DIFF_HEADS, 2 * DIFF_QK_DIM),
                        dv[:N_CTX].reshape(BATCH, SEQ, DIFF_HEADS, DIFF_V_DIM)))

    y_prompt = x[:N_CTX].reshape(BATCH, SEQ, D_MODEL)
    y_sample = x[N_CTX:].reshape(DEC_BATCH, DEC_SEQ, D_MODEL)
    new = [jnp.stack([t[k] for t in ctx_out], axis=1) for k in range(6)]
    return (y_prompt, y_sample, *new)
```

```python
import functools
import math

import jax
import jax.numpy as jnp
from jax import lax
from jax.experimental import pallas as pl
from jax.experimental.pallas import tpu as pltpu

F32 = jnp.float32
BF16 = jnp.bfloat16

D_MODEL = 1024
BATCH = 32
SEQ = 256
DEPTH = 2
DEC_BATCH = 8
DEC_SEQ = 1024
PAST_LEN = 512
GRID_W = 64
CHUNK = 128
SGU_GROUPS = 4
SGU_WIDTH = 256
MLA_HEADS = 4
MLA_Q_LORA = 256
MLA_KV_LORA = 128
MLA_NOPE = 64
MLA_ROPE = 32
MLA_V = 64
NA_HEADS = 4
NA_HEAD_DIM = 64
NA_WIN_ROWS = 8
NA_WIN_COLS = 16
DIFF_HEADS = 4
DIFF_QK_DIM = 64
DIFF_V_DIM = 128
N_BRANCHES = 4
PEER_HEADS = 8
PEER_N_KEYS = 128
PEER_KEY_DIM = 256
PEER_TOPK = 16
ROPE_THETA = 10000.0
LN_EPS = 1e-6
NEG_BIG = -1e30
DEEPNORM_ALPHA = (2 * DEPTH) ** 0.25

LANES = 128
SUBLANES = 8
N_CTX = BATCH * SEQ
N_LAT = DEC_BATCH * DEC_SEQ
N_TOK = N_CTX + N_LAT
N_COND = 16
TM = 512
ROWS = DEC_SEQ // GRID_W
Q_TILE = 256
ROUTER_TILE = 256
PEER_TILE = 512
EXPERT_BLOCK = 1024
KEYS_PER_BLOCK = EXPERT_BLOCK // PEER_N_KEYS
GATE_LANES = 256
VMEM_LIMIT = 56 * 1024 * 1024

C_AU, C_AV, C_CQ, C_CKV, C_KR = 0, 256, 512, 768, 896
C_NQ, C_NK, C_NV, C_DQ, C_DK, C_DV, C_END = 1024, 1280, 1536, 1792, 2304, 2816, 3328


def _ln(x):
    mu = jnp.mean(x, axis=-1, keepdims=True)
    xc = x - mu
    var = jnp.mean(xc * xc, axis=-1, keepdims=True)
    return xc * lax.rsqrt(var + LN_EPS)


def _rms(x):
    return x * lax.rsqrt(jnp.mean(x * x, axis=-1, keepdims=True) + LN_EPS)


def _gelu(x):
    return 0.5 * x * (1.0 + lax.erf(x * (1.0 / math.sqrt(2.0))))


def _dot(a, b):
    return jnp.dot(a, b, preferred_element_type=F32)


def _dot_nt(a, b):
    return lax.dot_general(a, b, (((1,), (1,)), ((), ())), preferred_element_type=F32)


def _rope(x, tab_ref, half):
    return (x * tab_ref[0] + pltpu.roll(x, LANES - half, 1) * tab_ref[1] + pltpu.roll(x, half, 1) * tab_ref[2])


def _cparams(*sem):
    return pltpu.CompilerParams(dimension_semantics=sem, vmem_limit_bytes=VMEM_LIMIT)


def _mod_kernel(cond_ref, w_ref, b_ref, o_ref):
    c = cond_ref[...]
    s = c * jax.nn.sigmoid(c)
    o_ref[...] = _dot(s, w_ref[...]) + b_ref[...]


def _mod_call(cond, w_mod, b_mod):
    nb = 1536
    return pl.pallas_call(
        _mod_kernel,
        grid=(DEPTH, 6 * D_MODEL // nb),
        in_specs=[pl.BlockSpec((N_COND, D_MODEL), lambda l, j: (0, 0)),
                  pl.BlockSpec((None, D_MODEL, nb), lambda l, j: (l, 0, j)),
                  pl.BlockSpec((None, 1, nb), lambda l, j: (l, 0, j))],
        out_specs=pl.BlockSpec((None, N_COND, nb), lambda l, j: (l, 0, j)),
        out_shape=jax.ShapeDtypeStruct((DEPTH, N_COND, 6 * D_MODEL), F32),
        compiler_params=_cparams("arbitrary", "arbitrary"),
        name="mod_vectors",
    )(cond, w_mod, b_mod.reshape(DEPTH, 1, 6 * D_MODEL))


def _mod_row(i):
    return i // (DEC_SEQ // TM)


def _pos_block(i):
    return i % (DEC_SEQ // TM)


def _inproj_kernel(x_ref, mod_ref, rb_ref, rd_ref, w_in_ref, sgu_g_ref, sgu_w_ref, sgu_bias_ref,
                   qg_ref, kvg_ref, wuq_ref,
                   oa_ref, mq_ref, ckv_ref, kr_ref, nq_ref, nk_ref, nv_ref, dq_ref, dk_ref, dv_ref):
    x = x_ref[...]
    shift = mod_ref[0, :, 0:D_MODEL]
    scale = mod_ref[0, :, D_MODEL:2 * D_MODEL]
    h = (_ln(x) * (1.0 + scale) + shift).astype(BF16)

    ya = _dot(h, w_in_ref[:, C_AU:C_CQ])
    u = _gelu(ya[:, :SGU_WIDTH])
    v = _gelu(ya[:, SGU_WIDTH:])
    vn = (_ln(v) * sgu_g_ref[...]).astype(BF16)
    group = lax.broadcasted_iota(jnp.int32, (CHUNK, SGU_WIDTH), 1) // (SGU_WIDTH // SGU_GROUPS)
    for c in range(TM // CHUNK):
        rows = slice(c * CHUNK, (c + 1) * CHUNK)
        mixed = sgu_bias_ref[...]
        for g in range(SGU_GROUPS):
            mixed = mixed + jnp.where(group == g, _dot(sgu_w_ref[g], vn[rows]), 0.0)
        oa_ref[rows, :] = (u[rows] * mixed).astype(oa_ref.dtype)

    ym = _dot(h, w_in_ref[:, C_CQ:C_NQ])
    cq = (_rms(ym[:, :MLA_Q_LORA]) * qg_ref[...]).astype(BF16)
    mq = _dot(cq, wuq_ref[...])
    for g in range(MLA_HEADS):
        lanes = slice(g * LANES, (g + 1) * LANES)
        mq_ref[:, lanes] = _rope(mq[:, lanes], rb_ref, MLA_ROPE // 2)
    ckv_ref[...] = _rms(ym[:, MLA_Q_LORA:MLA_Q_LORA + MLA_KV_LORA]) * kvg_ref[...]
    kr_ref[...] = _rope(ym[:, MLA_Q_LORA + MLA_KV_LORA:], rb_ref, MLA_ROPE // 2)

    yn = _dot(h, w_in_ref[:, C_NQ:C_DQ])
    nq_ref[...] = yn[:, 0:256]
    nk_ref[...] = yn[:, 256:512]
    nv_ref[...] = yn[:, 512:768]

    yd = _dot(h, w_in_ref[:, C_DQ:C_END])
    for g in range(4):
        lanes = slice(g * LANES, (g + 1) * LANES)
        dq_ref[:, lanes] = _rope(yd[:, g * LANES:(g + 1) * LANES], rd_ref, DIFF_QK_DIM // 2)
        dk_ref[:, lanes] = _rope(yd[:, 512 + g * LANES:512 + (g + 1) * LANES], rd_ref, DIFF_QK_DIM // 2)
    dv_ref[...] = yd[:, 1024:1536]


def _inproj_call(x, mod, rope_b, rope_d, w_in_r, sgu_g, sgu_w, sgu_bias, qg, kvg, wuq):
    tile = lambda w: pl.BlockSpec((TM, w), lambda i: (i, 0))
    full = lambda *s: pl.BlockSpec(s, lambda i: (0,) * len(s))
    widths = (SGU_WIDTH, 512, MLA_KV_LORA, LANES, 256, 256, 256, 512, 512, 512)
    dtypes = (BF16,) + (F32,) * 9
    n_tok = x.shape[0]
    return pl.pallas_call(
        _inproj_kernel,
        grid=(n_tok // TM,),
        in_specs=[tile(D_MODEL),
                  pl.BlockSpec((1, 1, 6 * D_MODEL), lambda i: (_mod_row(i), 0, 0)),
                  pl.BlockSpec((3, TM, LANES), lambda i: (0, _pos_block(i), 0)),
                  pl.BlockSpec((3, TM, LANES), lambda i: (0, _pos_block(i), 0)),
                  full(D_MODEL, C_END), full(1, SGU_WIDTH), full(SGU_GROUPS, CHUNK, CHUNK),
                  full(CHUNK, SGU_WIDTH), full(1, MLA_Q_LORA), full(1, MLA_KV_LORA),
                  full(MLA_Q_LORA, MLA_HEADS * LANES)],
        out_specs=[tile(w) for w in widths],
        out_shape=[jax.ShapeDtypeStruct((n_tok, w), dt) for w, dt in zip(widths, dtypes)],
        compiler_params=_cparams("parallel"),
        name="in_projection",
    )(x, mod, rope_b, rope_d, w_in_r, sgu_g, sgu_w, sgu_bias, qg, kvg, wuq)


def _half_mask(lo):
    lane = lax.broadcasted_iota(jnp.int32, (1, LANES), 1)
    return (lane >= lo) & (lane < lo + 64)


def _softmax_pv(scores, values, lanes):
    m = scores[0].max(axis=-1, keepdims=True)
    for s in scores[1:]:
        m = jnp.maximum(m, s.max(axis=-1, keepdims=True))
    den = None
    o = None
    for s, v in zip(scores, values):
        p = jnp.exp(s - m)
        d = p.sum(axis=-1, keepdims=True)
        den = d if den is None else den + d
        pv = _dot(p.astype(BF16), v[:, lanes])
        o = pv if o is None else o + pv
    return o / den


def _pair_attention(q, keys, vals, scale, bias_fn=None):
    outs = []
    for pair in range(2):
        lanes = slice(pair * LANES, (pair + 1) * LANES)
        qp = q[:, lanes]
        acc = None
        for sub in range(2):
            head = 2 * pair + sub
            mask = _half_mask(64 * sub)
            qm = jnp.where(mask, qp, jnp.zeros_like(qp))
            scores = [_dot_nt(qm, k[:, lanes]) * scale for k in keys]
            if bias_fn is not None:
                scores = bias_fn(head, scores)
            o = jnp.where(mask, _softmax_pv(scores, vals, lanes), 0.0)
            acc = o if acc is None else acc + o
        outs.append(acc)
    return outs


def _mla_attention(q, k_blocks, v_blocks, o_ref, rows):
    scale = (MLA_NOPE + MLA_ROPE) ** -0.5
    for pair in range(2):
        lanes = slice(pair * LANES, (pair + 1) * LANES)
        acc = None
        for sub in range(2):
            head = 2 * pair + sub
            hl = slice(head * LANES, (head + 1) * LANES)
            scores = [_dot_nt(q[:, hl], k[:, hl]) * scale for k in k_blocks]
            o = jnp.where(_half_mask(64 * sub), _softmax_pv(scores, v_blocks, lanes), 0.0)
            acc = o if acc is None else acc + o
        o_ref[rows, lanes] = acc.astype(o_ref.dtype)


def _diff_lambda(lq1, lk1, lq2, lk2, lambda_init):
    a = jnp.sum(lq1[...] * lk1[...], axis=-1, keepdims=True)
    b = jnp.sum(lq2[...] * lk2[...], axis=-1, keepdims=True)
    return jnp.exp(a) - jnp.exp(b) + lambda_init


def _diff_attention(q, k_blocks, v_blocks, lam, norm_g, lambda_init, o_ref, rows):
    scale = DIFF_QK_DIM ** -0.5
    for head in range(DIFF_HEADS):
        hl = slice(head * LANES, (head + 1) * LANES)
        qh = q[:, hl]
        probs = []
        for sub in range(2):
            qm = jnp.where(_half_mask(64 * sub), qh, jnp.zeros_like(qh))
            scores = [_dot_nt(qm, k[:, hl]) * scale for k in k_blocks]
            m = scores[0].max(axis=-1, keepdims=True)
            for s in scores[1:]:
                m = jnp.maximum(m, s.max(axis=-1, keepdims=True))
            ps = [jnp.exp(s - m) for s in scores]
            den = ps[0].sum(axis=-1, keepdims=True)
            for p in ps[1:]:
                den = den + p.sum(axis=-1, keepdims=True)
            probs.append((ps, 1.0 / den))
        o = None
        for i, v in enumerate(v_blocks):
            w = probs[0][0][i] * probs[0][1] - probs[1][0][i] * (lam * probs[1][1])
            pv = _dot(w.astype(BF16), v[:, hl])
            o = pv if o is None else o + pv
        o = _rms(o) * norm_g * (1.0 - lambda_init)
        o_ref[rows, hl] = o.astype(o_ref.dtype)


def _ctx_attn_kernel(lambda_init, mq_ref, ckv_ref, kr_ref, nq_ref, nk_ref, nv_ref, dq_ref, dk_ref, dv_ref,
                     wuk_ref, wuv_ref, lq1, lk1, lq2, lk2, dg_ref, ob_ref, oc_ref, od_ref):
    rows = slice(0, SEQ)
    ckv = ckv_ref[...].astype(BF16)
    kr = kr_ref[...]
    k_b = (_dot(ckv, wuk_ref[...]) + jnp.concatenate([kr] * MLA_HEADS, axis=1)).astype(BF16)
    v_b = _dot(ckv, wuv_ref[...]).astype(BF16)
    _mla_attention(mq_ref[...].astype(BF16), [k_b], [v_b], ob_ref, rows)

    outs = _pair_attention(nq_ref[...].astype(BF16), [nk_ref[...].astype(BF16)], [nv_ref[...].astype(BF16)],
                           NA_HEAD_DIM ** -0.5)
    for pair in range(2):
        oc_ref[:, pair * LANES:(pair + 1) * LANES] = outs[pair].astype(oc_ref.dtype)

    lam = _diff_lambda(lq1, lk1, lq2, lk2, lambda_init)
    _diff_attention(dq_ref[...].astype(BF16), [dk_ref[...].astype(BF16)], [dv_ref[...].astype(BF16)],
                    lam, dg_ref[...], lambda_init, od_ref, rows)


def _ctx_attn_call(lambda_init, acts, wuk, wuv, lams, dg):
    mq, ckv, kr, nq, nk, nv, dq, dk, dv = acts
    seq = lambda w: pl.BlockSpec((SEQ, w), lambda b: (b, 0))
    full = lambda *s: pl.BlockSpec(s, lambda b: (0,) * len(s))
    return pl.pallas_call(
        functools.partial(_ctx_attn_kernel, lambda_init),
        grid=(BATCH,),
        in_specs=[seq(512), seq(128), seq(128), seq(256), seq(256), seq(256), seq(512), seq(512), seq(512),
                  full(MLA_KV_LORA, 512), full(MLA_KV_LORA, 256)] + [full(1, DIFF_QK_DIM)] * 4
                 + [full(1, DIFF_V_DIM)],
        out_specs=[seq(256), seq(256), seq(512)],
        out_shape=[jax.ShapeDtypeStruct((N_CTX, w), BF16) for w in (256, 256, 512)],
        compiler_params=_cparams("parallel"),
        name="context_attention",
    )(mq, ckv, kr, nq, nk, nv, dq, dk, dv, wuk, wuv, *lams, dg)


def _lat_mla_kernel(mq_ref, ckv_ref, kr_ref, cckv_ref, ckr_ref, wuk_ref, wuv_ref, o_ref):
    def expand(ckv_f32, kr):
        ckv = ckv_f32.astype(BF16)
        k = (_dot(ckv, wuk_ref[...]) + jnp.concatenate([kr] * MLA_HEADS, axis=1)).astype(BF16)
        return k, _dot(ckv, wuv_ref[...]).astype(BF16)

    k_lat, v_lat = expand(ckv_ref[...], kr_ref[...])
    k_ctx, v_ctx = expand(cckv_ref[...], ckr_ref[...])
    for t in range(DEC_SEQ // Q_TILE):
        rows = slice(t * Q_TILE, (t + 1) * Q_TILE)
        _mla_attention(mq_ref[rows, :].astype(BF16), [k_lat, k_ctx], [v_lat, v_ctx], o_ref, rows)


def _lat_mla_call(l, mq, ckv, kr, cache_ckv, cache_kr_pad, wuk, wuv):
    seq = lambda w: pl.BlockSpec((DEC_SEQ, w), lambda b: (b, 0))
    cache = lambda w: pl.BlockSpec((None, None, PAST_LEN, w), lambda b: (b, l, 0, 0))
    full = lambda *s: pl.BlockSpec(s, lambda b: (0,) * len(s))
    return pl.pallas_call(
        _lat_mla_kernel,
        grid=(DEC_BATCH,),
        in_specs=[seq(512), seq(128), seq(128), cache(MLA_KV_LORA), cache(LANES),
                  full(MLA_KV_LORA, 512), full(MLA_KV_LORA, 256)],
        out_specs=pl.BlockSpec((DEC_SEQ, 256), lambda b: (b, 0)),
        out_shape=jax.ShapeDtypeStruct((N_LAT, 256), BF16),
        compiler_params=_cparams("parallel"),
        name="latent_mla_attention",
    )(mq, ckv, kr, cache_ckv, cache_kr_pad, wuk, wuv)


def _win_start(r):
    return jnp.clip(r - NA_WIN_ROWS // 2, 0, ROWS - NA_WIN_ROWS)


def _lat_na_kernel(nq_ref, nk_ref, nv_ref, ck_ref, cv_ref, bias_ref, o_ref):
    r = pl.program_id(1)
    start = pl.multiple_of(_win_start(r) * GRID_W, GRID_W)
    win = NA_WIN_ROWS * GRID_W
    k_w = nk_ref[pl.ds(start, win), :].astype(BF16)
    v_w = nv_ref[pl.ds(start, win), :].astype(BF16)
    k_c = ck_ref[...].astype(BF16)
    v_c = cv_ref[...].astype(BF16)
    q_col = lax.broadcasted_iota(jnp.int32, (GRID_W, win), 0)
    k_col = lax.broadcasted_iota(jnp.int32, (GRID_W, win), 1) % GRID_W
    c0 = jnp.clip(q_col - NA_WIN_COLS // 2, 0, GRID_W - NA_WIN_COLS)
    col_in = (k_col >= c0) & (k_col < c0 + NA_WIN_COLS)

    def bias_fn(head, scores):
        return [jnp.where(col_in, scores[0] + bias_ref[0, head], NEG_BIG), scores[1]]

    outs = _pair_attention(nq_ref[...].astype(BF16), [k_w, k_c], [v_w, v_c], NA_HEAD_DIM ** -0.5, bias_fn)
    for pair in range(2):
        o_ref[:, pair * LANES:(pair + 1) * LANES] = outs[pair].astype(o_ref.dtype)


def _lat_na_call(l, nq, nk, nv, cache_k, cache_v, bias_tab):
    seq = pl.BlockSpec((DEC_SEQ, 256), lambda b, r: (b, 0))
    cache = pl.BlockSpec((None, None, PAST_LEN, 256), lambda b, r: (b, l, 0, 0))
    return pl.pallas_call(
        _lat_na_kernel,
        grid=(DEC_BATCH, ROWS),
        in_specs=[pl.BlockSpec((GRID_W, 256), lambda b, r: (b * ROWS + r, 0)),
                  seq, seq, cache, cache,
                  pl.BlockSpec((1, NA_HEADS, GRID_W, NA_WIN_ROWS * GRID_W),
                               lambda b, r: (_win_start(r) - r + NA_WIN_ROWS - 1, 0, 0, 0))],
        out_specs=pl.BlockSpec((GRID_W, 256), lambda b, r: (b * ROWS + r, 0)),
        out_shape=jax.ShapeDtypeStruct((N_LAT, 256), BF16),
        compiler_params=_cparams("parallel", "arbitrary"),
        name="latent_neighbourhood_attention",
    )(nq, nk, nv, cache_k, cache_v, bias_tab)


def _lat_diff_kernel(lambda_init, dq_ref, dk_ref, dv_ref, ck_ref, cv_ref, lq1, lk1, lq2, lk2, dg_ref, o_ref):
    lam = _diff_lambda(lq1, lk1, lq2, lk2, lambda_init)
    k_blocks = [dk_ref[...].astype(BF16), ck_ref[...].astype(BF16)]
    v_blocks = [dv_ref[...].astype(BF16), cv_ref[...].astype(BF16)]
    for t in range(DEC_SEQ // Q_TILE):
        rows = slice(t * Q_TILE, (t + 1) * Q_TILE)
        _diff_attention(dq_ref[rows, :].astype(BF16), k_blocks, v_blocks, lam, dg_ref[...], lambda_init,
                        o_ref, rows)


def _lat_diff_call(l, lambda_init, dq, dk, dv, cache_k, cache_v, lams, dg):
    seq = pl.BlockSpec((DEC_SEQ, 512), lambda b: (b, 0))
    cache = pl.BlockSpec((None, None, PAST_LEN, 512), lambda b: (b, l, 0, 0))
    full = lambda *s: pl.BlockSpec(s, lambda b: (0,) * len(s))
    return pl.pallas_call(
        functools.partial(_lat_diff_kernel, lambda_init),
        grid=(DEC_BATCH,),
        in_specs=[seq, seq, seq, cache, cache] + [full(1, DIFF_QK_DIM)] * 4 + [full(1, DIFF_V_DIM)],
        out_specs=pl.BlockSpec((DEC_SEQ, 512), lambda b: (b, 0)),
        out_shape=jax.ShapeDtypeStruct((N_LAT, 512), BF16),
        compiler_params=_cparams("parallel"),
        name="latent_differential_attention",
    )(dq, dk, dv, cache_k, cache_v, *lams, dg)


def _merge_kernel(x_ref, mod_ref, oa_ref, ob_ref, oc_ref, od_ref, wg_ref, bg_ref,
                  wa_ref, wb_ref, wc_ref, wd_ref, wo_ref, g_ref, b_ref, x1_ref, h2t_ref):
    x = x_ref[...]
    mod = lambda k: mod_ref[0, :, k * D_MODEL:(k + 1) * D_MODEL]
    h = (_ln(x) * (1.0 + mod(1)) + mod(0)).astype(BF16)
    merged = None
    for i, (o_ref, w_ref) in enumerate(((oa_ref, wa_ref), (ob_ref, wb_ref), (oc_ref, wc_ref), (od_ref, wd_ref))):
        cols = slice(i * D_MODEL, (i + 1) * D_MODEL)
        gate = jax.nn.sigmoid(_dot(h, wg_ref[:, cols]) + bg_ref[:, cols])
        term = gate * _dot(o_ref[...], w_ref[...])
        merged = term if merged is None else merged + term
    mix = _dot(merged.astype(BF16), wo_ref[...])
    x1 = _ln(DEEPNORM_ALPHA * x + mod(2) * mix) * g_ref[...] + b_ref[...]
    x1_ref[...] = x1
    h2 = _ln(x1) * (1.0 + mod(4)) + mod(3)
    h2t_ref[...] = h2.T.astype(BF16)


def _merge_call(x, mod, oa, ob, oc, od, wg, bg, wa, wb, wc, wd, wo, g, b):
    tile = lambda w: pl.BlockSpec((TM, w), lambda i: (i, 0))
    full = lambda *s: pl.BlockSpec(s, lambda i: (0,) * len(s))
    n_tok = x.shape[0]
    return pl.pallas_call(
        _merge_kernel,
        grid=(n_tok // TM,),
        in_specs=[tile(D_MODEL), pl.BlockSpec((1, 1, 6 * D_MODEL), lambda i: (_mod_row(i), 0, 0)),
                  tile(256), tile(256), tile(256), tile(512),
                  full(D_MODEL, 4 * D_MODEL), full(1, 4 * D_MODEL),
                  full(256, D_MODEL), full(256, D_MODEL), full(256, D_MODEL), full(512, D_MODEL),
                  full(D_MODEL, D_MODEL), full(1, D_MODEL), full(1, D_MODEL)],
        out_specs=[tile(D_MODEL), pl.BlockSpec((D_MODEL, TM), lambda i: (0, i))],
        out_shape=[jax.ShapeDtypeStruct((n_tok, D_MODEL), F32), jax.ShapeDtypeStruct((D_MODEL, n_tok), BF16)],
        compiler_params=_cparams("parallel"),
        name="branch_merge",
    )(x, mod, oa, ob, oc, od, wg, bg, wa, wb, wc, wd, wo, g, b)


KEY_MIN = -2 ** 31


def _tree_sum(terms):
    while len(terms) > 1:
        terms = [a + b for a, b in zip(terms[0::2], terms[1::2])] + ([terms[-1]] if len(terms) % 2 else [])
    return terms[0]


def _row_gather(table, idx):
    ii = idx.astype(jnp.int32)
    low = ii & (SUBLANES - 1)
    outs = []
    for c in range(idx.shape[0] // SUBLANES):
        rows = slice(c * SUBLANES, (c + 1) * SUBLANES)
        lo = jnp.take_along_axis(table[0:SUBLANES], low[rows], axis=0)
        hi = jnp.take_along_axis(table[SUBLANES:PEER_TOPK], low[rows], axis=0)
        outs.append(jnp.where(ii[rows] < SUBLANES, lo, jnp.where(ii[rows] < PEER_TOPK, hi, 0.0)))
    return jnp.concatenate(outs, axis=0)


def _sort_key(x):
    b = lax.bitcast_convert_type(x + 0.0, jnp.int32)
    return b ^ ((b >> 31) & 0x7FFFFFFF)


def _key_value(k):
    return lax.bitcast_convert_type(k ^ ((k >> 31) & 0x7FFFFFFF), F32)


def _top16(s):
    row = lax.broadcasted_iota(jnp.int32, s.shape, 0).astype(F32)
    krow = lax.broadcasted_iota(jnp.int32, (PEER_TOPK, s.shape[1]), 0)

    def body(k, carry):
        work, rank, vals = carry
        m = jnp.max(work, axis=0, keepdims=True)
        idx = jnp.min(jnp.where(work == m, row, float(PEER_N_KEYS)), axis=0, keepdims=True)
        sel = row == idx
        rank = jnp.where(sel, jnp.asarray(k, jnp.int32).astype(F32), rank)
        work = jnp.where(sel, -jnp.inf, work)
        vals = jnp.where(krow == k, m, vals)
        return work, rank, vals

    init = (s, jnp.full(s.shape, float(PEER_N_KEYS), F32), jnp.zeros((PEER_TOPK, s.shape[1]), F32))
    _, rank, vals = lax.fori_loop(0, PEER_TOPK, body, init)
    return vals, rank


def _top16_pair(s1, s2):
    krow = lax.broadcasted_iota(jnp.int32, (PEER_TOPK, LANES), 0)

    def body(k, carry):
        w1, w2, v1, v2 = carry
        code = KEY_MIN + jnp.asarray(k, jnp.int32)
        m1 = jnp.max(w1, axis=0, keepdims=True)
        m2 = jnp.max(w2, axis=0, keepdims=True)
        w1 = jnp.where(w1 == m1, code, w1)
        w2 = jnp.where(w2 == m2, code, w2)
        return w1, w2, jnp.where(krow == k, m1, v1), jnp.where(krow == k, m2, v2)

    zeros = jnp.zeros((PEER_TOPK, LANES), jnp.int32)
    w1, w2, v1, v2 = lax.fori_loop(0, PEER_TOPK, body, (_sort_key(s1), _sort_key(s2), zeros, zeros))

    def decode(w):
        taken = w < KEY_MIN + PEER_TOPK
        rank = jnp.where(taken, (w - KEY_MIN).astype(F32), float(PEER_N_KEYS))
        return rank, jnp.sum(taken.astype(F32), axis=0, keepdims=True)

    r1, c1 = decode(w1)
    r2, c2 = decode(w2)
    ties = jnp.max(jnp.maximum(jnp.abs(c1 - PEER_TOPK), jnp.abs(c2 - PEER_TOPK))) > 0.5

    def exact():
        hs1, q1 = _top16(s1)
        hs2, q2 = _top16(s2)
        return hs1, hs2, q1, q2

    return lax.cond(ties, exact, lambda: (_key_value(v1), _key_value(v2), r1, r2))


def _merge_counts(hs1, hs2):
    krow = lax.broadcasted_iota(jnp.int32, hs1.shape, 0).astype(F32)

    def body(_, carry):
        cnt, front = carry
        m = jnp.max(front, axis=0, keepdims=True)
        win = jnp.min(jnp.where(front == m, krow, float(PEER_TOPK)), axis=0, keepdims=True)
        sel = krow == win
        cnt = jnp.where(sel, cnt + 1.0, cnt)
        nxt = jnp.where(cnt < float(PEER_TOPK), hs1 + _row_gather(hs2, cnt), -jnp.inf)
        return cnt, jnp.where(sel, nxt, front)

    cnt, _ = lax.fori_loop(0, PEER_TOPK, body, (jnp.zeros(hs1.shape, F32), hs1 + hs2[0:1, :]))
    return cnt


def _router_kernel(h2t_ref, wqt_ref, keys_ref, r2_ref, e2_ref, n1_ref, e1_ref, q_scr, s_scr, hs_scr, rank1_scr):
    t = ROUTER_TILE
    q_scr[...] = _dot(wqt_ref[...], h2t_ref[...]).astype(BF16)

    def head_body(hd, _):
        base = pl.multiple_of(hd * PEER_KEY_DIM, PEER_KEY_DIM)
        s_scr[hd, 0] = _dot(keys_ref[2 * hd], q_scr[pl.ds(base, LANES), :])
        s_scr[hd, 1] = _dot(keys_ref[2 * hd + 1], q_scr[pl.ds(base + LANES, LANES), :])
        for j in range(t // LANES):
            lanes = slice(j * LANES, (j + 1) * LANES)
            hs1, hs2, rank1, rank2 = _top16_pair(s_scr[hd, 0, :, lanes], s_scr[hd, 1, :, lanes])
            hs_scr[hd, 0, :, lanes] = hs1
            hs_scr[hd, 1, :, lanes] = hs2
            rank1_scr[hd, :, lanes] = rank1
            r2_ref[hd, :, lanes] = rank2.astype(BF16)
        return 0

    lax.fori_loop(0, PEER_HEADS, head_body, 0)

    for pair in range(PEER_HEADS // 2):
        heads = (2 * pair, 2 * pair + 1)
        hs1 = jnp.concatenate([hs_scr[h, 0] for h in heads], axis=1)
        hs2 = jnp.concatenate([hs_scr[h, 1] for h in heads], axis=1)
        cnt = _merge_counts(hs1, hs2)
        e1r = jnp.exp(hs1 - hs1[0:1, :])
        e2r = jnp.exp(hs2 - hs2[0:1, :])
        prefix = _tree_sum([jnp.where(cnt > float(kb), e2r[kb:kb + 1, :], 0.0) for kb in range(PEER_TOPK)])
        inv_z = 1.0 / jnp.sum(e1r * prefix, axis=0, keepdims=True)
        for i, h in enumerate(heads):
            lanes = slice(i * t, (i + 1) * t)
            e2_ref[h] = (jnp.exp(s_scr[h, 1] - hs2[0:1, lanes]) * inv_z[:, lanes]).astype(BF16)
            e1_ref[h] = 0.5 * jnp.exp(s_scr[h, 0] - hs1[0:1, lanes])
            n1_ref[h] = _row_gather(cnt[:, lanes], rank1_scr[h])


def _router_call(h2t, wqt, keys):
    t = ROUTER_TILE
    out = pl.BlockSpec((PEER_HEADS, PEER_N_KEYS, t), lambda i: (0, 0, i))
    n_tok = h2t.shape[1]
    shape = (PEER_HEADS, PEER_N_KEYS, n_tok)
    return pl.pallas_call(
        _router_kernel,
        grid=(n_tok // t,),
        in_specs=[pl.BlockSpec((D_MODEL, t), lambda i: (0, i)),
                  pl.BlockSpec((PEER_HEADS * PEER_KEY_DIM, D_MODEL), lambda i: (0, 0)),
                  pl.BlockSpec((2 * PEER_HEADS, PEER_N_KEYS, PEER_KEY_DIM // 2), lambda i: (0, 0, 0))],
        out_specs=[out] * 4,
        out_shape=[jax.ShapeDtypeStruct(shape, BF16), jax.ShapeDtypeStruct(shape, BF16),
                   jax.ShapeDtypeStruct(shape, F32), jax.ShapeDtypeStruct(shape, F32)],
        scratch_shapes=[pltpu.VMEM((PEER_HEADS * PEER_KEY_DIM, t), BF16),
                        pltpu.VMEM((PEER_HEADS, 2, PEER_N_KEYS, t), F32),
                        pltpu.VMEM((PEER_HEADS, 2, PEER_TOPK, t), F32),
                        pltpu.VMEM((PEER_HEADS, PEER_N_KEYS, t), F32)],
        compiler_params=_cparams("parallel"),
        name="peer_retrieval",
    )(h2t, wqt, keys)


def _gated_activations(ht_ref, w_ref, r2_ref, e2_ref, n1_ref, e1_ref):
    for i in range(KEYS_PER_BLOCK):
        rows = slice(i * PEER_N_KEYS, (i + 1) * PEER_N_KEYS)
        for j in range(PEER_TILE // GATE_LANES):
            lanes = slice(j * GATE_LANES, (j + 1) * GATE_LANES)
            gate = jnp.zeros((PEER_N_KEYS, GATE_LANES), BF16)
            for hd in range(PEER_HEADS):
                n_row = n1_ref[hd, i:i + 1, lanes].astype(BF16)
                c_row = e1_ref[hd, i:i + 1, lanes].astype(BF16)
                live = jnp.where(r2_ref[hd, :, lanes] < n_row, e2_ref[hd, :, lanes], jnp.zeros((), BF16))
                gate = gate + live * c_row
            x = ht_ref[rows, lanes]
            act = x * (1.0 + lax.erf(x * (1.0 / math.sqrt(2.0))))
            w_ref[rows, lanes] = act.astype(BF16) * gate


def _peer_kernel(h2t_ref, u_ref, vt_ref, r2_ref, e2_ref, n1_ref, e1_ref,
                 x1_ref, mod_ref, g_ref, b_ref, o_ref, acc_ref, ht_ref, w_ref):
    e = pl.program_id(1)

    @pl.when(e == 0)
    def _():
        acc_ref[...] = jnp.zeros_like(acc_ref)

    ht_ref[...] = _dot(u_ref[...], h2t_ref[...])
    _gated_activations(ht_ref, w_ref, r2_ref, e2_ref, n1_ref, e1_ref)
    acc_ref[...] += _dot(vt_ref[...], w_ref[...])

    @pl.when(e == pl.num_programs(1) - 1)
    def _():
        ffn = acc_ref[...].T
        g2 = mod_ref[0, :, 5 * D_MODEL:6 * D_MODEL]
        o_ref[...] = _ln(DEEPNORM_ALPHA * x1_ref[...] + g2 * ffn) * g_ref[...] + b_ref[...]


def _peer_mod_row(i):
    return i // (DEC_SEQ // PEER_TILE)


def _peer_call(h2t, u, vt, r2, e2, n1, e1, x1, mod, g, b):
    t = PEER_TILE
    n_tok = x1.shape[0]
    n_blocks = PEER_N_KEYS * PEER_N_KEYS // EXPERT_BLOCK
    gates = pl.BlockSpec((PEER_HEADS, PEER_N_KEYS, t), lambda i, g: (0, 0, i))
    keys = pl.BlockSpec((PEER_HEADS, KEYS_PER_BLOCK, t), lambda i, g: (0, g, i))
    return pl.pallas_call(
        _peer_kernel,
        grid=(n_tok // t, n_blocks),
        in_specs=[pl.BlockSpec((D_MODEL, t), lambda i, g: (0, i)),
                  pl.BlockSpec((EXPERT_BLOCK, D_MODEL), lambda i, g: (g, 0)),
                  pl.BlockSpec((D_MODEL, EXPERT_BLOCK), lambda i, g: (0, g)),
                  gates, gates, keys, keys,
                  pl.BlockSpec((t, D_MODEL), lambda i, g: (i, 0)),
                  pl.BlockSpec((1, 1, 6 * D_MODEL), lambda i, g: (_peer_mod_row(i), 0, 0)),
                  pl.BlockSpec((1, D_MODEL), lambda i, g: (0, 0)),
                  pl.BlockSpec((1, D_MODEL), lambda i, g: (0, 0))],
        out_specs=pl.BlockSpec((t, D_MODEL), lambda i, g: (i, 0)),
        out_shape=jax.ShapeDtypeStruct((n_tok, D_MODEL), F32),
        scratch_shapes=[pltpu.VMEM((D_MODEL, t), F32), pltpu.VMEM((EXPERT_BLOCK, t), F32),
                        pltpu.VMEM((EXPERT_BLOCK, t), BF16)],
        compiler_params=_cparams("parallel", "arbitrary"),
        name="peer_dense",
    )(h2t, u, vt, r2, e2, n1, e1, x1, mod, g, b)


def _rope_tables():
    t = jnp.arange(DEC_SEQ)
    row = (t // GRID_W).astype(F32)
    col = (t % GRID_W).astype(F32)

    def angles(rot_dim):
        n_freq = rot_dim // 4
        inv_freq = ROPE_THETA ** (-jnp.arange(n_freq, dtype=F32) / n_freq)
        return jnp.concatenate([row[:, None] * inv_freq, col[:, None] * inv_freq], axis=-1)

    def pack(cos_l, sa_l, sb_l):
        return jnp.stack([cos_l, sa_l, sb_l])

    ang_b = angles(MLA_ROPE)
    cb, sb = jnp.cos(ang_b), jnp.sin(ang_b)
    one, zero = jnp.ones((DEC_SEQ, 64), F32), jnp.zeros((DEC_SEQ, 64), F32)
    z16, z32 = jnp.zeros((DEC_SEQ, 16), F32), jnp.zeros((DEC_SEQ, 32), F32)
    rope_b = pack(jnp.concatenate([one, cb, cb, jnp.ones((DEC_SEQ, 32), F32)], axis=1),
                  jnp.concatenate([zero, -sb, z16, z32], axis=1),
                  jnp.concatenate([zero, z16, sb, z32], axis=1))
    ang_d = angles(DIFF_QK_DIM)
    cd, sd = jnp.cos(ang_d), jnp.sin(ang_d)
    rope_d = pack(jnp.concatenate([cd, cd, cd, cd], axis=1),
                  jnp.concatenate([-sd, z32, -sd, z32], axis=1),
                  jnp.concatenate([z32, sd, z32, sd], axis=1))
    ident = pack(jnp.ones((DEC_SEQ, LANES), F32), jnp.zeros((DEC_SEQ, LANES), F32), jnp.zeros((DEC_SEQ, LANES), F32))
    return rope_b, rope_d, ident


def _na_bias_table(rpb):
    col = jnp.arange(GRID_W)
    dc = jnp.clip(col[None, :] - col[:, None], -(NA_WIN_COLS - 1), NA_WIN_COLS - 1) + NA_WIN_COLS - 1
    rpb_cols = rpb[:, :, dc]
    tabs = [rpb_cols[:, off:off + NA_WIN_ROWS].transpose(0, 2, 1, 3).reshape(NA_HEADS, GRID_W, NA_WIN_ROWS * GRID_W)
            for off in range(NA_WIN_ROWS)]
    return jnp.stack(tabs)


def _pad_cols(w, left, right):
    return jnp.pad(w, ((0, 0), (left, right)))


def kernel(x_prompt, x_sample, cache_mla_ckv, cache_mla_krope, cache_na_k, cache_na_v, cache_diff_k, cache_diff_v, c, c_ctx, w_mod, b_mod, w_in, sgu_norm_g, sgu_w, sgu_b, mla_q_norm_g, mla_w_uq, mla_kv_norm_g, mla_w_ukv, na_rpb, diff_lambda_q1, diff_lambda_k1, diff_lambda_q2, diff_lambda_k2, diff_norm_g, w_branch_a, w_branch_b, w_branch_c, w_branch_d, w_gate, b_gate, w_out, ln1_g, ln1_b, peer_w_q, peer_subkeys, peer_u, peer_v, ln2_g, ln2_b):
    x_ctx = x_prompt.reshape(N_CTX, D_MODEL)
    x_lat = x_sample.reshape(N_LAT, D_MODEL)
    cond = jnp.concatenate([c_ctx[None], c, jnp.zeros((N_COND - 1 - DEC_BATCH, D_MODEL), F32)], axis=0)
    mod_all = _mod_call(cond, w_mod, b_mod)
    rope_b, rope_d, rope_id = _rope_tables()
    cache_kr_pad = jnp.pad(cache_mla_krope, ((0, 0), (0, 0), (0, 0), (MLA_NOPE, LANES - MLA_NOPE - MLA_ROPE)))
    cache_na_k2 = cache_na_k.reshape(DEC_BATCH, DEPTH, PAST_LEN, 256)
    cache_na_v2 = cache_na_v.reshape(DEC_BATCH, DEPTH, PAST_LEN, 256)
    cache_diff_k2 = cache_diff_k.reshape(DEC_BATCH, DEPTH, PAST_LEN, 512)
    cache_diff_v2 = cache_diff_v.reshape(DEC_BATCH, DEPTH, PAST_LEN, 512)

    ctx_out = []
    for l in range(DEPTH):
        lambda_init = 0.8 - 0.6 * math.exp(-0.3 * l)
        mod_ctx = jnp.broadcast_to(mod_all[l, 0], (N_CTX // DEC_SEQ, 1, 6 * D_MODEL))
        mod_lat = mod_all[l, 1:1 + DEC_BATCH].reshape(DEC_BATCH, 1, 6 * D_MODEL)

        wi = w_in[l]
        kr_cols = _pad_cols(wi[:, C_KR:C_KR + MLA_ROPE], MLA_NOPE, LANES - MLA_NOPE - MLA_ROPE)
        w_in_r = jnp.concatenate([wi[:, :C_KR], kr_cols, wi[:, C_KR + MLA_ROPE:]], axis=1).astype(BF16)
        wuq = mla_w_uq[l].reshape(MLA_Q_LORA, MLA_HEADS, MLA_NOPE + MLA_ROPE)
        wuq = jnp.pad(wuq, ((0, 0), (0, 0), (0, LANES - MLA_NOPE - MLA_ROPE))).reshape(MLA_Q_LORA, -1).astype(BF16)
        wukv = mla_w_ukv[l].reshape(MLA_KV_LORA, MLA_HEADS, MLA_NOPE + MLA_V)
        wuk = jnp.pad(wukv[:, :, :MLA_NOPE], ((0, 0), (0, 0), (0, LANES - MLA_NOPE))).reshape(MLA_KV_LORA, -1)
        wuk = wuk.astype(BF16)
        wuv = wukv[:, :, MLA_NOPE:].reshape(MLA_KV_LORA, -1).astype(BF16)
        sgu_bias = jnp.repeat(sgu_b[l].T, SGU_WIDTH // SGU_GROUPS, axis=1)
        lams = [p[l].reshape(1, DIFF_QK_DIM) for p in (diff_lambda_q1, diff_lambda_k1, diff_lambda_q2, diff_lambda_k2)]
        dg = diff_norm_g[l].reshape(1, DIFF_V_DIM)

        inproj_weights = (w_in_r, sgu_norm_g[l].reshape(1, -1), sgu_w[l].astype(BF16), sgu_bias,
                          mla_q_norm_g[l].reshape(1, -1), mla_kv_norm_g[l].reshape(1, -1), wuq)
        merge_weights = (w_gate[l].astype(BF16), b_gate[l].reshape(1, -1),
                         w_branch_a[l].astype(BF16), w_branch_b[l].astype(BF16), w_branch_c[l].astype(BF16),
                         w_branch_d[l].astype(BF16), w_out[l].astype(BF16), ln1_g[l].reshape(1, -1),
                         ln1_b[l].reshape(1, -1))
        keys = peer_subkeys[l].reshape(2 * PEER_HEADS, PEER_N_KEYS, PEER_KEY_DIM // 2).astype(BF16)
        wqt = peer_w_q[l].T.astype(BF16)
        u_bf, vt_bf = peer_u[l].astype(BF16), peer_v[l].T.astype(BF16)
        ln2 = (ln2_g[l].reshape(1, -1), ln2_b[l].reshape(1, -1))

        def channel_mix(x, mod, oa, ob, oc, od):
            x1, h2t = _merge_call(x, mod, oa, ob, oc, od, *merge_weights)
            r2, e2, n1, e1 = _router_call(h2t, wqt, keys)
            return _peer_call(h2t, u_bf, vt_bf, r2, e2, n1, e1, x1, mod, *ln2)

        oa, mq, ckv, kr, nq, nk, nv, dq, dk, dv = _inproj_call(x_ctx, mod_ctx, rope_id, rope_id, *inproj_weights)
        ob, oc, od = _ctx_attn_call(lambda_init, (mq, ckv, kr, nq, nk, nv, dq, dk, dv), wuk, wuv, lams, dg)
        x_ctx = channel_mix(x_ctx, mod_ctx, oa, ob, oc, od)
        ctx_out.append((ckv.reshape(BATCH, SEQ, MLA_KV_LORA),
                        kr[:, MLA_NOPE:MLA_NOPE + MLA_ROPE].reshape(BATCH, SEQ, MLA_ROPE),
                        nk.reshape(BATCH, SEQ, NA_HEADS, NA_HEAD_DIM),
                        nv.reshape(BATCH, SEQ, NA_HEADS, NA_HEAD_DIM),
                        dk.reshape(BATCH, SEQ, DIFF_HEADS, 2 * DIFF_QK_DIM),
                        dv.reshape(BATCH, SEQ, DIFF_HEADS, DIFF_V_DIM)))

        oa, mq, ckv, kr, nq, nk, nv, dq, dk, dv = _inproj_call(x_lat, mod_lat, rope_b, rope_d, *inproj_weights)
        ob = _lat_mla_call(l, mq, ckv, kr, cache_mla_ckv, cache_kr_pad, wuk, wuv)
        oc = _lat_na_call(l, nq, nk, nv, cache_na_k2, cache_na_v2, _na_bias_table(na_rpb[l]))
        od = _lat_diff_call(l, lambda_init, dq, dk, dv, cache_diff_k2, cache_diff_v2, lams, dg)
        x_lat = channel_mix(x_lat, mod_lat, oa, ob, oc, od)

    y_prompt = x_ctx.reshape(BATCH, SEQ, D_MODEL)
    y_sample = x_lat.reshape(DEC_BATCH, DEC_SEQ, D_MODEL)
    new = [jnp.stack([t[k] for t in ctx_out], axis=1) for k in range(6)]
    return (y_prompt, y_sample, *new)
```

```python
import functools
import math

import jax
import jax.numpy as jnp
from jax import lax
from jax.experimental import pallas as pl
from jax.experimental.pallas import tpu as pltpu

F32 = jnp.float32
BF16 = jnp.bfloat16

D_MODEL = 1024
BATCH = 32
SEQ = 256
DEPTH = 2
DEC_BATCH = 8
DEC_SEQ = 1024
PAST_LEN = 512
GRID_W = 64
CHUNK = 128
SGU_GROUPS = 4
SGU_WIDTH = 256
MLA_HEADS = 4
MLA_Q_LORA = 256
MLA_KV_LORA = 128
MLA_NOPE = 64
MLA_ROPE = 32
MLA_V = 64
NA_HEADS = 4
NA_HEAD_DIM = 64
NA_WIN_ROWS = 8
NA_WIN_COLS = 16
DIFF_HEADS = 4
DIFF_QK_DIM = 64
DIFF_V_DIM = 128
N_BRANCHES = 4
PEER_HEADS = 8
PEER_N_KEYS = 128
PEER_KEY_DIM = 256
PEER_TOPK = 16
ROPE_THETA = 10000.0
LN_EPS = 1e-6
NEG_BIG = -1e30
DEEPNORM_ALPHA = (2 * DEPTH) ** 0.25

LANES = 128
SUBLANES = 8
N_CTX = BATCH * SEQ
N_LAT = DEC_BATCH * DEC_SEQ
N_TOK = N_CTX + N_LAT
N_COND = 16
TM = 512
ROWS = DEC_SEQ // GRID_W
Q_TILE = 256
ROUTER_TILE = 256
PEER_TILE = 1024
EXPERT_BLOCK = 1024
KEYS_PER_BLOCK = EXPERT_BLOCK // PEER_N_KEYS
GATE_LANES = 256
VMEM_LIMIT = 56 * 1024 * 1024

C_AU, C_AV, C_CQ, C_CKV, C_KR = 0, 256, 512, 768, 896
C_NQ, C_NK, C_NV, C_DQ, C_DK, C_DV, C_END = 1024, 1280, 1536, 1792, 2304, 2816, 3328


def _ln(x):
    mu = jnp.mean(x, axis=-1, keepdims=True)
    xc = x - mu
    var = jnp.mean(xc * xc, axis=-1, keepdims=True)
    return xc * lax.rsqrt(var + LN_EPS)


def _rms(x):
    return x * lax.rsqrt(jnp.mean(x * x, axis=-1, keepdims=True) + LN_EPS)


def _gelu(x):
    return 0.5 * x * (1.0 + lax.erf(x * (1.0 / math.sqrt(2.0))))


def _dot(a, b):
    return jnp.dot(a, b, preferred_element_type=F32)


def _dot_nt(a, b):
    return lax.dot_general(a, b, (((1,), (1,)), ((), ())), preferred_element_type=F32)


def _rope(x, tab_ref, half):
    return (x * tab_ref[0] + pltpu.roll(x, LANES - half, 1) * tab_ref[1] + pltpu.roll(x, half, 1) * tab_ref[2])


def _cparams(*sem):
    return pltpu.CompilerParams(dimension_semantics=sem, vmem_limit_bytes=VMEM_LIMIT)


def _mod_kernel(cond_ref, w_ref, b_ref, o_ref):
    c = cond_ref[...]
    s = c * jax.nn.sigmoid(c)
    o_ref[...] = _dot(s, w_ref[...]) + b_ref[...]


def _mod_call(cond, w_mod, b_mod):
    nb = 1536
    return pl.pallas_call(
        _mod_kernel,
        grid=(DEPTH, 6 * D_MODEL // nb),
        in_specs=[pl.BlockSpec((N_COND, D_MODEL), lambda l, j: (0, 0)),
                  pl.BlockSpec((None, D_MODEL, nb), lambda l, j: (l, 0, j)),
                  pl.BlockSpec((None, 1, nb), lambda l, j: (l, 0, j))],
        out_specs=pl.BlockSpec((None, N_COND, nb), lambda l, j: (l, 0, j)),
        out_shape=jax.ShapeDtypeStruct((DEPTH, N_COND, 6 * D_MODEL), F32),
        compiler_params=_cparams("arbitrary", "arbitrary"),
        name="mod_vectors",
    )(cond, w_mod, b_mod.reshape(DEPTH, 1, 6 * D_MODEL))


def _mod_row(i):
    return i // (DEC_SEQ // TM)


def _pos_block(i):
    return i % (DEC_SEQ // TM)


def _inproj_kernel(x_ref, mod_ref, rb_ref, rd_ref, w_in_ref, sgu_g_ref, sgu_w_ref, sgu_bias_ref,
                   qg_ref, kvg_ref, wuq_ref,
                   oa_ref, mq_ref, ckv_ref, kr_ref, nq_ref, nk_ref, nv_ref, dq_ref, dk_ref, dv_ref):
    x = x_ref[...]
    shift = mod_ref[0, :, 0:D_MODEL]
    scale = mod_ref[0, :, D_MODEL:2 * D_MODEL]
    h = (_ln(x) * (1.0 + scale) + shift).astype(BF16)

    ya = _dot(h, w_in_ref[:, C_AU:C_CQ])
    u = _gelu(ya[:, :SGU_WIDTH])
    v = _gelu(ya[:, SGU_WIDTH:])
    vn = (_ln(v) * sgu_g_ref[...]).astype(BF16)
    group = lax.broadcasted_iota(jnp.int32, (CHUNK, SGU_WIDTH), 1) // (SGU_WIDTH // SGU_GROUPS)
    for c in range(TM // CHUNK):
        rows = slice(c * CHUNK, (c + 1) * CHUNK)
        mixed = sgu_bias_ref[...]
        for g in range(SGU_GROUPS):
            mixed = mixed + jnp.where(group == g, _dot(sgu_w_ref[g], vn[rows]), 0.0)
        oa_ref[rows, :] = (u[rows] * mixed).astype(oa_ref.dtype)

    ym = _dot(h, w_in_ref[:, C_CQ:C_NQ])
    cq = (_rms(ym[:, :MLA_Q_LORA]) * qg_ref[...]).astype(BF16)
    mq = _dot(cq, wuq_ref[...])
    for g in range(MLA_HEADS):
        lanes = slice(g * LANES, (g + 1) * LANES)
        mq_ref[:, lanes] = _rope(mq[:, lanes], rb_ref, MLA_ROPE // 2)
    ckv_ref[...] = _rms(ym[:, MLA_Q_LORA:MLA_Q_LORA + MLA_KV_LORA]) * kvg_ref[...]
    kr_ref[...] = _rope(ym[:, MLA_Q_LORA + MLA_KV_LORA:], rb_ref, MLA_ROPE // 2)

    yn = _dot(h, w_in_ref[:, C_NQ:C_DQ])
    nq_ref[...] = yn[:, 0:256]
    nk_ref[...] = yn[:, 256:512]
    nv_ref[...] = yn[:, 512:768]

    yd = _dot(h, w_in_ref[:, C_DQ:C_END])
    for g in range(4):
        lanes = slice(g * LANES, (g + 1) * LANES)
        dq_ref[:, lanes] = _rope(yd[:, g * LANES:(g + 1) * LANES], rd_ref, DIFF_QK_DIM // 2)
        dk_ref[:, lanes] = _rope(yd[:, 512 + g * LANES:512 + (g + 1) * LANES], rd_ref, DIFF_QK_DIM // 2)
    dv_ref[...] = yd[:, 1024:1536]


def _inproj_call(x, mod, rope_b, rope_d, w_in_r, sgu_g, sgu_w, sgu_bias, qg, kvg, wuq):
    tile = lambda w: pl.BlockSpec((TM, w), lambda i: (i, 0))
    full = lambda *s: pl.BlockSpec(s, lambda i: (0,) * len(s))
    widths = (SGU_WIDTH, 512, MLA_KV_LORA, LANES, 256, 256, 256, 512, 512, 512)
    dtypes = (BF16,) + (F32,) * 9
    n_tok = x.shape[0]
    return pl.pallas_call(
        _inproj_kernel,
        grid=(n_tok // TM,),
        in_specs=[tile(D_MODEL),
                  pl.BlockSpec((1, 1, 6 * D_MODEL), lambda i: (_mod_row(i), 0, 0)),
                  pl.BlockSpec((3, TM, LANES), lambda i: (0, _pos_block(i), 0)),
                  pl.BlockSpec((3, TM, LANES), lambda i: (0, _pos_block(i), 0)),
                  full(D_MODEL, C_END), full(1, SGU_WIDTH), full(SGU_GROUPS, CHUNK, CHUNK),
                  full(CHUNK, SGU_WIDTH), full(1, MLA_Q_LORA), full(1, MLA_KV_LORA),
                  full(MLA_Q_LORA, MLA_HEADS * LANES)],
        out_specs=[tile(w) for w in widths],
        out_shape=[jax.ShapeDtypeStruct((n_tok, w), dt) for w, dt in zip(widths, dtypes)],
        compiler_params=_cparams("parallel"),
        name="in_projection",
    )(x, mod, rope_b, rope_d, w_in_r, sgu_g, sgu_w, sgu_bias, qg, kvg, wuq)


def _half_mask(lo):
    lane = lax.broadcasted_iota(jnp.int32, (1, LANES), 1)
    return (lane >= lo) & (lane < lo + 64)


def _softmax_pv(scores, values, lanes):
    m = scores[0].max(axis=-1, keepdims=True)
    for s in scores[1:]:
        m = jnp.maximum(m, s.max(axis=-1, keepdims=True))
    den = None
    o = None
    for s, v in zip(scores, values):
        p = jnp.exp(s - m)
        d = p.sum(axis=-1, keepdims=True)
        den = d if den is None else den + d
        pv = _dot(p.astype(BF16), v[:, lanes])
        o = pv if o is None else o + pv
    return o / den


def _pair_attention(q, keys, vals, scale, bias_fn=None):
    assert math.frexp(scale)[0] == 0.5
    outs = []
    for pair in range(2):
        lanes = slice(pair * LANES, (pair + 1) * LANES)
        qp = q[:, lanes] * jnp.asarray(scale, BF16)
        acc = None
        for sub in range(2):
            head = 2 * pair + sub
            mask = _half_mask(64 * sub)
            qm = jnp.where(mask, qp, jnp.zeros_like(qp))
            scores = [_dot_nt(qm, k[:, lanes]) for k in keys]
            if bias_fn is not None:
                scores = bias_fn(head, scores)
            o = jnp.where(mask, _softmax_pv(scores, vals, lanes), 0.0)
            acc = o if acc is None else acc + o
        outs.append(acc)
    return outs


def _mla_attention(q, k_blocks, v_blocks, o_ref, rows):
    scale = (MLA_NOPE + MLA_ROPE) ** -0.5
    for pair in range(2):
        lanes = slice(pair * LANES, (pair + 1) * LANES)
        acc = None
        for sub in range(2):
            head = 2 * pair + sub
            hl = slice(head * LANES, (head + 1) * LANES)
            scores = [_dot_nt(q[:, hl], k[:, hl]) * scale for k in k_blocks]
            o = jnp.where(_half_mask(64 * sub), _softmax_pv(scores, v_blocks, lanes), 0.0)
            acc = o if acc is None else acc + o
        o_ref[rows, lanes] = acc.astype(o_ref.dtype)


def _diff_lambda(lq1, lk1, lq2, lk2, lambda_init):
    a = jnp.sum(lq1[...] * lk1[...], axis=-1, keepdims=True)
    b = jnp.sum(lq2[...] * lk2[...], axis=-1, keepdims=True)
    return jnp.exp(a) - jnp.exp(b) + lambda_init


def _diff_attention(q, k_blocks, v_blocks, lam, norm_g, lambda_init, o_ref, rows):
    scale = DIFF_QK_DIM ** -0.5
    assert math.frexp(scale)[0] == 0.5
    for head in range(DIFF_HEADS):
        hl = slice(head * LANES, (head + 1) * LANES)
        qh = q[:, hl] * jnp.asarray(scale, BF16)
        probs = []
        for sub in range(2):
            qm = jnp.where(_half_mask(64 * sub), qh, jnp.zeros_like(qh))
            scores = [_dot_nt(qm, k[:, hl]) for k in k_blocks]
            m = scores[0].max(axis=-1, keepdims=True)
            for s in scores[1:]:
                m = jnp.maximum(m, s.max(axis=-1, keepdims=True))
            ps = [jnp.exp(s - m) for s in scores]
            den = ps[0].sum(axis=-1, keepdims=True)
            for p in ps[1:]:
                den = den + p.sum(axis=-1, keepdims=True)
            probs.append((ps, 1.0 / den))
        o = None
        for i, v in enumerate(v_blocks):
            w = probs[0][0][i] * probs[0][1] - probs[1][0][i] * (lam * probs[1][1])
            pv = _dot(w.astype(BF16), v[:, hl])
            o = pv if o is None else o + pv
        o = _rms(o) * norm_g * (1.0 - lambda_init)
        o_ref[rows, hl] = o.astype(o_ref.dtype)


def _ctx_attn_kernel(lambda_init, mq_ref, ckv_ref, kr_ref, nq_ref, nk_ref, nv_ref, dq_ref, dk_ref, dv_ref,
                     wuk_ref, wuv_ref, lq1, lk1, lq2, lk2, dg_ref, ob_ref, oc_ref, od_ref):
    rows = slice(0, SEQ)
    ckv = ckv_ref[...].astype(BF16)
    kr = kr_ref[...]
    k_b = (_dot(ckv, wuk_ref[...]) + jnp.concatenate([kr] * MLA_HEADS, axis=1)).astype(BF16)
    v_b = _dot(ckv, wuv_ref[...]).astype(BF16)
    _mla_attention(mq_ref[...].astype(BF16), [k_b], [v_b], ob_ref, rows)

    outs = _pair_attention(nq_ref[...].astype(BF16), [nk_ref[...].astype(BF16)], [nv_ref[...].astype(BF16)],
                           NA_HEAD_DIM ** -0.5)
    for pair in range(2):
        oc_ref[:, pair * LANES:(pair + 1) * LANES] = outs[pair].astype(oc_ref.dtype)

    lam = _diff_lambda(lq1, lk1, lq2, lk2, lambda_init)
    _diff_attention(dq_ref[...].astype(BF16), [dk_ref[...].astype(BF16)], [dv_ref[...].astype(BF16)],
                    lam, dg_ref[...], lambda_init, od_ref, rows)


def _ctx_attn_call(lambda_init, acts, wuk, wuv, lams, dg):
    mq, ckv, kr, nq, nk, nv, dq, dk, dv = acts
    seq = lambda w: pl.BlockSpec((SEQ, w), lambda b: (b, 0))
    full = lambda *s: pl.BlockSpec(s, lambda b: (0,) * len(s))
    return pl.pallas_call(
        functools.partial(_ctx_attn_kernel, lambda_init),
        grid=(BATCH,),
        in_specs=[seq(512), seq(128), seq(128), seq(256), seq(256), seq(256), seq(512), seq(512), seq(512),
                  full(MLA_KV_LORA, 512), full(MLA_KV_LORA, 256)] + [full(1, DIFF_QK_DIM)] * 4
                 + [full(1, DIFF_V_DIM)],
        out_specs=[seq(256), seq(256), seq(512)],
        out_shape=[jax.ShapeDtypeStruct((N_CTX, w), BF16) for w in (256, 256, 512)],
        compiler_params=_cparams("parallel"),
        name="context_attention",
    )(mq, ckv, kr, nq, nk, nv, dq, dk, dv, wuk, wuv, *lams, dg)


def _lat_mla_kernel(mq_ref, ckv_ref, kr_ref, cckv_ref, ckr_ref, wuk_ref, wuv_ref, o_ref):
    def expand(ckv_f32, kr):
        ckv = ckv_f32.astype(BF16)
        k = (_dot(ckv, wuk_ref[...]) + jnp.concatenate([kr] * MLA_HEADS, axis=1)).astype(BF16)
        return k, _dot(ckv, wuv_ref[...]).astype(BF16)

    k_lat, v_lat = expand(ckv_ref[...], kr_ref[...])
    k_ctx, v_ctx = expand(cckv_ref[...], ckr_ref[...])
    for t in range(DEC_SEQ // Q_TILE):
        rows = slice(t * Q_TILE, (t + 1) * Q_TILE)
        _mla_attention(mq_ref[rows, :].astype(BF16), [k_lat, k_ctx], [v_lat, v_ctx], o_ref, rows)


def _lat_mla_call(l, mq, ckv, kr, cache_ckv, cache_kr_pad, wuk, wuv):
    seq = lambda w: pl.BlockSpec((DEC_SEQ, w), lambda b: (b, 0))
    cache = lambda w: pl.BlockSpec((None, None, PAST_LEN, w), lambda b: (b, l, 0, 0))
    full = lambda *s: pl.BlockSpec(s, lambda b: (0,) * len(s))
    return pl.pallas_call(
        _lat_mla_kernel,
        grid=(DEC_BATCH,),
        in_specs=[seq(512), seq(128), seq(128), cache(MLA_KV_LORA), cache(LANES),
                  full(MLA_KV_LORA, 512), full(MLA_KV_LORA, 256)],
        out_specs=pl.BlockSpec((DEC_SEQ, 256), lambda b: (b, 0)),
        out_shape=jax.ShapeDtypeStruct((N_LAT, 256), BF16),
        compiler_params=_cparams("parallel"),
        name="latent_mla_attention",
    )(mq, ckv, kr, cache_ckv, cache_kr_pad, wuk, wuv)


def _win_start(r):
    return jnp.clip(r - NA_WIN_ROWS // 2, 0, ROWS - NA_WIN_ROWS)


def _lat_na_kernel(nq_ref, nk_ref, nv_ref, ck_ref, cv_ref, bias_ref, o_ref):
    r = pl.program_id(1)
    start = pl.multiple_of(_win_start(r) * GRID_W, GRID_W)
    win = NA_WIN_ROWS * GRID_W
    k_w = nk_ref[pl.ds(start, win), :].astype(BF16)
    v_w = nv_ref[pl.ds(start, win), :].astype(BF16)
    k_c = ck_ref[...].astype(BF16)
    v_c = cv_ref[...].astype(BF16)
    q_col = lax.broadcasted_iota(jnp.int32, (GRID_W, win), 0)
    k_col = lax.broadcasted_iota(jnp.int32, (GRID_W, win), 1) % GRID_W
    c0 = jnp.clip(q_col - NA_WIN_COLS // 2, 0, GRID_W - NA_WIN_COLS)
    col_in = (k_col >= c0) & (k_col < c0 + NA_WIN_COLS)

    def bias_fn(head, scores):
        return [jnp.where(col_in, scores[0] + bias_ref[0, head], NEG_BIG), scores[1]]

    outs = _pair_attention(nq_ref[...].astype(BF16), [k_w, k_c], [v_w, v_c], NA_HEAD_DIM ** -0.5, bias_fn)
    for pair in range(2):
        o_ref[:, pair * LANES:(pair + 1) * LANES] = outs[pair].astype(o_ref.dtype)


def _lat_na_call(l, nq, nk, nv, cache_k, cache_v, bias_tab):
    seq = pl.BlockSpec((DEC_SEQ, 256), lambda b, r: (b, 0))
    cache = pl.BlockSpec((None, None, PAST_LEN, 256), lambda b, r: (b, l, 0, 0))
    return pl.pallas_call(
        _lat_na_kernel,
        grid=(DEC_BATCH, ROWS),
        in_specs=[pl.BlockSpec((GRID_W, 256), lambda b, r: (b * ROWS + r, 0)),
                  seq, seq, cache, cache,
                  pl.BlockSpec((1, NA_HEADS, GRID_W, NA_WIN_ROWS * GRID_W),
                               lambda b, r: (_win_start(r) - r + NA_WIN_ROWS - 1, 0, 0, 0))],
        out_specs=pl.BlockSpec((GRID_W, 256), lambda b, r: (b * ROWS + r, 0)),
        out_shape=jax.ShapeDtypeStruct((N_LAT, 256), BF16),
        compiler_params=_cparams("parallel", "arbitrary"),
        name="latent_neighbourhood_attention",
    )(nq, nk, nv, cache_k, cache_v, bias_tab)


def _lat_diff_kernel(lambda_init, dq_ref, dk_ref, dv_ref, ck_ref, cv_ref, lq1, lk1, lq2, lk2, dg_ref, o_ref):
    lam = _diff_lambda(lq1, lk1, lq2, lk2, lambda_init)
    k_blocks = [dk_ref[...].astype(BF16), ck_ref[...].astype(BF16)]
    v_blocks = [dv_ref[...].astype(BF16), cv_ref[...].astype(BF16)]
    for t in range(DEC_SEQ // Q_TILE):
        rows = slice(t * Q_TILE, (t + 1) * Q_TILE)
        _diff_attention(dq_ref[rows, :].astype(BF16), k_blocks, v_blocks, lam, dg_ref[...], lambda_init,
                        o_ref, rows)


def _lat_diff_call(l, lambda_init, dq, dk, dv, cache_k, cache_v, lams, dg):
    seq = pl.BlockSpec((DEC_SEQ, 512), lambda b: (b, 0))
    cache = pl.BlockSpec((None, None, PAST_LEN, 512), lambda b: (b, l, 0, 0))
    full = lambda *s: pl.BlockSpec(s, lambda b: (0,) * len(s))
    return pl.pallas_call(
        functools.partial(_lat_diff_kernel, lambda_init),
        grid=(DEC_BATCH,),
        in_specs=[seq, seq, seq, cache, cache] + [full(1, DIFF_QK_DIM)] * 4 + [full(1, DIFF_V_DIM)],
        out_specs=pl.BlockSpec((DEC_SEQ, 512), lambda b: (b, 0)),
        out_shape=jax.ShapeDtypeStruct((N_LAT, 512), BF16),
        compiler_params=_cparams("parallel"),
        name="latent_differential_attention",
    )(dq, dk, dv, cache_k, cache_v, *lams, dg)


def _merge_kernel(x_ref, mod_ref, oa_ref, ob_ref, oc_ref, od_ref, wg_ref, bg_ref,
                  wa_ref, wb_ref, wc_ref, wd_ref, wo_ref, g_ref, b_ref, x1_ref, h2t_ref):
    x = x_ref[...]
    mod = lambda k: mod_ref[0, :, k * D_MODEL:(k + 1) * D_MODEL]
    h = (_ln(x) * (1.0 + mod(1)) + mod(0)).astype(BF16)
    merged = None
    for i, (o_ref, w_ref) in enumerate(((oa_ref, wa_ref), (ob_ref, wb_ref), (oc_ref, wc_ref), (od_ref, wd_ref))):
        cols = slice(i * D_MODEL, (i + 1) * D_MODEL)
        gate = jax.nn.sigmoid(_dot(h, wg_ref[:, cols]) + bg_ref[:, cols])
        term = gate * _dot(o_ref[...], w_ref[...])
        merged = term if merged is None else merged + term
    mix = _dot(merged.astype(BF16), wo_ref[...])
    x1 = _ln(DEEPNORM_ALPHA * x + mod(2) * mix) * g_ref[...] + b_ref[...]
    x1_ref[...] = x1
    h2 = _ln(x1) * (1.0 + mod(4)) + mod(3)
    h2t_ref[...] = h2.T.astype(BF16)


def _merge_call(x, mod, oa, ob, oc, od, wg, bg, wa, wb, wc, wd, wo, g, b):
    tile = lambda w: pl.BlockSpec((TM, w), lambda i: (i, 0))
    full = lambda *s: pl.BlockSpec(s, lambda i: (0,) * len(s))
    n_tok = x.shape[0]
    return pl.pallas_call(
        _merge_kernel,
        grid=(n_tok // TM,),
        in_specs=[tile(D_MODEL), pl.BlockSpec((1, 1, 6 * D_MODEL), lambda i: (_mod_row(i), 0, 0)),
                  tile(256), tile(256), tile(256), tile(512),
                  full(D_MODEL, 4 * D_MODEL), full(1, 4 * D_MODEL),
                  full(256, D_MODEL), full(256, D_MODEL), full(256, D_MODEL), full(512, D_MODEL),
                  full(D_MODEL, D_MODEL), full(1, D_MODEL), full(1, D_MODEL)],
        out_specs=[tile(D_MODEL), pl.BlockSpec((D_MODEL, TM), lambda i: (0, i))],
        out_shape=[jax.ShapeDtypeStruct((n_tok, D_MODEL), F32), jax.ShapeDtypeStruct((D_MODEL, n_tok), BF16)],
        compiler_params=_cparams("parallel"),
        name="branch_merge",
    )(x, mod, oa, ob, oc, od, wg, bg, wa, wb, wc, wd, wo, g, b)


KEY_MIN = -2 ** 31


def _tree_sum(terms):
    while len(terms) > 1:
        terms = [a + b for a, b in zip(terms[0::2], terms[1::2])] + ([terms[-1]] if len(terms) % 2 else [])
    return terms[0]


def _row_gather(table, idx):
    ii = idx.astype(jnp.int32)
    low = ii & (SUBLANES - 1)
    outs = []
    for c in range(idx.shape[0] // SUBLANES):
        rows = slice(c * SUBLANES, (c + 1) * SUBLANES)
        lo = jnp.take_along_axis(table[0:SUBLANES], low[rows], axis=0)
        hi = jnp.take_along_axis(table[SUBLANES:PEER_TOPK], low[rows], axis=0)
        outs.append(jnp.where(ii[rows] < SUBLANES, lo, jnp.where(ii[rows] < PEER_TOPK, hi, 0.0)))
    return jnp.concatenate(outs, axis=0)


def _sort_key(x):
    b = lax.bitcast_convert_type(x + 0.0, jnp.int32)
    return b ^ ((b >> 31) & 0x7FFFFFFF)


def _key_value(k):
    return lax.bitcast_convert_type(k ^ ((k >> 31) & 0x7FFFFFFF), F32)


def _top16(s):
    row = lax.broadcasted_iota(jnp.int32, s.shape, 0).astype(F32)
    krow = lax.broadcasted_iota(jnp.int32, (PEER_TOPK, s.shape[1]), 0)

    def body(k, carry):
        work, rank, vals = carry
        m = jnp.max(work, axis=0, keepdims=True)
        idx = jnp.min(jnp.where(work == m, row, float(PEER_N_KEYS)), axis=0, keepdims=True)
        sel = row == idx
        rank = jnp.where(sel, jnp.asarray(k, jnp.int32).astype(F32), rank)
        work = jnp.where(sel, -jnp.inf, work)
        vals = jnp.where(krow == k, m, vals)
        return work, rank, vals

    init = (s, jnp.full(s.shape, float(PEER_N_KEYS), F32), jnp.zeros((PEER_TOPK, s.shape[1]), F32))
    _, rank, vals = lax.fori_loop(0, PEER_TOPK, body, init)
    return vals, rank


def _top16_pair(s1, s2):
    krow = lax.broadcasted_iota(jnp.int32, (PEER_TOPK, LANES), 0)

    def body(k, carry):
        w1, w2, v1, v2 = carry
        code = KEY_MIN + jnp.asarray(k, jnp.int32)
        m1 = jnp.max(w1, axis=0, keepdims=True)
        m2 = jnp.max(w2, axis=0, keepdims=True)
        w1 = jnp.where(w1 == m1, code, w1)
        w2 = jnp.where(w2 == m2, code, w2)
        return w1, w2, jnp.where(krow == k, m1, v1), jnp.where(krow == k, m2, v2)

    zeros = jnp.zeros((PEER_TOPK, LANES), jnp.int32)
    w1, w2, v1, v2 = lax.fori_loop(0, PEER_TOPK, body, (_sort_key(s1), _sort_key(s2), zeros, zeros))

    def decode(w):
        taken = w < KEY_MIN + PEER_TOPK
        rank = jnp.where(taken, (w - KEY_MIN).astype(F32), float(PEER_N_KEYS))
        return rank, jnp.sum(taken.astype(F32), axis=0, keepdims=True)

    r1, c1 = decode(w1)
    r2, c2 = decode(w2)
    ties = jnp.max(jnp.maximum(jnp.abs(c1 - PEER_TOPK), jnp.abs(c2 - PEER_TOPK))) > 0.5

    def exact():
        hs1, q1 = _top16(s1)
        hs2, q2 = _top16(s2)
        return hs1, hs2, q1, q2

    return lax.cond(ties, exact, lambda: (_key_value(v1), _key_value(v2), r1, r2))


def _merge_counts(hs1, hs2):
    krow = lax.broadcasted_iota(jnp.int32, hs1.shape, 0).astype(F32)

    def body(_, carry):
        cnt, front = carry
        m = jnp.max(front, axis=0, keepdims=True)
        win = jnp.min(jnp.where(front == m, krow, float(PEER_TOPK)), axis=0, keepdims=True)
        sel = krow == win
        cnt = jnp.where(sel, cnt + 1.0, cnt)
        nxt = jnp.where(cnt < float(PEER_TOPK), hs1 + _row_gather(hs2, cnt), -jnp.inf)
        return cnt, jnp.where(sel, nxt, front)

    cnt, _ = lax.fori_loop(0, PEER_TOPK, body, (jnp.zeros(hs1.shape, F32), hs1 + hs2[0:1, :]))
    return cnt


def _router_kernel(h2t_ref, wqt_ref, keys_ref, r2_ref, e2_ref, n1_ref, e1_ref, q_scr, s_scr, hs_scr, rank1_scr):
    t = ROUTER_TILE
    q_scr[...] = _dot(wqt_ref[...], h2t_ref[...]).astype(BF16)

    def head_body(hd, _):
        base = pl.multiple_of(hd * PEER_KEY_DIM, PEER_KEY_DIM)
        s_scr[hd, 0] = _dot(keys_ref[2 * hd], q_scr[pl.ds(base, LANES), :])
        s_scr[hd, 1] = _dot(keys_ref[2 * hd + 1], q_scr[pl.ds(base + LANES, LANES), :])
        for j in range(t // LANES):
            lanes = slice(j * LANES, (j + 1) * LANES)
            hs1, hs2, rank1, rank2 = _top16_pair(s_scr[hd, 0, :, lanes], s_scr[hd, 1, :, lanes])
            hs_scr[hd, 0, :, lanes] = hs1
            hs_scr[hd, 1, :, lanes] = hs2
            rank1_scr[hd, :, lanes] = rank1
            r2_ref[hd, :, lanes] = rank2.astype(BF16)
        return 0

    lax.fori_loop(0, PEER_HEADS, head_body, 0)

    for pair in range(PEER_HEADS // 2):
        heads = (2 * pair, 2 * pair + 1)
        hs1 = jnp.concatenate([hs_scr[h, 0] for h in heads], axis=1)
        hs2 = jnp.concatenate([hs_scr[h, 1] for h in heads], axis=1)
        cnt = _merge_counts(hs1, hs2)
        e1r = jnp.exp(hs1 - hs1[0:1, :])
        e2r = jnp.exp(hs2 - hs2[0:1, :])
        prefix = _tree_sum([jnp.where(cnt > float(kb), e2r[kb:kb + 1, :], 0.0) for kb in range(PEER_TOPK)])
        inv_z = 1.0 / jnp.sum(e1r * prefix, axis=0, keepdims=True)
        for i, h in enumerate(heads):
            lanes = slice(i * t, (i + 1) * t)
            e2_ref[h] = (jnp.exp(s_scr[h, 1] - hs2[0:1, lanes]) * inv_z[:, lanes]).astype(BF16)
            e1_ref[h] = 0.5 * jnp.exp(s_scr[h, 0] - hs1[0:1, lanes])
            n1_ref[h] = _row_gather(cnt[:, lanes], rank1_scr[h])


def _router_call(h2t, wqt, keys):
    t = ROUTER_TILE
    out = pl.BlockSpec((PEER_HEADS, PEER_N_KEYS, t), lambda i: (0, 0, i))
    n_tok = h2t.shape[1]
    shape = (PEER_HEADS, PEER_N_KEYS, n_tok)
    return pl.pallas_call(
        _router_kernel,
        grid=(n_tok // t,),
        in_specs=[pl.BlockSpec((D_MODEL, t), lambda i: (0, i)),
                  pl.BlockSpec((PEER_HEADS * PEER_KEY_DIM, D_MODEL), lambda i: (0, 0)),
                  pl.BlockSpec((2 * PEER_HEADS, PEER_N_KEYS, PEER_KEY_DIM // 2), lambda i: (0, 0, 0))],
        out_specs=[out] * 4,
        out_shape=[jax.ShapeDtypeStruct(shape, BF16), jax.ShapeDtypeStruct(shape, BF16),
                   jax.ShapeDtypeStruct(shape, F32), jax.ShapeDtypeStruct(shape, F32)],
        scratch_shapes=[pltpu.VMEM((PEER_HEADS * PEER_KEY_DIM, t), BF16),
                        pltpu.VMEM((PEER_HEADS, 2, PEER_N_KEYS, t), F32),
                        pltpu.VMEM((PEER_HEADS, 2, PEER_TOPK, t), F32),
                        pltpu.VMEM((PEER_HEADS, PEER_N_KEYS, t), F32)],
        compiler_params=_cparams("parallel"),
        name="peer_retrieval",
    )(h2t, wqt, keys)


def _gated_activations(ht_ref, w_ref, r2_ref, e2_ref, n1_ref, e1_ref):
    for i in range(KEYS_PER_BLOCK):
        rows = slice(i * PEER_N_KEYS, (i + 1) * PEER_N_KEYS)
        for j in range(PEER_TILE // GATE_LANES):
            lanes = slice(j * GATE_LANES, (j + 1) * GATE_LANES)
            gate = jnp.zeros((PEER_N_KEYS, GATE_LANES), BF16)
            for hd in range(PEER_HEADS):
                n_row = n1_ref[hd, i:i + 1, lanes].astype(BF16)
                c_row = e1_ref[hd, i:i + 1, lanes].astype(BF16)
                live = jnp.where(r2_ref[hd, :, lanes] < n_row, e2_ref[hd, :, lanes], jnp.zeros((), BF16))
                gate = gate + live * c_row
            x = ht_ref[rows, lanes]
            act = x * (1.0 + lax.erf(x * (1.0 / math.sqrt(2.0))))
            w_ref[rows, lanes] = act.astype(BF16) * gate


def _peer_kernel(h2t_ref, u_ref, vt_ref, r2_ref, e2_ref, n1_ref, e1_ref,
                 x1_ref, mod_ref, g_ref, b_ref, o_ref, acc_ref, ht_ref, w_ref):
    e = pl.program_id(1)

    @pl.when(e == 0)
    def _():
        acc_ref[...] = jnp.zeros_like(acc_ref)

    ht_ref[...] = _dot(u_ref[...], h2t_ref[...])
    _gated_activations(ht_ref, w_ref, r2_ref, e2_ref, n1_ref, e1_ref)
    acc_ref[...] += _dot(vt_ref[...], w_ref[...])

    @pl.when(e == pl.num_programs(1) - 1)
    def _():
        ffn = acc_ref[...].T
        g2 = mod_ref[0, :, 5 * D_MODEL:6 * D_MODEL]
        o_ref[...] = _ln(DEEPNORM_ALPHA * x1_ref[...] + g2 * ffn) * g_ref[...] + b_ref[...]


def _peer_mod_row(i):
    return i // (DEC_SEQ // PEER_TILE)


def _peer_call(h2t, u, vt, r2, e2, n1, e1, x1, mod, g, b):
    t = PEER_TILE
    n_tok = x1.shape[0]
    n_blocks = PEER_N_KEYS * PEER_N_KEYS // EXPERT_BLOCK
    gates = pl.BlockSpec((PEER_HEADS, PEER_N_KEYS, t), lambda i, g: (0, 0, i))
    keys = pl.BlockSpec((PEER_HEADS, KEYS_PER_BLOCK, t), lambda i, g: (0, g, i))
    return pl.pallas_call(
        _peer_kernel,
        grid=(n_tok // t, n_blocks),
        in_specs=[pl.BlockSpec((D_MODEL, t), lambda i, g: (0, i)),
                  pl.BlockSpec((EXPERT_BLOCK, D_MODEL), lambda i, g: (g, 0)),
                  pl.BlockSpec((D_MODEL, EXPERT_BLOCK), lambda i, g: (0, g)),
                  gates, gates, keys, keys,
                  pl.BlockSpec((t, D_MODEL), lambda i, g: (i, 0)),
                  pl.BlockSpec((1, 1, 6 * D_MODEL), lambda i, g: (_peer_mod_row(i), 0, 0)),
                  pl.BlockSpec((1, D_MODEL), lambda i, g: (0, 0)),
                  pl.BlockSpec((1, D_MODEL), lambda i, g: (0, 0))],
        out_specs=pl.BlockSpec((t, D_MODEL), lambda i, g: (i, 0)),
        out_shape=jax.ShapeDtypeStruct((n_tok, D_MODEL), F32),
        scratch_shapes=[pltpu.VMEM((D_MODEL, t), F32), pltpu.VMEM((EXPERT_BLOCK, t), F32),
                        pltpu.VMEM((EXPERT_BLOCK, t), BF16)],
        compiler_params=_cparams("parallel", "arbitrary"),
        name="peer_dense",
    )(h2t, u, vt, r2, e2, n1, e1, x1, mod, g, b)


def _rope_tables():
    t = jnp.arange(DEC_SEQ)
    row = (t // GRID_W).astype(F32)
    col = (t % GRID_W).astype(F32)

    def angles(rot_dim):
        n_freq = rot_dim // 4
        inv_freq = ROPE_THETA ** (-jnp.arange(n_freq, dtype=F32) / n_freq)
        return jnp.concatenate([row[:, None] * inv_freq, col[:, None] * inv_freq], axis=-1)

    def pack(cos_l, sa_l, sb_l):
        return jnp.stack([cos_l, sa_l, sb_l])

    ang_b = angles(MLA_ROPE)
    cb, sb = jnp.cos(ang_b), jnp.sin(ang_b)
    one, zero = jnp.ones((DEC_SEQ, 64), F32), jnp.zeros((DEC_SEQ, 64), F32)
    z16, z32 = jnp.zeros((DEC_SEQ, 16), F32), jnp.zeros((DEC_SEQ, 32), F32)
    rope_b = pack(jnp.concatenate([one, cb, cb, jnp.ones((DEC_SEQ, 32), F32)], axis=1),
                  jnp.concatenate([zero, -sb, z16, z32], axis=1),
                  jnp.concatenate([zero, z16, sb, z32], axis=1))
    ang_d = angles(DIFF_QK_DIM)
    cd, sd = jnp.cos(ang_d), jnp.sin(ang_d)
    rope_d = pack(jnp.concatenate([cd, cd, cd, cd], axis=1),
                  jnp.concatenate([-sd, z32, -sd, z32], axis=1),
                  jnp.concatenate([z32, sd, z32, sd], axis=1))
    ident = pack(jnp.ones((DEC_SEQ, LANES), F32), jnp.zeros((DEC_SEQ, LANES), F32), jnp.zeros((DEC_SEQ, LANES), F32))
    return rope_b, rope_d, ident


def _na_bias_table(rpb):
    col = jnp.arange(GRID_W)
    dc = jnp.clip(col[None, :] - col[:, None], -(NA_WIN_COLS - 1), NA_WIN_COLS - 1) + NA_WIN_COLS - 1
    rpb_cols = rpb[:, :, dc]
    tabs = [rpb_cols[:, off:off + NA_WIN_ROWS].transpose(0, 2, 1, 3).reshape(NA_HEADS, GRID_W, NA_WIN_ROWS * GRID_W)
            for off in range(NA_WIN_ROWS)]
    return jnp.stack(tabs)


def _pad_cols(w, left, right):
    return jnp.pad(w, ((0, 0), (left, right)))


def kernel(x_prompt, x_sample, cache_mla_ckv, cache_mla_krope, cache_na_k, cache_na_v, cache_diff_k, cache_diff_v, c, c_ctx, w_mod, b_mod, w_in, sgu_norm_g, sgu_w, sgu_b, mla_q_norm_g, mla_w_uq, mla_kv_norm_g, mla_w_ukv, na_rpb, diff_lambda_q1, diff_lambda_k1, diff_lambda_q2, diff_lambda_k2, diff_norm_g, w_branch_a, w_branch_b, w_branch_c, w_branch_d, w_gate, b_gate, w_out, ln1_g, ln1_b, peer_w_q, peer_subkeys, peer_u, peer_v, ln2_g, ln2_b):
    x_ctx = x_prompt.reshape(N_CTX, D_MODEL)
    x_lat = x_sample.reshape(N_LAT, D_MODEL)
    cond = jnp.concatenate([c_ctx[None], c, jnp.zeros((N_COND - 1 - DEC_BATCH, D_MODEL), F32)], axis=0)
    mod_all = _mod_call(cond, w_mod, b_mod)
    rope_b, rope_d, rope_id = _rope_tables()
    cache_kr_pad = jnp.pad(cache_mla_krope, ((0, 0), (0, 0), (0, 0), (MLA_NOPE, LANES - MLA_NOPE - MLA_ROPE)))
    cache_na_k2 = cache_na_k.reshape(DEC_BATCH, DEPTH, PAST_LEN, 256)
    cache_na_v2 = cache_na_v.reshape(DEC_BATCH, DEPTH, PAST_LEN, 256)
    cache_diff_k2 = cache_diff_k.reshape(DEC_BATCH, DEPTH, PAST_LEN, 512)
    cache_diff_v2 = cache_diff_v.reshape(DEC_BATCH, DEPTH, PAST_LEN, 512)

    ctx_out = []
    for l in range(DEPTH):
        lambda_init = 0.8 - 0.6 * math.exp(-0.3 * l)
        mod_ctx = jnp.broadcast_to(mod_all[l, 0], (N_CTX // DEC_SEQ, 1, 6 * D_MODEL))
        mod_lat = mod_all[l, 1:1 + DEC_BATCH].reshape(DEC_BATCH, 1, 6 * D_MODEL)

        wi = w_in[l]
        kr_cols = _pad_cols(wi[:, C_KR:C_KR + MLA_ROPE], MLA_NOPE, LANES - MLA_NOPE - MLA_ROPE)
        w_in_r = jnp.concatenate([wi[:, :C_KR], kr_cols, wi[:, C_KR + MLA_ROPE:]], axis=1).astype(BF16)
        wuq = mla_w_uq[l].reshape(MLA_Q_LORA, MLA_HEADS, MLA_NOPE + MLA_ROPE)
        wuq = jnp.pad(wuq, ((0, 0), (0, 0), (0, LANES - MLA_NOPE - MLA_ROPE))).reshape(MLA_Q_LORA, -1).astype(BF16)
        wukv = mla_w_ukv[l].reshape(MLA_KV_LORA, MLA_HEADS, MLA_NOPE + MLA_V)
        wuk = jnp.pad(wukv[:, :, :MLA_NOPE], ((0, 0), (0, 0), (0, LANES - MLA_NOPE))).reshape(MLA_KV_LORA, -1)
        wuk = wuk.astype(BF16)
        wuv = wukv[:, :, MLA_NOPE:].reshape(MLA_KV_LORA, -1).astype(BF16)
        sgu_bias = jnp.repeat(sgu_b[l].T, SGU_WIDTH // SGU_GROUPS, axis=1)
        lams = [p[l].reshape(1, DIFF_QK_DIM) for p in (diff_lambda_q1, diff_lambda_k1, diff_lambda_q2, diff_lambda_k2)]
        dg = diff_norm_g[l].reshape(1, DIFF_V_DIM)

        inproj_weights = (w_in_r, sgu_norm_g[l].reshape(1, -1), sgu_w[l].astype(BF16), sgu_bias,
                          mla_q_norm_g[l].reshape(1, -1), mla_kv_norm_g[l].reshape(1, -1), wuq)
        merge_weights = (w_gate[l].astype(BF16), b_gate[l].reshape(1, -1),
                         w_branch_a[l].astype(BF16), w_branch_b[l].astype(BF16), w_branch_c[l].astype(BF16),
                         w_branch_d[l].astype(BF16), w_out[l].astype(BF16), ln1_g[l].reshape(1, -1),
                         ln1_b[l].reshape(1, -1))
        keys = peer_subkeys[l].reshape(2 * PEER_HEADS, PEER_N_KEYS, PEER_KEY_DIM // 2).astype(BF16)
        wqt = peer_w_q[l].T.astype(BF16)
        u_bf, vt_bf = peer_u[l].astype(BF16), peer_v[l].astype(BF16).T
        ln2 = (ln2_g[l].reshape(1, -1), ln2_b[l].reshape(1, -1))

        def channel_mix(x, mod, oa, ob, oc, od):
            x1, h2t = _merge_call(x, mod, oa, ob, oc, od, *merge_weights)
            r2, e2, n1, e1 = _router_call(h2t, wqt, keys)
            return _peer_call(h2t, u_bf, vt_bf, r2, e2, n1, e1, x1, mod, *ln2)

        oa, mq, ckv, kr, nq, nk, nv, dq, dk, dv = _inproj_call(x_ctx, mod_ctx, rope_id, rope_id, *inproj_weights)
        ob, oc, od = _ctx_attn_call(lambda_init, (mq, ckv, kr, nq, nk, nv, dq, dk, dv), wuk, wuv, lams, dg)
        x_ctx = channel_mix(x_ctx, mod_ctx, oa, ob, oc, od)
        ctx_out.append((ckv.reshape(BATCH, SEQ, MLA_KV_LORA),
                        kr[:, MLA_NOPE:MLA_NOPE + MLA_ROPE].reshape(BATCH, SEQ, MLA_ROPE),
                        nk.reshape(BATCH, SEQ, NA_HEADS, NA_HEAD_DIM),
                        nv.reshape(BATCH, SEQ, NA_HEADS, NA_HEAD_DIM),
                        dk.reshape(BATCH, SEQ, DIFF_HEADS, 2 * DIFF_QK_DIM),
                        dv.reshape(BATCH, SEQ, DIFF_HEADS, DIFF_V_DIM)))

        oa, mq, ckv, kr, nq, nk, nv, dq, dk, dv = _inproj_call(x_lat, mod_lat, rope_b, rope_d, *inproj_weights)
        ob = _lat_mla_call(l, mq, ckv, kr, cache_mla_ckv, cache_kr_pad, wuk, wuv)
        oc = _lat_na_call(l, nq, nk, nv, cache_na_k2, cache_na_v2, _na_bias_table(na_rpb[l]))
        od = _lat_diff_call(l, lambda_init, dq, dk, dv, cache_diff_k2, cache_diff_v2, lams, dg)
        x_lat = channel_mix(x_lat, mod_lat, oa, ob, oc, od)

    y_prompt = x_ctx.reshape(BATCH, SEQ, D_MODEL)
    y_sample = x_lat.reshape(DEC_BATCH, DEC_SEQ, D_MODEL)
    new = [jnp.stack([t[k] for t in ctx_out], axis=1) for k in range(6)]
    return (y_prompt, y_sample, *new)
```

```python
import functools
import math

import jax
import jax.numpy as jnp
from jax import lax
from jax.experimental import pallas as pl
from jax.experimental.pallas import tpu as pltpu

F32 = jnp.float32
BF16 = jnp.bfloat16

D_MODEL = 1024
BATCH = 32
SEQ = 256
DEPTH = 2
DEC_BATCH = 8
DEC_SEQ = 1024
PAST_LEN = 512
GRID_W = 64
CHUNK = 128
SGU_GROUPS = 4
SGU_WIDTH = 256
MLA_HEADS = 4
MLA_Q_LORA = 256
MLA_KV_LORA = 128
MLA_NOPE = 64
MLA_ROPE = 32
MLA_V = 64
NA_HEADS = 4
NA_HEAD_DIM = 64
NA_WIN_ROWS = 8
NA_WIN_COLS = 16
DIFF_HEADS = 4
DIFF_QK_DIM = 64
DIFF_V_DIM = 128
N_BRANCHES = 4
PEER_HEADS = 8
PEER_N_KEYS = 128
PEER_KEY_DIM = 256
PEER_TOPK = 16
ROPE_THETA = 10000.0
LN_EPS = 1e-6
NEG_BIG = -1e30
DEEPNORM_ALPHA = (2 * DEPTH) ** 0.25

LANES = 128
SUBLANES = 8
N_CTX = BATCH * SEQ
N_LAT = DEC_BATCH * DEC_SEQ
N_TOK = N_CTX + N_LAT
N_COND = 16
TM = 512
ROWS = DEC_SEQ // GRID_W
Q_TILE = 256
NA_ROWS_PER_STEP = 2
ROUTER_TILE = 256
PEER_TILE = 1024
EXPERT_BLOCK = 1024
KEYS_PER_BLOCK = EXPERT_BLOCK // PEER_N_KEYS
GATE_LANES = 256
VMEM_LIMIT = 56 * 1024 * 1024

C_AU, C_AV, C_CQ, C_CKV, C_KR = 0, 256, 512, 768, 896
C_NQ, C_NK, C_NV, C_DQ, C_DK, C_DV, C_END = 1024, 1280, 1536, 1792, 2304, 2816, 3328


def _ln(x):
    mu = jnp.mean(x, axis=-1, keepdims=True)
    xc = x - mu
    var = jnp.mean(xc * xc, axis=-1, keepdims=True)
    return xc * lax.rsqrt(var + LN_EPS)


def _rms(x):
    return x * lax.rsqrt(jnp.mean(x * x, axis=-1, keepdims=True) + LN_EPS)


def _gelu(x):
    return 0.5 * x * (1.0 + lax.erf(x * (1.0 / math.sqrt(2.0))))


def _dot(a, b):
    return jnp.dot(a, b, preferred_element_type=F32)


def _dot_nt(a, b):
    return lax.dot_general(a, b, (((1,), (1,)), ((), ())), preferred_element_type=F32)


def _rope(x, tab_ref, half):
    return (x * tab_ref[0] + pltpu.roll(x, LANES - half, 1) * tab_ref[1] + pltpu.roll(x, half, 1) * tab_ref[2])


def _cparams(*sem):
    return pltpu.CompilerParams(dimension_semantics=sem, vmem_limit_bytes=VMEM_LIMIT)


def _mod_kernel(cond_ref, w_ref, b_ref, o_ref):
    c = cond_ref[...]
    s = c * jax.nn.sigmoid(c)
    o_ref[...] = _dot(s, w_ref[...]) + b_ref[...]


def _mod_call(cond, w_mod, b_mod):
    nb = 1536
    return pl.pallas_call(
        _mod_kernel,
        grid=(DEPTH, 6 * D_MODEL // nb),
        in_specs=[pl.BlockSpec((N_COND, D_MODEL), lambda l, j: (0, 0)),
                  pl.BlockSpec((None, D_MODEL, nb), lambda l, j: (l, 0, j)),
                  pl.BlockSpec((None, 1, nb), lambda l, j: (l, 0, j))],
        out_specs=pl.BlockSpec((None, N_COND, nb), lambda l, j: (l, 0, j)),
        out_shape=jax.ShapeDtypeStruct((DEPTH, N_COND, 6 * D_MODEL), F32),
        compiler_params=_cparams("arbitrary", "arbitrary"),
        name="mod_vectors",
    )(cond, w_mod, b_mod.reshape(DEPTH, 1, 6 * D_MODEL))


def _mod_row(i):
    return i // (DEC_SEQ // TM)


def _pos_block(i):
    return i % (DEC_SEQ // TM)


def _inproj_kernel(x_ref, mod_ref, rb_ref, rd_ref, w_in_ref, sgu_g_ref, sgu_w_ref, sgu_bias_ref,
                   qg_ref, kvg_ref, wuq_ref,
                   oa_ref, mq_ref, ckv_ref, kr_ref, nq_ref, nk_ref, nv_ref, dq_ref, dk_ref, dv_ref):
    x = x_ref[...]
    shift = mod_ref[0, :, 0:D_MODEL]
    scale = mod_ref[0, :, D_MODEL:2 * D_MODEL]
    h = (_ln(x) * (1.0 + scale) + shift).astype(BF16)

    ya = _dot(h, w_in_ref[:, C_AU:C_CQ])
    u = _gelu(ya[:, :SGU_WIDTH])
    v = _gelu(ya[:, SGU_WIDTH:])
    vn = (_ln(v) * sgu_g_ref[...]).astype(BF16)
    group = lax.broadcasted_iota(jnp.int32, (CHUNK, SGU_WIDTH), 1) // (SGU_WIDTH // SGU_GROUPS)
    for c in range(TM // CHUNK):
        rows = slice(c * CHUNK, (c + 1) * CHUNK)
        mixed = sgu_bias_ref[...]
        for g in range(SGU_GROUPS):
            mixed = mixed + jnp.where(group == g, _dot(sgu_w_ref[g], vn[rows]), 0.0)
        oa_ref[rows, :] = (u[rows] * mixed).astype(oa_ref.dtype)

    ym = _dot(h, w_in_ref[:, C_CQ:C_NQ])
    cq = (_rms(ym[:, :MLA_Q_LORA]) * qg_ref[...]).astype(BF16)
    mq = _dot(cq, wuq_ref[...])
    for g in range(MLA_HEADS):
        lanes = slice(g * LANES, (g + 1) * LANES)
        mq_ref[:, lanes] = _rope(mq[:, lanes], rb_ref, MLA_ROPE // 2)
    ckv_ref[...] = _rms(ym[:, MLA_Q_LORA:MLA_Q_LORA + MLA_KV_LORA]) * kvg_ref[...]
    kr_ref[...] = _rope(ym[:, MLA_Q_LORA + MLA_KV_LORA:], rb_ref, MLA_ROPE // 2)

    yn = _dot(h, w_in_ref[:, C_NQ:C_DQ])
    nq_ref[...] = yn[:, 0:256]
    nk_ref[...] = yn[:, 256:512]
    nv_ref[...] = yn[:, 512:768]

    yd = _dot(h, w_in_ref[:, C_DQ:C_END])
    for g in range(4):
        lanes = slice(g * LANES, (g + 1) * LANES)
        dq_ref[:, lanes] = _rope(yd[:, g * LANES:(g + 1) * LANES], rd_ref, DIFF_QK_DIM // 2)
        dk_ref[:, lanes] = _rope(yd[:, 512 + g * LANES:512 + (g + 1) * LANES], rd_ref, DIFF_QK_DIM // 2)
    dv_ref[...] = yd[:, 1024:1536]


def _inproj_call(x, mod, rope_b, rope_d, w_in_r, sgu_g, sgu_w, sgu_bias, qg, kvg, wuq):
    tile = lambda w: pl.BlockSpec((TM, w), lambda i: (i, 0))
    full = lambda *s: pl.BlockSpec(s, lambda i: (0,) * len(s))
    widths = (SGU_WIDTH, 512, MLA_KV_LORA, LANES, 256, 256, 256, 512, 512, 512)
    dtypes = (BF16,) + (F32,) * 9
    n_tok = x.shape[0]
    return pl.pallas_call(
        _inproj_kernel,
        grid=(n_tok // TM,),
        in_specs=[tile(D_MODEL),
                  pl.BlockSpec((1, 1, 6 * D_MODEL), lambda i: (_mod_row(i), 0, 0)),
                  pl.BlockSpec((3, TM, LANES), lambda i: (0, _pos_block(i), 0)),
                  pl.BlockSpec((3, TM, LANES), lambda i: (0, _pos_block(i), 0)),
                  full(D_MODEL, C_END), full(1, SGU_WIDTH), full(SGU_GROUPS, CHUNK, CHUNK),
                  full(CHUNK, SGU_WIDTH), full(1, MLA_Q_LORA), full(1, MLA_KV_LORA),
                  full(MLA_Q_LORA, MLA_HEADS * LANES)],
        out_specs=[tile(w) for w in widths],
        out_shape=[jax.ShapeDtypeStruct((n_tok, w), dt) for w, dt in zip(widths, dtypes)],
        compiler_params=_cparams("parallel"),
        name="in_projection",
    )(x, mod, rope_b, rope_d, w_in_r, sgu_g, sgu_w, sgu_bias, qg, kvg, wuq)


def _half_mask(lo):
    lane = lax.broadcasted_iota(jnp.int32, (1, LANES), 1)
    return (lane >= lo) & (lane < lo + 64)


def _softmax_pv(scores, values, lanes):
    m = scores[0].max(axis=-1, keepdims=True)
    for s in scores[1:]:
        m = jnp.maximum(m, s.max(axis=-1, keepdims=True))
    den = None
    o = None
    for s, v in zip(scores, values):
        p = jnp.exp(s - m)
        d = p.sum(axis=-1, keepdims=True)
        den = d if den is None else den + d
        pv = _dot(p.astype(BF16), v[:, lanes])
        o = pv if o is None else o + pv
    return o / den


def _pair_attention(q, keys, vals, scale, bias_fn=None):
    assert math.frexp(scale)[0] == 0.5
    outs = []
    for pair in range(2):
        lanes = slice(pair * LANES, (pair + 1) * LANES)
        qp = q[:, lanes] * jnp.asarray(scale, BF16)
        acc = None
        for sub in range(2):
            head = 2 * pair + sub
            mask = _half_mask(64 * sub)
            qm = jnp.where(mask, qp, jnp.zeros_like(qp))
            scores = [_dot_nt(qm, k[:, lanes]) for k in keys]
            if bias_fn is not None:
                scores = bias_fn(head, scores)
            o = jnp.where(mask, _softmax_pv(scores, vals, lanes), 0.0)
            acc = o if acc is None else acc + o
        outs.append(acc)
    return outs


def _mla_attention(q, k_blocks, v_blocks, o_ref, rows):
    scale = (MLA_NOPE + MLA_ROPE) ** -0.5
    for pair in range(2):
        lanes = slice(pair * LANES, (pair + 1) * LANES)
        acc = None
        for sub in range(2):
            head = 2 * pair + sub
            hl = slice(head * LANES, (head + 1) * LANES)
            scores = [_dot_nt(q[:, hl], k[:, hl]) * scale for k in k_blocks]
            o = jnp.where(_half_mask(64 * sub), _softmax_pv(scores, v_blocks, lanes), 0.0)
            acc = o if acc is None else acc + o
        o_ref[rows, lanes] = acc.astype(o_ref.dtype)


def _diff_lambda(lq1, lk1, lq2, lk2, lambda_init):
    a = jnp.sum(lq1[...] * lk1[...], axis=-1, keepdims=True)
    b = jnp.sum(lq2[...] * lk2[...], axis=-1, keepdims=True)
    return jnp.exp(a) - jnp.exp(b) + lambda_init


def _diff_attention(q, k_blocks, v_blocks, lam, norm_g, lambda_init, o_ref, rows):
    scale = DIFF_QK_DIM ** -0.5
    assert math.frexp(scale)[0] == 0.5
    for head in range(DIFF_HEADS):
        hl = slice(head * LANES, (head + 1) * LANES)
        qh = q[:, hl] * jnp.asarray(scale, BF16)
        probs = []
        for sub in range(2):
            qm = jnp.where(_half_mask(64 * sub), qh, jnp.zeros_like(qh))
            scores = [_dot_nt(qm, k[:, hl]) for k in k_blocks]
            m = scores[0].max(axis=-1, keepdims=True)
            for s in scores[1:]:
                m = jnp.maximum(m, s.max(axis=-1, keepdims=True))
            ps = [jnp.exp(s - m) for s in scores]
            den = ps[0].sum(axis=-1, keepdims=True)
            for p in ps[1:]:
                den = den + p.sum(axis=-1, keepdims=True)
            probs.append((ps, 1.0 / den))
        o = None
        for i, v in enumerate(v_blocks):
            w = probs[0][0][i] * probs[0][1] - probs[1][0][i] * (lam * probs[1][1])
            pv = _dot(w.astype(BF16), v[:, hl])
            o = pv if o is None else o + pv
        o = _rms(o) * norm_g * (1.0 - lambda_init)
        o_ref[rows, hl] = o.astype(o_ref.dtype)


def _ctx_attn_kernel(lambda_init, mq_ref, ckv_ref, kr_ref, nq_ref, nk_ref, nv_ref, dq_ref, dk_ref, dv_ref,
                     wuk_ref, wuv_ref, lq1, lk1, lq2, lk2, dg_ref, ob_ref, oc_ref, od_ref):
    rows = slice(0, SEQ)
    ckv = ckv_ref[...].astype(BF16)
    kr = kr_ref[...]
    k_b = (_dot(ckv, wuk_ref[...]) + jnp.concatenate([kr] * MLA_HEADS, axis=1)).astype(BF16)
    v_b = _dot(ckv, wuv_ref[...]).astype(BF16)
    _mla_attention(mq_ref[...].astype(BF16), [k_b], [v_b], ob_ref, rows)

    outs = _pair_attention(nq_ref[...].astype(BF16), [nk_ref[...].astype(BF16)], [nv_ref[...].astype(BF16)],
                           NA_HEAD_DIM ** -0.5)
    for pair in range(2):
        oc_ref[:, pair * LANES:(pair + 1) * LANES] = outs[pair].astype(oc_ref.dtype)

    lam = _diff_lambda(lq1, lk1, lq2, lk2, lambda_init)
    _diff_attention(dq_ref[...].astype(BF16), [dk_ref[...].astype(BF16)], [dv_ref[...].astype(BF16)],
                    lam, dg_ref[...], lambda_init, od_ref, rows)


def _ctx_attn_call(lambda_init, acts, wuk, wuv, lams, dg):
    mq, ckv, kr, nq, nk, nv, dq, dk, dv = acts
    seq = lambda w: pl.BlockSpec((SEQ, w), lambda b: (b, 0))
    full = lambda *s: pl.BlockSpec(s, lambda b: (0,) * len(s))
    return pl.pallas_call(
        functools.partial(_ctx_attn_kernel, lambda_init),
        grid=(BATCH,),
        in_specs=[seq(512), seq(128), seq(128), seq(256), seq(256), seq(256), seq(512), seq(512), seq(512),
                  full(MLA_KV_LORA, 512), full(MLA_KV_LORA, 256)] + [full(1, DIFF_QK_DIM)] * 4
                 + [full(1, DIFF_V_DIM)],
        out_specs=[seq(256), seq(256), seq(512)],
        out_shape=[jax.ShapeDtypeStruct((N_CTX, w), BF16) for w in (256, 256, 512)],
        compiler_params=_cparams("parallel"),
        name="context_attention",
    )(mq, ckv, kr, nq, nk, nv, dq, dk, dv, wuk, wuv, *lams, dg)


def _lat_mla_kernel(mq_ref, ckv_ref, kr_ref, cckv_ref, ckr_ref, wuk_ref, wuv_ref, o_ref):
    def expand(ckv_f32, kr):
        ckv = ckv_f32.astype(BF16)
        k = (_dot(ckv, wuk_ref[...]) + jnp.concatenate([kr] * MLA_HEADS, axis=1)).astype(BF16)
        return k, _dot(ckv, wuv_ref[...]).astype(BF16)

    k_lat, v_lat = expand(ckv_ref[...], kr_ref[...])
    k_ctx, v_ctx = expand(cckv_ref[...], ckr_ref[...])
    for t in range(DEC_SEQ // Q_TILE):
        rows = slice(t * Q_TILE, (t + 1) * Q_TILE)
        _mla_attention(mq_ref[rows, :].astype(BF16), [k_lat, k_ctx], [v_lat, v_ctx], o_ref, rows)


def _lat_mla_call(l, mq, ckv, kr, cache_ckv, cache_kr_pad, wuk, wuv):
    seq = lambda w: pl.BlockSpec((DEC_SEQ, w), lambda b: (b, 0))
    cache = lambda w: pl.BlockSpec((None, None, PAST_LEN, w), lambda b: (b, l, 0, 0))
    full = lambda *s: pl.BlockSpec(s, lambda b: (0,) * len(s))
    return pl.pallas_call(
        _lat_mla_kernel,
        grid=(DEC_BATCH,),
        in_specs=[seq(512), seq(128), seq(128), cache(MLA_KV_LORA), cache(LANES),
                  full(MLA_KV_LORA, 512), full(MLA_KV_LORA, 256)],
        out_specs=pl.BlockSpec((DEC_SEQ, 256), lambda b: (b, 0)),
        out_shape=jax.ShapeDtypeStruct((N_LAT, 256), BF16),
        compiler_params=_cparams("parallel"),
        name="latent_mla_attention",
    )(mq, ckv, kr, cache_ckv, cache_kr_pad, wuk, wuv)


def _win_start(r):
    return jnp.clip(r - NA_WIN_ROWS // 2, 0, ROWS - NA_WIN_ROWS)


def _lat_na_kernel(nq_ref, nk_ref, nv_ref, ck_ref, cv_ref, bias_ref, o_ref):
    win = NA_WIN_ROWS * GRID_W
    k_c = ck_ref[...].astype(BF16)
    v_c = cv_ref[...].astype(BF16)
    q_col = lax.broadcasted_iota(jnp.int32, (GRID_W, win), 0)
    k_col = lax.broadcasted_iota(jnp.int32, (GRID_W, win), 1) % GRID_W
    c0 = jnp.clip(q_col - NA_WIN_COLS // 2, 0, GRID_W - NA_WIN_COLS)
    col_in = (k_col >= c0) & (k_col < c0 + NA_WIN_COLS)

    for rr in range(NA_ROWS_PER_STEP):
        r = pl.program_id(1) * NA_ROWS_PER_STEP + rr
        first = _win_start(r)
        start = pl.multiple_of(first * GRID_W, GRID_W)
        off = first - r + NA_WIN_ROWS - 1
        k_w = nk_ref[pl.ds(start, win), :].astype(BF16)
        v_w = nv_ref[pl.ds(start, win), :].astype(BF16)

        def bias_fn(head, scores, off=off):
            return [jnp.where(col_in, scores[0] + bias_ref[off, head], NEG_BIG), scores[1]]

        rows = slice(rr * GRID_W, (rr + 1) * GRID_W)
        outs = _pair_attention(nq_ref[rows, :].astype(BF16), [k_w, k_c], [v_w, v_c], NA_HEAD_DIM ** -0.5, bias_fn)
        for pair in range(2):
            o_ref[rows, pair * LANES:(pair + 1) * LANES] = outs[pair].astype(o_ref.dtype)


def _lat_na_call(l, nq, nk, nv, cache_k, cache_v, bias_tab):
    seq = pl.BlockSpec((DEC_SEQ, 256), lambda b, r: (b, 0))
    cache = pl.BlockSpec((None, None, PAST_LEN, 256), lambda b, r: (b, l, 0, 0))
    steps = ROWS // NA_ROWS_PER_STEP
    q_rows = NA_ROWS_PER_STEP * GRID_W
    return pl.pallas_call(
        _lat_na_kernel,
        grid=(DEC_BATCH, steps),
        in_specs=[pl.BlockSpec((q_rows, 256), lambda b, r: (b * steps + r, 0)),
                  seq, seq, cache, cache,
                  pl.BlockSpec((NA_WIN_ROWS, NA_HEADS, GRID_W, NA_WIN_ROWS * GRID_W), lambda b, r: (0, 0, 0, 0))],
        out_specs=pl.BlockSpec((q_rows, 256), lambda b, r: (b * steps + r, 0)),
        out_shape=jax.ShapeDtypeStruct((N_LAT, 256), BF16),
        compiler_params=_cparams("parallel", "arbitrary"),
        name="latent_neighbourhood_attention",
    )(nq, nk, nv, cache_k, cache_v, bias_tab)


def _lat_diff_kernel(lambda_init, dq_ref, dk_ref, dv_ref, ck_ref, cv_ref, lq1, lk1, lq2, lk2, dg_ref, o_ref):
    lam = _diff_lambda(lq1, lk1, lq2, lk2, lambda_init)
    k_blocks = [dk_ref[...].astype(BF16), ck_ref[...].astype(BF16)]
    v_blocks = [dv_ref[...].astype(BF16), cv_ref[...].astype(BF16)]
    for t in range(DEC_SEQ // Q_TILE):
        rows = slice(t * Q_TILE, (t + 1) * Q_TILE)
        _diff_attention(dq_ref[rows, :].astype(BF16), k_blocks, v_blocks, lam, dg_ref[...], lambda_init,
                        o_ref, rows)


def _lat_diff_call(l, lambda_init, dq, dk, dv, cache_k, cache_v, lams, dg):
    seq = pl.BlockSpec((DEC_SEQ, 512), lambda b: (b, 0))
    cache = pl.BlockSpec((None, None, PAST_LEN, 512), lambda b: (b, l, 0, 0))
    full = lambda *s: pl.BlockSpec(s, lambda b: (0,) * len(s))
    return pl.pallas_call(
        functools.partial(_lat_diff_kernel, lambda_init),
        grid=(DEC_BATCH,),
        in_specs=[seq, seq, seq, cache, cache] + [full(1, DIFF_QK_DIM)] * 4 + [full(1, DIFF_V_DIM)],
        out_specs=pl.BlockSpec((DEC_SEQ, 512), lambda b: (b, 0)),
        out_shape=jax.ShapeDtypeStruct((N_LAT, 512), BF16),
        compiler_params=_cparams("parallel"),
        name="latent_differential_attention",
    )(dq, dk, dv, cache_k, cache_v, *lams, dg)


def _merge_kernel(x_ref, mod_ref, oa_ref, ob_ref, oc_ref, od_ref, wg_ref, bg_ref,
                  wa_ref, wb_ref, wc_ref, wd_ref, wo_ref, g_ref, b_ref, x1_ref, h2t_ref):
    x = x_ref[...]
    mod = lambda k: mod_ref[0, :, k * D_MODEL:(k + 1) * D_MODEL]
    h = (_ln(x) * (1.0 + mod(1)) + mod(0)).astype(BF16)
    merged = None
    for i, (o_ref, w_ref) in enumerate(((oa_ref, wa_ref), (ob_ref, wb_ref), (oc_ref, wc_ref), (od_ref, wd_ref))):
        cols = slice(i * D_MODEL, (i + 1) * D_MODEL)
        gate = jax.nn.sigmoid(_dot(h, wg_ref[:, cols]) + bg_ref[:, cols])
        term = gate * _dot(o_ref[...], w_ref[...])
        merged = term if merged is None else merged + term
    mix = _dot(merged.astype(BF16), wo_ref[...])
    x1 = _ln(DEEPNORM_ALPHA * x + mod(2) * mix) * g_ref[...] + b_ref[...]
    x1_ref[...] = x1
    h2 = _ln(x1) * (1.0 + mod(4)) + mod(3)
    h2t_ref[...] = h2.T.astype(BF16)


def _merge_call(x, mod, oa, ob, oc, od, wg, bg, wa, wb, wc, wd, wo, g, b):
    tile = lambda w: pl.BlockSpec((TM, w), lambda i: (i, 0))
    full = lambda *s: pl.BlockSpec(s, lambda i: (0,) * len(s))
    n_tok = x.shape[0]
    return pl.pallas_call(
        _merge_kernel,
        grid=(n_tok // TM,),
        in_specs=[tile(D_MODEL), pl.BlockSpec((1, 1, 6 * D_MODEL), lambda i: (_mod_row(i), 0, 0)),
                  tile(256), tile(256), tile(256), tile(512),
                  full(D_MODEL, 4 * D_MODEL), full(1, 4 * D_MODEL),
                  full(256, D_MODEL), full(256, D_MODEL), full(256, D_MODEL), full(512, D_MODEL),
                  full(D_MODEL, D_MODEL), full(1, D_MODEL), full(1, D_MODEL)],
        out_specs=[tile(D_MODEL), pl.BlockSpec((D_MODEL, TM), lambda i: (0, i))],
        out_shape=[jax.ShapeDtypeStruct((n_tok, D_MODEL), F32), jax.ShapeDtypeStruct((D_MODEL, n_tok), BF16)],
        compiler_params=_cparams("parallel"),
        name="branch_merge",
    )(x, mod, oa, ob, oc, od, wg, bg, wa, wb, wc, wd, wo, g, b)


KEY_MIN = -2 ** 31


def _tree_sum(terms):
    while len(terms) > 1:
        terms = [a + b for a, b in zip(terms[0::2], terms[1::2])] + ([terms[-1]] if len(terms) % 2 else [])
    return terms[0]


def _row_gather(table, idx):
    ii = idx.astype(jnp.int32)
    low = ii & (SUBLANES - 1)
    outs = []
    for c in range(idx.shape[0] // SUBLANES):
        rows = slice(c * SUBLANES, (c + 1) * SUBLANES)
        lo = jnp.take_along_axis(table[0:SUBLANES], low[rows], axis=0)
        hi = jnp.take_along_axis(table[SUBLANES:PEER_TOPK], low[rows], axis=0)
        outs.append(jnp.where(ii[rows] < SUBLANES, lo, jnp.where(ii[rows] < PEER_TOPK, hi, 0.0)))
    return jnp.concatenate(outs, axis=0)


def _sort_key(x):
    b = lax.bitcast_convert_type(x + 0.0, jnp.int32)
    return b ^ ((b >> 31) & 0x7FFFFFFF)


def _key_value(k):
    return lax.bitcast_convert_type(k ^ ((k >> 31) & 0x7FFFFFFF), F32)


def _top16(s):
    row = lax.broadcasted_iota(jnp.int32, s.shape, 0).astype(F32)
    krow = lax.broadcasted_iota(jnp.int32, (PEER_TOPK, s.shape[1]), 0)

    def body(k, carry):
        work, rank, vals = carry
        m = jnp.max(work, axis=0, keepdims=True)
        idx = jnp.min(jnp.where(work == m, row, float(PEER_N_KEYS)), axis=0, keepdims=True)
        sel = row == idx
        rank = jnp.where(sel, jnp.asarray(k, jnp.int32).astype(F32), rank)
        work = jnp.where(sel, -jnp.inf, work)
        vals = jnp.where(krow == k, m, vals)
        return work, rank, vals

    init = (s, jnp.full(s.shape, float(PEER_N_KEYS), F32), jnp.zeros((PEER_TOPK, s.shape[1]), F32))
    _, rank, vals = lax.fori_loop(0, PEER_TOPK, body, init)
    return vals, rank


def _top16_pair(s1, s2):
    krow = lax.broadcasted_iota(jnp.int32, (PEER_TOPK, LANES), 0)

    def body(k, carry):
        w1, w2, v1, v2 = carry
        code = KEY_MIN + jnp.asarray(k, jnp.int32)
        m1 = jnp.max(w1, axis=0, keepdims=True)
        m2 = jnp.max(w2, axis=0, keepdims=True)
        w1 = jnp.where(w1 == m1, code, w1)
        w2 = jnp.where(w2 == m2, code, w2)
        return w1, w2, jnp.where(krow == k, m1, v1), jnp.where(krow == k, m2, v2)

    zeros = jnp.zeros((PEER_TOPK, LANES), jnp.int32)
    w1, w2, v1, v2 = lax.fori_loop(0, PEER_TOPK, body, (_sort_key(s1), _sort_key(s2), zeros, zeros))

    def decode(w):
        taken = w < KEY_MIN + PEER_TOPK
        rank = jnp.where(taken, (w - KEY_MIN).astype(F32), float(PEER_N_KEYS))
        return rank, jnp.sum(taken.astype(F32), axis=0, keepdims=True)

    r1, c1 = decode(w1)
    r2, c2 = decode(w2)
    ties = jnp.max(jnp.maximum(jnp.abs(c1 - PEER_TOPK), jnp.abs(c2 - PEER_TOPK))) > 0.5
    return _key_value(v1), _key_value(v2), r1, r2, ties


def _merge_counts(hs1, hs2):
    krow = lax.broadcasted_iota(jnp.int32, hs1.shape, 0).astype(F32)

    def body(_, carry):
        cnt, front = carry
        m = jnp.max(front, axis=0, keepdims=True)
        win = jnp.min(jnp.where(front == m, krow, float(PEER_TOPK)), axis=0, keepdims=True)
        sel = krow == win
        cnt = jnp.where(sel, cnt + 1.0, cnt)
        nxt = jnp.where(cnt < float(PEER_TOPK), hs1 + _row_gather(hs2, cnt), -jnp.inf)
        return cnt, jnp.where(sel, nxt, front)

    cnt, _ = lax.fori_loop(0, PEER_TOPK, body, (jnp.zeros(hs1.shape, F32), hs1 + hs2[0:1, :]))
    return cnt


def _router_kernel(h2t_ref, wqt_ref, keys_ref, r2_ref, e2_ref, n1_ref, e1_ref, q_scr, s_scr, hs_scr, rank1_scr):
    t = ROUTER_TILE
    q_scr[...] = _dot(wqt_ref[...], h2t_ref[...]).astype(BF16)

    def head_body(hd, _):
        base = pl.multiple_of(hd * PEER_KEY_DIM, PEER_KEY_DIM)
        s_scr[hd, 0] = _dot(keys_ref[2 * hd], q_scr[pl.ds(base, LANES), :])
        s_scr[hd, 1] = _dot(keys_ref[2 * hd + 1], q_scr[pl.ds(base + LANES, LANES), :])
        for j in range(t // LANES):
            lanes = slice(j * LANES, (j + 1) * LANES)

            def put(hs1, hs2, rank1, rank2, lanes=lanes):
                hs_scr[hd, 0, :, lanes] = hs1
                hs_scr[hd, 1, :, lanes] = hs2
                rank1_scr[hd, :, lanes] = rank1
                r2_ref[hd, :, lanes] = rank2.astype(BF16)

            *quick, ties = _top16_pair(s_scr[hd, 0, :, lanes], s_scr[hd, 1, :, lanes])
            put(*quick)

            @pl.when(ties)
            def _(lanes=lanes, put=put):
                hs1, rank1 = _top16(s_scr[hd, 0, :, lanes])
                hs2, rank2 = _top16(s_scr[hd, 1, :, lanes])
                put(hs1, hs2, rank1, rank2)
        return 0

    lax.fori_loop(0, PEER_HEADS, head_body, 0)

    for pair in range(PEER_HEADS // 2):
        heads = (2 * pair, 2 * pair + 1)
        hs1 = jnp.concatenate([hs_scr[h, 0] for h in heads], axis=1)
        hs2 = jnp.concatenate([hs_scr[h, 1] for h in heads], axis=1)
        cnt = _merge_counts(hs1, hs2)
        e1r = jnp.exp(hs1 - hs1[0:1, :])
        e2r = jnp.exp(hs2 - hs2[0:1, :])
        prefix = _tree_sum([jnp.where(cnt > float(kb), e2r[kb:kb + 1, :], 0.0) for kb in range(PEER_TOPK)])
        inv_z = 1.0 / jnp.sum(e1r * prefix, axis=0, keepdims=True)
        for i, h in enumerate(heads):
            lanes = slice(i * t, (i + 1) * t)
            e2_ref[h] = (jnp.exp(s_scr[h, 1] - hs2[0:1, lanes]) * inv_z[:, lanes]).astype(BF16)
            e1_ref[h] = 0.5 * jnp.exp(s_scr[h, 0] - hs1[0:1, lanes])
            n1_ref[h] = _row_gather(cnt[:, lanes], rank1_scr[h])


def _router_call(h2t, wqt, keys):
    t = ROUTER_TILE
    out = pl.BlockSpec((PEER_HEADS, PEER_N_KEYS, t), lambda i: (0, 0, i))
    n_tok = h2t.shape[1]
    shape = (PEER_HEADS, PEER_N_KEYS, n_tok)
    return pl.pallas_call(
        _router_kernel,
        grid=(n_tok // t,),
        in_specs=[pl.BlockSpec((D_MODEL, t), lambda i: (0, i)),
                  pl.BlockSpec((PEER_HEADS * PEER_KEY_DIM, D_MODEL), lambda i: (0, 0)),
                  pl.BlockSpec((2 * PEER_HEADS, PEER_N_KEYS, PEER_KEY_DIM // 2), lambda i: (0, 0, 0))],
        out_specs=[out] * 4,
        out_shape=[jax.ShapeDtypeStruct(shape, BF16), jax.ShapeDtypeStruct(shape, BF16),
                   jax.ShapeDtypeStruct(shape, F32), jax.ShapeDtypeStruct(shape, F32)],
        scratch_shapes=[pltpu.VMEM((PEER_HEADS * PEER_KEY_DIM, t), BF16),
                        pltpu.VMEM((PEER_HEADS, 2, PEER_N_KEYS, t), F32),
                        pltpu.VMEM((PEER_HEADS, 2, PEER_TOPK, t), F32),
                        pltpu.VMEM((PEER_HEADS, PEER_N_KEYS, t), F32)],
        compiler_params=_cparams("parallel"),
        name="peer_retrieval",
    )(h2t, wqt, keys)


def _gated_activations(ht_ref, w_ref, r2_ref, e2_ref, n1_ref, e1_ref):
    for i in range(KEYS_PER_BLOCK):
        rows = slice(i * PEER_N_KEYS, (i + 1) * PEER_N_KEYS)
        for j in range(PEER_TILE // GATE_LANES):
            lanes = slice(j * GATE_LANES, (j + 1) * GATE_LANES)
            gate = jnp.zeros((PEER_N_KEYS, GATE_LANES), BF16)
            for hd in range(PEER_HEADS):
                n_row = n1_ref[hd, i:i + 1, lanes].astype(BF16)
                c_row = e1_ref[hd, i:i + 1, lanes].astype(BF16)
                live = jnp.where(r2_ref[hd, :, lanes] < n_row, e2_ref[hd, :, lanes], jnp.zeros((), BF16))
                gate = gate + live * c_row
            x = ht_ref[rows, lanes]
            act = x * (1.0 + lax.erf(x * (1.0 / math.sqrt(2.0))))
            w_ref[rows, lanes] = act.astype(BF16) * gate


def _peer_kernel(h2t_ref, u_ref, v_ref, r2_ref, e2_ref, n1_ref, e1_ref,
                 x1_ref, mod_ref, g_ref, b_ref, o_ref, acc_ref, ht_ref, w_ref):
    e = pl.program_id(1)

    @pl.when(e == 0)
    def _():
        acc_ref[...] = jnp.zeros_like(acc_ref)

    ht_ref[...] = _dot(u_ref[...], h2t_ref[...])
    _gated_activations(ht_ref, w_ref, r2_ref, e2_ref, n1_ref, e1_ref)
    acc_ref[...] += lax.dot_general(v_ref[...], w_ref[...], (((0,), (0,)), ((), ())), preferred_element_type=F32)

    @pl.when(e == pl.num_programs(1) - 1)
    def _():
        ffn = acc_ref[...].T
        g2 = mod_ref[0, :, 5 * D_MODEL:6 * D_MODEL]
        o_ref[...] = _ln(DEEPNORM_ALPHA * x1_ref[...] + g2 * ffn) * g_ref[...] + b_ref[...]


def _peer_mod_row(i):
    return i // (DEC_SEQ // PEER_TILE)


def _peer_call(h2t, u, v, r2, e2, n1, e1, x1, mod, g, b):
    t = PEER_TILE
    n_tok = x1.shape[0]
    n_blocks = PEER_N_KEYS * PEER_N_KEYS // EXPERT_BLOCK
    gates = pl.BlockSpec((PEER_HEADS, PEER_N_KEYS, t), lambda i, g: (0, 0, i))
    keys = pl.BlockSpec((PEER_HEADS, KEYS_PER_BLOCK, t), lambda i, g: (0, g, i))
    return pl.pallas_call(
        _peer_kernel,
        grid=(n_tok // t, n_blocks),
        in_specs=[pl.BlockSpec((D_MODEL, t), lambda i, g: (0, i)),
                  pl.BlockSpec((EXPERT_BLOCK, D_MODEL), lambda i, g: (g, 0)),
                  pl.BlockSpec((EXPERT_BLOCK, D_MODEL), lambda i, g: (g, 0)),
                  gates, gates, keys, keys,
                  pl.BlockSpec((t, D_MODEL), lambda i, g: (i, 0)),
                  pl.BlockSpec((1, 1, 6 * D_MODEL), lambda i, g: (_peer_mod_row(i), 0, 0)),
                  pl.BlockSpec((1, D_MODEL), lambda i, g: (0, 0)),
                  pl.BlockSpec((1, D_MODEL), lambda i, g: (0, 0))],
        out_specs=pl.BlockSpec((t, D_MODEL), lambda i, g: (i, 0)),
        out_shape=jax.ShapeDtypeStruct((n_tok, D_MODEL), F32),
        scratch_shapes=[pltpu.VMEM((D_MODEL, t), F32), pltpu.VMEM((EXPERT_BLOCK, t), F32),
                        pltpu.VMEM((EXPERT_BLOCK, t), BF16)],
        compiler_params=_cparams("parallel", "arbitrary"),
        name="peer_dense",
    )(h2t, u, v, r2, e2, n1, e1, x1, mod, g, b)


def _rope_tables():
    t = jnp.arange(DEC_SEQ)
    row = (t // GRID_W).astype(F32)
    col = (t % GRID_W).astype(F32)

    def angles(rot_dim):
        n_freq = rot_dim // 4
        inv_freq = ROPE_THETA ** (-jnp.arange(n_freq, dtype=F32) / n_freq)
        return jnp.concatenate([row[:, None] * inv_freq, col[:, None] * inv_freq], axis=-1)

    def pack(cos_l, sa_l, sb_l):
        return jnp.stack([cos_l, sa_l, sb_l])

    ang_b = angles(MLA_ROPE)
    cb, sb = jnp.cos(ang_b), jnp.sin(ang_b)
    one, zero = jnp.ones((DEC_SEQ, 64), F32), jnp.zeros((DEC_SEQ, 64), F32)
    z16, z32 = jnp.zeros((DEC_SEQ, 16), F32), jnp.zeros((DEC_SEQ, 32), F32)
    rope_b = pack(jnp.concatenate([one, cb, cb, jnp.ones((DEC_SEQ, 32), F32)], axis=1),
                  jnp.concatenate([zero, -sb, z16, z32], axis=1),
                  jnp.concatenate([zero, z16, sb, z32], axis=1))
    ang_d = angles(DIFF_QK_DIM)
    cd, sd = jnp.cos(ang_d), jnp.sin(ang_d)
    rope_d = pack(jnp.concatenate([cd, cd, cd, cd], axis=1),
                  jnp.concatenate([-sd, z32, -sd, z32], axis=1),
                  jnp.concatenate([z32, sd, z32, sd], axis=1))
    ident = pack(jnp.ones((DEC_SEQ, LANES), F32), jnp.zeros((DEC_SEQ, LANES), F32), jnp.zeros((DEC_SEQ, LANES), F32))
    return rope_b, rope_d, ident


def _na_bias_table(rpb):
    col = jnp.arange(GRID_W)
    dc = jnp.clip(col[None, :] - col[:, None], -(NA_WIN_COLS - 1), NA_WIN_COLS - 1) + NA_WIN_COLS - 1
    rpb_cols = rpb[:, :, dc]
    tabs = [rpb_cols[:, off:off + NA_WIN_ROWS].transpose(0, 2, 1, 3).reshape(NA_HEADS, GRID_W, NA_WIN_ROWS * GRID_W)
            for off in range(NA_WIN_ROWS)]
    return jnp.stack(tabs)


def _pad_cols(w, left, right):
    return jnp.pad(w, ((0, 0), (left, right)))


def kernel(x_prompt, x_sample, cache_mla_ckv, cache_mla_krope, cache_na_k, cache_na_v, cache_diff_k, cache_diff_v, c, c_ctx, w_mod, b_mod, w_in, sgu_norm_g, sgu_w, sgu_b, mla_q_norm_g, mla_w_uq, mla_kv_norm_g, mla_w_ukv, na_rpb, diff_lambda_q1, diff_lambda_k1, diff_lambda_q2, diff_lambda_k2, diff_norm_g, w_branch_a, w_branch_b, w_branch_c, w_branch_d, w_gate, b_gate, w_out, ln1_g, ln1_b, peer_w_q, peer_subkeys, peer_u, peer_v, ln2_g, ln2_b):
    x_ctx = x_prompt.reshape(N_CTX, D_MODEL)
    x_lat = x_sample.reshape(N_LAT, D_MODEL)
    cond = jnp.concatenate([c_ctx[None], c, jnp.zeros((N_COND - 1 - DEC_BATCH, D_MODEL), F32)], axis=0)
    mod_all = _mod_call(cond, w_mod, b_mod)
    rope_b, rope_d, rope_id = _rope_tables()
    cache_kr_pad = jnp.pad(cache_mla_krope, ((0, 0), (0, 0), (0, 0), (MLA_NOPE, LANES - MLA_NOPE - MLA_ROPE)))
    cache_na_k2 = cache_na_k.reshape(DEC_BATCH, DEPTH, PAST_LEN, 256)
    cache_na_v2 = cache_na_v.reshape(DEC_BATCH, DEPTH, PAST_LEN, 256)
    cache_diff_k2 = cache_diff_k.reshape(DEC_BATCH, DEPTH, PAST_LEN, 512)
    cache_diff_v2 = cache_diff_v.reshape(DEC_BATCH, DEPTH, PAST_LEN, 512)

    ctx_out = []
    for l in range(DEPTH):
        lambda_init = 0.8 - 0.6 * math.exp(-0.3 * l)
        mod_ctx = jnp.broadcast_to(mod_all[l, 0], (N_CTX // DEC_SEQ, 1, 6 * D_MODEL))
        mod_lat = mod_all[l, 1:1 + DEC_BATCH].reshape(DEC_BATCH, 1, 6 * D_MODEL)

        wi = w_in[l]
        kr_cols = _pad_cols(wi[:, C_KR:C_KR + MLA_ROPE], MLA_NOPE, LANES - MLA_NOPE - MLA_ROPE)
        w_in_r = jnp.concatenate([wi[:, :C_KR], kr_cols, wi[:, C_KR + MLA_ROPE:]], axis=1).astype(BF16)
        wuq = mla_w_uq[l].reshape(MLA_Q_LORA, MLA_HEADS, MLA_NOPE + MLA_ROPE)
        wuq = jnp.pad(wuq, ((0, 0), (0, 0), (0, LANES - MLA_NOPE - MLA_ROPE))).reshape(MLA_Q_LORA, -1).astype(BF16)
        wukv = mla_w_ukv[l].reshape(MLA_KV_LORA, MLA_HEADS, MLA_NOPE + MLA_V)
        wuk = jnp.pad(wukv[:, :, :MLA_NOPE], ((0, 0), (0, 0), (0, LANES - MLA_NOPE))).reshape(MLA_KV_LORA, -1)
        wuk = wuk.astype(BF16)
        wuv = wukv[:, :, MLA_NOPE:].reshape(MLA_KV_LORA, -1).astype(BF16)
        sgu_bias = jnp.repeat(sgu_b[l].T, SGU_WIDTH // SGU_GROUPS, axis=1)
        lams = [p[l].reshape(1, DIFF_QK_DIM) for p in (diff_lambda_q1, diff_lambda_k1, diff_lambda_q2, diff_lambda_k2)]
        dg = diff_norm_g[l].reshape(1, DIFF_V_DIM)

        inproj_weights = (w_in_r, sgu_norm_g[l].reshape(1, -1), sgu_w[l].astype(BF16), sgu_bias,
                          mla_q_norm_g[l].reshape(1, -1), mla_kv_norm_g[l].reshape(1, -1), wuq)
        merge_weights = (w_gate[l].astype(BF16), b_gate[l].reshape(1, -1),
                         w_branch_a[l].astype(BF16), w_branch_b[l].astype(BF16), w_branch_c[l].astype(BF16),
                         w_branch_d[l].astype(BF16), w_out[l].astype(BF16), ln1_g[l].reshape(1, -1),
                         ln1_b[l].reshape(1, -1))
        keys = peer_subkeys[l].reshape(2 * PEER_HEADS, PEER_N_KEYS, PEER_KEY_DIM // 2).astype(BF16)
        wqt = peer_w_q[l].T.astype(BF16)
        u_bf, v_bf = peer_u[l].astype(BF16), peer_v[l].astype(BF16)
        ln2 = (ln2_g[l].reshape(1, -1), ln2_b[l].reshape(1, -1))

        def channel_mix(x, mod, oa, ob, oc, od):
            x1, h2t = _merge_call(x, mod, oa, ob, oc, od, *merge_weights)
            r2, e2, n1, e1 = _router_call(h2t, wqt, keys)
            return _peer_call(h2t, u_bf, v_bf, r2, e2, n1, e1, x1, mod, *ln2)

        oa, mq, ckv, kr, nq, nk, nv, dq, dk, dv = _inproj_call(x_ctx, mod_ctx, rope_id, rope_id, *inproj_weights)
        ob, oc, od = _ctx_attn_call(lambda_init, (mq, ckv, kr, nq, nk, nv, dq, dk, dv), wuk, wuv, lams, dg)
        x_ctx = channel_mix(x_ctx, mod_ctx, oa, ob, oc, od)
        ctx_out.append((ckv.reshape(BATCH, SEQ, MLA_KV_LORA),
                        kr[:, MLA_NOPE:MLA_NOPE + MLA_ROPE].reshape(BATCH, SEQ, MLA_ROPE),
                        nk.reshape(BATCH, SEQ, NA_HEADS, NA_HEAD_DIM),
                        nv.reshape(BATCH, SEQ, NA_HEADS, NA_HEAD_DIM),
                        dk.reshape(BATCH, SEQ, DIFF_HEADS, 2 * DIFF_QK_DIM),
                        dv.reshape(BATCH, SEQ, DIFF_HEADS, DIFF_V_DIM)))

        oa, mq, ckv, kr, nq, nk, nv, dq, dk, dv = _inproj_call(x_lat, mod_lat, rope_b, rope_d, *inproj_weights)
        ob = _lat_mla_call(l, mq, ckv, kr, cache_mla_ckv, cache_kr_pad, wuk, wuv)
        oc = _lat_na_call(l, nq, nk, nv, cache_na_k2, cache_na_v2, _na_bias_table(na_rpb[l]))
        od = _lat_diff_call(l, lambda_init, dq, dk, dv, cache_diff_k2, cache_diff_v2, lams, dg)
        x_lat = channel_mix(x_lat, mod_lat, oa, ob, oc, od)

    y_prompt = x_ctx.reshape(BATCH, SEQ, D_MODEL)
    y_sample = x_lat.reshape(DEC_BATCH, DEC_SEQ, D_MODEL)
    new = [jnp.stack([t[k] for t in ctx_out], axis=1) for k in range(6)]
    return (y_prompt, y_sample, *new)
```

```python
import functools
import math

import jax
import jax.numpy as jnp
from jax import lax
from jax.experimental import pallas as pl
from jax.experimental.pallas import tpu as pltpu

F32 = jnp.float32
BF16 = jnp.bfloat16

D_MODEL = 1024
BATCH = 32
SEQ = 256
DEPTH = 2
DEC_BATCH = 8
DEC_SEQ = 1024
PAST_LEN = 512
GRID_W = 64
CHUNK = 128
SGU_GROUPS = 4
SGU_WIDTH = 256
MLA_HEADS = 4
MLA_Q_LORA = 256
MLA_KV_LORA = 128
MLA_NOPE = 64
MLA_ROPE = 32
MLA_V = 64
NA_HEADS = 4
NA_HEAD_DIM = 64
NA_WIN_ROWS = 8
NA_WIN_COLS = 16
DIFF_HEADS = 4
DIFF_QK_DIM = 64
DIFF_V_DIM = 128
N_BRANCHES = 4
PEER_HEADS = 8
PEER_N_KEYS = 128
PEER_KEY_DIM = 256
PEER_TOPK = 16
ROPE_THETA = 10000.0
LN_EPS = 1e-6
NEG_BIG = -1e30
DEEPNORM_ALPHA = (2 * DEPTH) ** 0.25

LANES = 128
SUBLANES = 8
N_CTX = BATCH * SEQ
N_LAT = DEC_BATCH * DEC_SEQ
N_TOK = N_CTX + N_LAT
N_COND = 16
TM = 512
ROWS = DEC_SEQ // GRID_W
Q_TILE = 256
NA_ROWS_PER_STEP = 2
ROUTER_TILE = 256
PEER_TILE = 1024
EXPERT_BLOCK = 1024
KEYS_PER_BLOCK = EXPERT_BLOCK // PEER_N_KEYS
GATE_LANES = 256
CAST_ROWS = 2048
VMEM_LIMIT = 56 * 1024 * 1024

C_AU, C_AV, C_CQ, C_CKV, C_KR = 0, 256, 512, 768, 896
C_NQ, C_NK, C_NV, C_DQ, C_DK, C_DV, C_END = 1024, 1280, 1536, 1792, 2304, 2816, 3328


def _ln(x):
    mu = jnp.mean(x, axis=-1, keepdims=True)
    xc = x - mu
    var = jnp.mean(xc * xc, axis=-1, keepdims=True)
    return xc * lax.rsqrt(var + LN_EPS)


def _rms(x):
    return x * lax.rsqrt(jnp.mean(x * x, axis=-1, keepdims=True) + LN_EPS)


def _gelu(x):
    return 0.5 * x * (1.0 + lax.erf(x * (1.0 / math.sqrt(2.0))))


def _dot(a, b):
    return jnp.dot(a, b, preferred_element_type=F32)


def _dot_nt(a, b):
    return lax.dot_general(a, b, (((1,), (1,)), ((), ())), preferred_element_type=F32)


def _rope(x, tab_ref, half):
    return (x * tab_ref[0] + pltpu.roll(x, LANES - half, 1) * tab_ref[1] + pltpu.roll(x, half, 1) * tab_ref[2])


def _cparams(*sem):
    return pltpu.CompilerParams(dimension_semantics=sem, vmem_limit_bytes=VMEM_LIMIT)


def _mod_kernel(cond_ref, w_ref, b_ref, o_ref):
    c = cond_ref[...]
    s = c * jax.nn.sigmoid(c)
    o_ref[...] = _dot(s, w_ref[...]) + b_ref[...]


def _mod_call(cond, w_mod, b_mod):
    nb = 1536
    return pl.pallas_call(
        _mod_kernel,
        grid=(DEPTH, 6 * D_MODEL // nb),
        in_specs=[pl.BlockSpec((N_COND, D_MODEL), lambda l, j: (0, 0)),
                  pl.BlockSpec((None, D_MODEL, nb), lambda l, j: (l, 0, j)),
                  pl.BlockSpec((None, 1, nb), lambda l, j: (l, 0, j))],
        out_specs=pl.BlockSpec((None, N_COND, nb), lambda l, j: (l, 0, j)),
        out_shape=jax.ShapeDtypeStruct((DEPTH, N_COND, 6 * D_MODEL), F32),
        compiler_params=_cparams("arbitrary", "arbitrary"),
        name="mod_vectors",
    )(cond, w_mod, b_mod.reshape(DEPTH, 1, 6 * D_MODEL))


def _mod_row(i):
    return i // (DEC_SEQ // TM)


def _pos_block(i):
    return i % (DEC_SEQ // TM)


def _inproj_kernel(x_ref, mod_ref, rb_ref, rd_ref, w_in_ref, sgu_g_ref, sgu_w_ref, sgu_bias_ref,
                   qg_ref, kvg_ref, wuq_ref,
                   oa_ref, mq_ref, ckv_ref, kr_ref, nq_ref, nk_ref, nv_ref, dq_ref, dk_ref, dv_ref):
    x = x_ref[...]
    shift = mod_ref[0, :, 0:D_MODEL]
    scale = mod_ref[0, :, D_MODEL:2 * D_MODEL]
    h = (_ln(x) * (1.0 + scale) + shift).astype(BF16)

    ya = _dot(h, w_in_ref[:, C_AU:C_CQ])
    u = _gelu(ya[:, :SGU_WIDTH])
    v = _gelu(ya[:, SGU_WIDTH:])
    vn = (_ln(v) * sgu_g_ref[...]).astype(BF16)
    group = lax.broadcasted_iota(jnp.int32, (CHUNK, SGU_WIDTH), 1) // (SGU_WIDTH // SGU_GROUPS)
    for c in range(TM // CHUNK):
        rows = slice(c * CHUNK, (c + 1) * CHUNK)
        mixed = sgu_bias_ref[...]
        for g in range(SGU_GROUPS):
            mixed = mixed + jnp.where(group == g, _dot(sgu_w_ref[g], vn[rows]), 0.0)
        oa_ref[rows, :] = (u[rows] * mixed).astype(oa_ref.dtype)

    ym = _dot(h, w_in_ref[:, C_CQ:C_NQ])
    cq = (_rms(ym[:, :MLA_Q_LORA]) * qg_ref[...]).astype(BF16)
    mq = _dot(cq, wuq_ref[...])
    for g in range(MLA_HEADS):
        lanes = slice(g * LANES, (g + 1) * LANES)
        mq_ref[:, lanes] = _rope(mq[:, lanes], rb_ref, MLA_ROPE // 2)
    ckv_ref[...] = _rms(ym[:, MLA_Q_LORA:MLA_Q_LORA + MLA_KV_LORA]) * kvg_ref[...]
    kr_ref[...] = _rope(ym[:, MLA_Q_LORA + MLA_KV_LORA:], rb_ref, MLA_ROPE // 2)

    yn = _dot(h, w_in_ref[:, C_NQ:C_DQ])
    nq_ref[...] = yn[:, 0:256]
    nk_ref[...] = yn[:, 256:512]
    nv_ref[...] = yn[:, 512:768]

    yd = _dot(h, w_in_ref[:, C_DQ:C_END])
    for g in range(4):
        lanes = slice(g * LANES, (g + 1) * LANES)
        dq_ref[:, lanes] = _rope(yd[:, g * LANES:(g + 1) * LANES], rd_ref, DIFF_QK_DIM // 2)
        dk_ref[:, lanes] = _rope(yd[:, 512 + g * LANES:512 + (g + 1) * LANES], rd_ref, DIFF_QK_DIM // 2)
    dv_ref[...] = yd[:, 1024:1536]


def _inproj_call(x, mod, rope_b, rope_d, w_in_r, sgu_g, sgu_w, sgu_bias, qg, kvg, wuq):
    tile = lambda w: pl.BlockSpec((TM, w), lambda i: (i, 0))
    full = lambda *s: pl.BlockSpec(s, lambda i: (0,) * len(s))
    widths = (SGU_WIDTH, 512, MLA_KV_LORA, LANES, 256, 256, 256, 512, 512, 512)
    dtypes = (BF16,) + (F32,) * 9
    n_tok = x.shape[0]
    return pl.pallas_call(
        _inproj_kernel,
        grid=(n_tok // TM,),
        in_specs=[tile(D_MODEL),
                  pl.BlockSpec((1, 1, 6 * D_MODEL), lambda i: (_mod_row(i), 0, 0)),
                  pl.BlockSpec((3, TM, LANES), lambda i: (0, _pos_block(i), 0)),
                  pl.BlockSpec((3, TM, LANES), lambda i: (0, _pos_block(i), 0)),
                  full(D_MODEL, C_END), full(1, SGU_WIDTH), full(SGU_GROUPS, CHUNK, CHUNK),
                  full(CHUNK, SGU_WIDTH), full(1, MLA_Q_LORA), full(1, MLA_KV_LORA),
                  full(MLA_Q_LORA, MLA_HEADS * LANES)],
        out_specs=[tile(w) for w in widths],
        out_shape=[jax.ShapeDtypeStruct((n_tok, w), dt) for w, dt in zip(widths, dtypes)],
        compiler_params=_cparams("parallel"),
        name="in_projection",
    )(x, mod, rope_b, rope_d, w_in_r, sgu_g, sgu_w, sgu_bias, qg, kvg, wuq)


def _half_mask(lo):
    lane = lax.broadcasted_iota(jnp.int32, (1, LANES), 1)
    return (lane >= lo) & (lane < lo + 64)


def _softmax_pv(scores, values, lanes):
    m = scores[0].max(axis=-1, keepdims=True)
    for s in scores[1:]:
        m = jnp.maximum(m, s.max(axis=-1, keepdims=True))
    den = None
    o = None
    for s, v in zip(scores, values):
        p = jnp.exp(s - m)
        d = p.sum(axis=-1, keepdims=True)
        den = d if den is None else den + d
        pv = _dot(p.astype(BF16), v[:, lanes])
        o = pv if o is None else o + pv
    return o / den


def _pair_attention(q, keys, vals, scale, bias_fn=None):
    assert math.frexp(scale)[0] == 0.5
    n = q.shape[0]
    outs = []
    for pair in range(2):
        lanes = slice(pair * LANES, (pair + 1) * LANES)
        qp = q[:, lanes] * jnp.asarray(scale, BF16)
        masks = [_half_mask(64 * sub) for sub in range(2)]
        qs = jnp.concatenate([jnp.where(m, qp, jnp.zeros_like(qp)) for m in masks], axis=0)
        scores = [_dot_nt(qs, k[:, lanes]) for k in keys]
        if bias_fn is not None:
            scores = bias_fn(pair, scores)
        o = _softmax_pv(scores, vals, lanes)
        outs.append(jnp.where(masks[0], o[:n], 0.0) + jnp.where(masks[1], o[n:], 0.0))
    return outs


def _mla_attention(q, k_blocks, v_blocks, o_ref, rows):
    scale = (MLA_NOPE + MLA_ROPE) ** -0.5
    for pair in range(2):
        lanes = slice(pair * LANES, (pair + 1) * LANES)
        acc = None
        for sub in range(2):
            head = 2 * pair + sub
            hl = slice(head * LANES, (head + 1) * LANES)
            scores = [_dot_nt(q[:, hl], k[:, hl]) * scale for k in k_blocks]
            o = jnp.where(_half_mask(64 * sub), _softmax_pv(scores, v_blocks, lanes), 0.0)
            acc = o if acc is None else acc + o
        o_ref[rows, lanes] = acc.astype(o_ref.dtype)


def _diff_lambda(lq1, lk1, lq2, lk2, lambda_init):
    a = jnp.sum(lq1[...] * lk1[...], axis=-1, keepdims=True)
    b = jnp.sum(lq2[...] * lk2[...], axis=-1, keepdims=True)
    return jnp.exp(a) - jnp.exp(b) + lambda_init


def _diff_attention(q, k_blocks, v_blocks, lam, norm_g, lambda_init, o_ref, rows):
    scale = DIFF_QK_DIM ** -0.5
    assert math.frexp(scale)[0] == 0.5
    for head in range(DIFF_HEADS):
        hl = slice(head * LANES, (head + 1) * LANES)
        qh = q[:, hl] * jnp.asarray(scale, BF16)
        probs = []
        for sub in range(2):
            qm = jnp.where(_half_mask(64 * sub), qh, jnp.zeros_like(qh))
            scores = [_dot_nt(qm, k[:, hl]) for k in k_blocks]
            m = scores[0].max(axis=-1, keepdims=True)
            for s in scores[1:]:
                m = jnp.maximum(m, s.max(axis=-1, keepdims=True))
            ps = [jnp.exp(s - m) for s in scores]
            den = ps[0].sum(axis=-1, keepdims=True)
            for p in ps[1:]:
                den = den + p.sum(axis=-1, keepdims=True)
            probs.append((ps, 1.0 / den))
        o = None
        for i, v in enumerate(v_blocks):
            w = probs[0][0][i] * probs[0][1] - probs[1][0][i] * (lam * probs[1][1])
            pv = _dot(w.astype(BF16), v[:, hl])
            o = pv if o is None else o + pv
        o = _rms(o) * norm_g * (1.0 - lambda_init)
        o_ref[rows, hl] = o.astype(o_ref.dtype)


def _ctx_attn_kernel(lambda_init, mq_ref, ckv_ref, kr_ref, nq_ref, nk_ref, nv_ref, dq_ref, dk_ref, dv_ref,
                     wuk_ref, wuv_ref, lq1, lk1, lq2, lk2, dg_ref, ob_ref, oc_ref, od_ref):
    rows = slice(0, SEQ)
    ckv = ckv_ref[...].astype(BF16)
    kr = kr_ref[...]
    k_b = (_dot(ckv, wuk_ref[...]) + jnp.concatenate([kr] * MLA_HEADS, axis=1)).astype(BF16)
    v_b = _dot(ckv, wuv_ref[...]).astype(BF16)
    _mla_attention(mq_ref[...].astype(BF16), [k_b], [v_b], ob_ref, rows)

    outs = _pair_attention(nq_ref[...].astype(BF16), [nk_ref[...].astype(BF16)], [nv_ref[...].astype(BF16)],
                           NA_HEAD_DIM ** -0.5)
    for pair in range(2):
        oc_ref[:, pair * LANES:(pair + 1) * LANES] = outs[pair].astype(oc_ref.dtype)

    lam = _diff_lambda(lq1, lk1, lq2, lk2, lambda_init)
    _diff_attention(dq_ref[...].astype(BF16), [dk_ref[...].astype(BF16)], [dv_ref[...].astype(BF16)],
                    lam, dg_ref[...], lambda_init, od_ref, rows)


def _ctx_attn_call(lambda_init, acts, wuk, wuv, lams, dg):
    mq, ckv, kr, nq, nk, nv, dq, dk, dv = acts
    seq = lambda w: pl.BlockSpec((SEQ, w), lambda b: (b, 0))
    full = lambda *s: pl.BlockSpec(s, lambda b: (0,) * len(s))
    return pl.pallas_call(
        functools.partial(_ctx_attn_kernel, lambda_init),
        grid=(BATCH,),
        in_specs=[seq(512), seq(128), seq(128), seq(256), seq(256), seq(256), seq(512), seq(512), seq(512),
                  full(MLA_KV_LORA, 512), full(MLA_KV_LORA, 256)] + [full(1, DIFF_QK_DIM)] * 4
                 + [full(1, DIFF_V_DIM)],
        out_specs=[seq(256), seq(256), seq(512)],
        out_shape=[jax.ShapeDtypeStruct((N_CTX, w), BF16) for w in (256, 256, 512)],
        compiler_params=_cparams("parallel"),
        name="context_attention",
    )(mq, ckv, kr, nq, nk, nv, dq, dk, dv, wuk, wuv, *lams, dg)


def _lat_mla_kernel(mq_ref, ckv_ref, kr_ref, cckv_ref, ckr_ref, wuk_ref, wuv_ref, o_ref):
    def expand(ckv_f32, kr):
        ckv = ckv_f32.astype(BF16)
        k = (_dot(ckv, wuk_ref[...]) + jnp.concatenate([kr] * MLA_HEADS, axis=1)).astype(BF16)
        return k, _dot(ckv, wuv_ref[...]).astype(BF16)

    k_lat, v_lat = expand(ckv_ref[...], kr_ref[...])
    k_ctx, v_ctx = expand(cckv_ref[...], ckr_ref[...])
    k_all = jnp.concatenate([k_lat, k_ctx], axis=0)
    v_all = jnp.concatenate([v_lat, v_ctx], axis=0)
    for t in range(DEC_SEQ // Q_TILE):
        rows = slice(t * Q_TILE, (t + 1) * Q_TILE)
        _mla_attention(mq_ref[rows, :].astype(BF16), [k_all], [v_all], o_ref, rows)


def _lat_mla_call(l, mq, ckv, kr, cache_ckv, cache_kr_pad, wuk, wuv):
    seq = lambda w: pl.BlockSpec((DEC_SEQ, w), lambda b: (b, 0))
    cache = lambda w: pl.BlockSpec((None, None, PAST_LEN, w), lambda b: (b, l, 0, 0))
    full = lambda *s: pl.BlockSpec(s, lambda b: (0,) * len(s))
    return pl.pallas_call(
        _lat_mla_kernel,
        grid=(DEC_BATCH,),
        in_specs=[seq(512), seq(128), seq(128), cache(MLA_KV_LORA), cache(LANES),
                  full(MLA_KV_LORA, 512), full(MLA_KV_LORA, 256)],
        out_specs=pl.BlockSpec((DEC_SEQ, 256), lambda b: (b, 0)),
        out_shape=jax.ShapeDtypeStruct((N_LAT, 256), BF16),
        compiler_params=_cparams("parallel"),
        name="latent_mla_attention",
    )(mq, ckv, kr, cache_ckv, cache_kr_pad, wuk, wuv)


def _win_start(r):
    return jnp.clip(r - NA_WIN_ROWS // 2, 0, ROWS - NA_WIN_ROWS)


def _lat_na_kernel(nq_ref, nk_ref, nv_ref, ck_ref, cv_ref, bias_ref, o_ref):
    win = NA_WIN_ROWS * GRID_W
    k_c = ck_ref[...].astype(BF16)
    v_c = cv_ref[...].astype(BF16)
    q_col = lax.broadcasted_iota(jnp.int32, (2 * GRID_W, win), 0) % GRID_W
    k_col = lax.broadcasted_iota(jnp.int32, (2 * GRID_W, win), 1) % GRID_W
    c0 = jnp.clip(q_col - NA_WIN_COLS // 2, 0, GRID_W - NA_WIN_COLS)
    col_in = (k_col >= c0) & (k_col < c0 + NA_WIN_COLS)

    for rr in range(NA_ROWS_PER_STEP):
        r = pl.program_id(1) * NA_ROWS_PER_STEP + rr
        first = _win_start(r)
        start = pl.multiple_of(first * GRID_W, GRID_W)
        off = first - r + NA_WIN_ROWS - 1
        k_all = jnp.concatenate([nk_ref[pl.ds(start, win), :].astype(BF16), k_c], axis=0)
        v_all = jnp.concatenate([nv_ref[pl.ds(start, win), :].astype(BF16), v_c], axis=0)

        def bias_fn(pair, scores, off=off):
            s = scores[0]
            bias = jnp.concatenate([bias_ref[off, 2 * pair], bias_ref[off, 2 * pair + 1]], axis=0)
            s_win = jnp.where(col_in, s[:, :win] + bias, NEG_BIG)
            return [jnp.concatenate([s_win, s[:, win:]], axis=1)]

        rows = slice(rr * GRID_W, (rr + 1) * GRID_W)
        outs = _pair_attention(nq_ref[rows, :].astype(BF16), [k_all], [v_all], NA_HEAD_DIM ** -0.5, bias_fn)
        for pair in range(2):
            o_ref[rows, pair * LANES:(pair + 1) * LANES] = outs[pair].astype(o_ref.dtype)


def _lat_na_call(l, nq, nk, nv, cache_k, cache_v, bias_tab):
    seq = pl.BlockSpec((DEC_SEQ, 256), lambda b, r: (b, 0))
    cache = pl.BlockSpec((None, None, PAST_LEN, 256), lambda b, r: (b, l, 0, 0))
    steps = ROWS // NA_ROWS_PER_STEP
    q_rows = NA_ROWS_PER_STEP * GRID_W
    return pl.pallas_call(
        _lat_na_kernel,
        grid=(DEC_BATCH, steps),
        in_specs=[pl.BlockSpec((q_rows, 256), lambda b, r: (b * steps + r, 0)),
                  seq, seq, cache, cache,
                  pl.BlockSpec((NA_WIN_ROWS, NA_HEADS, GRID_W, NA_WIN_ROWS * GRID_W), lambda b, r: (0, 0, 0, 0))],
        out_specs=pl.BlockSpec((q_rows, 256), lambda b, r: (b * steps + r, 0)),
        out_shape=jax.ShapeDtypeStruct((N_LAT, 256), BF16),
        compiler_params=_cparams("parallel", "arbitrary"),
        name="latent_neighbourhood_attention",
    )(nq, nk, nv, cache_k, cache_v, bias_tab)


def _lat_diff_kernel(lambda_init, dq_ref, dk_ref, dv_ref, ck_ref, cv_ref, lq1, lk1, lq2, lk2, dg_ref, o_ref):
    lam = _diff_lambda(lq1, lk1, lq2, lk2, lambda_init)
    k_blocks = [dk_ref[...].astype(BF16), ck_ref[...].astype(BF16)]
    v_blocks = [dv_ref[...].astype(BF16), cv_ref[...].astype(BF16)]
    for t in range(DEC_SEQ // Q_TILE):
        rows = slice(t * Q_TILE, (t + 1) * Q_TILE)
        _diff_attention(dq_ref[rows, :].astype(BF16), k_blocks, v_blocks, lam, dg_ref[...], lambda_init,
                        o_ref, rows)


def _lat_diff_call(l, lambda_init, dq, dk, dv, cache_k, cache_v, lams, dg):
    seq = pl.BlockSpec((DEC_SEQ, 512), lambda b: (b, 0))
    cache = pl.BlockSpec((None, None, PAST_LEN, 512), lambda b: (b, l, 0, 0))
    full = lambda *s: pl.BlockSpec(s, lambda b: (0,) * len(s))
    return pl.pallas_call(
        functools.partial(_lat_diff_kernel, lambda_init),
        grid=(DEC_BATCH,),
        in_specs=[seq, seq, seq, cache, cache] + [full(1, DIFF_QK_DIM)] * 4 + [full(1, DIFF_V_DIM)],
        out_specs=pl.BlockSpec((DEC_SEQ, 512), lambda b: (b, 0)),
        out_shape=jax.ShapeDtypeStruct((N_LAT, 512), BF16),
        compiler_params=_cparams("parallel"),
        name="latent_differential_attention",
    )(dq, dk, dv, cache_k, cache_v, *lams, dg)


def _merge_kernel(x_ref, mod_ref, oa_ref, ob_ref, oc_ref, od_ref, wg_ref, bg_ref,
                  wa_ref, wb_ref, wc_ref, wd_ref, wo_ref, g_ref, b_ref, x1_ref, h2t_ref):
    x = x_ref[...]
    mod = lambda k: mod_ref[0, :, k * D_MODEL:(k + 1) * D_MODEL]
    h = (_ln(x) * (1.0 + mod(1)) + mod(0)).astype(BF16)
    merged = None
    for i, (o_ref, w_ref) in enumerate(((oa_ref, wa_ref), (ob_ref, wb_ref), (oc_ref, wc_ref), (od_ref, wd_ref))):
        cols = slice(i * D_MODEL, (i + 1) * D_MODEL)
        gate = jax.nn.sigmoid(_dot(h, wg_ref[:, cols]) + bg_ref[:, cols])
        term = gate * _dot(o_ref[...], w_ref[...])
        merged = term if merged is None else merged + term
    mix = _dot(merged.astype(BF16), wo_ref[...])
    x1 = _ln(DEEPNORM_ALPHA * x + mod(2) * mix) * g_ref[...] + b_ref[...]
    x1_ref[...] = x1
    h2 = _ln(x1) * (1.0 + mod(4)) + mod(3)
    h2t_ref[...] = h2.T.astype(BF16)


def _merge_call(x, mod, oa, ob, oc, od, wg, bg, wa, wb, wc, wd, wo, g, b):
    tile = lambda w: pl.BlockSpec((TM, w), lambda i: (i, 0))
    full = lambda *s: pl.BlockSpec(s, lambda i: (0,) * len(s))
    n_tok = x.shape[0]
    return pl.pallas_call(
        _merge_kernel,
        grid=(n_tok // TM,),
        in_specs=[tile(D_MODEL), pl.BlockSpec((1, 1, 6 * D_MODEL), lambda i: (_mod_row(i), 0, 0)),
                  tile(256), tile(256), tile(256), tile(512),
                  full(D_MODEL, 4 * D_MODEL), full(1, 4 * D_MODEL),
                  full(256, D_MODEL), full(256, D_MODEL), full(256, D_MODEL), full(512, D_MODEL),
                  full(D_MODEL, D_MODEL), full(1, D_MODEL), full(1, D_MODEL)],
        out_specs=[tile(D_MODEL), pl.BlockSpec((D_MODEL, TM), lambda i: (0, i))],
        out_shape=[jax.ShapeDtypeStruct((n_tok, D_MODEL), F32), jax.ShapeDtypeStruct((D_MODEL, n_tok), BF16)],
        compiler_params=_cparams("parallel"),
        name="branch_merge",
    )(x, mod, oa, ob, oc, od, wg, bg, wa, wb, wc, wd, wo, g, b)


KEY_MIN = -2 ** 31


def _tree_sum(terms):
    while len(terms) > 1:
        terms = [a + b for a, b in zip(terms[0::2], terms[1::2])] + ([terms[-1]] if len(terms) % 2 else [])
    return terms[0]


def _row_gather(table, idx):
    ii = idx.astype(jnp.int32)
    low = ii & (SUBLANES - 1)
    outs = []
    for c in range(idx.shape[0] // SUBLANES):
        rows = slice(c * SUBLANES, (c + 1) * SUBLANES)
        lo = jnp.take_along_axis(table[0:SUBLANES], low[rows], axis=0)
        hi = jnp.take_along_axis(table[SUBLANES:PEER_TOPK], low[rows], axis=0)
        outs.append(jnp.where(ii[rows] < SUBLANES, lo, jnp.where(ii[rows] < PEER_TOPK, hi, 0.0)))
    return jnp.concatenate(outs, axis=0)


def _sort_key(x):
    b = lax.bitcast_convert_type(x + 0.0, jnp.int32)
    return b ^ ((b >> 31) & 0x7FFFFFFF)


def _key_value(k):
    return lax.bitcast_convert_type(k ^ ((k >> 31) & 0x7FFFFFFF), F32)


def _top16(s):
    row = lax.broadcasted_iota(jnp.int32, s.shape, 0).astype(F32)
    krow = lax.broadcasted_iota(jnp.int32, (PEER_TOPK, s.shape[1]), 0)

    def body(k, carry):
        work, rank, vals = carry
        m = jnp.max(work, axis=0, keepdims=True)
        idx = jnp.min(jnp.where(work == m, row, float(PEER_N_KEYS)), axis=0, keepdims=True)
        sel = row == idx
        rank = jnp.where(sel, jnp.asarray(k, jnp.int32).astype(F32), rank)
        work = jnp.where(sel, -jnp.inf, work)
        vals = jnp.where(krow == k, m, vals)
        return work, rank, vals

    init = (s, jnp.full(s.shape, float(PEER_N_KEYS), F32), jnp.zeros((PEER_TOPK, s.shape[1]), F32))
    _, rank, vals = lax.fori_loop(0, PEER_TOPK, body, init)
    return vals, rank


def _top16_pair(s1, s2):
    krow = lax.broadcasted_iota(jnp.int32, (PEER_TOPK, LANES), 0)

    def body(k, carry):
        w1, w2, v1, v2 = carry
        code = KEY_MIN + jnp.asarray(k, jnp.int32)
        m1 = jnp.max(w1, axis=0, keepdims=True)
        m2 = jnp.max(w2, axis=0, keepdims=True)
        w1 = jnp.where(w1 == m1, code, w1)
        w2 = jnp.where(w2 == m2, code, w2)
        return w1, w2, jnp.where(krow == k, m1, v1), jnp.where(krow == k, m2, v2)

    zeros = jnp.zeros((PEER_TOPK, LANES), jnp.int32)
    w1, w2, v1, v2 = lax.fori_loop(0, PEER_TOPK, body, (_sort_key(s1), _sort_key(s2), zeros, zeros))

    def decode(w):
        taken = w < KEY_MIN + PEER_TOPK
        rank = jnp.where(taken, (w - KEY_MIN).astype(F32), float(PEER_N_KEYS))
        return rank, jnp.sum(taken.astype(F32), axis=0, keepdims=True)

    r1, c1 = decode(w1)
    r2, c2 = decode(w2)
    ties = jnp.max(jnp.maximum(jnp.abs(c1 - PEER_TOPK), jnp.abs(c2 - PEER_TOPK))) > 0.5
    return _key_value(v1), _key_value(v2), r1, r2, ties


def _merge_counts(hs1, hs2):
    krow = lax.broadcasted_iota(jnp.int32, hs1.shape, 0).astype(F32)

    def body(_, carry):
        cnt, front = carry
        m = jnp.max(front, axis=0, keepdims=True)
        win = jnp.min(jnp.where(front == m, krow, float(PEER_TOPK)), axis=0, keepdims=True)
        sel = krow == win
        cnt = jnp.where(sel, cnt + 1.0, cnt)
        nxt = jnp.where(cnt < float(PEER_TOPK), hs1 + _row_gather(hs2, cnt), -jnp.inf)
        return cnt, jnp.where(sel, nxt, front)

    cnt, _ = lax.fori_loop(0, PEER_TOPK, body, (jnp.zeros(hs1.shape, F32), hs1 + hs2[0:1, :]))
    return cnt


def _router_kernel(h2t_ref, wqt_ref, keys_ref, r2_ref, e2_ref, n1_ref, e1_ref, q_scr, s_scr, hs_scr, rank1_scr):
    t = ROUTER_TILE
    q_scr[...] = _dot(wqt_ref[...], h2t_ref[...]).astype(BF16)

    def head_body(hd, _):
        base = pl.multiple_of(hd * PEER_KEY_DIM, PEER_KEY_DIM)
        s_scr[hd, 0] = _dot(keys_ref[2 * hd], q_scr[pl.ds(base, LANES), :])
        s_scr[hd, 1] = _dot(keys_ref[2 * hd + 1], q_scr[pl.ds(base + LANES, LANES), :])
        for j in range(t // LANES):
            lanes = slice(j * LANES, (j + 1) * LANES)

            def put(hs1, hs2, rank1, rank2, lanes=lanes):
                hs_scr[hd, 0, :, lanes] = hs1
                hs_scr[hd, 1, :, lanes] = hs2
                rank1_scr[hd, :, lanes] = rank1
                r2_ref[hd, :, lanes] = rank2.astype(BF16)

            *quick, ties = _top16_pair(s_scr[hd, 0, :, lanes], s_scr[hd, 1, :, lanes])
            put(*quick)

            @pl.when(ties)
            def _(lanes=lanes, put=put):
                hs1, rank1 = _top16(s_scr[hd, 0, :, lanes])
                hs2, rank2 = _top16(s_scr[hd, 1, :, lanes])
                put(hs1, hs2, rank1, rank2)
        return 0

    lax.fori_loop(0, PEER_HEADS, head_body, 0)

    for pair in range(PEER_HEADS // 2):
        heads = (2 * pair, 2 * pair + 1)
        hs1 = jnp.concatenate([hs_scr[h, 0] for h in heads], axis=1)
        hs2 = jnp.concatenate([hs_scr[h, 1] for h in heads], axis=1)
        cnt = _merge_counts(hs1, hs2)
        e1r = jnp.exp(hs1 - hs1[0:1, :])
        e2r = jnp.exp(hs2 - hs2[0:1, :])
        prefix = _tree_sum([jnp.where(cnt > float(kb), e2r[kb:kb + 1, :], 0.0) for kb in range(PEER_TOPK)])
        inv_z = 1.0 / jnp.sum(e1r * prefix, axis=0, keepdims=True)
        for i, h in enumerate(heads):
            lanes = slice(i * t, (i + 1) * t)
            e2_ref[h] = (jnp.exp(s_scr[h, 1] - hs2[0:1, lanes]) * inv_z[:, lanes]).astype(BF16)
            e1_ref[h] = 0.5 * jnp.exp(s_scr[h, 0] - hs1[0:1, lanes])
            n1_ref[h] = _row_gather(cnt[:, lanes], rank1_scr[h])


def _router_call(h2t, wqt, keys):
    t = ROUTER_TILE
    out = pl.BlockSpec((PEER_HEADS, PEER_N_KEYS, t), lambda i: (0, 0, i))
    n_tok = h2t.shape[1]
    shape = (PEER_HEADS, PEER_N_KEYS, n_tok)
    return pl.pallas_call(
        _router_kernel,
        grid=(n_tok // t,),
        in_specs=[pl.BlockSpec((D_MODEL, t), lambda i: (0, i)),
                  pl.BlockSpec((PEER_HEADS * PEER_KEY_DIM, D_MODEL), lambda i: (0, 0)),
                  pl.BlockSpec((2 * PEER_HEADS, PEER_N_KEYS, PEER_KEY_DIM // 2), lambda i: (0, 0, 0))],
        out_specs=[out] * 4,
        out_shape=[jax.ShapeDtypeStruct(shape, BF16), jax.ShapeDtypeStruct(shape, BF16),
                   jax.ShapeDtypeStruct(shape, F32), jax.ShapeDtypeStruct(shape, F32)],
        scratch_shapes=[pltpu.VMEM((PEER_HEADS * PEER_KEY_DIM, t), BF16),
                        pltpu.VMEM((PEER_HEADS, 2, PEER_N_KEYS, t), F32),
                        pltpu.VMEM((PEER_HEADS, 2, PEER_TOPK, t), F32),
                        pltpu.VMEM((PEER_HEADS, PEER_N_KEYS, t), F32)],
        compiler_params=_cparams("parallel"),
        name="peer_retrieval",
    )(h2t, wqt, keys)


def _gated_activations(ht_ref, w_ref, r2_ref, e2_ref, n1_ref, e1_ref):
    for i in range(KEYS_PER_BLOCK):
        rows = slice(i * PEER_N_KEYS, (i + 1) * PEER_N_KEYS)
        for j in range(PEER_TILE // GATE_LANES):
            lanes = slice(j * GATE_LANES, (j + 1) * GATE_LANES)
            gate = jnp.zeros((PEER_N_KEYS, GATE_LANES), BF16)
            for hd in range(PEER_HEADS):
                n_row = n1_ref[hd, i:i + 1, lanes].astype(BF16)
                c_row = e1_ref[hd, i:i + 1, lanes].astype(BF16)
                live = jnp.where(r2_ref[hd, :, lanes] < n_row, e2_ref[hd, :, lanes], jnp.zeros((), BF16))
                gate = gate + live * c_row
            x = ht_ref[rows, lanes]
            act = x * (1.0 + lax.erf(x * (1.0 / math.sqrt(2.0))))
            w_ref[rows, lanes] = act.astype(BF16) * gate


def _peer_kernel(h2t_ref, u_ref, v_ref, r2_ref, e2_ref, n1_ref, e1_ref,
                 x1_ref, mod_ref, g_ref, b_ref, o_ref, acc_ref, ht_ref, w_ref):
    e = pl.program_id(1)

    @pl.when(e == 0)
    def _():
        acc_ref[...] = jnp.zeros_like(acc_ref)

    ht_ref[...] = _dot(u_ref[...], h2t_ref[...])
    _gated_activations(ht_ref, w_ref, r2_ref, e2_ref, n1_ref, e1_ref)
    acc_ref[...] += lax.dot_general(v_ref[...], w_ref[...], (((0,), (0,)), ((), ())), preferred_element_type=F32)

    @pl.when(e == pl.num_programs(1) - 1)
    def _():
        ffn = acc_ref[...].T
        g2 = mod_ref[0, :, 5 * D_MODEL:6 * D_MODEL]
        o_ref[...] = _ln(DEEPNORM_ALPHA * x1_ref[...] + g2 * ffn) * g_ref[...] + b_ref[...]


def _peer_mod_row(i):
    return i // (DEC_SEQ // PEER_TILE)


def _peer_call(h2t, u, v, r2, e2, n1, e1, x1, mod, g, b):
    t = PEER_TILE
    n_tok = x1.shape[0]
    n_blocks = PEER_N_KEYS * PEER_N_KEYS // EXPERT_BLOCK
    gates = pl.BlockSpec((PEER_HEADS, PEER_N_KEYS, t), lambda i, g: (0, 0, i))
    keys = pl.BlockSpec((PEER_HEADS, KEYS_PER_BLOCK, t), lambda i, g: (0, g, i))
    return pl.pallas_call(
        _peer_kernel,
        grid=(n_tok // t, n_blocks),
        in_specs=[pl.BlockSpec((D_MODEL, t), lambda i, g: (0, i)),
                  pl.BlockSpec((EXPERT_BLOCK, D_MODEL), lambda i, g: (g, 0)),
                  pl.BlockSpec((EXPERT_BLOCK, D_MODEL), lambda i, g: (g, 0)),
                  gates, gates, keys, keys,
                  pl.BlockSpec((t, D_MODEL), lambda i, g: (i, 0)),
                  pl.BlockSpec((1, 1, 6 * D_MODEL), lambda i, g: (_peer_mod_row(i), 0, 0)),
                  pl.BlockSpec((1, D_MODEL), lambda i, g: (0, 0)),
                  pl.BlockSpec((1, D_MODEL), lambda i, g: (0, 0))],
        out_specs=pl.BlockSpec((t, D_MODEL), lambda i, g: (i, 0)),
        out_shape=jax.ShapeDtypeStruct((n_tok, D_MODEL), F32),
        scratch_shapes=[pltpu.VMEM((D_MODEL, t), F32), pltpu.VMEM((EXPERT_BLOCK, t), F32),
                        pltpu.VMEM((EXPERT_BLOCK, t), BF16)],
        compiler_params=_cparams("parallel", "arbitrary"),
        name="peer_dense",
    )(h2t, u, v, r2, e2, n1, e1, x1, mod, g, b)


def _cast_kernel(x_ref, o_ref):
    o_ref[...] = x_ref[...].astype(o_ref.dtype)


def _table_bf16(table, l):
    n_exp = table.shape[1]
    return pl.pallas_call(
        _cast_kernel,
        grid=(n_exp // CAST_ROWS,),
        in_specs=[pl.BlockSpec((None, CAST_ROWS, D_MODEL), lambda i: (l, i, 0))],
        out_specs=pl.BlockSpec((CAST_ROWS, D_MODEL), lambda i: (i, 0)),
        out_shape=jax.ShapeDtypeStruct((n_exp, D_MODEL), BF16),
        compiler_params=_cparams("parallel"),
        name="expert_table_bf16",
    )(table)


def _rope_tables():
    t = jnp.arange(DEC_SEQ)
    row = (t // GRID_W).astype(F32)
    col = (t % GRID_W).astype(F32)

    def angles(rot_dim):
        n_freq = rot_dim // 4
        inv_freq = ROPE_THETA ** (-jnp.arange(n_freq, dtype=F32) / n_freq)
        return jnp.concatenate([row[:, None] * inv_freq, col[:, None] * inv_freq], axis=-1)

    def pack(cos_l, sa_l, sb_l):
        return jnp.stack([cos_l, sa_l, sb_l])

    ang_b = angles(MLA_ROPE)
    cb, sb = jnp.cos(ang_b), jnp.sin(ang_b)
    one, zero = jnp.ones((DEC_SEQ, 64), F32), jnp.zeros((DEC_SEQ, 64), F32)
    z16, z32 = jnp.zeros((DEC_SEQ, 16), F32), jnp.zeros((DEC_SEQ, 32), F32)
    rope_b = pack(jnp.concatenate([one, cb, cb, jnp.ones((DEC_SEQ, 32), F32)], axis=1),
                  jnp.concatenate([zero, -sb, z16, z32], axis=1),
                  jnp.concatenate([zero, z16, sb, z32], axis=1))
    ang_d = angles(DIFF_QK_DIM)
    cd, sd = jnp.cos(ang_d), jnp.sin(ang_d)
    rope_d = pack(jnp.concatenate([cd, cd, cd, cd], axis=1),
                  jnp.concatenate([-sd, z32, -sd, z32], axis=1),
                  jnp.concatenate([z32, sd, z32, sd], axis=1))
    ident = pack(jnp.ones((DEC_SEQ, LANES), F32), jnp.zeros((DEC_SEQ, LANES), F32), jnp.zeros((DEC_SEQ, LANES), F32))
    return rope_b, rope_d, ident


def _na_bias_table(rpb):
    col = jnp.arange(GRID_W)
    dc = jnp.clip(col[None, :] - col[:, None], -(NA_WIN_COLS - 1), NA_WIN_COLS - 1) + NA_WIN_COLS - 1
    rpb_cols = rpb[:, :, dc]
    tabs = [rpb_cols[:, off:off + NA_WIN_ROWS].transpose(0, 2, 1, 3).reshape(NA_HEADS, GRID_W, NA_WIN_ROWS * GRID_W)
            for off in range(NA_WIN_ROWS)]
    return jnp.stack(tabs)


def _pad_cols(w, left, right):
    return jnp.pad(w, ((0, 0), (left, right)))


def kernel(x_prompt, x_sample, cache_mla_ckv, cache_mla_krope, cache_na_k, cache_na_v, cache_diff_k, cache_diff_v, c, c_ctx, w_mod, b_mod, w_in, sgu_norm_g, sgu_w, sgu_b, mla_q_norm_g, mla_w_uq, mla_kv_norm_g, mla_w_ukv, na_rpb, diff_lambda_q1, diff_lambda_k1, diff_lambda_q2, diff_lambda_k2, diff_norm_g, w_branch_a, w_branch_b, w_branch_c, w_branch_d, w_gate, b_gate, w_out, ln1_g, ln1_b, peer_w_q, peer_subkeys, peer_u, peer_v, ln2_g, ln2_b):
    x_ctx = x_prompt.reshape(N_CTX, D_MODEL)
    x_lat = x_sample.reshape(N_LAT, D_MODEL)
    cond = jnp.concatenate([c_ctx[None], c, jnp.zeros((N_COND - 1 - DEC_BATCH, D_MODEL), F32)], axis=0)
    mod_all = _mod_call(cond, w_mod, b_mod)
    rope_b, rope_d, rope_id = _rope_tables()
    cache_kr_pad = jnp.pad(cache_mla_krope, ((0, 0), (0, 0), (0, 0), (MLA_NOPE, LANES - MLA_NOPE - MLA_ROPE)))
    cache_na_k2 = cache_na_k.reshape(DEC_BATCH, DEPTH, PAST_LEN, 256)
    cache_na_v2 = cache_na_v.reshape(DEC_BATCH, DEPTH, PAST_LEN, 256)
    cache_diff_k2 = cache_diff_k.reshape(DEC_BATCH, DEPTH, PAST_LEN, 512)
    cache_diff_v2 = cache_diff_v.reshape(DEC_BATCH, DEPTH, PAST_LEN, 512)

    ctx_out = []
    for l in range(DEPTH):
        lambda_init = 0.8 - 0.6 * math.exp(-0.3 * l)
        mod_ctx = jnp.broadcast_to(mod_all[l, 0], (N_CTX // DEC_SEQ, 1, 6 * D_MODEL))
        mod_lat = mod_all[l, 1:1 + DEC_BATCH].reshape(DEC_BATCH, 1, 6 * D_MODEL)

        wi = w_in[l]
        kr_cols = _pad_cols(wi[:, C_KR:C_KR + MLA_ROPE], MLA_NOPE, LANES - MLA_NOPE - MLA_ROPE)
        w_in_r = jnp.concatenate([wi[:, :C_KR], kr_cols, wi[:, C_KR + MLA_ROPE:]], axis=1).astype(BF16)
        wuq = mla_w_uq[l].reshape(MLA_Q_LORA, MLA_HEADS, MLA_NOPE + MLA_ROPE)
        wuq = jnp.pad(wuq, ((0, 0), (0, 0), (0, LANES - MLA_NOPE - MLA_ROPE))).reshape(MLA_Q_LORA, -1).astype(BF16)
        wukv = mla_w_ukv[l].reshape(MLA_KV_LORA, MLA_HEADS, MLA_NOPE + MLA_V)
        wuk = jnp.pad(wukv[:, :, :MLA_NOPE], ((0, 0), (0, 0), (0, LANES - MLA_NOPE))).reshape(MLA_KV_LORA, -1)
        wuk = wuk.astype(BF16)
        wuv = wukv[:, :, MLA_NOPE:].reshape(MLA_KV_LORA, -1).astype(BF16)
        sgu_bias = jnp.repeat(sgu_b[l].T, SGU_WIDTH // SGU_GROUPS, axis=1)
        lams = [p[l].reshape(1, DIFF_QK_DIM) for p in (diff_lambda_q1, diff_lambda_k1, diff_lambda_q2, diff_lambda_k2)]
        dg = diff_norm_g[l].reshape(1, DIFF_V_DIM)

        inproj_weights = (w_in_r, sgu_norm_g[l].reshape(1, -1), sgu_w[l].astype(BF16), sgu_bias,
                          mla_q_norm_g[l].reshape(1, -1), mla_kv_norm_g[l].reshape(1, -1), wuq)
        merge_weights = (w_gate[l].astype(BF16), b_gate[l].reshape(1, -1),
                         w_branch_a[l].astype(BF16), w_branch_b[l].astype(BF16), w_branch_c[l].astype(BF16),
                         w_branch_d[l].astype(BF16), w_out[l].astype(BF16), ln1_g[l].reshape(1, -1),
                         ln1_b[l].reshape(1, -1))
        keys = peer_subkeys[l].reshape(2 * PEER_HEADS, PEER_N_KEYS, PEER_KEY_DIM // 2).astype(BF16)
        wqt = peer_w_q[l].T.astype(BF16)
        u_bf, v_bf = _table_bf16(peer_u, l), _table_bf16(peer_v, l)
        ln2 = (ln2_g[l].reshape(1, -1), ln2_b[l].reshape(1, -1))

        def channel_mix(x, mod, oa, ob, oc, od):
            x1, h2t = _merge_call(x, mod, oa, ob, oc, od, *merge_weights)
            r2, e2, n1, e1 = _router_call(h2t, wqt, keys)
            return _peer_call(h2t, u_bf, v_bf, r2, e2, n1, e1, x1, mod, *ln2)

        oa, mq, ckv, kr, nq, nk, nv, dq, dk, dv = _inproj_call(x_ctx, mod_ctx, rope_id, rope_id, *inproj_weights)
        ob, oc, od = _ctx_attn_call(lambda_init, (mq, ckv, kr, nq, nk, nv, dq, dk, dv), wuk, wuv, lams, dg)
        x_ctx = channel_mix(x_ctx, mod_ctx, oa, ob, oc, od)
        ctx_out.append((ckv.reshape(BATCH, SEQ, MLA_KV_LORA),
                        kr[:, MLA_NOPE:MLA_NOPE + MLA_ROPE].reshape(BATCH, SEQ, MLA_ROPE),
                        nk.reshape(BATCH, SEQ, NA_HEADS, NA_HEAD_DIM),
                        nv.reshape(BATCH, SEQ, NA_HEADS, NA_HEAD_DIM),
                        dk.reshape(BATCH, SEQ, DIFF_HEADS, 2 * DIFF_QK_DIM),
                        dv.reshape(BATCH, SEQ, DIFF_HEADS, DIFF_V_DIM)))

        oa, mq, ckv, kr, nq, nk, nv, dq, dk, dv = _inproj_call(x_lat, mod_lat, rope_b, rope_d, *inproj_weights)
        ob = _lat_mla_call(l, mq, ckv, kr, cache_mla_ckv, cache_kr_pad, wuk, wuv)
        oc = _lat_na_call(l, nq, nk, nv, cache_na_k2, cache_na_v2, _na_bias_table(na_rpb[l]))
        od = _lat_diff_call(l, lambda_init, dq, dk, dv, cache_diff_k2, cache_diff_v2, lams, dg)
        x_lat = channel_mix(x_lat, mod_lat, oa, ob, oc, od)

    y_prompt = x_ctx.reshape(BATCH, SEQ, D_MODEL)
    y_sample = x_lat.reshape(DEC_BATCH, DEC_SEQ, D_MODEL)
    new = [jnp.stack([t[k] for t in ctx_out], axis=1) for k in range(6)]
    return (y_prompt, y_sample, *new)
```

```python
import functools
import math

import jax
import jax.numpy as jnp
from jax import lax
from jax.experimental import pallas as pl
from jax.experimental.pallas import tpu as pltpu

F32 = jnp.float32
BF16 = jnp.bfloat16

D_MODEL = 1024
BATCH = 32
SEQ = 256
DEPTH = 2
DEC_BATCH = 8
DEC_SEQ = 1024
PAST_LEN = 512
GRID_W = 64
CHUNK = 128
SGU_GROUPS = 4
SGU_WIDTH = 256
MLA_HEADS = 4
MLA_Q_LORA = 256
MLA_KV_LORA = 128
MLA_NOPE = 64
MLA_ROPE = 32
MLA_V = 64
NA_HEADS = 4
NA_HEAD_DIM = 64
NA_WIN_ROWS = 8
NA_WIN_COLS = 16
DIFF_HEADS = 4
DIFF_QK_DIM = 64
DIFF_V_DIM = 128
N_BRANCHES = 4
PEER_HEADS = 8
PEER_N_KEYS = 128
PEER_KEY_DIM = 256
PEER_TOPK = 16
ROPE_THETA = 10000.0
LN_EPS = 1e-6
NEG_BIG = -1e30
DEEPNORM_ALPHA = (2 * DEPTH) ** 0.25

LANES = 128
SUBLANES = 8
N_CTX = BATCH * SEQ
N_LAT = DEC_BATCH * DEC_SEQ
N_TOK = N_CTX + N_LAT
N_COND = 16
TM = 512
ROWS = DEC_SEQ // GRID_W
Q_TILE = 256
NA_ROWS_PER_STEP = 2
ROUTER_TILE = 256
PEER_TILE = 1024
EXPERT_BLOCK = 1024
KEYS_PER_BLOCK = EXPERT_BLOCK // PEER_N_KEYS
GATE_LANES = 256
CAST_ROWS = 2048
VMEM_LIMIT = 56 * 1024 * 1024

C_AU, C_AV, C_CQ, C_CKV, C_KR = 0, 256, 512, 768, 896
C_NQ, C_NK, C_NV, C_DQ, C_DK, C_DV, C_END = 1024, 1280, 1536, 1792, 2304, 2816, 3328


def _ln(x):
    mu = jnp.mean(x, axis=-1, keepdims=True)
    xc = x - mu
    var = jnp.mean(xc * xc, axis=-1, keepdims=True)
    return xc * lax.rsqrt(var + LN_EPS)


def _rms(x):
    return x * lax.rsqrt(jnp.mean(x * x, axis=-1, keepdims=True) + LN_EPS)


def _gelu(x):
    return 0.5 * x * (1.0 + lax.erf(x * (1.0 / math.sqrt(2.0))))


def _dot(a, b):
    return jnp.dot(a, b, preferred_element_type=F32)


def _dot_nt(a, b):
    return lax.dot_general(a, b, (((1,), (1,)), ((), ())), preferred_element_type=F32)


def _rope(x, tab_ref, half):
    return (x * tab_ref[0] + pltpu.roll(x, LANES - half, 1) * tab_ref[1] + pltpu.roll(x, half, 1) * tab_ref[2])


def _cparams(*sem):
    return pltpu.CompilerParams(dimension_semantics=sem, vmem_limit_bytes=VMEM_LIMIT)


def _mod_kernel(cond_ref, w_ref, b_ref, o_ref):
    c = cond_ref[...]
    s = c * jax.nn.sigmoid(c)
    o_ref[...] = _dot(s, w_ref[...]) + b_ref[...]


def _mod_call(cond, w_mod, b_mod):
    nb = 1536
    return pl.pallas_call(
        _mod_kernel,
        grid=(DEPTH, 6 * D_MODEL // nb),
        in_specs=[pl.BlockSpec((N_COND, D_MODEL), lambda l, j: (0, 0)),
                  pl.BlockSpec((None, D_MODEL, nb), lambda l, j: (l, 0, j)),
                  pl.BlockSpec((None, 1, nb), lambda l, j: (l, 0, j))],
        out_specs=pl.BlockSpec((None, N_COND, nb), lambda l, j: (l, 0, j)),
        out_shape=jax.ShapeDtypeStruct((DEPTH, N_COND, 6 * D_MODEL), F32),
        compiler_params=_cparams("arbitrary", "arbitrary"),
        name="mod_vectors",
    )(cond, w_mod, b_mod.reshape(DEPTH, 1, 6 * D_MODEL))


def _mod_row(i):
    return i // (DEC_SEQ // TM)


def _pos_block(i):
    return i % (DEC_SEQ // TM)


def _inproj_kernel(x_ref, mod_ref, rb_ref, rd_ref, w_in_ref, sgu_g_ref, sgu_w_ref, sgu_bias_ref,
                   qg_ref, kvg_ref, wuq_ref,
                   oa_ref, mq_ref, ckv_ref, kr_ref, nq_ref, nk_ref, nv_ref, dq_ref, dk_ref, dv_ref):
    x = x_ref[...]
    shift = mod_ref[0, :, 0:D_MODEL]
    scale = mod_ref[0, :, D_MODEL:2 * D_MODEL]
    h = (_ln(x) * (1.0 + scale) + shift).astype(BF16)

    ya = _dot(h, w_in_ref[:, C_AU:C_CQ])
    u = _gelu(ya[:, :SGU_WIDTH])
    v = _gelu(ya[:, SGU_WIDTH:])
    vn = (_ln(v) * sgu_g_ref[...]).astype(BF16)
    group = lax.broadcasted_iota(jnp.int32, (CHUNK, SGU_WIDTH), 1) // (SGU_WIDTH // SGU_GROUPS)
    for c in range(TM // CHUNK):
        rows = slice(c * CHUNK, (c + 1) * CHUNK)
        mixed = sgu_bias_ref[...]
        for g in range(SGU_GROUPS):
            mixed = mixed + jnp.where(group == g, _dot(sgu_w_ref[g], vn[rows]), 0.0)
        oa_ref[rows, :] = (u[rows] * mixed).astype(oa_ref.dtype)

    ym = _dot(h, w_in_ref[:, C_CQ:C_NQ])
    cq = (_rms(ym[:, :MLA_Q_LORA]) * qg_ref[...]).astype(BF16)
    mq = _dot(cq, wuq_ref[...])
    for g in range(MLA_HEADS):
        lanes = slice(g * LANES, (g + 1) * LANES)
        mq_ref[:, lanes] = _rope(mq[:, lanes], rb_ref, MLA_ROPE // 2)
    ckv_ref[...] = _rms(ym[:, MLA_Q_LORA:MLA_Q_LORA + MLA_KV_LORA]) * kvg_ref[...]
    kr_ref[...] = _rope(ym[:, MLA_Q_LORA + MLA_KV_LORA:], rb_ref, MLA_ROPE // 2)

    yn = _dot(h, w_in_ref[:, C_NQ:C_DQ])
    nq_ref[...] = yn[:, 0:256]
    nk_ref[...] = yn[:, 256:512]
    nv_ref[...] = yn[:, 512:768]

    yd = _dot(h, w_in_ref[:, C_DQ:C_END])
    for g in range(4):
        lanes = slice(g * LANES, (g + 1) * LANES)
        dq_ref[:, lanes] = _rope(yd[:, g * LANES:(g + 1) * LANES], rd_ref, DIFF_QK_DIM // 2)
        dk_ref[:, lanes] = _rope(yd[:, 512 + g * LANES:512 + (g + 1) * LANES], rd_ref, DIFF_QK_DIM // 2)
    dv_ref[...] = yd[:, 1024:1536]


def _inproj_call(x, mod, rope_b, rope_d, w_in_r, sgu_g, sgu_w, sgu_bias, qg, kvg, wuq):
    tile = lambda w: pl.BlockSpec((TM, w), lambda i: (i, 0))
    full = lambda *s: pl.BlockSpec(s, lambda i: (0,) * len(s))
    widths = (SGU_WIDTH, 512, MLA_KV_LORA, LANES, 256, 256, 256, 512, 512, 512)
    dtypes = (BF16,) + (F32,) * 9
    n_tok = x.shape[0]
    return pl.pallas_call(
        _inproj_kernel,
        grid=(n_tok // TM,),
        in_specs=[tile(D_MODEL),
                  pl.BlockSpec((1, 1, 6 * D_MODEL), lambda i: (_mod_row(i), 0, 0)),
                  pl.BlockSpec((3, TM, LANES), lambda i: (0, _pos_block(i), 0)),
                  pl.BlockSpec((3, TM, LANES), lambda i: (0, _pos_block(i), 0)),
                  full(D_MODEL, C_END), full(1, SGU_WIDTH), full(SGU_GROUPS, CHUNK, CHUNK),
                  full(CHUNK, SGU_WIDTH), full(1, MLA_Q_LORA), full(1, MLA_KV_LORA),
                  full(MLA_Q_LORA, MLA_HEADS * LANES)],
        out_specs=[tile(w) for w in widths],
        out_shape=[jax.ShapeDtypeStruct((n_tok, w), dt) for w, dt in zip(widths, dtypes)],
        compiler_params=_cparams("parallel"),
        name="in_projection",
    )(x, mod, rope_b, rope_d, w_in_r, sgu_g, sgu_w, sgu_bias, qg, kvg, wuq)


def _half_mask(lo):
    lane = lax.broadcasted_iota(jnp.int32, (1, LANES), 1)
    return (lane >= lo) & (lane < lo + 64)


def _softmax_pv(scores, values, lanes):
    m = scores[0].max(axis=-1, keepdims=True)
    for s in scores[1:]:
        m = jnp.maximum(m, s.max(axis=-1, keepdims=True))
    den = None
    o = None
    for s, v in zip(scores, values):
        p = jnp.exp(s - m)
        d = p.sum(axis=-1, keepdims=True)
        den = d if den is None else den + d
        pv = _dot(p.astype(BF16), v[:, lanes])
        o = pv if o is None else o + pv
    return o / den


def _pair_attention(q, keys, vals, scale, bias_fn=None):
    assert math.frexp(scale)[0] == 0.5
    n = q.shape[0]
    outs = []
    for pair in range(2):
        lanes = slice(pair * LANES, (pair + 1) * LANES)
        qp = q[:, lanes] * jnp.asarray(scale, BF16)
        masks = [_half_mask(64 * sub) for sub in range(2)]
        qs = jnp.concatenate([jnp.where(m, qp, jnp.zeros_like(qp)) for m in masks], axis=0)
        scores = [_dot_nt(qs, k[:, lanes]) for k in keys]
        if bias_fn is not None:
            scores = bias_fn(pair, scores)
        o = _softmax_pv(scores, vals, lanes)
        outs.append(jnp.where(masks[0], o[:n], 0.0) + jnp.where(masks[1], o[n:], 0.0))
    return outs


def _mla_attention(q, k_blocks, v_blocks, o_ref, rows):
    scale = (MLA_NOPE + MLA_ROPE) ** -0.5
    for pair in range(2):
        lanes = slice(pair * LANES, (pair + 1) * LANES)
        acc = None
        for sub in range(2):
            head = 2 * pair + sub
            hl = slice(head * LANES, (head + 1) * LANES)
            scores = [_dot_nt(q[:, hl], k[:, hl]) * scale for k in k_blocks]
            o = jnp.where(_half_mask(64 * sub), _softmax_pv(scores, v_blocks, lanes), 0.0)
            acc = o if acc is None else acc + o
        o_ref[rows, lanes] = acc.astype(o_ref.dtype)


def _diff_lambda(lq1, lk1, lq2, lk2, lambda_init):
    a = jnp.sum(lq1[...] * lk1[...], axis=-1, keepdims=True)
    b = jnp.sum(lq2[...] * lk2[...], axis=-1, keepdims=True)
    return jnp.exp(a) - jnp.exp(b) + lambda_init


def _diff_attention(q, k_blocks, v_blocks, lam, norm_g, lambda_init, o_ref, rows):
    scale = DIFF_QK_DIM ** -0.5
    assert math.frexp(scale)[0] == 0.5
    for head in range(DIFF_HEADS):
        hl = slice(head * LANES, (head + 1) * LANES)
        qh = q[:, hl] * jnp.asarray(scale, BF16)
        probs = []
        for sub in range(2):
            qm = jnp.where(_half_mask(64 * sub), qh, jnp.zeros_like(qh))
            scores = [_dot_nt(qm, k[:, hl]) for k in k_blocks]
            m = scores[0].max(axis=-1, keepdims=True)
            for s in scores[1:]:
                m = jnp.maximum(m, s.max(axis=-1, keepdims=True))
            ps = [jnp.exp(s - m) for s in scores]
            den = ps[0].sum(axis=-1, keepdims=True)
            for p in ps[1:]:
                den = den + p.sum(axis=-1, keepdims=True)
            probs.append((ps, 1.0 / den))
        o = None
        for i, v in enumerate(v_blocks):
            w = probs[0][0][i] * probs[0][1] - probs[1][0][i] * (lam * probs[1][1])
            pv = _dot(w.astype(BF16), v[:, hl])
            o = pv if o is None else o + pv
        o = _rms(o) * norm_g * (1.0 - lambda_init)
        o_ref[rows, hl] = o.astype(o_ref.dtype)


def _ctx_attn_kernel(lambda_init, mq_ref, ckv_ref, kr_ref, nq_ref, nk_ref, nv_ref, dq_ref, dk_ref, dv_ref,
                     wuk_ref, wuv_ref, lq1, lk1, lq2, lk2, dg_ref, ob_ref, oc_ref, od_ref):
    rows = slice(0, SEQ)
    ckv = ckv_ref[...].astype(BF16)
    kr = kr_ref[...]
    k_b = (_dot(ckv, wuk_ref[...]) + jnp.concatenate([kr] * MLA_HEADS, axis=1)).astype(BF16)
    v_b = _dot(ckv, wuv_ref[...]).astype(BF16)
    _mla_attention(mq_ref[...].astype(BF16), [k_b], [v_b], ob_ref, rows)

    outs = _pair_attention(nq_ref[...].astype(BF16), [nk_ref[...].astype(BF16)], [nv_ref[...].astype(BF16)],
                           NA_HEAD_DIM ** -0.5)
    for pair in range(2):
        oc_ref[:, pair * LANES:(pair + 1) * LANES] = outs[pair].astype(oc_ref.dtype)

    lam = _diff_lambda(lq1, lk1, lq2, lk2, lambda_init)
    _diff_attention(dq_ref[...].astype(BF16), [dk_ref[...].astype(BF16)], [dv_ref[...].astype(BF16)],
                    lam, dg_ref[...], lambda_init, od_ref, rows)


def _ctx_attn_call(lambda_init, acts, wuk, wuv, lams, dg):
    mq, ckv, kr, nq, nk, nv, dq, dk, dv = acts
    seq = lambda w: pl.BlockSpec((SEQ, w), lambda b: (b, 0))
    full = lambda *s: pl.BlockSpec(s, lambda b: (0,) * len(s))
    return pl.pallas_call(
        functools.partial(_ctx_attn_kernel, lambda_init),
        grid=(BATCH,),
        in_specs=[seq(512), seq(128), seq(128), seq(256), seq(256), seq(256), seq(512), seq(512), seq(512),
                  full(MLA_KV_LORA, 512), full(MLA_KV_LORA, 256)] + [full(1, DIFF_QK_DIM)] * 4
                 + [full(1, DIFF_V_DIM)],
        out_specs=[seq(256), seq(256), seq(512)],
        out_shape=[jax.ShapeDtypeStruct((N_CTX, w), BF16) for w in (256, 256, 512)],
        compiler_params=_cparams("parallel"),
        name="context_attention",
    )(mq, ckv, kr, nq, nk, nv, dq, dk, dv, wuk, wuv, *lams, dg)


def _lat_mla_kernel(mq_ref, ckv_ref, kr_ref, cckv_ref, ckr_ref, wuk_ref, wuv_ref, o_ref):
    def expand(ckv_f32, kr):
        ckv = ckv_f32.astype(BF16)
        k = (_dot(ckv, wuk_ref[...]) + jnp.concatenate([kr] * MLA_HEADS, axis=1)).astype(BF16)
        return k, _dot(ckv, wuv_ref[...]).astype(BF16)

    k_lat, v_lat = expand(ckv_ref[...], kr_ref[...])
    k_ctx, v_ctx = expand(cckv_ref[...], ckr_ref[...])
    k_all = jnp.concatenate([k_lat, k_ctx], axis=0)
    v_all = jnp.concatenate([v_lat, v_ctx], axis=0)
    for t in range(DEC_SEQ // Q_TILE):
        rows = slice(t * Q_TILE, (t + 1) * Q_TILE)
        _mla_attention(mq_ref[rows, :].astype(BF16), [k_all], [v_all], o_ref, rows)


def _lat_mla_call(l, mq, ckv, kr, cache_ckv, cache_kr_pad, wuk, wuv):
    seq = lambda w: pl.BlockSpec((DEC_SEQ, w), lambda b: (b, 0))
    cache = lambda w: pl.BlockSpec((None, None, PAST_LEN, w), lambda b: (b, l, 0, 0))
    full = lambda *s: pl.BlockSpec(s, lambda b: (0,) * len(s))
    return pl.pallas_call(
        _lat_mla_kernel,
        grid=(DEC_BATCH,),
        in_specs=[seq(512), seq(128), seq(128), cache(MLA_KV_LORA), cache(LANES),
                  full(MLA_KV_LORA, 512), full(MLA_KV_LORA, 256)],
        out_specs=pl.BlockSpec((DEC_SEQ, 256), lambda b: (b, 0)),
        out_shape=jax.ShapeDtypeStruct((N_LAT, 256), BF16),
        compiler_params=_cparams("parallel"),
        name="latent_mla_attention",
    )(mq, ckv, kr, cache_ckv, cache_kr_pad, wuk, wuv)


def _win_start(r):
    return jnp.clip(r - NA_WIN_ROWS // 2, 0, ROWS - NA_WIN_ROWS)


def _lat_na_kernel(nq_ref, nk_ref, nv_ref, ck_ref, cv_ref, bias_ref, o_ref):
    win = NA_WIN_ROWS * GRID_W
    k_c = ck_ref[...].astype(BF16)
    v_c = cv_ref[...].astype(BF16)
    q_col = lax.broadcasted_iota(jnp.int32, (2 * GRID_W, win), 0) % GRID_W
    k_col = lax.broadcasted_iota(jnp.int32, (2 * GRID_W, win), 1) % GRID_W
    c0 = jnp.clip(q_col - NA_WIN_COLS // 2, 0, GRID_W - NA_WIN_COLS)
    col_in = (k_col >= c0) & (k_col < c0 + NA_WIN_COLS)

    for rr in range(NA_ROWS_PER_STEP):
        r = pl.program_id(1) * NA_ROWS_PER_STEP + rr
        first = _win_start(r)
        start = pl.multiple_of(first * GRID_W, GRID_W)
        off = first - r + NA_WIN_ROWS - 1
        k_all = jnp.concatenate([nk_ref[pl.ds(start, win), :].astype(BF16), k_c], axis=0)
        v_all = jnp.concatenate([nv_ref[pl.ds(start, win), :].astype(BF16), v_c], axis=0)

        def bias_fn(pair, scores, off=off):
            s = scores[0]
            bias = jnp.concatenate([bias_ref[off, 2 * pair], bias_ref[off, 2 * pair + 1]], axis=0)
            s_win = jnp.where(col_in, s[:, :win] + bias, NEG_BIG)
            return [jnp.concatenate([s_win, s[:, win:]], axis=1)]

        rows = slice(rr * GRID_W, (rr + 1) * GRID_W)
        outs = _pair_attention(nq_ref[rows, :].astype(BF16), [k_all], [v_all], NA_HEAD_DIM ** -0.5, bias_fn)
        for pair in range(2):
            o_ref[rows, pair * LANES:(pair + 1) * LANES] = outs[pair].astype(o_ref.dtype)


def _lat_na_call(l, nq, nk, nv, cache_k, cache_v, bias_tab):
    seq = pl.BlockSpec((DEC_SEQ, 256), lambda b, r: (b, 0))
    cache = pl.BlockSpec((None, None, PAST_LEN, 256), lambda b, r: (b, l, 0, 0))
    steps = ROWS // NA_ROWS_PER_STEP
    q_rows = NA_ROWS_PER_STEP * GRID_W
    return pl.pallas_call(
        _lat_na_kernel,
        grid=(DEC_BATCH, steps),
        in_specs=[pl.BlockSpec((q_rows, 256), lambda b, r: (b * steps + r, 0)),
                  seq, seq, cache, cache,
                  pl.BlockSpec((NA_WIN_ROWS, NA_HEADS, GRID_W, NA_WIN_ROWS * GRID_W), lambda b, r: (0, 0, 0, 0))],
        out_specs=pl.BlockSpec((q_rows, 256), lambda b, r: (b * steps + r, 0)),
        out_shape=jax.ShapeDtypeStruct((N_LAT, 256), BF16),
        compiler_params=_cparams("parallel", "arbitrary"),
        name="latent_neighbourhood_attention",
    )(nq, nk, nv, cache_k, cache_v, bias_tab)


def _lat_diff_kernel(lambda_init, dq_ref, dk_ref, dv_ref, ck_ref, cv_ref, lq1, lk1, lq2, lk2, dg_ref, o_ref):
    lam = _diff_lambda(lq1, lk1, lq2, lk2, lambda_init)
    k_blocks = [dk_ref[...].astype(BF16), ck_ref[...].astype(BF16)]
    v_blocks = [dv_ref[...].astype(BF16), cv_ref[...].astype(BF16)]
    for t in range(DEC_SEQ // Q_TILE):
        rows = slice(t * Q_TILE, (t + 1) * Q_TILE)
        _diff_attention(dq_ref[rows, :].astype(BF16), k_blocks, v_blocks, lam, dg_ref[...], lambda_init,
                        o_ref, rows)


def _lat_diff_call(l, lambda_init, dq, dk, dv, cache_k, cache_v, lams, dg):
    seq = pl.BlockSpec((DEC_SEQ, 512), lambda b: (b, 0))
    cache = pl.BlockSpec((None, None, PAST_LEN, 512), lambda b: (b, l, 0, 0))
    full = lambda *s: pl.BlockSpec(s, lambda b: (0,) * len(s))
    return pl.pallas_call(
        functools.partial(_lat_diff_kernel, lambda_init),
        grid=(DEC_BATCH,),
        in_specs=[seq, seq, seq, cache, cache] + [full(1, DIFF_QK_DIM)] * 4 + [full(1, DIFF_V_DIM)],
        out_specs=pl.BlockSpec((DEC_SEQ, 512), lambda b: (b, 0)),
        out_shape=jax.ShapeDtypeStruct((N_LAT, 512), BF16),
        compiler_params=_cparams("parallel"),
        name="latent_differential_attention",
    )(dq, dk, dv, cache_k, cache_v, *lams, dg)


def _merge_kernel(x_ref, mod_ref, oa_ref, ob_ref, oc_ref, od_ref, wg_ref, bg_ref,
                  wa_ref, wb_ref, wc_ref, wd_ref, wo_ref, g_ref, b_ref, x1_ref, h2t_ref):
    x = x_ref[...]
    mod = lambda k: mod_ref[0, :, k * D_MODEL:(k + 1) * D_MODEL]
    h = (_ln(x) * (1.0 + mod(1)) + mod(0)).astype(BF16)
    merged = None
    for i, (o_ref, w_ref) in enumerate(((oa_ref, wa_ref), (ob_ref, wb_ref), (oc_ref, wc_ref), (od_ref, wd_ref))):
        cols = slice(i * D_MODEL, (i + 1) * D_MODEL)
        gate = jax.nn.sigmoid(_dot(h, wg_ref[:, cols]) + bg_ref[:, cols])
        term = gate * _dot(o_ref[...], w_ref[...])
        merged = term if merged is None else merged + term
    mix = _dot(merged.astype(BF16), wo_ref[...])
    x1 = _ln(DEEPNORM_ALPHA * x + mod(2) * mix) * g_ref[...] + b_ref[...]
    x1_ref[...] = x1
    h2 = _ln(x1) * (1.0 + mod(4)) + mod(3)
    h2t_ref[...] = h2.T.astype(BF16)


def _merge_call(x, mod, oa, ob, oc, od, wg, bg, wa, wb, wc, wd, wo, g, b):
    tile = lambda w: pl.BlockSpec((TM, w), lambda i: (i, 0))
    full = lambda *s: pl.BlockSpec(s, lambda i: (0,) * len(s))
    n_tok = x.shape[0]
    return pl.pallas_call(
        _merge_kernel,
        grid=(n_tok // TM,),
        in_specs=[tile(D_MODEL), pl.BlockSpec((1, 1, 6 * D_MODEL), lambda i: (_mod_row(i), 0, 0)),
                  tile(256), tile(256), tile(256), tile(512),
                  full(D_MODEL, 4 * D_MODEL), full(1, 4 * D_MODEL),
                  full(256, D_MODEL), full(256, D_MODEL), full(256, D_MODEL), full(512, D_MODEL),
                  full(D_MODEL, D_MODEL), full(1, D_MODEL), full(1, D_MODEL)],
        out_specs=[tile(D_MODEL), pl.BlockSpec((D_MODEL, TM), lambda i: (0, i))],
        out_shape=[jax.ShapeDtypeStruct((n_tok, D_MODEL), F32), jax.ShapeDtypeStruct((D_MODEL, n_tok), BF16)],
        compiler_params=_cparams("parallel"),
        name="branch_merge",
    )(x, mod, oa, ob, oc, od, wg, bg, wa, wb, wc, wd, wo, g, b)


KEY_MIN = -2 ** 31


def _tree_sum(terms):
    while len(terms) > 1:
        terms = [a + b for a, b in zip(terms[0::2], terms[1::2])] + ([terms[-1]] if len(terms) % 2 else [])
    return terms[0]


def _row_gather(table, idx):
    ii = idx.astype(jnp.int32)
    low = ii & (SUBLANES - 1)
    outs = []
    for c in range(idx.shape[0] // SUBLANES):
        rows = slice(c * SUBLANES, (c + 1) * SUBLANES)
        lo = jnp.take_along_axis(table[0:SUBLANES], low[rows], axis=0)
        hi = jnp.take_along_axis(table[SUBLANES:PEER_TOPK], low[rows], axis=0)
        outs.append(jnp.where(ii[rows] < SUBLANES, lo, jnp.where(ii[rows] < PEER_TOPK, hi, 0.0)))
    return jnp.concatenate(outs, axis=0)


def _sort_key(x):
    b = lax.bitcast_convert_type(x + 0.0, jnp.int32)
    return b ^ ((b >> 31) & 0x7FFFFFFF)


def _key_value(k):
    return lax.bitcast_convert_type(k ^ ((k >> 31) & 0x7FFFFFFF), F32)


def _top16(s):
    row = lax.broadcasted_iota(jnp.int32, s.shape, 0).astype(F32)
    krow = lax.broadcasted_iota(jnp.int32, (PEER_TOPK, s.shape[1]), 0)

    def body(k, carry):
        work, rank, vals = carry
        m = jnp.max(work, axis=0, keepdims=True)
        idx = jnp.min(jnp.where(work == m, row, float(PEER_N_KEYS)), axis=0, keepdims=True)
        sel = row == idx
        rank = jnp.where(sel, jnp.asarray(k, jnp.int32).astype(F32), rank)
        work = jnp.where(sel, -jnp.inf, work)
        vals = jnp.where(krow == k, m, vals)
        return work, rank, vals

    init = (s, jnp.full(s.shape, float(PEER_N_KEYS), F32), jnp.zeros((PEER_TOPK, s.shape[1]), F32))
    _, rank, vals = lax.fori_loop(0, PEER_TOPK, body, init)
    return vals, rank


def _top16_pair(s1, s2):
    krow = lax.broadcasted_iota(jnp.int32, (PEER_TOPK, LANES), 0)

    def body(k, carry):
        w1, w2, v1, v2 = carry
        code = KEY_MIN + jnp.asarray(k, jnp.int32)
        m1 = jnp.max(w1, axis=0, keepdims=True)
        m2 = jnp.max(w2, axis=0, keepdims=True)
        w1 = jnp.where(w1 == m1, code, w1)
        w2 = jnp.where(w2 == m2, code, w2)
        return w1, w2, jnp.where(krow == k, m1, v1), jnp.where(krow == k, m2, v2)

    zeros = jnp.zeros((PEER_TOPK, LANES), jnp.int32)
    w1, w2, v1, v2 = lax.fori_loop(0, PEER_TOPK, body, (_sort_key(s1), _sort_key(s2), zeros, zeros))

    def decode(w):
        taken = w < KEY_MIN + PEER_TOPK
        rank = jnp.where(taken, (w - KEY_MIN).astype(F32), float(PEER_N_KEYS))
        return rank, jnp.sum(taken.astype(F32), axis=0, keepdims=True)

    r1, c1 = decode(w1)
    r2, c2 = decode(w2)
    ties = jnp.max(jnp.maximum(jnp.abs(c1 - PEER_TOPK), jnp.abs(c2 - PEER_TOPK))) > 0.5
    return _key_value(v1), _key_value(v2), r1, r2, ties


def _merge_counts(hs1, hs2):
    krow = lax.broadcasted_iota(jnp.int32, hs1.shape, 0).astype(F32)

    def body(_, carry):
        cnt, front = carry
        m = jnp.max(front, axis=0, keepdims=True)
        win = jnp.min(jnp.where(front == m, krow, float(PEER_TOPK)), axis=0, keepdims=True)
        sel = krow == win
        cnt = jnp.where(sel, cnt + 1.0, cnt)
        nxt = jnp.where(cnt < float(PEER_TOPK), hs1 + _row_gather(hs2, cnt), -jnp.inf)
        return cnt, jnp.where(sel, nxt, front)

    cnt, _ = lax.fori_loop(0, PEER_TOPK, body, (jnp.zeros(hs1.shape, F32), hs1 + hs2[0:1, :]))
    return cnt


def _router_kernel(h2t_ref, wqt_ref, keys_ref, r2_ref, e2_ref, n1_ref, e1_ref, q_scr, s_scr, hs_scr, rank1_scr):
    t = ROUTER_TILE
    q_scr[...] = _dot(wqt_ref[...], h2t_ref[...]).astype(BF16)

    for hd in range(PEER_HEADS):
        for half in range(2):
            rows = slice((2 * hd + half) * LANES, (2 * hd + half + 1) * LANES)
            s_scr[hd, half] = _dot(keys_ref[2 * hd + half], q_scr[rows, :])

    def head_body(hd, _):
        for j in range(t // LANES):
            lanes = slice(j * LANES, (j + 1) * LANES)

            def put(hs1, hs2, rank1, rank2, lanes=lanes):
                hs_scr[hd, 0, :, lanes] = hs1
                hs_scr[hd, 1, :, lanes] = hs2
                rank1_scr[hd, :, lanes] = rank1
                r2_ref[hd, :, lanes] = rank2.astype(BF16)

            *quick, ties = _top16_pair(s_scr[hd, 0, :, lanes], s_scr[hd, 1, :, lanes])
            put(*quick)

            @pl.when(ties)
            def _(lanes=lanes, put=put):
                hs1, rank1 = _top16(s_scr[hd, 0, :, lanes])
                hs2, rank2 = _top16(s_scr[hd, 1, :, lanes])
                put(hs1, hs2, rank1, rank2)
        return 0

    lax.fori_loop(0, PEER_HEADS, head_body, 0)

    for pair in range(PEER_HEADS // 2):
        heads = (2 * pair, 2 * pair + 1)
        hs1 = jnp.concatenate([hs_scr[h, 0] for h in heads], axis=1)
        hs2 = jnp.concatenate([hs_scr[h, 1] for h in heads], axis=1)
        cnt = _merge_counts(hs1, hs2)
        e1r = jnp.exp(hs1 - hs1[0:1, :])
        e2r = jnp.exp(hs2 - hs2[0:1, :])
        prefix = _tree_sum([jnp.where(cnt > float(kb), e2r[kb:kb + 1, :], 0.0) for kb in range(PEER_TOPK)])
        inv_z = 1.0 / jnp.sum(e1r * prefix, axis=0, keepdims=True)
        for i, h in enumerate(heads):
            lanes = slice(i * t, (i + 1) * t)
            e2_ref[h] = (jnp.exp(s_scr[h, 1] - hs2[0:1, lanes]) * inv_z[:, lanes]).astype(BF16)
            e1_ref[h] = 0.5 * jnp.exp(s_scr[h, 0] - hs1[0:1, lanes])
            n1_ref[h] = _row_gather(cnt[:, lanes], rank1_scr[h])


def _router_call(h2t, wqt, keys):
    t = ROUTER_TILE
    out = pl.BlockSpec((PEER_HEADS, PEER_N_KEYS, t), lambda i: (0, 0, i))
    n_tok = h2t.shape[1]
    shape = (PEER_HEADS, PEER_N_KEYS, n_tok)
    return pl.pallas_call(
        _router_kernel,
        grid=(n_tok // t,),
        in_specs=[pl.BlockSpec((D_MODEL, t), lambda i: (0, i)),
                  pl.BlockSpec((PEER_HEADS * PEER_KEY_DIM, D_MODEL), lambda i: (0, 0)),
                  pl.BlockSpec((2 * PEER_HEADS, PEER_N_KEYS, PEER_KEY_DIM // 2), lambda i: (0, 0, 0))],
        out_specs=[out] * 4,
        out_shape=[jax.ShapeDtypeStruct(shape, BF16), jax.ShapeDtypeStruct(shape, BF16),
                   jax.ShapeDtypeStruct(shape, F32), jax.ShapeDtypeStruct(shape, F32)],
        scratch_shapes=[pltpu.VMEM((PEER_HEADS * PEER_KEY_DIM, t), BF16),
                        pltpu.VMEM((PEER_HEADS, 2, PEER_N_KEYS, t), F32),
                        pltpu.VMEM((PEER_HEADS, 2, PEER_TOPK, t), F32),
                        pltpu.VMEM((PEER_HEADS, PEER_N_KEYS, t), F32)],
        compiler_params=_cparams("parallel"),
        name="peer_retrieval",
    )(h2t, wqt, keys)


def _gated_activations(ht_ref, w_ref, r2_ref, e2_ref, n1_ref, e1_ref):
    for i in range(KEYS_PER_BLOCK):
        rows = slice(i * PEER_N_KEYS, (i + 1) * PEER_N_KEYS)
        for j in range(PEER_TILE // GATE_LANES):
            lanes = slice(j * GATE_LANES, (j + 1) * GATE_LANES)
            gate = jnp.zeros((PEER_N_KEYS, GATE_LANES), BF16)
            for hd in range(PEER_HEADS):
                n_row = n1_ref[hd, i:i + 1, lanes].astype(BF16)
                c_row = e1_ref[hd, i:i + 1, lanes].astype(BF16)
                live = jnp.where(r2_ref[hd, :, lanes] < n_row, e2_ref[hd, :, lanes], jnp.zeros((), BF16))
                gate = gate + live * c_row
            x = ht_ref[rows, lanes]
            act = x * (1.0 + lax.erf(x * (1.0 / math.sqrt(2.0))))
            w_ref[rows, lanes] = act.astype(BF16) * gate


def _peer_kernel(h2t_ref, u_ref, v_ref, r2_ref, e2_ref, n1_ref, e1_ref,
                 x1_ref, mod_ref, g_ref, b_ref, o_ref, acc_ref, ht_ref, w_ref):
    e = pl.program_id(1)

    @pl.when(e == 0)
    def _():
        acc_ref[...] = jnp.zeros_like(acc_ref)

    ht_ref[...] = _dot(u_ref[...], h2t_ref[...])
    _gated_activations(ht_ref, w_ref, r2_ref, e2_ref, n1_ref, e1_ref)
    acc_ref[...] += lax.dot_general(v_ref[...], w_ref[...], (((0,), (0,)), ((), ())), preferred_element_type=F32)

    @pl.when(e == pl.num_programs(1) - 1)
    def _():
        ffn = acc_ref[...].T
        g2 = mod_ref[0, :, 5 * D_MODEL:6 * D_MODEL]
        o_ref[...] = _ln(DEEPNORM_ALPHA * x1_ref[...] + g2 * ffn) * g_ref[...] + b_ref[...]


def _peer_mod_row(i):
    return i // (DEC_SEQ // PEER_TILE)


def _peer_call(h2t, u, v, r2, e2, n1, e1, x1, mod, g, b):
    t = PEER_TILE
    n_tok = x1.shape[0]
    n_blocks = PEER_N_KEYS * PEER_N_KEYS // EXPERT_BLOCK
    gates = pl.BlockSpec((PEER_HEADS, PEER_N_KEYS, t), lambda i, g: (0, 0, i))
    keys = pl.BlockSpec((PEER_HEADS, KEYS_PER_BLOCK, t), lambda i, g: (0, g, i))
    return pl.pallas_call(
        _peer_kernel,
        grid=(n_tok // t, n_blocks),
        in_specs=[pl.BlockSpec((D_MODEL, t), lambda i, g: (0, i)),
                  pl.BlockSpec((EXPERT_BLOCK, D_MODEL), lambda i, g: (g, 0)),
                  pl.BlockSpec((EXPERT_BLOCK, D_MODEL), lambda i, g: (g, 0)),
                  gates, gates, keys, keys,
                  pl.BlockSpec((t, D_MODEL), lambda i, g: (i, 0)),
                  pl.BlockSpec((1, 1, 6 * D_MODEL), lambda i, g: (_peer_mod_row(i), 0, 0)),
                  pl.BlockSpec((1, D_MODEL), lambda i, g: (0, 0)),
                  pl.BlockSpec((1, D_MODEL), lambda i, g: (0, 0))],
        out_specs=pl.BlockSpec((t, D_MODEL), lambda i, g: (i, 0)),
        out_shape=jax.ShapeDtypeStruct((n_tok, D_MODEL), F32),
        scratch_shapes=[pltpu.VMEM((D_MODEL, t), F32), pltpu.VMEM((EXPERT_BLOCK, t), F32),
                        pltpu.VMEM((EXPERT_BLOCK, t), BF16)],
        compiler_params=_cparams("parallel", "arbitrary"),
        name="peer_dense",
    )(h2t, u, v, r2, e2, n1, e1, x1, mod, g, b)


def _cast_kernel(x_ref, o_ref):
    o_ref[...] = x_ref[...].astype(o_ref.dtype)


def _table_bf16(table, l):
    n_exp = table.shape[1]
    return pl.pallas_call(
        _cast_kernel,
        grid=(n_exp // CAST_ROWS,),
        in_specs=[pl.BlockSpec((None, CAST_ROWS, D_MODEL), lambda i: (l, i, 0))],
        out_specs=pl.BlockSpec((CAST_ROWS, D_MODEL), lambda i: (i, 0)),
        out_shape=jax.ShapeDtypeStruct((n_exp, D_MODEL), BF16),
        compiler_params=_cparams("parallel"),
        name="expert_table_bf16",
    )(table)


def _rope_tables():
    t = jnp.arange(DEC_SEQ)
    row = (t // GRID_W).astype(F32)
    col = (t % GRID_W).astype(F32)

    def angles(rot_dim):
        n_freq = rot_dim // 4
        inv_freq = ROPE_THETA ** (-jnp.arange(n_freq, dtype=F32) / n_freq)
        return jnp.concatenate([row[:, None] * inv_freq, col[:, None] * inv_freq], axis=-1)

    def pack(cos_l, sa_l, sb_l):
        return jnp.stack([cos_l, sa_l, sb_l])

    ang_b = angles(MLA_ROPE)
    cb, sb = jnp.cos(ang_b), jnp.sin(ang_b)
    one, zero = jnp.ones((DEC_SEQ, 64), F32), jnp.zeros((DEC_SEQ, 64), F32)
    z16, z32 = jnp.zeros((DEC_SEQ, 16), F32), jnp.zeros((DEC_SEQ, 32), F32)
    rope_b = pack(jnp.concatenate([one, cb, cb, jnp.ones((DEC_SEQ, 32), F32)], axis=1),
                  jnp.concatenate([zero, -sb, z16, z32], axis=1),
                  jnp.concatenate([zero, z16, sb, z32], axis=1))
    ang_d = angles(DIFF_QK_DIM)
    cd, sd = jnp.cos(ang_d), jnp.sin(ang_d)
    rope_d = pack(jnp.concatenate([cd, cd, cd, cd], axis=1),
                  jnp.concatenate([-sd, z32, -sd, z32], axis=1),
                  jnp.concatenate([z32, sd, z32, sd], axis=1))
    ident = pack(jnp.ones((DEC_SEQ, LANES), F32), jnp.zeros((DEC_SEQ, LANES), F32), jnp.zeros((DEC_SEQ, LANES), F32))
    return rope_b, rope_d, ident


def _na_bias_table(rpb):
    col = jnp.arange(GRID_W)
    dc = jnp.clip(col[None, :] - col[:, None], -(NA_WIN_COLS - 1), NA_WIN_COLS - 1) + NA_WIN_COLS - 1
    rpb_cols = rpb[:, :, dc]
    tabs = [rpb_cols[:, off:off + NA_WIN_ROWS].transpose(0, 2, 1, 3).reshape(NA_HEADS, GRID_W, NA_WIN_ROWS * GRID_W)
            for off in range(NA_WIN_ROWS)]
    return jnp.stack(tabs)


def _pad_cols(w, left, right):
    return jnp.pad(w, ((0, 0), (left, right)))


def kernel(x_prompt, x_sample, cache_mla_ckv, cache_mla_krope, cache_na_k, cache_na_v, cache_diff_k, cache_diff_v, c, c_ctx, w_mod, b_mod, w_in, sgu_norm_g, sgu_w, sgu_b, mla_q_norm_g, mla_w_uq, mla_kv_norm_g, mla_w_ukv, na_rpb, diff_lambda_q1, diff_lambda_k1, diff_lambda_q2, diff_lambda_k2, diff_norm_g, w_branch_a, w_branch_b, w_branch_c, w_branch_d, w_gate, b_gate, w_out, ln1_g, ln1_b, peer_w_q, peer_subkeys, peer_u, peer_v, ln2_g, ln2_b):
    x_ctx = x_prompt.reshape(N_CTX, D_MODEL)
    x_lat = x_sample.reshape(N_LAT, D_MODEL)
    cond = jnp.concatenate([c_ctx[None], c, jnp.zeros((N_COND - 1 - DEC_BATCH, D_MODEL), F32)], axis=0)
    mod_all = _mod_call(cond, w_mod, b_mod)
    rope_b, rope_d, rope_id = _rope_tables()
    cache_kr_pad = jnp.pad(cache_mla_krope, ((0, 0), (0, 0), (0, 0), (MLA_NOPE, LANES - MLA_NOPE - MLA_ROPE)))
    cache_na_k2 = cache_na_k.reshape(DEC_BATCH, DEPTH, PAST_LEN, 256)
    cache_na_v2 = cache_na_v.reshape(DEC_BATCH, DEPTH, PAST_LEN, 256)
    cache_diff_k2 = cache_diff_k.reshape(DEC_BATCH, DEPTH, PAST_LEN, 512)
    cache_diff_v2 = cache_diff_v.reshape(DEC_BATCH, DEPTH, PAST_LEN, 512)

    ctx_out = []
    for l in range(DEPTH):
        lambda_init = 0.8 - 0.6 * math.exp(-0.3 * l)
        mod_ctx = jnp.broadcast_to(mod_all[l, 0], (N_CTX // DEC_SEQ, 1, 6 * D_MODEL))
        mod_lat = mod_all[l, 1:1 + DEC_BATCH].reshape(DEC_BATCH, 1, 6 * D_MODEL)

        wi = w_in[l]
        kr_cols = _pad_cols(wi[:, C_KR:C_KR + MLA_ROPE], MLA_NOPE, LANES - MLA_NOPE - MLA_ROPE)
        w_in_r = jnp.concatenate([wi[:, :C_KR], kr_cols, wi[:, C_KR + MLA_ROPE:]], axis=1).astype(BF16)
        wuq = mla_w_uq[l].reshape(MLA_Q_LORA, MLA_HEADS, MLA_NOPE + MLA_ROPE)
        wuq = jnp.pad(wuq, ((0, 0), (0, 0), (0, LANES - MLA_NOPE - MLA_ROPE))).reshape(MLA_Q_LORA, -1).astype(BF16)
        wukv = mla_w_ukv[l].reshape(MLA_KV_LORA, MLA_HEADS, MLA_NOPE + MLA_V)
        wuk = jnp.pad(wukv[:, :, :MLA_NOPE], ((0, 0), (0, 0), (0, LANES - MLA_NOPE))).reshape(MLA_KV_LORA, -1)
        wuk = wuk.astype(BF16)
        wuv = wukv[:, :, MLA_NOPE:].reshape(MLA_KV_LORA, -1).astype(BF16)
        sgu_bias = jnp.repeat(sgu_b[l].T, SGU_WIDTH // SGU_GROUPS, axis=1)
        lams = [p[l].reshape(1, DIFF_QK_DIM) for p in (diff_lambda_q1, diff_lambda_k1, diff_lambda_q2, diff_lambda_k2)]
        dg = diff_norm_g[l].reshape(1, DIFF_V_DIM)

        inproj_weights = (w_in_r, sgu_norm_g[l].reshape(1, -1), sgu_w[l].astype(BF16), sgu_bias,
                          mla_q_norm_g[l].reshape(1, -1), mla_kv_norm_g[l].reshape(1, -1), wuq)
        merge_weights = (w_gate[l].astype(BF16), b_gate[l].reshape(1, -1),
                         w_branch_a[l].astype(BF16), w_branch_b[l].astype(BF16), w_branch_c[l].astype(BF16),
                         w_branch_d[l].astype(BF16), w_out[l].astype(BF16), ln1_g[l].reshape(1, -1),
                         ln1_b[l].reshape(1, -1))
        keys = peer_subkeys[l].reshape(2 * PEER_HEADS, PEER_N_KEYS, PEER_KEY_DIM // 2).astype(BF16)
        wqt = peer_w_q[l].T.astype(BF16)
        u_bf, v_bf = _table_bf16(peer_u, l), _table_bf16(peer_v, l)
        ln2 = (ln2_g[l].reshape(1, -1), ln2_b[l].reshape(1, -1))

        def channel_mix(x, mod, oa, ob, oc, od):
            x1, h2t = _merge_call(x, mod, oa, ob, oc, od, *merge_weights)
            r2, e2, n1, e1 = _router_call(h2t, wqt, keys)
            return _peer_call(h2t, u_bf, v_bf, r2, e2, n1, e1, x1, mod, *ln2)

        oa, mq, ckv, kr, nq, nk, nv, dq, dk, dv = _inproj_call(x_ctx, mod_ctx, rope_id, rope_id, *inproj_weights)
        ob, oc, od = _ctx_attn_call(lambda_init, (mq, ckv, kr, nq, nk, nv, dq, dk, dv), wuk, wuv, lams, dg)
        x_ctx = channel_mix(x_ctx, mod_ctx, oa, ob, oc, od)
        ctx_out.append((ckv.reshape(BATCH, SEQ, MLA_KV_LORA),
                        kr[:, MLA_NOPE:MLA_NOPE + MLA_ROPE].reshape(BATCH, SEQ, MLA_ROPE),
                        nk.reshape(BATCH, SEQ, NA_HEADS, NA_HEAD_DIM),
                        nv.reshape(BATCH, SEQ, NA_HEADS, NA_HEAD_DIM),
                        dk.reshape(BATCH, SEQ, DIFF_HEADS, 2 * DIFF_QK_DIM),
                        dv.reshape(BATCH, SEQ, DIFF_HEADS, DIFF_V_DIM)))

        oa, mq, ckv, kr, nq, nk, nv, dq, dk, dv = _inproj_call(x_lat, mod_lat, rope_b, rope_d, *inproj_weights)
        ob = _lat_mla_call(l, mq, ckv, kr, cache_mla_ckv, cache_kr_pad, wuk, wuv)
        oc = _lat_na_call(l, nq, nk, nv, cache_na_k2, cache_na_v2, _na_bias_table(na_rpb[l]))
        od = _lat_diff_call(l, lambda_init, dq, dk, dv, cache_diff_k2, cache_diff_v2, lams, dg)
        x_lat = channel_mix(x_lat, mod_lat, oa, ob, oc, od)

    y_prompt = x_ctx.reshape(BATCH, SEQ, D_MODEL)
    y_sample = x_lat.reshape(DEC_BATCH, DEC_SEQ, D_MODEL)
    new = [jnp.stack([t[k] for t in ctx_out], axis=1) for k in range(6)]
    return (y_prompt, y_sample, *new)
```

```python
import functools
import math

import jax
import jax.numpy as jnp
from jax import lax
from jax.experimental import pallas as pl
from jax.experimental.pallas import tpu as pltpu

F32 = jnp.float32
BF16 = jnp.bfloat16

D_MODEL = 1024
BATCH = 32
SEQ = 256
DEPTH = 2
DEC_BATCH = 8
DEC_SEQ = 1024
PAST_LEN = 512
GRID_W = 64
CHUNK = 128
SGU_GROUPS = 4
SGU_WIDTH = 256
MLA_HEADS = 4
MLA_Q_LORA = 256
MLA_KV_LORA = 128
MLA_NOPE = 64
MLA_ROPE = 32
MLA_V = 64
NA_HEADS = 4
NA_HEAD_DIM = 64
NA_WIN_ROWS = 8
NA_WIN_COLS = 16
DIFF_HEADS = 4
DIFF_QK_DIM = 64
DIFF_V_DIM = 128
PEER_HEADS = 8
PEER_N_KEYS = 128
PEER_KEY_DIM = 256
PEER_TOPK = 16
ROPE_THETA = 10000.0
LN_EPS = 1e-6
NEG_BIG = -1e30
DEEPNORM_ALPHA = (2 * DEPTH) ** 0.25

LANES = 128
SUBLANES = 8
N_CTX = BATCH * SEQ
N_LAT = DEC_BATCH * DEC_SEQ
N_COND = 16
TM = 512
ROWS = DEC_SEQ // GRID_W
Q_TILE = 256
NA_ROWS_PER_STEP = 2
ROUTER_TILE = 256
PEER_TILE = 1024
EXPERT_BLOCK = 1024
KEYS_PER_BLOCK = EXPERT_BLOCK // PEER_N_KEYS
GATE_LANES = 256
CAST_ROWS = 2048
VMEM_LIMIT = 56 * 1024 * 1024

C_AU, C_AV, C_CQ, C_CKV, C_KR = 0, 256, 512, 768, 896
C_NQ, C_NK, C_NV, C_DQ, C_DK, C_DV, C_END = 1024, 1280, 1536, 1792, 2304, 2816, 3328


def _ln(x):
    mu = jnp.mean(x, axis=-1, keepdims=True)
    xc = x - mu
    var = jnp.mean(xc * xc, axis=-1, keepdims=True)
    return xc * lax.rsqrt(var + LN_EPS)


def _rms(x):
    return x * lax.rsqrt(jnp.mean(x * x, axis=-1, keepdims=True) + LN_EPS)


def _gelu(x):
    return 0.5 * x * (1.0 + lax.erf(x * (1.0 / math.sqrt(2.0))))


def _dot(a, b):
    return jnp.dot(a, b, preferred_element_type=F32)


def _dot_nt(a, b):
    return lax.dot_general(a, b, (((1,), (1,)), ((), ())), preferred_element_type=F32)


def _rope(x, tab_ref, half):
    return (x * tab_ref[0] + pltpu.roll(x, LANES - half, 1) * tab_ref[1] + pltpu.roll(x, half, 1) * tab_ref[2])


def _cparams(*sem):
    return pltpu.CompilerParams(dimension_semantics=sem, vmem_limit_bytes=VMEM_LIMIT)


def _mod_kernel(cond_ref, w_ref, b_ref, o_ref):
    c = cond_ref[...]
    s = c * jax.nn.sigmoid(c)
    o_ref[...] = _dot(s, w_ref[...]) + b_ref[...]


def _mod_call(cond, w_mod, b_mod):
    nb = 1536
    return pl.pallas_call(
        _mod_kernel,
        grid=(DEPTH, 6 * D_MODEL // nb),
        in_specs=[pl.BlockSpec((N_COND, D_MODEL), lambda l, j: (0, 0)),
                  pl.BlockSpec((None, D_MODEL, nb), lambda l, j: (l, 0, j)),
                  pl.BlockSpec((None, 1, nb), lambda l, j: (l, 0, j))],
        out_specs=pl.BlockSpec((None, N_COND, nb), lambda l, j: (l, 0, j)),
        out_shape=jax.ShapeDtypeStruct((DEPTH, N_COND, 6 * D_MODEL), F32),
        compiler_params=_cparams("arbitrary", "arbitrary"),
        name="mod_vectors",
    )(cond, w_mod, b_mod.reshape(DEPTH, 1, 6 * D_MODEL))


def _mod_row(i):
    return i // (DEC_SEQ // TM)


def _pos_block(i):
    return i % (DEC_SEQ // TM)


def _inproj_kernel(x_ref, mod_ref, rb_ref, rd_ref, w_in_ref, sgu_g_ref, sgu_w_ref, sgu_bias_ref,
                   qg_ref, kvg_ref, wuq_ref,
                   oa_ref, mq_ref, ckv_ref, kr_ref, nq_ref, nk_ref, nv_ref, dq_ref, dk_ref, dv_ref):
    x = x_ref[...]
    shift = mod_ref[0, :, 0:D_MODEL]
    scale = mod_ref[0, :, D_MODEL:2 * D_MODEL]
    h = (_ln(x) * (1.0 + scale) + shift).astype(BF16)

    ya = _dot(h, w_in_ref[:, C_AU:C_CQ])
    u = _gelu(ya[:, :SGU_WIDTH])
    v = _gelu(ya[:, SGU_WIDTH:])
    vn = (_ln(v) * sgu_g_ref[...]).astype(BF16)
    group = lax.broadcasted_iota(jnp.int32, (CHUNK, SGU_WIDTH), 1) // (SGU_WIDTH // SGU_GROUPS)
    for c in range(TM // CHUNK):
        rows = slice(c * CHUNK, (c + 1) * CHUNK)
        mixed = sgu_bias_ref[...]
        for g in range(SGU_GROUPS):
            mixed = mixed + jnp.where(group == g, _dot(sgu_w_ref[g], vn[rows]), 0.0)
        oa_ref[rows, :] = (u[rows] * mixed).astype(oa_ref.dtype)

    ym = _dot(h, w_in_ref[:, C_CQ:C_NQ])
    cq = (_rms(ym[:, :MLA_Q_LORA]) * qg_ref[...]).astype(BF16)
    mq = _dot(cq, wuq_ref[...])
    for g in range(MLA_HEADS):
        lanes = slice(g * LANES, (g + 1) * LANES)
        mq_ref[:, lanes] = _rope(mq[:, lanes], rb_ref, MLA_ROPE // 2)
    ckv_ref[...] = _rms(ym[:, MLA_Q_LORA:MLA_Q_LORA + MLA_KV_LORA]) * kvg_ref[...]
    kr_ref[...] = _rope(ym[:, MLA_Q_LORA + MLA_KV_LORA:], rb_ref, MLA_ROPE // 2)

    yn = _dot(h, w_in_ref[:, C_NQ:C_DQ])
    nq_ref[...] = yn[:, 0:256]
    nk_ref[...] = yn[:, 256:512]
    nv_ref[...] = yn[:, 512:768]

    yd = _dot(h, w_in_ref[:, C_DQ:C_END])
    for g in range(4):
        lanes = slice(g * LANES, (g + 1) * LANES)
        dq_ref[:, lanes] = _rope(yd[:, g * LANES:(g + 1) * LANES], rd_ref, DIFF_QK_DIM // 2)
        dk_ref[:, lanes] = _rope(yd[:, 512 + g * LANES:512 + (g + 1) * LANES], rd_ref, DIFF_QK_DIM // 2)
    dv_ref[...] = yd[:, 1024:1536]


def _inproj_call(x, mod, rope_b, rope_d, w_in_r, sgu_g, sgu_w, sgu_bias, qg, kvg, wuq):
    tile = lambda w: pl.BlockSpec((TM, w), lambda i: (i, 0))
    full = lambda *s: pl.BlockSpec(s, lambda i: (0,) * len(s))
    widths = (SGU_WIDTH, 512, MLA_KV_LORA, LANES, 256, 256, 256, 512, 512, 512)
    dtypes = (BF16,) + (F32,) * 9
    n_tok = x.shape[0]
    return pl.pallas_call(
        _inproj_kernel,
        grid=(n_tok // TM,),
        in_specs=[tile(D_MODEL),
                  pl.BlockSpec((1, 1, 6 * D_MODEL), lambda i: (_mod_row(i), 0, 0)),
                  pl.BlockSpec((3, TM, LANES), lambda i: (0, _pos_block(i), 0)),
                  pl.BlockSpec((3, TM, LANES), lambda i: (0, _pos_block(i), 0)),
                  full(D_MODEL, C_END), full(1, SGU_WIDTH), full(SGU_GROUPS, CHUNK, CHUNK),
                  full(CHUNK, SGU_WIDTH), full(1, MLA_Q_LORA), full(1, MLA_KV_LORA),
                  full(MLA_Q_LORA, MLA_HEADS * LANES)],
        out_specs=[tile(w) for w in widths],
        out_shape=[jax.ShapeDtypeStruct((n_tok, w), dt) for w, dt in zip(widths, dtypes)],
        compiler_params=_cparams("parallel"),
        name="in_projection",
    )(x, mod, rope_b, rope_d, w_in_r, sgu_g, sgu_w, sgu_bias, qg, kvg, wuq)


def _half_mask(lo):
    lane = lax.broadcasted_iota(jnp.int32, (1, LANES), 1)
    return (lane >= lo) & (lane < lo + 64)


def _softmax_pv(scores, values, lanes):
    m = scores[0].max(axis=-1, keepdims=True)
    for s in scores[1:]:
        m = jnp.maximum(m, s.max(axis=-1, keepdims=True))
    den = None
    o = None
    for s, v in zip(scores, values):
        p = jnp.exp(s - m)
        d = p.sum(axis=-1, keepdims=True)
        den = d if den is None else den + d
        pv = _dot(p.astype(BF16), v[:, lanes])
        o = pv if o is None else o + pv
    return o / den


def _pair_attention(q, keys, vals, scale, bias_fn=None):
    assert math.frexp(scale)[0] == 0.5
    n = q.shape[0]
    outs = []
    for pair in range(2):
        lanes = slice(pair * LANES, (pair + 1) * LANES)
        qp = q[:, lanes] * jnp.asarray(scale, BF16)
        masks = [_half_mask(64 * sub) for sub in range(2)]
        qs = jnp.concatenate([jnp.where(m, qp, jnp.zeros_like(qp)) for m in masks], axis=0)
        scores = [_dot_nt(qs, k[:, lanes]) for k in keys]
        if bias_fn is not None:
            scores = bias_fn(pair, scores)
        o = _softmax_pv(scores, vals, lanes)
        outs.append(jnp.where(masks[0], o[:n], 0.0) + jnp.where(masks[1], o[n:], 0.0))
    return outs


def _mla_attention(q, k_blocks, v_blocks, o_ref, rows):
    scale = (MLA_NOPE + MLA_ROPE) ** -0.5
    for pair in range(2):
        lanes = slice(pair * LANES, (pair + 1) * LANES)
        acc = None
        for sub in range(2):
            head = 2 * pair + sub
            hl = slice(head * LANES, (head + 1) * LANES)
            scores = [_dot_nt(q[:, hl], k[:, hl]) * scale for k in k_blocks]
            o = jnp.where(_half_mask(64 * sub), _softmax_pv(scores, v_blocks, lanes), 0.0)
            acc = o if acc is None else acc + o
        o_ref[rows, lanes] = acc.astype(o_ref.dtype)


def _diff_lambda(lq1, lk1, lq2, lk2, lambda_init):
    a = jnp.sum(lq1[...] * lk1[...], axis=-1, keepdims=True)
    b = jnp.sum(lq2[...] * lk2[...], axis=-1, keepdims=True)
    return jnp.exp(a) - jnp.exp(b) + lambda_init


def _diff_attention(q, k_blocks, v_blocks, lam, norm_g, lambda_init, o_ref, rows):
    scale = DIFF_QK_DIM ** -0.5
    assert math.frexp(scale)[0] == 0.5
    for head in range(DIFF_HEADS):
        hl = slice(head * LANES, (head + 1) * LANES)
        qh = q[:, hl] * jnp.asarray(scale, BF16)
        probs = []
        for sub in range(2):
            qm = jnp.where(_half_mask(64 * sub), qh, jnp.zeros_like(qh))
            scores = [_dot_nt(qm, k[:, hl]) for k in k_blocks]
            m = scores[0].max(axis=-1, keepdims=True)
            for s in scores[1:]:
                m = jnp.maximum(m, s.max(axis=-1, keepdims=True))
            ps = [jnp.exp(s - m) for s in scores]
            den = ps[0].sum(axis=-1, keepdims=True)
            for p in ps[1:]:
                den = den + p.sum(axis=-1, keepdims=True)
            probs.append((ps, 1.0 / den))
        o = None
        for i, v in enumerate(v_blocks):
            w = probs[0][0][i] * probs[0][1] - probs[1][0][i] * (lam * probs[1][1])
            pv = _dot(w.astype(BF16), v[:, hl])
            o = pv if o is None else o + pv
        o = _rms(o) * norm_g * (1.0 - lambda_init)
        o_ref[rows, hl] = o.astype(o_ref.dtype)


def _ctx_attn_kernel(lambda_init, mq_ref, ckv_ref, kr_ref, nq_ref, nk_ref, nv_ref, dq_ref, dk_ref, dv_ref,
                     wuk_ref, wuv_ref, lq1, lk1, lq2, lk2, dg_ref, ob_ref, oc_ref, od_ref):
    rows = slice(0, SEQ)
    ckv = ckv_ref[...].astype(BF16)
    kr = kr_ref[...]
    k_b = (_dot(ckv, wuk_ref[...]) + jnp.concatenate([kr] * MLA_HEADS, axis=1)).astype(BF16)
    v_b = _dot(ckv, wuv_ref[...]).astype(BF16)
    _mla_attention(mq_ref[...].astype(BF16), [k_b], [v_b], ob_ref, rows)

    outs = _pair_attention(nq_ref[...].astype(BF16), [nk_ref[...].astype(BF16)], [nv_ref[...].astype(BF16)],
                           NA_HEAD_DIM ** -0.5)
    for pair in range(2):
        oc_ref[:, pair * LANES:(pair + 1) * LANES] = outs[pair].astype(oc_ref.dtype)

    lam = _diff_lambda(lq1, lk1, lq2, lk2, lambda_init)
    _diff_attention(dq_ref[...].astype(BF16), [dk_ref[...].astype(BF16)], [dv_ref[...].astype(BF16)],
                    lam, dg_ref[...], lambda_init, od_ref, rows)


def _ctx_attn_call(lambda_init, acts, wuk, wuv, lams, dg):
    mq, ckv, kr, nq, nk, nv, dq, dk, dv = acts
    seq = lambda w: pl.BlockSpec((SEQ, w), lambda b: (b, 0))
    full = lambda *s: pl.BlockSpec(s, lambda b: (0,) * len(s))
    return pl.pallas_call(
        functools.partial(_ctx_attn_kernel, lambda_init),
        grid=(BATCH,),
        in_specs=[seq(512), seq(128), seq(128), seq(256), seq(256), seq(256), seq(512), seq(512), seq(512),
                  full(MLA_KV_LORA, 512), full(MLA_KV_LORA, 256)] + [full(1, DIFF_QK_DIM)] * 4
                 + [full(1, DIFF_V_DIM)],
        out_specs=[seq(256), seq(256), seq(512)],
        out_shape=[jax.ShapeDtypeStruct((N_CTX, w), BF16) for w in (256, 256, 512)],
        compiler_params=_cparams("parallel"),
        name="context_attention",
    )(mq, ckv, kr, nq, nk, nv, dq, dk, dv, wuk, wuv, *lams, dg)


def _lat_mla_kernel(mq_ref, ckv_ref, kr_ref, cckv_ref, ckr_ref, wuk_ref, wuv_ref, o_ref):
    def expand(ckv_f32, kr):
        ckv = ckv_f32.astype(BF16)
        k = (_dot(ckv, wuk_ref[...]) + jnp.concatenate([kr] * MLA_HEADS, axis=1)).astype(BF16)
        return k, _dot(ckv, wuv_ref[...]).astype(BF16)

    k_lat, v_lat = expand(ckv_ref[...], kr_ref[...])
    k_ctx, v_ctx = expand(cckv_ref[...], ckr_ref[...])
    k_all = jnp.concatenate([k_lat, k_ctx], axis=0)
    v_all = jnp.concatenate([v_lat, v_ctx], axis=0)
    for t in range(DEC_SEQ // Q_TILE):
        rows = slice(t * Q_TILE, (t + 1) * Q_TILE)
        _mla_attention(mq_ref[rows, :].astype(BF16), [k_all], [v_all], o_ref, rows)


def _lat_mla_call(l, mq, ckv, kr, cache_ckv, cache_kr_pad, wuk, wuv):
    seq = lambda w: pl.BlockSpec((DEC_SEQ, w), lambda b: (b, 0))
    cache = lambda w: pl.BlockSpec((None, None, PAST_LEN, w), lambda b: (b, l, 0, 0))
    full = lambda *s: pl.BlockSpec(s, lambda b: (0,) * len(s))
    return pl.pallas_call(
        _lat_mla_kernel,
        grid=(DEC_BATCH,),
        in_specs=[seq(512), seq(128), seq(128), cache(MLA_KV_LORA), cache(LANES),
                  full(MLA_KV_LORA, 512), full(MLA_KV_LORA, 256)],
        out_specs=pl.BlockSpec((DEC_SEQ, 256), lambda b: (b, 0)),
        out_shape=jax.ShapeDtypeStruct((N_LAT, 256), BF16),
        compiler_params=_cparams("parallel"),
        name="latent_mla_attention",
    )(mq, ckv, kr, cache_ckv, cache_kr_pad, wuk, wuv)


def _win_start(r):
    return jnp.clip(r - NA_WIN_ROWS // 2, 0, ROWS - NA_WIN_ROWS)


def _lat_na_kernel(nq_ref, nk_ref, nv_ref, ck_ref, cv_ref, bias_ref, o_ref):
    win = NA_WIN_ROWS * GRID_W
    k_c = ck_ref[...].astype(BF16)
    v_c = cv_ref[...].astype(BF16)
    q_col = lax.broadcasted_iota(jnp.int32, (2 * GRID_W, win), 0) % GRID_W
    k_col = lax.broadcasted_iota(jnp.int32, (2 * GRID_W, win), 1) % GRID_W
    c0 = jnp.clip(q_col - NA_WIN_COLS // 2, 0, GRID_W - NA_WIN_COLS)
    col_in = (k_col >= c0) & (k_col < c0 + NA_WIN_COLS)

    for rr in range(NA_ROWS_PER_STEP):
        r = pl.program_id(1) * NA_ROWS_PER_STEP + rr
        first = _win_start(r)
        start = pl.multiple_of(first * GRID_W, GRID_W)
        off = first - r + NA_WIN_ROWS - 1
        k_all = jnp.concatenate([nk_ref[pl.ds(start, win), :].astype(BF16), k_c], axis=0)
        v_all = jnp.concatenate([nv_ref[pl.ds(start, win), :].astype(BF16), v_c], axis=0)

        def bias_fn(pair, scores, off=off):
            s = scores[0]
            bias = jnp.concatenate([bias_ref[off, 2 * pair], bias_ref[off, 2 * pair + 1]], axis=0)
            s_win = jnp.where(col_in, s[:, :win] + bias, NEG_BIG)
            return [jnp.concatenate([s_win, s[:, win:]], axis=1)]

        rows = slice(rr * GRID_W, (rr + 1) * GRID_W)
        outs = _pair_attention(nq_ref[rows, :].astype(BF16), [k_all], [v_all], NA_HEAD_DIM ** -0.5, bias_fn)
        for pair in range(2):
            o_ref[rows, pair * LANES:(pair + 1) * LANES] = outs[pair].astype(o_ref.dtype)


def _lat_na_call(l, nq, nk, nv, cache_k, cache_v, bias_tab):
    seq = pl.BlockSpec((DEC_SEQ, 256), lambda b, r: (b, 0))
    cache = pl.BlockSpec((None, None, PAST_LEN, 256), lambda b, r: (b, l, 0, 0))
    steps = ROWS // NA_ROWS_PER_STEP
    q_rows = NA_ROWS_PER_STEP * GRID_W
    return pl.pallas_call(
        _lat_na_kernel,
        grid=(DEC_BATCH, steps),
        in_specs=[pl.BlockSpec((q_rows, 256), lambda b, r: (b * steps + r, 0)),
                  seq, seq, cache, cache,
                  pl.BlockSpec((NA_WIN_ROWS, NA_HEADS, GRID_W, NA_WIN_ROWS * GRID_W), lambda b, r: (0, 0, 0, 0))],
        out_specs=pl.BlockSpec((q_rows, 256), lambda b, r: (b * steps + r, 0)),
        out_shape=jax.ShapeDtypeStruct((N_LAT, 256), BF16),
        compiler_params=_cparams("parallel", "arbitrary"),
        name="latent_neighbourhood_attention",
    )(nq, nk, nv, cache_k, cache_v, bias_tab)


def _lat_diff_kernel(lambda_init, dq_ref, dk_ref, dv_ref, ck_ref, cv_ref, lq1, lk1, lq2, lk2, dg_ref, o_ref):
    lam = _diff_lambda(lq1, lk1, lq2, lk2, lambda_init)
    k_blocks = [dk_ref[...].astype(BF16), ck_ref[...].astype(BF16)]
    v_blocks = [dv_ref[...].astype(BF16), cv_ref[...].astype(BF16)]
    for t in range(DEC_SEQ // Q_TILE):
        rows = slice(t * Q_TILE, (t + 1) * Q_TILE)
        _diff_attention(dq_ref[rows, :].astype(BF16), k_blocks, v_blocks, lam, dg_ref[...], lambda_init,
                        o_ref, rows)


def _lat_diff_call(l, lambda_init, dq, dk, dv, cache_k, cache_v, lams, dg):
    seq = pl.BlockSpec((DEC_SEQ, 512), lambda b: (b, 0))
    cache = pl.BlockSpec((None, None, PAST_LEN, 512), lambda b: (b, l, 0, 0))
    full = lambda *s: pl.BlockSpec(s, lambda b: (0,) * len(s))
    return pl.pallas_call(
        functools.partial(_lat_diff_kernel, lambda_init),
        grid=(DEC_BATCH,),
        in_specs=[seq, seq, seq, cache, cache] + [full(1, DIFF_QK_DIM)] * 4 + [full(1, DIFF_V_DIM)],
        out_specs=pl.BlockSpec((DEC_SEQ, 512), lambda b: (b, 0)),
        out_shape=jax.ShapeDtypeStruct((N_LAT, 512), BF16),
        compiler_params=_cparams("parallel"),
        name="latent_differential_attention",
    )(dq, dk, dv, cache_k, cache_v, *lams, dg)


def _merge_kernel(x_ref, mod_ref, oa_ref, ob_ref, oc_ref, od_ref, wg_ref, bg_ref,
                  wa_ref, wb_ref, wc_ref, wd_ref, wo_ref, g_ref, b_ref, x1_ref, h2t_ref):
    x = x_ref[...]
    mod = lambda k: mod_ref[0, :, k * D_MODEL:(k + 1) * D_MODEL]
    h = (_ln(x) * (1.0 + mod(1)) + mod(0)).astype(BF16)
    merged = None
    for i, (o_ref, w_ref) in enumerate(((oa_ref, wa_ref), (ob_ref, wb_ref), (oc_ref, wc_ref), (od_ref, wd_ref))):
        cols = slice(i * D_MODEL, (i + 1) * D_MODEL)
        gate = jax.nn.sigmoid(_dot(h, wg_ref[:, cols]) + bg_ref[:, cols])
        term = gate * _dot(o_ref[...], w_ref[...])
        merged = term if merged is None else merged + term
    mix = _dot(merged.astype(BF16), wo_ref[...])
    x1 = _ln(DEEPNORM_ALPHA * x + mod(2) * mix) * g_ref[...] + b_ref[...]
    x1_ref[...] = x1
    h2 = _ln(x1) * (1.0 + mod(4)) + mod(3)
    h2t_ref[...] = h2.T.astype(BF16)


def _merge_call(x, mod, oa, ob, oc, od, wg, bg, wa, wb, wc, wd, wo, g, b):
    tile = lambda w: pl.BlockSpec((TM, w), lambda i: (i, 0))
    full = lambda *s: pl.BlockSpec(s, lambda i: (0,) * len(s))
    n_tok = x.shape[0]
    return pl.pallas_call(
        _merge_kernel,
        grid=(n_tok // TM,),
        in_specs=[tile(D_MODEL), pl.BlockSpec((1, 1, 6 * D_MODEL), lambda i: (_mod_row(i), 0, 0)),
                  tile(256), tile(256), tile(256), tile(512),
                  full(D_MODEL, 4 * D_MODEL), full(1, 4 * D_MODEL),
                  full(256, D_MODEL), full(256, D_MODEL), full(256, D_MODEL), full(512, D_MODEL),
                  full(D_MODEL, D_MODEL), full(1, D_MODEL), full(1, D_MODEL)],
        out_specs=[tile(D_MODEL), pl.BlockSpec((D_MODEL, TM), lambda i: (0, i))],
        out_shape=[jax.ShapeDtypeStruct((n_tok, D_MODEL), F32), jax.ShapeDtypeStruct((D_MODEL, n_tok), BF16)],
        compiler_params=_cparams("parallel"),
        name="branch_merge",
    )(x, mod, oa, ob, oc, od, wg, bg, wa, wb, wc, wd, wo, g, b)


KEY_MIN = -2 ** 31


def _tree_sum(terms):
    while len(terms) > 1:
        terms = [a + b for a, b in zip(terms[0::2], terms[1::2])] + ([terms[-1]] if len(terms) % 2 else [])
    return terms[0]


def _row_gather(table, idx):
    ii = idx.astype(jnp.int32)
    low = ii & (SUBLANES - 1)
    outs = []
    for c in range(idx.shape[0] // SUBLANES):
        rows = slice(c * SUBLANES, (c + 1) * SUBLANES)
        lo = jnp.take_along_axis(table[0:SUBLANES], low[rows], axis=0)
        hi = jnp.take_along_axis(table[SUBLANES:PEER_TOPK], low[rows], axis=0)
        outs.append(jnp.where(ii[rows] < SUBLANES, lo, jnp.where(ii[rows] < PEER_TOPK, hi, 0.0)))
    return jnp.concatenate(outs, axis=0)


def _sort_key(x):
    b = lax.bitcast_convert_type(x + 0.0, jnp.int32)
    return b ^ ((b >> 31) & 0x7FFFFFFF)


def _key_value(k):
    return lax.bitcast_convert_type(k ^ ((k >> 31) & 0x7FFFFFFF), F32)


def _top16(s):
    row = lax.broadcasted_iota(jnp.int32, s.shape, 0).astype(F32)
    krow = lax.broadcasted_iota(jnp.int32, (PEER_TOPK, s.shape[1]), 0)

    def body(k, carry):
        work, rank, vals = carry
        m = jnp.max(work, axis=0, keepdims=True)
        idx = jnp.min(jnp.where(work == m, row, float(PEER_N_KEYS)), axis=0, keepdims=True)
        sel = row == idx
        rank = jnp.where(sel, jnp.asarray(k, jnp.int32).astype(F32), rank)
        work = jnp.where(sel, -jnp.inf, work)
        vals = jnp.where(krow == k, m, vals)
        return work, rank, vals

    init = (s, jnp.full(s.shape, float(PEER_N_KEYS), F32), jnp.zeros((PEER_TOPK, s.shape[1]), F32))
    _, rank, vals = lax.fori_loop(0, PEER_TOPK, body, init)
    return vals, rank


def _top16_pair(s1, s2):
    krow = lax.broadcasted_iota(jnp.int32, (PEER_TOPK, LANES), 0)

    def body(k, carry):
        w1, w2, v1, v2 = carry
        code = KEY_MIN + jnp.asarray(k, jnp.int32)
        m1 = jnp.max(w1, axis=0, keepdims=True)
        m2 = jnp.max(w2, axis=0, keepdims=True)
        w1 = jnp.where(w1 == m1, code, w1)
        w2 = jnp.where(w2 == m2, code, w2)
        return w1, w2, jnp.where(krow == k, m1, v1), jnp.where(krow == k, m2, v2)

    zeros = jnp.zeros((PEER_TOPK, LANES), jnp.int32)
    w1, w2, v1, v2 = lax.fori_loop(0, PEER_TOPK, body, (_sort_key(s1), _sort_key(s2), zeros, zeros))

    def decode(w):
        taken = w < KEY_MIN + PEER_TOPK
        rank = jnp.where(taken, (w - KEY_MIN).astype(F32), float(PEER_N_KEYS))
        return rank, jnp.sum(taken.astype(F32), axis=0, keepdims=True)

    r1, c1 = decode(w1)
    r2, c2 = decode(w2)
    ties = jnp.max(jnp.maximum(jnp.abs(c1 - PEER_TOPK), jnp.abs(c2 - PEER_TOPK))) > 0.5
    return _key_value(v1), _key_value(v2), r1, r2, ties


def _merge_counts(hs1, hs2):
    krow = lax.broadcasted_iota(jnp.int32, hs1.shape, 0).astype(F32)

    def body(_, carry):
        cnt, front = carry
        m = jnp.max(front, axis=0, keepdims=True)
        win = jnp.min(jnp.where(front == m, krow, float(PEER_TOPK)), axis=0, keepdims=True)
        sel = krow == win
        cnt = jnp.where(sel, cnt + 1.0, cnt)
        nxt = jnp.where(cnt < float(PEER_TOPK), hs1 + _row_gather(hs2, cnt), -jnp.inf)
        return cnt, jnp.where(sel, nxt, front)

    cnt, _ = lax.fori_loop(0, PEER_TOPK, body, (jnp.zeros(hs1.shape, F32), hs1 + hs2[0:1, :]))
    return cnt


def _router_kernel(h2t_ref, wqt_ref, keys_ref, r2_ref, e2_ref, n1_ref, e1_ref, q_scr, s_scr, hs_scr, rank1_scr):
    t = ROUTER_TILE
    q_scr[...] = _dot(wqt_ref[...], h2t_ref[...]).astype(BF16)

    for hd in range(PEER_HEADS):
        for half in range(2):
            rows = slice((2 * hd + half) * LANES, (2 * hd + half + 1) * LANES)
            s_scr[hd, half] = _dot(keys_ref[2 * hd + half], q_scr[rows, :])

    def head_body(hd, _):
        for j in range(t // LANES):
            lanes = slice(j * LANES, (j + 1) * LANES)

            def put(hs1, hs2, rank1, rank2, lanes=lanes):
                hs_scr[hd, 0, :, lanes] = hs1
                hs_scr[hd, 1, :, lanes] = hs2
                rank1_scr[hd, :, lanes] = rank1
                r2_ref[hd, :, lanes] = rank2.astype(BF16)

            *quick, ties = _top16_pair(s_scr[hd, 0, :, lanes], s_scr[hd, 1, :, lanes])
            put(*quick)

            @pl.when(ties)
            def _(lanes=lanes, put=put):
                hs1, rank1 = _top16(s_scr[hd, 0, :, lanes])
                hs2, rank2 = _top16(s_scr[hd, 1, :, lanes])
                put(hs1, hs2, rank1, rank2)
        return 0

    lax.fori_loop(0, PEER_HEADS, head_body, 0)

    for pair in range(PEER_HEADS // 2):
        heads = (2 * pair, 2 * pair + 1)
        hs1 = jnp.concatenate([hs_scr[h, 0] for h in heads], axis=1)
        hs2 = jnp.concatenate([hs_scr[h, 1] for h in heads], axis=1)
        cnt = _merge_counts(hs1, hs2)
        e1r = jnp.exp(hs1 - hs1[0:1, :])
        e2r = jnp.exp(hs2 - hs2[0:1, :])
        prefix = _tree_sum([jnp.where(cnt > float(kb), e2r[kb:kb + 1, :], 0.0) for kb in range(PEER_TOPK)])
        inv_z = 1.0 / jnp.sum(e1r * prefix, axis=0, keepdims=True)
        for i, h in enumerate(heads):
            lanes = slice(i * t, (i + 1) * t)
            e2_ref[h] = (jnp.exp(s_scr[h, 1] - hs2[0:1, lanes]) * inv_z[:, lanes]).astype(BF16)
            e1_ref[h] = 0.5 * jnp.exp(s_scr[h, 0] - hs1[0:1, lanes])
            n1_ref[h] = _row_gather(cnt[:, lanes], rank1_scr[h])


def _router_call(h2t, wqt, keys):
    t = ROUTER_TILE
    out = pl.BlockSpec((PEER_HEADS, PEER_N_KEYS, t), lambda i: (0, 0, i))
    n_tok = h2t.shape[1]
    shape = (PEER_HEADS, PEER_N_KEYS, n_tok)
    return pl.pallas_call(
        _router_kernel,
        grid=(n_tok // t,),
        in_specs=[pl.BlockSpec((D_MODEL, t), lambda i: (0, i)),
                  pl.BlockSpec((PEER_HEADS * PEER_KEY_DIM, D_MODEL), lambda i: (0, 0)),
                  pl.BlockSpec((2 * PEER_HEADS, PEER_N_KEYS, PEER_KEY_DIM // 2), lambda i: (0, 0, 0))],
        out_specs=[out] * 4,
        out_shape=[jax.ShapeDtypeStruct(shape, BF16), jax.ShapeDtypeStruct(shape, BF16),
                   jax.ShapeDtypeStruct(shape, F32), jax.ShapeDtypeStruct(shape, F32)],
        scratch_shapes=[pltpu.VMEM((PEER_HEADS * PEER_KEY_DIM, t), BF16),
                        pltpu.VMEM((PEER_HEADS, 2, PEER_N_KEYS, t), F32),
                        pltpu.VMEM((PEER_HEADS, 2, PEER_TOPK, t), F32),
                        pltpu.VMEM((PEER_HEADS, PEER_N_KEYS, t), F32)],
        compiler_params=_cparams("parallel"),
        name="peer_retrieval",
    )(h2t, wqt, keys)


def _gated_activations(ht_ref, w_ref, r2_ref, e2_ref, n1_ref, e1_ref):
    for i in range(KEYS_PER_BLOCK):
        rows = slice(i * PEER_N_KEYS, (i + 1) * PEER_N_KEYS)
        for j in range(PEER_TILE // GATE_LANES):
            lanes = slice(j * GATE_LANES, (j + 1) * GATE_LANES)
            gate = jnp.zeros((PEER_N_KEYS, GATE_LANES), BF16)
            for hd in range(PEER_HEADS):
                n_row = n1_ref[hd, i:i + 1, lanes].astype(BF16)
                c_row = e1_ref[hd, i:i + 1, lanes].astype(BF16)
                live = jnp.where(r2_ref[hd, :, lanes] < n_row, e2_ref[hd, :, lanes], jnp.zeros((), BF16))
                gate = gate + live * c_row
            x = ht_ref[rows, lanes]
            act = x * (1.0 + lax.erf(x * (1.0 / math.sqrt(2.0))))
            w_ref[rows, lanes] = act.astype(BF16) * gate


def _peer_kernel(h2t_ref, u_ref, v_ref, r2_ref, e2_ref, n1_ref, e1_ref,
                 x1_ref, mod_ref, g_ref, b_ref, o_ref, acc_ref, ht_ref, w_ref):
    e = pl.program_id(1)

    @pl.when(e == 0)
    def _():
        acc_ref[...] = jnp.zeros_like(acc_ref)

    ht_ref[...] = _dot(u_ref[...], h2t_ref[...])
    _gated_activations(ht_ref, w_ref, r2_ref, e2_ref, n1_ref, e1_ref)
    acc_ref[...] += lax.dot_general(v_ref[...], w_ref[...], (((0,), (0,)), ((), ())), preferred_element_type=F32)

    @pl.when(e == pl.num_programs(1) - 1)
    def _():
        ffn = acc_ref[...].T
        g2 = mod_ref[0, :, 5 * D_MODEL:6 * D_MODEL]
        o_ref[...] = _ln(DEEPNORM_ALPHA * x1_ref[...] + g2 * ffn) * g_ref[...] + b_ref[...]


def _peer_mod_row(i):
    return i // (DEC_SEQ // PEER_TILE)


def _peer_call(h2t, u, v, r2, e2, n1, e1, x1, mod, g, b):
    t = PEER_TILE
    n_tok = x1.shape[0]
    n_blocks = PEER_N_KEYS * PEER_N_KEYS // EXPERT_BLOCK
    gates = pl.BlockSpec((PEER_HEADS, PEER_N_KEYS, t), lambda i, g: (0, 0, i))
    keys = pl.BlockSpec((PEER_HEADS, KEYS_PER_BLOCK, t), lambda i, g: (0, g, i))
    return pl.pallas_call(
        _peer_kernel,
        grid=(n_tok // t, n_blocks),
        in_specs=[pl.BlockSpec((D_MODEL, t), lambda i, g: (0, i)),
                  pl.BlockSpec((EXPERT_BLOCK, D_MODEL), lambda i, g: (g, 0)),
                  pl.BlockSpec((EXPERT_BLOCK, D_MODEL), lambda i, g: (g, 0)),
                  gates, gates, keys, keys,
                  pl.BlockSpec((t, D_MODEL), lambda i, g: (i, 0)),
                  pl.BlockSpec((1, 1, 6 * D_MODEL), lambda i, g: (_peer_mod_row(i), 0, 0)),
                  pl.BlockSpec((1, D_MODEL), lambda i, g: (0, 0)),
                  pl.BlockSpec((1, D_MODEL), lambda i, g: (0, 0))],
        out_specs=pl.BlockSpec((t, D_MODEL), lambda i, g: (i, 0)),
        out_shape=jax.ShapeDtypeStruct((n_tok, D_MODEL), F32),
        scratch_shapes=[pltpu.VMEM((D_MODEL, t), F32), pltpu.VMEM((EXPERT_BLOCK, t), F32),
                        pltpu.VMEM((EXPERT_BLOCK, t), BF16)],
        compiler_params=_cparams("parallel", "arbitrary"),
        name="peer_dense",
    )(h2t, u, v, r2, e2, n1, e1, x1, mod, g, b)


def _cast_kernel(x_ref, o_ref):
    o_ref[...] = x_ref[...].astype(o_ref.dtype)


def _table_bf16(table, l):
    n_exp = table.shape[1]
    return pl.pallas_call(
        _cast_kernel,
        grid=(n_exp // CAST_ROWS,),
        in_specs=[pl.BlockSpec((None, CAST_ROWS, D_MODEL), lambda i: (l, i, 0))],
        out_specs=pl.BlockSpec((CAST_ROWS, D_MODEL), lambda i: (i, 0)),
        out_shape=jax.ShapeDtypeStruct((n_exp, D_MODEL), BF16),
        compiler_params=_cparams("parallel"),
        name="expert_table_bf16",
    )(table)


def _rope_tables():
    t = jnp.arange(DEC_SEQ)
    row = (t // GRID_W).astype(F32)
    col = (t % GRID_W).astype(F32)

    def angles(rot_dim):
        n_freq = rot_dim // 4
        inv_freq = ROPE_THETA ** (-jnp.arange(n_freq, dtype=F32) / n_freq)
        return jnp.concatenate([row[:, None] * inv_freq, col[:, None] * inv_freq], axis=-1)

    def pack(cos_l, sa_l, sb_l):
        return jnp.stack([cos_l, sa_l, sb_l])

    ang_b = angles(MLA_ROPE)
    cb, sb = jnp.cos(ang_b), jnp.sin(ang_b)
    one, zero = jnp.ones((DEC_SEQ, 64), F32), jnp.zeros((DEC_SEQ, 64), F32)
    z16, z32 = jnp.zeros((DEC_SEQ, 16), F32), jnp.zeros((DEC_SEQ, 32), F32)
    rope_b = pack(jnp.concatenate([one, cb, cb, jnp.ones((DEC_SEQ, 32), F32)], axis=1),
                  jnp.concatenate([zero, -sb, z16, z32], axis=1),
                  jnp.concatenate([zero, z16, sb, z32], axis=1))
    ang_d = angles(DIFF_QK_DIM)
    cd, sd = jnp.cos(ang_d), jnp.sin(ang_d)
    rope_d = pack(jnp.concatenate([cd, cd, cd, cd], axis=1),
                  jnp.concatenate([-sd, z32, -sd, z32], axis=1),
                  jnp.concatenate([z32, sd, z32, sd], axis=1))
    ident = pack(jnp.ones((DEC_SEQ, LANES), F32), jnp.zeros((DEC_SEQ, LANES), F32), jnp.zeros((DEC_SEQ, LANES), F32))
    return rope_b, rope_d, ident


def _na_bias_table(rpb):
    col = jnp.arange(GRID_W)
    dc = jnp.clip(col[None, :] - col[:, None], -(NA_WIN_COLS - 1), NA_WIN_COLS - 1) + NA_WIN_COLS - 1
    rpb_cols = rpb[:, :, dc]
    tabs = [rpb_cols[:, off:off + NA_WIN_ROWS].transpose(0, 2, 1, 3).reshape(NA_HEADS, GRID_W, NA_WIN_ROWS * GRID_W)
            for off in range(NA_WIN_ROWS)]
    return jnp.stack(tabs)


def _pad_cols(w, left, right):
    return jnp.pad(w, ((0, 0), (left, right)))


def kernel(x_prompt, x_sample, cache_mla_ckv, cache_mla_krope, cache_na_k, cache_na_v, cache_diff_k, cache_diff_v, c, c_ctx, w_mod, b_mod, w_in, sgu_norm_g, sgu_w, sgu_b, mla_q_norm_g, mla_w_uq, mla_kv_norm_g, mla_w_ukv, na_rpb, diff_lambda_q1, diff_lambda_k1, diff_lambda_q2, diff_lambda_k2, diff_norm_g, w_branch_a, w_branch_b, w_branch_c, w_branch_d, w_gate, b_gate, w_out, ln1_g, ln1_b, peer_w_q, peer_subkeys, peer_u, peer_v, ln2_g, ln2_b):
    x_ctx = x_prompt.reshape(N_CTX, D_MODEL)
    x_lat = x_sample.reshape(N_LAT, D_MODEL)
    cond = jnp.concatenate([c_ctx[None], c, jnp.zeros((N_COND - 1 - DEC_BATCH, D_MODEL), F32)], axis=0)
    mod_all = _mod_call(cond, w_mod, b_mod)
    rope_b, rope_d, rope_id = _rope_tables()
    cache_kr_pad = jnp.pad(cache_mla_krope, ((0, 0), (0, 0), (0, 0), (MLA_NOPE, LANES - MLA_NOPE - MLA_ROPE)))
    cache_na_k2 = cache_na_k.reshape(DEC_BATCH, DEPTH, PAST_LEN, 256)
    cache_na_v2 = cache_na_v.reshape(DEC_BATCH, DEPTH, PAST_LEN, 256)
    cache_diff_k2 = cache_diff_k.reshape(DEC_BATCH, DEPTH, PAST_LEN, 512)
    cache_diff_v2 = cache_diff_v.reshape(DEC_BATCH, DEPTH, PAST_LEN, 512)

    ctx_out = []
    for l in range(DEPTH):
        lambda_init = 0.8 - 0.6 * math.exp(-0.3 * l)
        mod_ctx = jnp.broadcast_to(mod_all[l, 0], (N_CTX // DEC_SEQ, 1, 6 * D_MODEL))
        mod_lat = mod_all[l, 1:1 + DEC_BATCH].reshape(DEC_BATCH, 1, 6 * D_MODEL)

        wi = w_in[l]
        kr_cols = _pad_cols(wi[:, C_KR:C_KR + MLA_ROPE], MLA_NOPE, LANES - MLA_NOPE - MLA_ROPE)
        w_in_r = jnp.concatenate([wi[:, :C_KR], kr_cols, wi[:, C_KR + MLA_ROPE:]], axis=1).astype(BF16)
        wuq = mla_w_uq[l].reshape(MLA_Q_LORA, MLA_HEADS, MLA_NOPE + MLA_ROPE)
        wuq = jnp.pad(wuq, ((0, 0), (0, 0), (0, LANES - MLA_NOPE - MLA_ROPE))).reshape(MLA_Q_LORA, -1).astype(BF16)
        wukv = mla_w_ukv[l].reshape(MLA_KV_LORA, MLA_HEADS, MLA_NOPE + MLA_V)
        wuk = jnp.pad(wukv[:, :, :MLA_NOPE], ((0, 0), (0, 0), (0, LANES - MLA_NOPE))).reshape(MLA_KV_LORA, -1)
        wuk = wuk.astype(BF16)
        wuv = wukv[:, :, MLA_NOPE:].reshape(MLA_KV_LORA, -1).astype(BF16)
        sgu_bias = jnp.repeat(sgu_b[l].T, SGU_WIDTH // SGU_GROUPS, axis=1)
        lams = [p[l].reshape(1, DIFF_QK_DIM) for p in (diff_lambda_q1, diff_lambda_k1, diff_lambda_q2, diff_lambda_k2)]
        dg = diff_norm_g[l].reshape(1, DIFF_V_DIM)

        inproj_weights = (w_in_r, sgu_norm_g[l].reshape(1, -1), sgu_w[l].astype(BF16), sgu_bias,
                          mla_q_norm_g[l].reshape(1, -1), mla_kv_norm_g[l].reshape(1, -1), wuq)
        merge_weights = (w_gate[l].astype(BF16), b_gate[l].reshape(1, -1),
                         w_branch_a[l].astype(BF16), w_branch_b[l].astype(BF16), w_branch_c[l].astype(BF16),
                         w_branch_d[l].astype(BF16), w_out[l].astype(BF16), ln1_g[l].reshape(1, -1),
                         ln1_b[l].reshape(1, -1))
        keys = peer_subkeys[l].reshape(2 * PEER_HEADS, PEER_N_KEYS, PEER_KEY_DIM // 2).astype(BF16)
        wqt = peer_w_q[l].T.astype(BF16)
        u_bf, v_bf = _table_bf16(peer_u, l), _table_bf16(peer_v, l)
        ln2 = (ln2_g[l].reshape(1, -1), ln2_b[l].reshape(1, -1))

        def channel_mix(x, mod, oa, ob, oc, od):
            x1, h2t = _merge_call(x, mod, oa, ob, oc, od, *merge_weights)
            r2, e2, n1, e1 = _router_call(h2t, wqt, keys)
            return _peer_call(h2t, u_bf, v_bf, r2, e2, n1, e1, x1, mod, *ln2)

        oa, mq, ckv, kr, nq, nk, nv, dq, dk, dv = _inproj_call(x_ctx, mod_ctx, rope_id, rope_id, *inproj_weights)
        ob, oc, od = _ctx_attn_call(lambda_init, (mq, ckv, kr, nq, nk, nv, dq, dk, dv), wuk, wuv, lams, dg)
        x_ctx = channel_mix(x_ctx, mod_ctx, oa, ob, oc, od)
        ctx_out.append((ckv.reshape(BATCH, SEQ, MLA_KV_LORA),
                        kr[:, MLA_NOPE:MLA_NOPE + MLA_ROPE].reshape(BATCH, SEQ, MLA_ROPE),
                        nk.reshape(BATCH, SEQ, NA_HEADS, NA_HEAD_DIM),
                        nv.reshape(BATCH, SEQ, NA_HEADS, NA_HEAD_DIM),
                        dk.reshape(BATCH, SEQ, DIFF_HEADS, 2 * DIFF_QK_DIM),
                        dv.reshape(BATCH, SEQ, DIFF_HEADS, DIFF_V_DIM)))

        oa, mq, ckv, kr, nq, nk, nv, dq, dk, dv = _inproj_call(x_lat, mod_lat, rope_b, rope_d, *inproj_weights)
        ob = _lat_mla_call(l, mq, ckv, kr, cache_mla_ckv, cache_kr_pad, wuk, wuv)
        oc = _lat_na_call(l, nq, nk, nv, cache_na_k2, cache_na_v2, _na_bias_table(na_rpb[l]))
        od = _lat_diff_call(l, lambda_init, dq, dk, dv, cache_diff_k2, cache_diff_v2, lams, dg)
        x_lat = channel_mix(x_lat, mod_lat, oa, ob, oc, od)

    y_prompt = x_ctx.reshape(BATCH, SEQ, D_MODEL)
    y_sample = x_lat.reshape(DEC_BATCH, DEC_SEQ, D_MODEL)
    new = [jnp.stack([t[k] for t in ctx_out], axis=1) for k in range(6)]
    return (y_prompt, y_sample, *new)
```

```python
import functools
import math

import jax
import jax.numpy as jnp
from jax import lax
from jax.experimental import pallas as pl
from jax.experimental.pallas import tpu as pltpu

F32 = jnp.float32
BF16 = jnp.bfloat16

D_MODEL = 1024
BATCH = 32
SEQ = 256
DEPTH = 2
DEC_BATCH = 8
DEC_SEQ = 1024
PAST_LEN = 512
GRID_W = 64
CHUNK = 128
SGU_GROUPS = 4
SGU_WIDTH = 256
MLA_HEADS = 4
MLA_Q_LORA = 256
MLA_KV_LORA = 128
MLA_NOPE = 64
MLA_ROPE = 32
MLA_V = 64
NA_HEADS = 4
NA_HEAD_DIM = 64
NA_WIN_ROWS = 8
NA_WIN_COLS = 16
DIFF_HEADS = 4
DIFF_QK_DIM = 64
DIFF_V_DIM = 128
PEER_HEADS = 8
PEER_N_KEYS = 128
PEER_KEY_DIM = 256
PEER_TOPK = 16
ROPE_THETA = 10000.0
LN_EPS = 1e-6
NEG_BIG = -1e30
DEEPNORM_ALPHA = (2 * DEPTH) ** 0.25

LANES = 128
SUBLANES = 8
N_CTX = BATCH * SEQ
N_LAT = DEC_BATCH * DEC_SEQ
N_COND = 16
TM = 512
ROWS = DEC_SEQ // GRID_W
Q_TILE = 256
NA_ROWS_PER_STEP = 4
ROUTER_TILE = 512
PEER_TILE = 1024
EXPERT_BLOCK = 1024
KEYS_PER_BLOCK = EXPERT_BLOCK // PEER_N_KEYS
GATE_LANES = 256
CAST_ROWS = 2048
VMEM_LIMIT = 56 * 1024 * 1024

C_AU, C_AV, C_CQ, C_CKV, C_KR = 0, 256, 512, 768, 896
C_NQ, C_NK, C_NV, C_DQ, C_DK, C_DV, C_END = 1024, 1280, 1536, 1792, 2304, 2816, 3328


def _ln(x):
    mu = jnp.mean(x, axis=-1, keepdims=True)
    xc = x - mu
    var = jnp.mean(xc * xc, axis=-1, keepdims=True)
    return xc * lax.rsqrt(var + LN_EPS)


def _rms(x):
    return x * lax.rsqrt(jnp.mean(x * x, axis=-1, keepdims=True) + LN_EPS)


def _gelu(x):
    return 0.5 * x * (1.0 + lax.erf(x * (1.0 / math.sqrt(2.0))))


def _dot(a, b):
    return jnp.dot(a, b, preferred_element_type=F32)


def _dot_nt(a, b):
    return lax.dot_general(a, b, (((1,), (1,)), ((), ())), preferred_element_type=F32)


def _rope(x, tab_ref, half):
    return (x * tab_ref[0] + pltpu.roll(x, LANES - half, 1) * tab_ref[1] + pltpu.roll(x, half, 1) * tab_ref[2])


def _cparams(*sem):
    return pltpu.CompilerParams(dimension_semantics=sem, vmem_limit_bytes=VMEM_LIMIT)


def _mod_kernel(cond_ref, w_ref, b_ref, o_ref):
    c = cond_ref[...]
    s = c * jax.nn.sigmoid(c)
    o_ref[...] = _dot(s, w_ref[...]) + b_ref[...]


def _mod_call(cond, w_mod, b_mod):
    nb = 1536
    return pl.pallas_call(
        _mod_kernel,
        grid=(DEPTH, 6 * D_MODEL // nb),
        in_specs=[pl.BlockSpec((N_COND, D_MODEL), lambda l, j: (0, 0)),
                  pl.BlockSpec((None, D_MODEL, nb), lambda l, j: (l, 0, j)),
                  pl.BlockSpec((None, 1, nb), lambda l, j: (l, 0, j))],
        out_specs=pl.BlockSpec((None, N_COND, nb), lambda l, j: (l, 0, j)),
        out_shape=jax.ShapeDtypeStruct((DEPTH, N_COND, 6 * D_MODEL), F32),
        compiler_params=_cparams("arbitrary", "arbitrary"),
        name="mod_vectors",
    )(cond, w_mod, b_mod.reshape(DEPTH, 1, 6 * D_MODEL))


def _mod_row(i):
    return i // (DEC_SEQ // TM)


def _pos_block(i):
    return i % (DEC_SEQ // TM)


def _inproj_kernel(x_ref, mod_ref, rb_ref, rd_ref, w_in_ref, sgu_g_ref, sgu_w_ref, sgu_bias_ref,
                   qg_ref, kvg_ref, wuq_ref,
                   oa_ref, mq_ref, ckv_ref, kr_ref, nq_ref, nk_ref, nv_ref, dq_ref, dk_ref, dv_ref):
    x = x_ref[...]
    shift = mod_ref[0, :, 0:D_MODEL]
    scale = mod_ref[0, :, D_MODEL:2 * D_MODEL]
    h = (_ln(x) * (1.0 + scale) + shift).astype(BF16)

    ya = _dot(h, w_in_ref[:, C_AU:C_CQ])
    u = _gelu(ya[:, :SGU_WIDTH])
    v = _gelu(ya[:, SGU_WIDTH:])
    vn = (_ln(v) * sgu_g_ref[...]).astype(BF16)
    group = lax.broadcasted_iota(jnp.int32, (CHUNK, SGU_WIDTH), 1) // (SGU_WIDTH // SGU_GROUPS)
    for c in range(TM // CHUNK):
        rows = slice(c * CHUNK, (c + 1) * CHUNK)
        mixed = sgu_bias_ref[...]
        for g in range(SGU_GROUPS):
            mixed = mixed + jnp.where(group == g, _dot(sgu_w_ref[g], vn[rows]), 0.0)
        oa_ref[rows, :] = (u[rows] * mixed).astype(oa_ref.dtype)

    ym = _dot(h, w_in_ref[:, C_CQ:C_NQ])
    cq = (_rms(ym[:, :MLA_Q_LORA]) * qg_ref[...]).astype(BF16)
    mq = _dot(cq, wuq_ref[...])
    for g in range(MLA_HEADS):
        lanes = slice(g * LANES, (g + 1) * LANES)
        mq_ref[:, lanes] = _rope(mq[:, lanes], rb_ref, MLA_ROPE // 2)
    ckv_ref[...] = _rms(ym[:, MLA_Q_LORA:MLA_Q_LORA + MLA_KV_LORA]) * kvg_ref[...]
    kr_ref[...] = _rope(ym[:, MLA_Q_LORA + MLA_KV_LORA:], rb_ref, MLA_ROPE // 2)

    yn = _dot(h, w_in_ref[:, C_NQ:C_DQ])
    nq_ref[...] = yn[:, 0:256]
    nk_ref[...] = yn[:, 256:512]
    nv_ref[...] = yn[:, 512:768]

    yd = _dot(h, w_in_ref[:, C_DQ:C_END])
    for g in range(4):
        lanes = slice(g * LANES, (g + 1) * LANES)
        dq_ref[:, lanes] = _rope(yd[:, g * LANES:(g + 1) * LANES], rd_ref, DIFF_QK_DIM // 2)
        dk_ref[:, lanes] = _rope(yd[:, 512 + g * LANES:512 + (g + 1) * LANES], rd_ref, DIFF_QK_DIM // 2)
    dv_ref[...] = yd[:, 1024:1536]


def _inproj_call(x, mod, rope_b, rope_d, w_in_r, sgu_g, sgu_w, sgu_bias, qg, kvg, wuq):
    tile = lambda w: pl.BlockSpec((TM, w), lambda i: (i, 0))
    full = lambda *s: pl.BlockSpec(s, lambda i: (0,) * len(s))
    widths = (SGU_WIDTH, 512, MLA_KV_LORA, LANES, 256, 256, 256, 512, 512, 512)
    dtypes = (BF16,) + (F32,) * 9
    n_tok = x.shape[0]
    return pl.pallas_call(
        _inproj_kernel,
        grid=(n_tok // TM,),
        in_specs=[tile(D_MODEL),
                  pl.BlockSpec((1, 1, 6 * D_MODEL), lambda i: (_mod_row(i), 0, 0)),
                  pl.BlockSpec((3, TM, LANES), lambda i: (0, _pos_block(i), 0)),
                  pl.BlockSpec((3, TM, LANES), lambda i: (0, _pos_block(i), 0)),
                  full(D_MODEL, C_END), full(1, SGU_WIDTH), full(SGU_GROUPS, CHUNK, CHUNK),
                  full(CHUNK, SGU_WIDTH), full(1, MLA_Q_LORA), full(1, MLA_KV_LORA),
                  full(MLA_Q_LORA, MLA_HEADS * LANES)],
        out_specs=[tile(w) for w in widths],
        out_shape=[jax.ShapeDtypeStruct((n_tok, w), dt) for w, dt in zip(widths, dtypes)],
        compiler_params=_cparams("parallel"),
        name="in_projection",
    )(x, mod, rope_b, rope_d, w_in_r, sgu_g, sgu_w, sgu_bias, qg, kvg, wuq)


def _half_mask(lo):
    lane = lax.broadcasted_iota(jnp.int32, (1, LANES), 1)
    return (lane >= lo) & (lane < lo + 64)


def _softmax_pv(scores, values, lanes):
    m = scores[0].max(axis=-1, keepdims=True)
    for s in scores[1:]:
        m = jnp.maximum(m, s.max(axis=-1, keepdims=True))
    den = None
    o = None
    for s, v in zip(scores, values):
        p = jnp.exp(s - m)
        d = p.sum(axis=-1, keepdims=True)
        den = d if den is None else den + d
        pv = _dot(p.astype(BF16), v[:, lanes])
        o = pv if o is None else o + pv
    return o / den


def _pair_attention(q, keys, vals, scale, bias_fn=None):
    assert math.frexp(scale)[0] == 0.5
    n = q.shape[0]
    outs = []
    for pair in range(2):
        lanes = slice(pair * LANES, (pair + 1) * LANES)
        qp = q[:, lanes] * jnp.asarray(scale, BF16)
        masks = [_half_mask(64 * sub) for sub in range(2)]
        qs = jnp.concatenate([jnp.where(m, qp, jnp.zeros_like(qp)) for m in masks], axis=0)
        scores = [_dot_nt(qs, k[:, lanes]) for k in keys]
        if bias_fn is not None:
            scores = bias_fn(pair, scores)
        o = _softmax_pv(scores, vals, lanes)
        outs.append(jnp.where(masks[0], o[:n], 0.0) + jnp.where(masks[1], o[n:], 0.0))
    return outs


def _mla_attention(q, k_blocks, v_blocks, o_ref, rows):
    scale = (MLA_NOPE + MLA_ROPE) ** -0.5
    for pair in range(2):
        lanes = slice(pair * LANES, (pair + 1) * LANES)
        acc = None
        for sub in range(2):
            head = 2 * pair + sub
            hl = slice(head * LANES, (head + 1) * LANES)
            scores = [_dot_nt(q[:, hl], k[:, hl]) * scale for k in k_blocks]
            o = jnp.where(_half_mask(64 * sub), _softmax_pv(scores, v_blocks, lanes), 0.0)
            acc = o if acc is None else acc + o
        o_ref[rows, lanes] = acc.astype(o_ref.dtype)


def _diff_lambda(lq1, lk1, lq2, lk2, lambda_init):
    a = jnp.sum(lq1[...] * lk1[...], axis=-1, keepdims=True)
    b = jnp.sum(lq2[...] * lk2[...], axis=-1, keepdims=True)
    return jnp.exp(a) - jnp.exp(b) + lambda_init


def _diff_attention(q, k_blocks, v_blocks, lam, norm_g, lambda_init, o_ref, rows):
    scale = DIFF_QK_DIM ** -0.5
    assert math.frexp(scale)[0] == 0.5
    for head in range(DIFF_HEADS):
        hl = slice(head * LANES, (head + 1) * LANES)
        qh = q[:, hl] * jnp.asarray(scale, BF16)
        probs = []
        for sub in range(2):
            qm = jnp.where(_half_mask(64 * sub), qh, jnp.zeros_like(qh))
            scores = [_dot_nt(qm, k[:, hl]) for k in k_blocks]
            m = scores[0].max(axis=-1, keepdims=True)
            for s in scores[1:]:
                m = jnp.maximum(m, s.max(axis=-1, keepdims=True))
            ps = [jnp.exp(s - m) for s in scores]
            den = ps[0].sum(axis=-1, keepdims=True)
            for p in ps[1:]:
                den = den + p.sum(axis=-1, keepdims=True)
            probs.append((ps, 1.0 / den))
        o = None
        for i, v in enumerate(v_blocks):
            w = probs[0][0][i] * probs[0][1] - probs[1][0][i] * (lam * probs[1][1])
            pv = _dot(w.astype(BF16), v[:, hl])
            o = pv if o is None else o + pv
        o = _rms(o) * norm_g * (1.0 - lambda_init)
        o_ref[rows, hl] = o.astype(o_ref.dtype)


def _ctx_attn_kernel(lambda_init, mq_ref, ckv_ref, kr_ref, nq_ref, nk_ref, nv_ref, dq_ref, dk_ref, dv_ref,
                     wuk_ref, wuv_ref, lq1, lk1, lq2, lk2, dg_ref, ob_ref, oc_ref, od_ref):
    rows = slice(0, SEQ)
    ckv = ckv_ref[...].astype(BF16)
    kr = kr_ref[...]
    k_b = (_dot(ckv, wuk_ref[...]) + jnp.concatenate([kr] * MLA_HEADS, axis=1)).astype(BF16)
    v_b = _dot(ckv, wuv_ref[...]).astype(BF16)
    _mla_attention(mq_ref[...].astype(BF16), [k_b], [v_b], ob_ref, rows)

    outs = _pair_attention(nq_ref[...].astype(BF16), [nk_ref[...].astype(BF16)], [nv_ref[...].astype(BF16)],
                           NA_HEAD_DIM ** -0.5)
    for pair in range(2):
        oc_ref[:, pair * LANES:(pair + 1) * LANES] = outs[pair].astype(oc_ref.dtype)

    lam = _diff_lambda(lq1, lk1, lq2, lk2, lambda_init)
    _diff_attention(dq_ref[...].astype(BF16), [dk_ref[...].astype(BF16)], [dv_ref[...].astype(BF16)],
                    lam, dg_ref[...], lambda_init, od_ref, rows)


def _ctx_attn_call(lambda_init, acts, wuk, wuv, lams, dg):
    mq, ckv, kr, nq, nk, nv, dq, dk, dv = acts
    seq = lambda w: pl.BlockSpec((SEQ, w), lambda b: (b, 0))
    full = lambda *s: pl.BlockSpec(s, lambda b: (0,) * len(s))
    return pl.pallas_call(
        functools.partial(_ctx_attn_kernel, lambda_init),
        grid=(BATCH,),
        in_specs=[seq(512), seq(128), seq(128), seq(256), seq(256), seq(256), seq(512), seq(512), seq(512),
                  full(MLA_KV_LORA, 512), full(MLA_KV_LORA, 256)] + [full(1, DIFF_QK_DIM)] * 4
                 + [full(1, DIFF_V_DIM)],
        out_specs=[seq(256), seq(256), seq(512)],
        out_shape=[jax.ShapeDtypeStruct((N_CTX, w), BF16) for w in (256, 256, 512)],
        compiler_params=_cparams("parallel"),
        name="context_attention",
    )(mq, ckv, kr, nq, nk, nv, dq, dk, dv, wuk, wuv, *lams, dg)


def _lat_mla_kernel(mq_ref, ckv_ref, kr_ref, cckv_ref, ckr_ref, wuk_ref, wuv_ref, o_ref):
    def expand(ckv_f32, kr):
        ckv = ckv_f32.astype(BF16)
        k = (_dot(ckv, wuk_ref[...]) + jnp.concatenate([kr] * MLA_HEADS, axis=1)).astype(BF16)
        return k, _dot(ckv, wuv_ref[...]).astype(BF16)

    k_lat, v_lat = expand(ckv_ref[...], kr_ref[...])
    k_ctx, v_ctx = expand(cckv_ref[...], ckr_ref[...])
    k_all = jnp.concatenate([k_lat, k_ctx], axis=0)
    v_all = jnp.concatenate([v_lat, v_ctx], axis=0)
    for t in range(DEC_SEQ // Q_TILE):
        rows = slice(t * Q_TILE, (t + 1) * Q_TILE)
        _mla_attention(mq_ref[rows, :].astype(BF16), [k_all], [v_all], o_ref, rows)


def _lat_mla_call(l, mq, ckv, kr, cache_ckv, cache_kr_pad, wuk, wuv):
    seq = lambda w: pl.BlockSpec((DEC_SEQ, w), lambda b: (b, 0))
    cache = lambda w: pl.BlockSpec((None, None, PAST_LEN, w), lambda b: (b, l, 0, 0))
    full = lambda *s: pl.BlockSpec(s, lambda b: (0,) * len(s))
    return pl.pallas_call(
        _lat_mla_kernel,
        grid=(DEC_BATCH,),
        in_specs=[seq(512), seq(128), seq(128), cache(MLA_KV_LORA), cache(LANES),
                  full(MLA_KV_LORA, 512), full(MLA_KV_LORA, 256)],
        out_specs=pl.BlockSpec((DEC_SEQ, 256), lambda b: (b, 0)),
        out_shape=jax.ShapeDtypeStruct((N_LAT, 256), BF16),
        compiler_params=_cparams("parallel"),
        name="latent_mla_attention",
    )(mq, ckv, kr, cache_ckv, cache_kr_pad, wuk, wuv)


def _win_start(r):
    return jnp.clip(r - NA_WIN_ROWS // 2, 0, ROWS - NA_WIN_ROWS)


def _lat_na_kernel(nq_ref, nk_ref, nv_ref, ck_ref, cv_ref, bias_ref, o_ref):
    win = NA_WIN_ROWS * GRID_W
    k_c = ck_ref[...].astype(BF16)
    v_c = cv_ref[...].astype(BF16)
    q_col = lax.broadcasted_iota(jnp.int32, (2 * GRID_W, win), 0) % GRID_W
    k_col = lax.broadcasted_iota(jnp.int32, (2 * GRID_W, win), 1) % GRID_W
    c0 = jnp.clip(q_col - NA_WIN_COLS // 2, 0, GRID_W - NA_WIN_COLS)
    col_in = (k_col >= c0) & (k_col < c0 + NA_WIN_COLS)

    for rr in range(NA_ROWS_PER_STEP):
        r = pl.program_id(1) * NA_ROWS_PER_STEP + rr
        first = _win_start(r)
        start = pl.multiple_of(first * GRID_W, GRID_W)
        off = first - r + NA_WIN_ROWS - 1
        k_all = jnp.concatenate([nk_ref[pl.ds(start, win), :].astype(BF16), k_c], axis=0)
        v_all = jnp.concatenate([nv_ref[pl.ds(start, win), :].astype(BF16), v_c], axis=0)

        def bias_fn(pair, scores, off=off):
            s = scores[0]
            bias = jnp.concatenate([bias_ref[off, 2 * pair], bias_ref[off, 2 * pair + 1]], axis=0)
            s_win = jnp.where(col_in, s[:, :win] + bias, NEG_BIG)
            return [jnp.concatenate([s_win, s[:, win:]], axis=1)]

        rows = slice(rr * GRID_W, (rr + 1) * GRID_W)
        outs = _pair_attention(nq_ref[rows, :].astype(BF16), [k_all], [v_all], NA_HEAD_DIM ** -0.5, bias_fn)
        for pair in range(2):
            o_ref[rows, pair * LANES:(pair + 1) * LANES] = outs[pair].astype(o_ref.dtype)


def _lat_na_call(l, nq, nk, nv, cache_k, cache_v, bias_tab):
    seq = pl.BlockSpec((DEC_SEQ, 256), lambda b, r: (b, 0))
    cache = pl.BlockSpec((None, None, PAST_LEN, 256), lambda b, r: (b, l, 0, 0))
    steps = ROWS // NA_ROWS_PER_STEP
    q_rows = NA_ROWS_PER_STEP * GRID_W
    return pl.pallas_call(
        _lat_na_kernel,
        grid=(DEC_BATCH, steps),
        in_specs=[pl.BlockSpec((q_rows, 256), lambda b, r: (b * steps + r, 0)),
                  seq, seq, cache, cache,
                  pl.BlockSpec((NA_WIN_ROWS, NA_HEADS, GRID_W, NA_WIN_ROWS * GRID_W), lambda b, r: (0, 0, 0, 0))],
        out_specs=pl.BlockSpec((q_rows, 256), lambda b, r: (b * steps + r, 0)),
        out_shape=jax.ShapeDtypeStruct((N_LAT, 256), BF16),
        compiler_params=_cparams("parallel", "arbitrary"),
        name="latent_neighbourhood_attention",
    )(nq, nk, nv, cache_k, cache_v, bias_tab)


def _lat_diff_kernel(lambda_init, dq_ref, dk_ref, dv_ref, ck_ref, cv_ref, lq1, lk1, lq2, lk2, dg_ref, o_ref):
    lam = _diff_lambda(lq1, lk1, lq2, lk2, lambda_init)
    k_blocks = [dk_ref[...].astype(BF16), ck_ref[...].astype(BF16)]
    v_blocks = [dv_ref[...].astype(BF16), cv_ref[...].astype(BF16)]
    for t in range(DEC_SEQ // Q_TILE):
        rows = slice(t * Q_TILE, (t + 1) * Q_TILE)
        _diff_attention(dq_ref[rows, :].astype(BF16), k_blocks, v_blocks, lam, dg_ref[...], lambda_init,
                        o_ref, rows)


def _lat_diff_call(l, lambda_init, dq, dk, dv, cache_k, cache_v, lams, dg):
    seq = pl.BlockSpec((DEC_SEQ, 512), lambda b: (b, 0))
    cache = pl.BlockSpec((None, None, PAST_LEN, 512), lambda b: (b, l, 0, 0))
    full = lambda *s: pl.BlockSpec(s, lambda b: (0,) * len(s))
    return pl.pallas_call(
        functools.partial(_lat_diff_kernel, lambda_init),
        grid=(DEC_BATCH,),
        in_specs=[seq, seq, seq, cache, cache] + [full(1, DIFF_QK_DIM)] * 4 + [full(1, DIFF_V_DIM)],
        out_specs=pl.BlockSpec((DEC_SEQ, 512), lambda b: (b, 0)),
        out_shape=jax.ShapeDtypeStruct((N_LAT, 512), BF16),
        compiler_params=_cparams("parallel"),
        name="latent_differential_attention",
    )(dq, dk, dv, cache_k, cache_v, *lams, dg)


def _merge_kernel(x_ref, mod_ref, oa_ref, ob_ref, oc_ref, od_ref, wg_ref, bg_ref,
                  wa_ref, wb_ref, wc_ref, wd_ref, wo_ref, g_ref, b_ref, x1_ref, h2t_ref):
    x = x_ref[...]
    mod = lambda k: mod_ref[0, :, k * D_MODEL:(k + 1) * D_MODEL]
    h = (_ln(x) * (1.0 + mod(1)) + mod(0)).astype(BF16)
    merged = None
    for i, (o_ref, w_ref) in enumerate(((oa_ref, wa_ref), (ob_ref, wb_ref), (oc_ref, wc_ref), (od_ref, wd_ref))):
        cols = slice(i * D_MODEL, (i + 1) * D_MODEL)
        gate = jax.nn.sigmoid(_dot(h, wg_ref[:, cols]) + bg_ref[:, cols])
        term = gate * _dot(o_ref[...], w_ref[...])
        merged = term if merged is None else merged + term
    mix = _dot(merged.astype(BF16), wo_ref[...])
    x1 = _ln(DEEPNORM_ALPHA * x + mod(2) * mix) * g_ref[...] + b_ref[...]
    x1_ref[...] = x1
    h2 = _ln(x1) * (1.0 + mod(4)) + mod(3)
    h2t_ref[...] = h2.T.astype(BF16)


def _merge_call(x, mod, oa, ob, oc, od, wg, bg, wa, wb, wc, wd, wo, g, b):
    tile = lambda w: pl.BlockSpec((TM, w), lambda i: (i, 0))
    full = lambda *s: pl.BlockSpec(s, lambda i: (0,) * len(s))
    n_tok = x.shape[0]
    return pl.pallas_call(
        _merge_kernel,
        grid=(n_tok // TM,),
        in_specs=[tile(D_MODEL), pl.BlockSpec((1, 1, 6 * D_MODEL), lambda i: (_mod_row(i), 0, 0)),
                  tile(256), tile(256), tile(256), tile(512),
                  full(D_MODEL, 4 * D_MODEL), full(1, 4 * D_MODEL),
                  full(256, D_MODEL), full(256, D_MODEL), full(256, D_MODEL), full(512, D_MODEL),
                  full(D_MODEL, D_MODEL), full(1, D_MODEL), full(1, D_MODEL)],
        out_specs=[tile(D_MODEL), pl.BlockSpec((D_MODEL, TM), lambda i: (0, i))],
        out_shape=[jax.ShapeDtypeStruct((n_tok, D_MODEL), F32), jax.ShapeDtypeStruct((D_MODEL, n_tok), BF16)],
        compiler_params=_cparams("parallel"),
        name="branch_merge",
    )(x, mod, oa, ob, oc, od, wg, bg, wa, wb, wc, wd, wo, g, b)


KEY_MIN = -2 ** 31


def _tree_sum(terms):
    while len(terms) > 1:
        terms = [a + b for a, b in zip(terms[0::2], terms[1::2])] + ([terms[-1]] if len(terms) % 2 else [])
    return terms[0]


def _row_gather(table, idx):
    ii = idx.astype(jnp.int32)
    low = ii & (SUBLANES - 1)
    outs = []
    for c in range(idx.shape[0] // SUBLANES):
        rows = slice(c * SUBLANES, (c + 1) * SUBLANES)
        lo = jnp.take_along_axis(table[0:SUBLANES], low[rows], axis=0)
        hi = jnp.take_along_axis(table[SUBLANES:PEER_TOPK], low[rows], axis=0)
        outs.append(jnp.where(ii[rows] < SUBLANES, lo, jnp.where(ii[rows] < PEER_TOPK, hi, 0.0)))
    return jnp.concatenate(outs, axis=0)


def _sort_key(x):
    b = lax.bitcast_convert_type(x + 0.0, jnp.int32)
    return b ^ ((b >> 31) & 0x7FFFFFFF)


def _key_value(k):
    return lax.bitcast_convert_type(k ^ ((k >> 31) & 0x7FFFFFFF), F32)


def _top16(s):
    row = lax.broadcasted_iota(jnp.int32, s.shape, 0).astype(F32)
    krow = lax.broadcasted_iota(jnp.int32, (PEER_TOPK, s.shape[1]), 0)

    def body(k, carry):
        work, rank, vals = carry
        m = jnp.max(work, axis=0, keepdims=True)
        idx = jnp.min(jnp.where(work == m, row, float(PEER_N_KEYS)), axis=0, keepdims=True)
        sel = row == idx
        rank = jnp.where(sel, jnp.asarray(k, jnp.int32).astype(F32), rank)
        work = jnp.where(sel, -jnp.inf, work)
        vals = jnp.where(krow == k, m, vals)
        return work, rank, vals

    init = (s, jnp.full(s.shape, float(PEER_N_KEYS), F32), jnp.zeros((PEER_TOPK, s.shape[1]), F32))
    _, rank, vals = lax.fori_loop(0, PEER_TOPK, body, init)
    return vals, rank


def _top16_pair(s1, s2):
    krow = lax.broadcasted_iota(jnp.int32, (PEER_TOPK, LANES), 0)

    def body(k, carry):
        w1, w2, v1, v2 = carry
        code = KEY_MIN + jnp.asarray(k, jnp.int32)
        m1 = jnp.max(w1, axis=0, keepdims=True)
        m2 = jnp.max(w2, axis=0, keepdims=True)
        w1 = jnp.where(w1 == m1, code, w1)
        w2 = jnp.where(w2 == m2, code, w2)
        return w1, w2, jnp.where(krow == k, m1, v1), jnp.where(krow == k, m2, v2)

    zeros = jnp.zeros((PEER_TOPK, LANES), jnp.int32)
    w1, w2, v1, v2 = lax.fori_loop(0, PEER_TOPK, body, (_sort_key(s1), _sort_key(s2), zeros, zeros))

    def decode(w):
        taken = w < KEY_MIN + PEER_TOPK
        rank = jnp.where(taken, (w - KEY_MIN).astype(F32), float(PEER_N_KEYS))
        return rank, jnp.sum(taken.astype(F32), axis=0, keepdims=True)

    r1, c1 = decode(w1)
    r2, c2 = decode(w2)
    ties = jnp.max(jnp.maximum(jnp.abs(c1 - PEER_TOPK), jnp.abs(c2 - PEER_TOPK))) > 0.5
    return _key_value(v1), _key_value(v2), r1, r2, ties


def _merge_counts(hs1, hs2):
    krow = lax.broadcasted_iota(jnp.int32, hs1.shape, 0).astype(F32)

    def body(_, carry):
        cnt, front = carry
        m = jnp.max(front, axis=0, keepdims=True)
        win = jnp.min(jnp.where(front == m, krow, float(PEER_TOPK)), axis=0, keepdims=True)
        sel = krow == win
        cnt = jnp.where(sel, cnt + 1.0, cnt)
        nxt = jnp.where(cnt < float(PEER_TOPK), hs1 + _row_gather(hs2, cnt), -jnp.inf)
        return cnt, jnp.where(sel, nxt, front)

    cnt, _ = lax.fori_loop(0, PEER_TOPK, body, (jnp.zeros(hs1.shape, F32), hs1 + hs2[0:1, :]))
    return cnt


def _router_kernel(h2t_ref, wqt_ref, keys_ref, r2_ref, e2_ref, n1_ref, e1_ref, q_scr, s_scr, hs_scr, rank1_scr):
    t = ROUTER_TILE
    q_scr[...] = _dot(wqt_ref[...], h2t_ref[...]).astype(BF16)

    for hd in range(PEER_HEADS):
        for half in range(2):
            rows = slice((2 * hd + half) * LANES, (2 * hd + half + 1) * LANES)
            s_scr[hd, half] = _dot(keys_ref[2 * hd + half], q_scr[rows, :])

    def head_body(hd, _):
        for j in range(t // LANES):
            lanes = slice(j * LANES, (j + 1) * LANES)

            def put(hs1, hs2, rank1, rank2, lanes=lanes):
                hs_scr[hd, 0, :, lanes] = hs1
                hs_scr[hd, 1, :, lanes] = hs2
                rank1_scr[hd, :, lanes] = rank1
                r2_ref[hd, :, lanes] = rank2.astype(BF16)

            *quick, ties = _top16_pair(s_scr[hd, 0, :, lanes], s_scr[hd, 1, :, lanes])
            put(*quick)

            @pl.when(ties)
            def _(lanes=lanes, put=put):
                hs1, rank1 = _top16(s_scr[hd, 0, :, lanes])
                hs2, rank2 = _top16(s_scr[hd, 1, :, lanes])
                put(hs1, hs2, rank1, rank2)
        return 0

    lax.fori_loop(0, PEER_HEADS, head_body, 0)

    for pair in range(PEER_HEADS // 2):
        heads = (2 * pair, 2 * pair + 1)
        hs1 = jnp.concatenate([hs_scr[h, 0] for h in heads], axis=1)
        hs2 = jnp.concatenate([hs_scr[h, 1] for h in heads], axis=1)
        cnt = _merge_counts(hs1, hs2)
        e1r = jnp.exp(hs1 - hs1[0:1, :])
        e2r = jnp.exp(hs2 - hs2[0:1, :])
        prefix = _tree_sum([jnp.where(cnt > float(kb), e2r[kb:kb + 1, :], 0.0) for kb in range(PEER_TOPK)])
        inv_z = 1.0 / jnp.sum(e1r * prefix, axis=0, keepdims=True)
        for i, h in enumerate(heads):
            lanes = slice(i * t, (i + 1) * t)
            e2_ref[h] = (jnp.exp(s_scr[h, 1] - hs2[0:1, lanes]) * inv_z[:, lanes]).astype(BF16)
            e1_ref[h] = 0.5 * jnp.exp(s_scr[h, 0] - hs1[0:1, lanes])
            n1_ref[h] = _row_gather(cnt[:, lanes], rank1_scr[h])


def _router_call(h2t, wqt, keys):
    t = ROUTER_TILE
    out = pl.BlockSpec((PEER_HEADS, PEER_N_KEYS, t), lambda i: (0, 0, i))
    n_tok = h2t.shape[1]
    shape = (PEER_HEADS, PEER_N_KEYS, n_tok)
    return pl.pallas_call(
        _router_kernel,
        grid=(n_tok // t,),
        in_specs=[pl.BlockSpec((D_MODEL, t), lambda i: (0, i)),
                  pl.BlockSpec((PEER_HEADS * PEER_KEY_DIM, D_MODEL), lambda i: (0, 0)),
                  pl.BlockSpec((2 * PEER_HEADS, PEER_N_KEYS, PEER_KEY_DIM // 2), lambda i: (0, 0, 0))],
        out_specs=[out] * 4,
        out_shape=[jax.ShapeDtypeStruct(shape, BF16), jax.ShapeDtypeStruct(shape, BF16),
                   jax.ShapeDtypeStruct(shape, F32), jax.ShapeDtypeStruct(shape, F32)],
        scratch_shapes=[pltpu.VMEM((PEER_HEADS * PEER_KEY_DIM, t), BF16),
                        pltpu.VMEM((PEER_HEADS, 2, PEER_N_KEYS, t), F32),
                        pltpu.VMEM((PEER_HEADS, 2, PEER_TOPK, t), F32),
                        pltpu.VMEM((PEER_HEADS, PEER_N_KEYS, t), F32)],
        compiler_params=_cparams("parallel"),
        name="peer_retrieval",
    )(h2t, wqt, keys)


def _gated_activations(ht_ref, w_ref, r2_ref, e2_ref, n1_ref, e1_ref):
    for i in range(KEYS_PER_BLOCK):
        rows = slice(i * PEER_N_KEYS, (i + 1) * PEER_N_KEYS)
        for j in range(PEER_TILE // GATE_LANES):
            lanes = slice(j * GATE_LANES, (j + 1) * GATE_LANES)
            gate = jnp.zeros((PEER_N_KEYS, GATE_LANES), BF16)
            for hd in range(PEER_HEADS):
                n_row = n1_ref[hd, i:i + 1, lanes].astype(BF16)
                c_row = e1_ref[hd, i:i + 1, lanes].astype(BF16)
                live = jnp.where(r2_ref[hd, :, lanes] < n_row, e2_ref[hd, :, lanes], jnp.zeros((), BF16))
                gate = gate + live * c_row
            x = ht_ref[rows, lanes]
            act = x * (1.0 + lax.erf(x * (1.0 / math.sqrt(2.0))))
            w_ref[rows, lanes] = act.astype(BF16) * gate


def _peer_kernel(h2t_ref, u_ref, v_ref, r2_ref, e2_ref, n1_ref, e1_ref,
                 x1_ref, mod_ref, g_ref, b_ref, o_ref, acc_ref, ht_ref, w_ref):
    e = pl.program_id(1)

    @pl.when(e == 0)
    def _():
        acc_ref[...] = jnp.zeros_like(acc_ref)

    ht_ref[...] = _dot(u_ref[...], h2t_ref[...])
    _gated_activations(ht_ref, w_ref, r2_ref, e2_ref, n1_ref, e1_ref)
    acc_ref[...] += lax.dot_general(v_ref[...], w_ref[...], (((0,), (0,)), ((), ())), preferred_element_type=F32)

    @pl.when(e == pl.num_programs(1) - 1)
    def _():
        ffn = acc_ref[...].T
        g2 = mod_ref[0, :, 5 * D_MODEL:6 * D_MODEL]
        o_ref[...] = _ln(DEEPNORM_ALPHA * x1_ref[...] + g2 * ffn) * g_ref[...] + b_ref[...]


def _peer_mod_row(i):
    return i // (DEC_SEQ // PEER_TILE)


def _peer_call(h2t, u, v, r2, e2, n1, e1, x1, mod, g, b):
    t = PEER_TILE
    n_tok = x1.shape[0]
    n_blocks = PEER_N_KEYS * PEER_N_KEYS // EXPERT_BLOCK
    gates = pl.BlockSpec((PEER_HEADS, PEER_N_KEYS, t), lambda i, g: (0, 0, i))
    keys = pl.BlockSpec((PEER_HEADS, KEYS_PER_BLOCK, t), lambda i, g: (0, g, i))
    return pl.pallas_call(
        _peer_kernel,
        grid=(n_tok // t, n_blocks),
        in_specs=[pl.BlockSpec((D_MODEL, t), lambda i, g: (0, i)),
                  pl.BlockSpec((EXPERT_BLOCK, D_MODEL), lambda i, g: (g, 0)),
                  pl.BlockSpec((EXPERT_BLOCK, D_MODEL), lambda i, g: (g, 0)),
                  gates, gates, keys, keys,
                  pl.BlockSpec((t, D_MODEL), lambda i, g: (i, 0)),
                  pl.BlockSpec((1, 1, 6 * D_MODEL), lambda i, g: (_peer_mod_row(i), 0, 0)),
                  pl.BlockSpec((1, D_MODEL), lambda i, g: (0, 0)),
                  pl.BlockSpec((1, D_MODEL), lambda i, g: (0, 0))],
        out_specs=pl.BlockSpec((t, D_MODEL), lambda i, g: (i, 0)),
        out_shape=jax.ShapeDtypeStruct((n_tok, D_MODEL), F32),
        scratch_shapes=[pltpu.VMEM((D_MODEL, t), F32), pltpu.VMEM((EXPERT_BLOCK, t), F32),
                        pltpu.VMEM((EXPERT_BLOCK, t), BF16)],
        compiler_params=_cparams("parallel", "arbitrary"),
        name="peer_dense",
    )(h2t, u, v, r2, e2, n1, e1, x1, mod, g, b)


def _cast_kernel(x_ref, o_ref):
    o_ref[...] = x_ref[...].astype(o_ref.dtype)


def _table_bf16(table, l):
    n_exp = table.shape[1]
    return pl.pallas_call(
        _cast_kernel,
        grid=(n_exp // CAST_ROWS,),
        in_specs=[pl.BlockSpec((None, CAST_ROWS, D_MODEL), lambda i: (l, i, 0))],
        out_specs=pl.BlockSpec((CAST_ROWS, D_MODEL), lambda i: (i, 0)),
        out_shape=jax.ShapeDtypeStruct((n_exp, D_MODEL), BF16),
        compiler_params=_cparams("parallel"),
        name="expert_table_bf16",
    )(table)


def _rope_tables():
    t = jnp.arange(DEC_SEQ)
    row = (t // GRID_W).astype(F32)
    col = (t % GRID_W).astype(F32)

    def angles(rot_dim):
        n_freq = rot_dim // 4
        inv_freq = ROPE_THETA ** (-jnp.arange(n_freq, dtype=F32) / n_freq)
        return jnp.concatenate([row[:, None] * inv_freq, col[:, None] * inv_freq], axis=-1)

    def pack(cos_l, sa_l, sb_l):
        return jnp.stack([cos_l, sa_l, sb_l])

    ang_b = angles(MLA_ROPE)
    cb, sb = jnp.cos(ang_b), jnp.sin(ang_b)
    one, zero = jnp.ones((DEC_SEQ, 64), F32), jnp.zeros((DEC_SEQ, 64), F32)
    z16, z32 = jnp.zeros((DEC_SEQ, 16), F32), jnp.zeros((DEC_SEQ, 32), F32)
    rope_b = pack(jnp.concatenate([one, cb, cb, jnp.ones((DEC_SEQ, 32), F32)], axis=1),
                  jnp.concatenate([zero, -sb, z16, z32], axis=1),
                  jnp.concatenate([zero, z16, sb, z32], axis=1))
    ang_d = angles(DIFF_QK_DIM)
    cd, sd = jnp.cos(ang_d), jnp.sin(ang_d)
    rope_d = pack(jnp.concatenate([cd, cd, cd, cd], axis=1),
                  jnp.concatenate([-sd, z32, -sd, z32], axis=1),
                  jnp.concatenate([z32, sd, z32, sd], axis=1))
    ident = pack(jnp.ones((DEC_SEQ, LANES), F32), jnp.zeros((DEC_SEQ, LANES), F32), jnp.zeros((DEC_SEQ, LANES), F32))
    return rope_b, rope_d, ident


def _na_bias_table(rpb):
    col = jnp.arange(GRID_W)
    dc = jnp.clip(col[None, :] - col[:, None], -(NA_WIN_COLS - 1), NA_WIN_COLS - 1) + NA_WIN_COLS - 1
    rpb_cols = rpb[:, :, dc]
    tabs = [rpb_cols[:, off:off + NA_WIN_ROWS].transpose(0, 2, 1, 3).reshape(NA_HEADS, GRID_W, NA_WIN_ROWS * GRID_W)
            for off in range(NA_WIN_ROWS)]
    return jnp.stack(tabs)


def _pad_cols(w, left, right):
    return jnp.pad(w, ((0, 0), (left, right)))


def kernel(x_prompt, x_sample, cache_mla_ckv, cache_mla_krope, cache_na_k, cache_na_v, cache_diff_k, cache_diff_v, c, c_ctx, w_mod, b_mod, w_in, sgu_norm_g, sgu_w, sgu_b, mla_q_norm_g, mla_w_uq, mla_kv_norm_g, mla_w_ukv, na_rpb, diff_lambda_q1, diff_lambda_k1, diff_lambda_q2, diff_lambda_k2, diff_norm_g, w_branch_a, w_branch_b, w_branch_c, w_branch_d, w_gate, b_gate, w_out, ln1_g, ln1_b, peer_w_q, peer_subkeys, peer_u, peer_v, ln2_g, ln2_b):
    x_ctx = x_prompt.reshape(N_CTX, D_MODEL)
    x_lat = x_sample.reshape(N_LAT, D_MODEL)
    cond = jnp.concatenate([c_ctx[None], c, jnp.zeros((N_COND - 1 - DEC_BATCH, D_MODEL), F32)], axis=0)
    mod_all = _mod_call(cond, w_mod, b_mod)
    rope_b, rope_d, rope_id = _rope_tables()
    cache_kr_pad = jnp.pad(cache_mla_krope, ((0, 0), (0, 0), (0, 0), (MLA_NOPE, LANES - MLA_NOPE - MLA_ROPE)))
    cache_na_k2 = cache_na_k.reshape(DEC_BATCH, DEPTH, PAST_LEN, 256)
    cache_na_v2 = cache_na_v.reshape(DEC_BATCH, DEPTH, PAST_LEN, 256)
    cache_diff_k2 = cache_diff_k.reshape(DEC_BATCH, DEPTH, PAST_LEN, 512)
    cache_diff_v2 = cache_diff_v.reshape(DEC_BATCH, DEPTH, PAST_LEN, 512)

    ctx_out = []
    for l in range(DEPTH):
        lambda_init = 0.8 - 0.6 * math.exp(-0.3 * l)
        mod_ctx = jnp.broadcast_to(mod_all[l, 0], (N_CTX // DEC_SEQ, 1, 6 * D_MODEL))
        mod_lat = mod_all[l, 1:1 + DEC_BATCH].reshape(DEC_BATCH, 1, 6 * D_MODEL)

        wi = w_in[l]
        kr_cols = _pad_cols(wi[:, C_KR:C_KR + MLA_ROPE], MLA_NOPE, LANES - MLA_NOPE - MLA_ROPE)
        w_in_r = jnp.concatenate([wi[:, :C_KR], kr_cols, wi[:, C_KR + MLA_ROPE:]], axis=1).astype(BF16)
        wuq = mla_w_uq[l].reshape(MLA_Q_LORA, MLA_HEADS, MLA_NOPE + MLA_ROPE)
        wuq = jnp.pad(wuq, ((0, 0), (0, 0), (0, LANES - MLA_NOPE - MLA_ROPE))).reshape(MLA_Q_LORA, -1).astype(BF16)
        wukv = mla_w_ukv[l].reshape(MLA_KV_LORA, MLA_HEADS, MLA_NOPE + MLA_V)
        wuk = jnp.pad(wukv[:, :, :MLA_NOPE], ((0, 0), (0, 0), (0, LANES - MLA_NOPE))).reshape(MLA_KV_LORA, -1)
        wuk = wuk.astype(BF16)
        wuv = wukv[:, :, MLA_NOPE:].reshape(MLA_KV_LORA, -1).astype(BF16)
        sgu_bias = jnp.repeat(sgu_b[l].T, SGU_WIDTH // SGU_GROUPS, axis=1)
        lams = [p[l].reshape(1, DIFF_QK_DIM) for p in (diff_lambda_q1, diff_lambda_k1, diff_lambda_q2, diff_lambda_k2)]
        dg = diff_norm_g[l].reshape(1, DIFF_V_DIM)

        inproj_weights = (w_in_r, sgu_norm_g[l].reshape(1, -1), sgu_w[l].astype(BF16), sgu_bias,
                          mla_q_norm_g[l].reshape(1, -1), mla_kv_norm_g[l].reshape(1, -1), wuq)
        merge_weights = (w_gate[l].astype(BF16), b_gate[l].reshape(1, -1),
                         w_branch_a[l].astype(BF16), w_branch_b[l].astype(BF16), w_branch_c[l].astype(BF16),
                         w_branch_d[l].astype(BF16), w_out[l].astype(BF16), ln1_g[l].reshape(1, -1),
                         ln1_b[l].reshape(1, -1))
        keys = peer_subkeys[l].reshape(2 * PEER_HEADS, PEER_N_KEYS, PEER_KEY_DIM // 2).astype(BF16)
        wqt = peer_w_q[l].T.astype(BF16)
        u_bf, v_bf = _table_bf16(peer_u, l), _table_bf16(peer_v, l)
        ln2 = (ln2_g[l].reshape(1, -1), ln2_b[l].reshape(1, -1))

        def channel_mix(x, mod, oa, ob, oc, od):
            x1, h2t = _merge_call(x, mod, oa, ob, oc, od, *merge_weights)
            r2, e2, n1, e1 = _router_call(h2t, wqt, keys)
            return _peer_call(h2t, u_bf, v_bf, r2, e2, n1, e1, x1, mod, *ln2)

        oa, mq, ckv, kr, nq, nk, nv, dq, dk, dv = _inproj_call(x_ctx, mod_ctx, rope_id, rope_id, *inproj_weights)
        ob, oc, od = _ctx_attn_call(lambda_init, (mq, ckv, kr, nq, nk, nv, dq, dk, dv), wuk, wuv, lams, dg)
        x_ctx = channel_mix(x_ctx, mod_ctx, oa, ob, oc, od)
        ctx_out.append((ckv.reshape(BATCH, SEQ, MLA_KV_LORA),
                        kr[:, MLA_NOPE:MLA_NOPE + MLA_ROPE].reshape(BATCH, SEQ, MLA_ROPE),
                        nk.reshape(BATCH, SEQ, NA_HEADS, NA_HEAD_DIM),
                        nv.reshape(BATCH, SEQ, NA_HEADS, NA_HEAD_DIM),
                        dk.reshape(BATCH, SEQ, DIFF_HEADS, 2 * DIFF_QK_DIM),
                        dv.reshape(BATCH, SEQ, DIFF_HEADS, DIFF_V_DIM)))

        oa, mq, ckv, kr, nq, nk, nv, dq, dk, dv = _inproj_call(x_lat, mod_lat, rope_b, rope_d, *inproj_weights)
        ob = _lat_mla_call(l, mq, ckv, kr, cache_mla_ckv, cache_kr_pad, wuk, wuv)
        oc = _lat_na_call(l, nq, nk, nv, cache_na_k2, cache_na_v2, _na_bias_table(na_rpb[l]))
        od = _lat_diff_call(l, lambda_init, dq, dk, dv, cache_diff_k2, cache_diff_v2, lams, dg)
        x_lat = channel_mix(x_lat, mod_lat, oa, ob, oc, od)

    y_prompt = x_ctx.reshape(BATCH, SEQ, D_MODEL)
    y_sample = x_lat.reshape(DEC_BATCH, DEC_SEQ, D_MODEL)
    new = [jnp.stack([t[k] for t in ctx_out], axis=1) for k in range(6)]
    return (y_prompt, y_sample, *new)
```

```python
import functools
import math

import jax
import jax.numpy as jnp
from jax import lax
from jax.experimental import pallas as pl
from jax.experimental.pallas import tpu as pltpu

F32 = jnp.float32
BF16 = jnp.bfloat16

D_MODEL = 1024
BATCH = 32
SEQ = 256
DEPTH = 2
DEC_BATCH = 8
DEC_SEQ = 1024
PAST_LEN = 512
GRID_W = 64
CHUNK = 128
SGU_GROUPS = 4
SGU_WIDTH = 256
MLA_HEADS = 4
MLA_Q_LORA = 256
MLA_KV_LORA = 128
MLA_NOPE = 64
MLA_ROPE = 32
MLA_V = 64
NA_HEADS = 4
NA_HEAD_DIM = 64
NA_WIN_ROWS = 8
NA_WIN_COLS = 16
DIFF_HEADS = 4
DIFF_QK_DIM = 64
DIFF_V_DIM = 128
PEER_HEADS = 8
PEER_N_KEYS = 128
PEER_KEY_DIM = 256
PEER_TOPK = 16
ROPE_THETA = 10000.0
LN_EPS = 1e-6
NEG_BIG = -1e30
DEEPNORM_ALPHA = (2 * DEPTH) ** 0.25

LANES = 128
SUBLANES = 8
N_CTX = BATCH * SEQ
N_LAT = DEC_BATCH * DEC_SEQ
N_COND = 16
TM = 512
ROWS = DEC_SEQ // GRID_W
Q_TILE = 256
NA_ROWS_PER_STEP = 4
ROUTER_TILE = 512
PEER_TILE = 1024
EXPERT_BLOCK = 1024
KEYS_PER_BLOCK = EXPERT_BLOCK // PEER_N_KEYS
GATE_LANES = 256
CAST_ROWS = 2048
VMEM_LIMIT = 56 * 1024 * 1024

C_AU, C_AV, C_CQ, C_CKV, C_KR = 0, 256, 512, 768, 896
C_NQ, C_NK, C_NV, C_DQ, C_DK, C_DV, C_END = 1024, 1280, 1536, 1792, 2304, 2816, 3328


def _ln(x):
    mu = jnp.mean(x, axis=-1, keepdims=True)
    xc = x - mu
    var = jnp.mean(xc * xc, axis=-1, keepdims=True)
    return xc * lax.rsqrt(var + LN_EPS)


def _rms(x):
    return x * lax.rsqrt(jnp.mean(x * x, axis=-1, keepdims=True) + LN_EPS)


def _gelu(x):
    return 0.5 * x * (1.0 + lax.erf(x * (1.0 / math.sqrt(2.0))))


def _dot(a, b):
    return jnp.dot(a, b, preferred_element_type=F32)


def _dot_nt(a, b):
    return lax.dot_general(a, b, (((1,), (1,)), ((), ())), preferred_element_type=F32)


def _rope(x, tab_ref, half):
    return (x * tab_ref[0] + pltpu.roll(x, LANES - half, 1) * tab_ref[1] + pltpu.roll(x, half, 1) * tab_ref[2])


def _cparams(*sem, fuse=None):
    return pltpu.CompilerParams(dimension_semantics=sem, vmem_limit_bytes=VMEM_LIMIT, allow_input_fusion=fuse)


def _mod_kernel(cond_ref, w_ref, b_ref, o_ref):
    c = cond_ref[...]
    s = c * jax.nn.sigmoid(c)
    o_ref[...] = _dot(s, w_ref[...]) + b_ref[...]


def _mod_call(cond, w_mod, b_mod):
    nb = 1536
    return pl.pallas_call(
        _mod_kernel,
        grid=(DEPTH, 6 * D_MODEL // nb),
        in_specs=[pl.BlockSpec((N_COND, D_MODEL), lambda l, j: (0, 0)),
                  pl.BlockSpec((None, D_MODEL, nb), lambda l, j: (l, 0, j)),
                  pl.BlockSpec((None, 1, nb), lambda l, j: (l, 0, j))],
        out_specs=pl.BlockSpec((None, N_COND, nb), lambda l, j: (l, 0, j)),
        out_shape=jax.ShapeDtypeStruct((DEPTH, N_COND, 6 * D_MODEL), F32),
        compiler_params=_cparams("arbitrary", "arbitrary"),
        name="mod_vectors",
    )(cond, w_mod, b_mod.reshape(DEPTH, 1, 6 * D_MODEL))


def _mod_row(i):
    return i // (DEC_SEQ // TM)


def _pos_block(i):
    return i % (DEC_SEQ // TM)


def _inproj_kernel(x_ref, mod_ref, rb_ref, rd_ref, w_in_ref, sgu_g_ref, sgu_w_ref, sgu_bias_ref,
                   qg_ref, kvg_ref, wuq_ref,
                   oa_ref, mq_ref, ckv_ref, kr_ref, nq_ref, nk_ref, nv_ref, dq_ref, dk_ref, dv_ref):
    x = x_ref[...]
    shift = mod_ref[0, :, 0:D_MODEL]
    scale = mod_ref[0, :, D_MODEL:2 * D_MODEL]
    h = (_ln(x) * (1.0 + scale) + shift).astype(BF16)

    ya = _dot(h, w_in_ref[:, C_AU:C_CQ])
    u = _gelu(ya[:, :SGU_WIDTH])
    v = _gelu(ya[:, SGU_WIDTH:])
    vn = (_ln(v) * sgu_g_ref[...]).astype(BF16)
    group = lax.broadcasted_iota(jnp.int32, (CHUNK, SGU_WIDTH), 1) // (SGU_WIDTH // SGU_GROUPS)
    for c in range(TM // CHUNK):
        rows = slice(c * CHUNK, (c + 1) * CHUNK)
        mixed = sgu_bias_ref[...]
        for g in range(SGU_GROUPS):
            mixed = mixed + jnp.where(group == g, _dot(sgu_w_ref[g], vn[rows]), 0.0)
        oa_ref[rows, :] = (u[rows] * mixed).astype(oa_ref.dtype)

    ym = _dot(h, w_in_ref[:, C_CQ:C_NQ])
    cq = (_rms(ym[:, :MLA_Q_LORA]) * qg_ref[...]).astype(BF16)
    mq = _dot(cq, wuq_ref[...])
    for g in range(MLA_HEADS):
        lanes = slice(g * LANES, (g + 1) * LANES)
        mq_ref[:, lanes] = _rope(mq[:, lanes], rb_ref, MLA_ROPE // 2)
    ckv_ref[...] = _rms(ym[:, MLA_Q_LORA:MLA_Q_LORA + MLA_KV_LORA]) * kvg_ref[...]
    kr_ref[...] = _rope(ym[:, MLA_Q_LORA + MLA_KV_LORA:], rb_ref, MLA_ROPE // 2)

    yn = _dot(h, w_in_ref[:, C_NQ:C_DQ])
    nq_ref[...] = yn[:, 0:256]
    nk_ref[...] = yn[:, 256:512]
    nv_ref[...] = yn[:, 512:768]

    yd = _dot(h, w_in_ref[:, C_DQ:C_END])
    for g in range(4):
        lanes = slice(g * LANES, (g + 1) * LANES)
        dq_ref[:, lanes] = _rope(yd[:, g * LANES:(g + 1) * LANES], rd_ref, DIFF_QK_DIM // 2)
        dk_ref[:, lanes] = _rope(yd[:, 512 + g * LANES:512 + (g + 1) * LANES], rd_ref, DIFF_QK_DIM // 2)
    dv_ref[...] = yd[:, 1024:1536]


def _inproj_call(x, mod, rope_b, rope_d, w_in_r, sgu_g, sgu_w, sgu_bias, qg, kvg, wuq):
    tile = lambda w: pl.BlockSpec((TM, w), lambda i: (i, 0))
    full = lambda *s: pl.BlockSpec(s, lambda i: (0,) * len(s))
    widths = (SGU_WIDTH, 512, MLA_KV_LORA, LANES, 256, 256, 256, 512, 512, 512)
    dtypes = (BF16,) + (F32,) * 9
    n_tok = x.shape[0]
    return pl.pallas_call(
        _inproj_kernel,
        grid=(n_tok // TM,),
        in_specs=[tile(D_MODEL),
                  pl.BlockSpec((1, 1, 6 * D_MODEL), lambda i: (_mod_row(i), 0, 0)),
                  pl.BlockSpec((3, TM, LANES), lambda i: (0, _pos_block(i), 0)),
                  pl.BlockSpec((3, TM, LANES), lambda i: (0, _pos_block(i), 0)),
                  full(D_MODEL, C_END), full(1, SGU_WIDTH), full(SGU_GROUPS, CHUNK, CHUNK),
                  full(CHUNK, SGU_WIDTH), full(1, MLA_Q_LORA), full(1, MLA_KV_LORA),
                  full(MLA_Q_LORA, MLA_HEADS * LANES)],
        out_specs=[tile(w) for w in widths],
        out_shape=[jax.ShapeDtypeStruct((n_tok, w), dt) for w, dt in zip(widths, dtypes)],
        compiler_params=_cparams("parallel", fuse=[False] * 4 + [True, False, True] + [False] * 3 + [True]),
        name="in_projection",
    )(x, mod, rope_b, rope_d, w_in_r, sgu_g, sgu_w, sgu_bias, qg, kvg, wuq)


def _half_mask(lo):
    lane = lax.broadcasted_iota(jnp.int32, (1, LANES), 1)
    return (lane >= lo) & (lane < lo + 64)


def _softmax_pv(scores, values, lanes):
    m = scores[0].max(axis=-1, keepdims=True)
    for s in scores[1:]:
        m = jnp.maximum(m, s.max(axis=-1, keepdims=True))
    den = None
    o = None
    for s, v in zip(scores, values):
        p = jnp.exp(s - m)
        d = p.sum(axis=-1, keepdims=True)
        den = d if den is None else den + d
        pv = _dot(p.astype(BF16), v[:, lanes])
        o = pv if o is None else o + pv
    return o / den


def _pair_attention(q, keys, vals, scale, bias_fn=None):
    assert math.frexp(scale)[0] == 0.5
    n = q.shape[0]
    outs = []
    for pair in range(2):
        lanes = slice(pair * LANES, (pair + 1) * LANES)
        qp = q[:, lanes] * jnp.asarray(scale, BF16)
        masks = [_half_mask(64 * sub) for sub in range(2)]
        qs = jnp.concatenate([jnp.where(m, qp, jnp.zeros_like(qp)) for m in masks], axis=0)
        scores = [_dot_nt(qs, k[:, lanes]) for k in keys]
        if bias_fn is not None:
            scores = bias_fn(pair, scores)
        o = _softmax_pv(scores, vals, lanes)
        outs.append(jnp.where(masks[0], o[:n], 0.0) + jnp.where(masks[1], o[n:], 0.0))
    return outs


def _mla_attention(q, k_blocks, v_blocks, o_ref, rows):
    scale = (MLA_NOPE + MLA_ROPE) ** -0.5
    for pair in range(2):
        lanes = slice(pair * LANES, (pair + 1) * LANES)
        acc = None
        for sub in range(2):
            head = 2 * pair + sub
            hl = slice(head * LANES, (head + 1) * LANES)
            scores = [_dot_nt(q[:, hl], k[:, hl]) * scale for k in k_blocks]
            o = jnp.where(_half_mask(64 * sub), _softmax_pv(scores, v_blocks, lanes), 0.0)
            acc = o if acc is None else acc + o
        o_ref[rows, lanes] = acc.astype(o_ref.dtype)


def _diff_lambda(lq1, lk1, lq2, lk2, lambda_init):
    a = jnp.sum(lq1[...] * lk1[...], axis=-1, keepdims=True)
    b = jnp.sum(lq2[...] * lk2[...], axis=-1, keepdims=True)
    return jnp.exp(a) - jnp.exp(b) + lambda_init


def _diff_attention(q, k_blocks, v_blocks, lam, norm_g, lambda_init, o_ref, rows):
    scale = DIFF_QK_DIM ** -0.5
    assert math.frexp(scale)[0] == 0.5
    for head in range(DIFF_HEADS):
        hl = slice(head * LANES, (head + 1) * LANES)
        qh = q[:, hl] * jnp.asarray(scale, BF16)
        probs = []
        for sub in range(2):
            qm = jnp.where(_half_mask(64 * sub), qh, jnp.zeros_like(qh))
            scores = [_dot_nt(qm, k[:, hl]) for k in k_blocks]
            m = scores[0].max(axis=-1, keepdims=True)
            for s in scores[1:]:
                m = jnp.maximum(m, s.max(axis=-1, keepdims=True))
            ps = [jnp.exp(s - m) for s in scores]
            den = ps[0].sum(axis=-1, keepdims=True)
            for p in ps[1:]:
                den = den + p.sum(axis=-1, keepdims=True)
            probs.append((ps, 1.0 / den))
        o = None
        for i, v in enumerate(v_blocks):
            w = probs[0][0][i] * probs[0][1] - probs[1][0][i] * (lam * probs[1][1])
            pv = _dot(w.astype(BF16), v[:, hl])
            o = pv if o is None else o + pv
        o = _rms(o) * norm_g * (1.0 - lambda_init)
        o_ref[rows, hl] = o.astype(o_ref.dtype)


def _ctx_attn_kernel(lambda_init, mq_ref, ckv_ref, kr_ref, nq_ref, nk_ref, nv_ref, dq_ref, dk_ref, dv_ref,
                     wuk_ref, wuv_ref, lq1, lk1, lq2, lk2, dg_ref, ob_ref, oc_ref, od_ref):
    rows = slice(0, SEQ)
    ckv = ckv_ref[...].astype(BF16)
    kr = kr_ref[...]
    k_b = (_dot(ckv, wuk_ref[...]) + jnp.concatenate([kr] * MLA_HEADS, axis=1)).astype(BF16)
    v_b = _dot(ckv, wuv_ref[...]).astype(BF16)
    _mla_attention(mq_ref[...].astype(BF16), [k_b], [v_b], ob_ref, rows)

    outs = _pair_attention(nq_ref[...].astype(BF16), [nk_ref[...].astype(BF16)], [nv_ref[...].astype(BF16)],
                           NA_HEAD_DIM ** -0.5)
    for pair in range(2):
        oc_ref[:, pair * LANES:(pair + 1) * LANES] = outs[pair].astype(oc_ref.dtype)

    lam = _diff_lambda(lq1, lk1, lq2, lk2, lambda_init)
    _diff_attention(dq_ref[...].astype(BF16), [dk_ref[...].astype(BF16)], [dv_ref[...].astype(BF16)],
                    lam, dg_ref[...], lambda_init, od_ref, rows)


def _ctx_attn_call(lambda_init, acts, wuk, wuv, lams, dg):
    mq, ckv, kr, nq, nk, nv, dq, dk, dv = acts
    seq = lambda w: pl.BlockSpec((SEQ, w), lambda b: (b, 0))
    full = lambda *s: pl.BlockSpec(s, lambda b: (0,) * len(s))
    return pl.pallas_call(
        functools.partial(_ctx_attn_kernel, lambda_init),
        grid=(BATCH,),
        in_specs=[seq(512), seq(128), seq(128), seq(256), seq(256), seq(256), seq(512), seq(512), seq(512),
                  full(MLA_KV_LORA, 512), full(MLA_KV_LORA, 256)] + [full(1, DIFF_QK_DIM)] * 4
                 + [full(1, DIFF_V_DIM)],
        out_specs=[seq(256), seq(256), seq(512)],
        out_shape=[jax.ShapeDtypeStruct((N_CTX, w), BF16) for w in (256, 256, 512)],
        compiler_params=_cparams("parallel"),
        name="context_attention",
    )(mq, ckv, kr, nq, nk, nv, dq, dk, dv, wuk, wuv, *lams, dg)


def _lat_mla_kernel(mq_ref, ckv_ref, kr_ref, cckv_ref, ckr_ref, wuk_ref, wuv_ref, o_ref):
    def expand(ckv_f32, kr):
        ckv = ckv_f32.astype(BF16)
        k = (_dot(ckv, wuk_ref[...]) + jnp.concatenate([kr] * MLA_HEADS, axis=1)).astype(BF16)
        return k, _dot(ckv, wuv_ref[...]).astype(BF16)

    k_lat, v_lat = expand(ckv_ref[...], kr_ref[...])
    k_ctx, v_ctx = expand(cckv_ref[...], ckr_ref[...])
    k_all = jnp.concatenate([k_lat, k_ctx], axis=0)
    v_all = jnp.concatenate([v_lat, v_ctx], axis=0)
    for t in range(DEC_SEQ // Q_TILE):
        rows = slice(t * Q_TILE, (t + 1) * Q_TILE)
        _mla_attention(mq_ref[rows, :].astype(BF16), [k_all], [v_all], o_ref, rows)


def _lat_mla_call(l, mq, ckv, kr, cache_ckv, cache_kr_pad, wuk, wuv):
    seq = lambda w: pl.BlockSpec((DEC_SEQ, w), lambda b: (b, 0))
    cache = lambda w: pl.BlockSpec((None, None, PAST_LEN, w), lambda b: (b, l, 0, 0))
    full = lambda *s: pl.BlockSpec(s, lambda b: (0,) * len(s))
    return pl.pallas_call(
        _lat_mla_kernel,
        grid=(DEC_BATCH,),
        in_specs=[seq(512), seq(128), seq(128), cache(MLA_KV_LORA), cache(LANES),
                  full(MLA_KV_LORA, 512), full(MLA_KV_LORA, 256)],
        out_specs=pl.BlockSpec((DEC_SEQ, 256), lambda b: (b, 0)),
        out_shape=jax.ShapeDtypeStruct((N_LAT, 256), BF16),
        compiler_params=_cparams("parallel"),
        name="latent_mla_attention",
    )(mq, ckv, kr, cache_ckv, cache_kr_pad, wuk, wuv)


def _win_start(r):
    return jnp.clip(r - NA_WIN_ROWS // 2, 0, ROWS - NA_WIN_ROWS)


def _lat_na_kernel(nq_ref, nk_ref, nv_ref, ck_ref, cv_ref, bias_ref, o_ref):
    win = NA_WIN_ROWS * GRID_W
    k_c = ck_ref[...].astype(BF16)
    v_c = cv_ref[...].astype(BF16)
    q_col = lax.broadcasted_iota(jnp.int32, (2 * GRID_W, win), 0) % GRID_W
    k_col = lax.broadcasted_iota(jnp.int32, (2 * GRID_W, win), 1) % GRID_W
    c0 = jnp.clip(q_col - NA_WIN_COLS // 2, 0, GRID_W - NA_WIN_COLS)
    col_in = (k_col >= c0) & (k_col < c0 + NA_WIN_COLS)

    for rr in range(NA_ROWS_PER_STEP):
        r = pl.program_id(1) * NA_ROWS_PER_STEP + rr
        first = _win_start(r)
        start = pl.multiple_of(first * GRID_W, GRID_W)
        off = first - r + NA_WIN_ROWS - 1
        k_all = jnp.concatenate([nk_ref[pl.ds(start, win), :].astype(BF16), k_c], axis=0)
        v_all = jnp.concatenate([nv_ref[pl.ds(start, win), :].astype(BF16), v_c], axis=0)

        def bias_fn(pair, scores, off=off):
            s = scores[0]
            bias = jnp.concatenate([bias_ref[off, 2 * pair], bias_ref[off, 2 * pair + 1]], axis=0)
            s_win = jnp.where(col_in, s[:, :win] + bias, NEG_BIG)
            return [jnp.concatenate([s_win, s[:, win:]], axis=1)]

        rows = slice(rr * GRID_W, (rr + 1) * GRID_W)
        outs = _pair_attention(nq_ref[rows, :].astype(BF16), [k_all], [v_all], NA_HEAD_DIM ** -0.5, bias_fn)
        for pair in range(2):
            o_ref[rows, pair * LANES:(pair + 1) * LANES] = outs[pair].astype(o_ref.dtype)


def _lat_na_call(l, nq, nk, nv, cache_k, cache_v, bias_tab):
    seq = pl.BlockSpec((DEC_SEQ, 256), lambda b, r: (b, 0))
    cache = pl.BlockSpec((None, None, PAST_LEN, 256), lambda b, r: (b, l, 0, 0))
    steps = ROWS // NA_ROWS_PER_STEP
    q_rows = NA_ROWS_PER_STEP * GRID_W
    return pl.pallas_call(
        _lat_na_kernel,
        grid=(DEC_BATCH, steps),
        in_specs=[pl.BlockSpec((q_rows, 256), lambda b, r: (b * steps + r, 0)),
                  seq, seq, cache, cache,
                  pl.BlockSpec((NA_WIN_ROWS, NA_HEADS, GRID_W, NA_WIN_ROWS * GRID_W), lambda b, r: (0, 0, 0, 0))],
        out_specs=pl.BlockSpec((q_rows, 256), lambda b, r: (b * steps + r, 0)),
        out_shape=jax.ShapeDtypeStruct((N_LAT, 256), BF16),
        compiler_params=_cparams("parallel", "arbitrary"),
        name="latent_neighbourhood_attention",
    )(nq, nk, nv, cache_k, cache_v, bias_tab)


def _lat_diff_kernel(lambda_init, dq_ref, dk_ref, dv_ref, ck_ref, cv_ref, lq1, lk1, lq2, lk2, dg_ref, o_ref):
    lam = _diff_lambda(lq1, lk1, lq2, lk2, lambda_init)
    k_blocks = [dk_ref[...].astype(BF16), ck_ref[...].astype(BF16)]
    v_blocks = [dv_ref[...].astype(BF16), cv_ref[...].astype(BF16)]
    for t in range(DEC_SEQ // Q_TILE):
        rows = slice(t * Q_TILE, (t + 1) * Q_TILE)
        _diff_attention(dq_ref[rows, :].astype(BF16), k_blocks, v_blocks, lam, dg_ref[...], lambda_init,
                        o_ref, rows)


def _lat_diff_call(l, lambda_init, dq, dk, dv, cache_k, cache_v, lams, dg):
    seq = pl.BlockSpec((DEC_SEQ, 512), lambda b: (b, 0))
    cache = pl.BlockSpec((None, None, PAST_LEN, 512), lambda b: (b, l, 0, 0))
    full = lambda *s: pl.BlockSpec(s, lambda b: (0,) * len(s))
    return pl.pallas_call(
        functools.partial(_lat_diff_kernel, lambda_init),
        grid=(DEC_BATCH,),
        in_specs=[seq, seq, seq, cache, cache] + [full(1, DIFF_QK_DIM)] * 4 + [full(1, DIFF_V_DIM)],
        out_specs=pl.BlockSpec((DEC_SEQ, 512), lambda b: (b, 0)),
        out_shape=jax.ShapeDtypeStruct((N_LAT, 512), BF16),
        compiler_params=_cparams("parallel"),
        name="latent_differential_attention",
    )(dq, dk, dv, cache_k, cache_v, *lams, dg)


def _merge_kernel(x_ref, mod_ref, oa_ref, ob_ref, oc_ref, od_ref, wg_ref, bg_ref,
                  wa_ref, wb_ref, wc_ref, wd_ref, wo_ref, g_ref, b_ref, x1_ref, h2t_ref):
    x = x_ref[...]
    mod = lambda k: mod_ref[0, :, k * D_MODEL:(k + 1) * D_MODEL]
    h = (_ln(x) * (1.0 + mod(1)) + mod(0)).astype(BF16)
    merged = None
    for i, (o_ref, w_ref) in enumerate(((oa_ref, wa_ref), (ob_ref, wb_ref), (oc_ref, wc_ref), (od_ref, wd_ref))):
        cols = slice(i * D_MODEL, (i + 1) * D_MODEL)
        gate = jax.nn.sigmoid(_dot(h, wg_ref[:, cols]) + bg_ref[:, cols])
        term = gate * _dot(o_ref[...], w_ref[...])
        merged = term if merged is None else merged + term
    mix = _dot(merged.astype(BF16), wo_ref[...])
    x1 = _ln(DEEPNORM_ALPHA * x + mod(2) * mix) * g_ref[...] + b_ref[...]
    x1_ref[...] = x1
    h2 = _ln(x1) * (1.0 + mod(4)) + mod(3)
    h2t_ref[...] = h2.T.astype(BF16)


def _merge_call(x, mod, oa, ob, oc, od, wg, bg, wa, wb, wc, wd, wo, g, b):
    tile = lambda w: pl.BlockSpec((TM, w), lambda i: (i, 0))
    full = lambda *s: pl.BlockSpec(s, lambda i: (0,) * len(s))
    n_tok = x.shape[0]
    return pl.pallas_call(
        _merge_kernel,
        grid=(n_tok // TM,),
        in_specs=[tile(D_MODEL), pl.BlockSpec((1, 1, 6 * D_MODEL), lambda i: (_mod_row(i), 0, 0)),
                  tile(256), tile(256), tile(256), tile(512),
                  full(D_MODEL, 4 * D_MODEL), full(1, 4 * D_MODEL),
                  full(256, D_MODEL), full(256, D_MODEL), full(256, D_MODEL), full(512, D_MODEL),
                  full(D_MODEL, D_MODEL), full(1, D_MODEL), full(1, D_MODEL)],
        out_specs=[tile(D_MODEL), pl.BlockSpec((D_MODEL, TM), lambda i: (0, i))],
        out_shape=[jax.ShapeDtypeStruct((n_tok, D_MODEL), F32), jax.ShapeDtypeStruct((D_MODEL, n_tok), BF16)],
        compiler_params=_cparams("parallel", fuse=[False] * 6 + [True, False] + [True] * 5 + [False] * 2),
        name="branch_merge",
    )(x, mod, oa, ob, oc, od, wg, bg, wa, wb, wc, wd, wo, g, b)


KEY_MIN = -2 ** 31


def _tree_sum(terms):
    while len(terms) > 1:
        terms = [a + b for a, b in zip(terms[0::2], terms[1::2])] + ([terms[-1]] if len(terms) % 2 else [])
    return terms[0]


def _row_gather(table, idx):
    ii = idx.astype(jnp.int32)
    low = ii & (SUBLANES - 1)
    outs = []
    for c in range(idx.shape[0] // SUBLANES):
        rows = slice(c * SUBLANES, (c + 1) * SUBLANES)
        lo = jnp.take_along_axis(table[0:SUBLANES], low[rows], axis=0)
        hi = jnp.take_along_axis(table[SUBLANES:PEER_TOPK], low[rows], axis=0)
        outs.append(jnp.where(ii[rows] < SUBLANES, lo, jnp.where(ii[rows] < PEER_TOPK, hi, 0.0)))
    return jnp.concatenate(outs, axis=0)


def _sort_key(x):
    b = lax.bitcast_convert_type(x + 0.0, jnp.int32)
    return b ^ ((b >> 31) & 0x7FFFFFFF)


def _key_value(k):
    return lax.bitcast_convert_type(k ^ ((k >> 31) & 0x7FFFFFFF), F32)


def _top16(s):
    row = lax.broadcasted_iota(jnp.int32, s.shape, 0).astype(F32)
    krow = lax.broadcasted_iota(jnp.int32, (PEER_TOPK, s.shape[1]), 0)

    def body(k, carry):
        work, rank, vals = carry
        m = jnp.max(work, axis=0, keepdims=True)
        idx = jnp.min(jnp.where(work == m, row, float(PEER_N_KEYS)), axis=0, keepdims=True)
        sel = row == idx
        rank = jnp.where(sel, jnp.asarray(k, jnp.int32).astype(F32), rank)
        work = jnp.where(sel, -jnp.inf, work)
        vals = jnp.where(krow == k, m, vals)
        return work, rank, vals

    init = (s, jnp.full(s.shape, float(PEER_N_KEYS), F32), jnp.zeros((PEER_TOPK, s.shape[1]), F32))
    _, rank, vals = lax.fori_loop(0, PEER_TOPK, body, init)
    return vals, rank


def _top16_pair(s1, s2):
    krow = lax.broadcasted_iota(jnp.int32, (PEER_TOPK, LANES), 0)

    def body(k, carry):
        w1, w2, v1, v2 = carry
        code = KEY_MIN + jnp.asarray(k, jnp.int32)
        m1 = jnp.max(w1, axis=0, keepdims=True)
        m2 = jnp.max(w2, axis=0, keepdims=True)
        w1 = jnp.where(w1 == m1, code, w1)
        w2 = jnp.where(w2 == m2, code, w2)
        return w1, w2, jnp.where(krow == k, m1, v1), jnp.where(krow == k, m2, v2)

    zeros = jnp.zeros((PEER_TOPK, LANES), jnp.int32)
    w1, w2, v1, v2 = lax.fori_loop(0, PEER_TOPK, body, (_sort_key(s1), _sort_key(s2), zeros, zeros))

    def decode(w):
        taken = w < KEY_MIN + PEER_TOPK
        rank = jnp.where(taken, (w - KEY_MIN).astype(F32), float(PEER_N_KEYS))
        return rank, jnp.sum(taken.astype(F32), axis=0, keepdims=True)

    r1, c1 = decode(w1)
    r2, c2 = decode(w2)
    ties = jnp.max(jnp.maximum(jnp.abs(c1 - PEER_TOPK), jnp.abs(c2 - PEER_TOPK))) > 0.5
    return _key_value(v1), _key_value(v2), r1, r2, ties


def _merge_counts(hs1, hs2):
    krow = lax.broadcasted_iota(jnp.int32, hs1.shape, 0).astype(F32)

    def body(_, carry):
        cnt, front = carry
        m = jnp.max(front, axis=0, keepdims=True)
        win = jnp.min(jnp.where(front == m, krow, float(PEER_TOPK)), axis=0, keepdims=True)
        sel = krow == win
        cnt = jnp.where(sel, cnt + 1.0, cnt)
        nxt = jnp.where(cnt < float(PEER_TOPK), hs1 + _row_gather(hs2, cnt), -jnp.inf)
        return cnt, jnp.where(sel, nxt, front)

    cnt, _ = lax.fori_loop(0, PEER_TOPK, body, (jnp.zeros(hs1.shape, F32), hs1 + hs2[0:1, :]))
    return cnt


def _router_kernel(h2t_ref, wqt_ref, keys_ref, r2_ref, e2_ref, n1_ref, e1_ref, q_scr, s_scr, hs_scr, rank1_scr):
    t = ROUTER_TILE
    q_scr[...] = _dot(wqt_ref[...], h2t_ref[...]).astype(BF16)

    for hd in range(PEER_HEADS):
        for half in range(2):
            rows = slice((2 * hd + half) * LANES, (2 * hd + half + 1) * LANES)
            s_scr[hd, half] = _dot(keys_ref[2 * hd + half], q_scr[rows, :])

    def head_body(hd, _):
        for j in range(t // LANES):
            lanes = slice(j * LANES, (j + 1) * LANES)

            def put(hs1, hs2, rank1, rank2, lanes=lanes):
                hs_scr[hd, 0, :, lanes] = hs1
                hs_scr[hd, 1, :, lanes] = hs2
                rank1_scr[hd, :, lanes] = rank1
                r2_ref[hd, :, lanes] = rank2.astype(BF16)

            *quick, ties = _top16_pair(s_scr[hd, 0, :, lanes], s_scr[hd, 1, :, lanes])
            put(*quick)

            @pl.when(ties)
            def _(lanes=lanes, put=put):
                hs1, rank1 = _top16(s_scr[hd, 0, :, lanes])
                hs2, rank2 = _top16(s_scr[hd, 1, :, lanes])
                put(hs1, hs2, rank1, rank2)
        return 0

    lax.fori_loop(0, PEER_HEADS, head_body, 0)

    for pair in range(PEER_HEADS // 2):
        heads = (2 * pair, 2 * pair + 1)
        hs1 = jnp.concatenate([hs_scr[h, 0] for h in heads], axis=1)
        hs2 = jnp.concatenate([hs_scr[h, 1] for h in heads], axis=1)
        cnt = _merge_counts(hs1, hs2)
        e1r = jnp.exp(hs1 - hs1[0:1, :])
        e2r = jnp.exp(hs2 - hs2[0:1, :])
        prefix = _tree_sum([jnp.where(cnt > float(kb), e2r[kb:kb + 1, :], 0.0) for kb in range(PEER_TOPK)])
        inv_z = 1.0 / jnp.sum(e1r * prefix, axis=0, keepdims=True)
        for i, h in enumerate(heads):
            lanes = slice(i * t, (i + 1) * t)
            e2_ref[h] = (jnp.exp(s_scr[h, 1] - hs2[0:1, lanes]) * inv_z[:, lanes]).astype(BF16)
            e1_ref[h] = 0.5 * jnp.exp(s_scr[h, 0] - hs1[0:1, lanes])
            n1_ref[h] = _row_gather(cnt[:, lanes], rank1_scr[h])


def _router_call(h2t, wqt, keys):
    t = ROUTER_TILE
    out = pl.BlockSpec((PEER_HEADS, PEER_N_KEYS, t), lambda i: (0, 0, i))
    n_tok = h2t.shape[1]
    shape = (PEER_HEADS, PEER_N_KEYS, n_tok)
    return pl.pallas_call(
        _router_kernel,
        grid=(n_tok // t,),
        in_specs=[pl.BlockSpec((D_MODEL, t), lambda i: (0, i)),
                  pl.BlockSpec((PEER_HEADS * PEER_KEY_DIM, D_MODEL), lambda i: (0, 0)),
                  pl.BlockSpec((2 * PEER_HEADS, PEER_N_KEYS, PEER_KEY_DIM // 2), lambda i: (0, 0, 0))],
        out_specs=[out] * 4,
        out_shape=[jax.ShapeDtypeStruct(shape, BF16), jax.ShapeDtypeStruct(shape, BF16),
                   jax.ShapeDtypeStruct(shape, F32), jax.ShapeDtypeStruct(shape, F32)],
        scratch_shapes=[pltpu.VMEM((PEER_HEADS * PEER_KEY_DIM, t), BF16),
                        pltpu.VMEM((PEER_HEADS, 2, PEER_N_KEYS, t), F32),
                        pltpu.VMEM((PEER_HEADS, 2, PEER_TOPK, t), F32),
                        pltpu.VMEM((PEER_HEADS, PEER_N_KEYS, t), F32)],
        compiler_params=_cparams("parallel", fuse=[False, True, True]),
        name="peer_retrieval",
    )(h2t, wqt, keys)


def _gated_activations(ht_ref, w_ref, r2_ref, e2_ref, n1_ref, e1_ref):
    for i in range(KEYS_PER_BLOCK):
        rows = slice(i * PEER_N_KEYS, (i + 1) * PEER_N_KEYS)
        for j in range(PEER_TILE // GATE_LANES):
            lanes = slice(j * GATE_LANES, (j + 1) * GATE_LANES)
            gate = jnp.zeros((PEER_N_KEYS, GATE_LANES), BF16)
            for hd in range(PEER_HEADS):
                n_row = n1_ref[hd, i:i + 1, lanes].astype(BF16)
                c_row = e1_ref[hd, i:i + 1, lanes].astype(BF16)
                live = jnp.where(r2_ref[hd, :, lanes] < n_row, e2_ref[hd, :, lanes], jnp.zeros((), BF16))
                gate = gate + live * c_row
            x = ht_ref[rows, lanes]
            act = x * (1.0 + lax.erf(x * (1.0 / math.sqrt(2.0))))
            w_ref[rows, lanes] = act.astype(BF16) * gate


def _peer_kernel(h2t_ref, u_ref, v_ref, r2_ref, e2_ref, n1_ref, e1_ref,
                 x1_ref, mod_ref, g_ref, b_ref, o_ref, acc_ref, ht_ref, w_ref):
    e = pl.program_id(1)

    @pl.when(e == 0)
    def _():
        acc_ref[...] = jnp.zeros_like(acc_ref)

    ht_ref[...] = _dot(u_ref[...], h2t_ref[...])
    _gated_activations(ht_ref, w_ref, r2_ref, e2_ref, n1_ref, e1_ref)
    acc_ref[...] += lax.dot_general(v_ref[...], w_ref[...], (((0,), (0,)), ((), ())), preferred_element_type=F32)

    @pl.when(e == pl.num_programs(1) - 1)
    def _():
        ffn = acc_ref[...].T
        g2 = mod_ref[0, :, 5 * D_MODEL:6 * D_MODEL]
        o_ref[...] = _ln(DEEPNORM_ALPHA * x1_ref[...] + g2 * ffn) * g_ref[...] + b_ref[...]


def _peer_mod_row(i):
    return i // (DEC_SEQ // PEER_TILE)


def _peer_call(h2t, u, v, r2, e2, n1, e1, x1, mod, g, b):
    t = PEER_TILE
    n_tok = x1.shape[0]
    n_blocks = PEER_N_KEYS * PEER_N_KEYS // EXPERT_BLOCK
    gates = pl.BlockSpec((PEER_HEADS, PEER_N_KEYS, t), lambda i, g: (0, 0, i))
    keys = pl.BlockSpec((PEER_HEADS, KEYS_PER_BLOCK, t), lambda i, g: (0, g, i))
    return pl.pallas_call(
        _peer_kernel,
        grid=(n_tok // t, n_blocks),
        in_specs=[pl.BlockSpec((D_MODEL, t), lambda i, g: (0, i)),
                  pl.BlockSpec((EXPERT_BLOCK, D_MODEL), lambda i, g: (g, 0)),
                  pl.BlockSpec((EXPERT_BLOCK, D_MODEL), lambda i, g: (g, 0)),
                  gates, gates, keys, keys,
                  pl.BlockSpec((t, D_MODEL), lambda i, g: (i, 0)),
                  pl.BlockSpec((1, 1, 6 * D_MODEL), lambda i, g: (_peer_mod_row(i), 0, 0)),
                  pl.BlockSpec((1, D_MODEL), lambda i, g: (0, 0)),
                  pl.BlockSpec((1, D_MODEL), lambda i, g: (0, 0))],
        out_specs=pl.BlockSpec((t, D_MODEL), lambda i, g: (i, 0)),
        out_shape=jax.ShapeDtypeStruct((n_tok, D_MODEL), F32),
        scratch_shapes=[pltpu.VMEM((D_MODEL, t), F32), pltpu.VMEM((EXPERT_BLOCK, t), F32),
                        pltpu.VMEM((EXPERT_BLOCK, t), BF16)],
        compiler_params=_cparams("parallel", "arbitrary"),
        name="peer_dense",
    )(h2t, u, v, r2, e2, n1, e1, x1, mod, g, b)


def _cast_kernel(x_ref, o_ref):
    o_ref[...] = x_ref[...].astype(o_ref.dtype)


def _table_bf16(table, l):
    n_exp = table.shape[1]
    return pl.pallas_call(
        _cast_kernel,
        grid=(n_exp // CAST_ROWS,),
        in_specs=[pl.BlockSpec((None, CAST_ROWS, D_MODEL), lambda i: (l, i, 0))],
        out_specs=pl.BlockSpec((CAST_ROWS, D_MODEL), lambda i: (i, 0)),
        out_shape=jax.ShapeDtypeStruct((n_exp, D_MODEL), BF16),
        compiler_params=_cparams("parallel"),
        name="expert_table_bf16",
    )(table)


def _rope_tables():
    t = jnp.arange(DEC_SEQ)
    row = (t // GRID_W).astype(F32)
    col = (t % GRID_W).astype(F32)

    def angles(rot_dim):
        n_freq = rot_dim // 4
        inv_freq = ROPE_THETA ** (-jnp.arange(n_freq, dtype=F32) / n_freq)
        return jnp.concatenate([row[:, None] * inv_freq, col[:, None] * inv_freq], axis=-1)

    def pack(cos_l, sa_l, sb_l):
        return jnp.stack([cos_l, sa_l, sb_l])

    ang_b = angles(MLA_ROPE)
    cb, sb = jnp.cos(ang_b), jnp.sin(ang_b)
    one, zero = jnp.ones((DEC_SEQ, 64), F32), jnp.zeros((DEC_SEQ, 64), F32)
    z16, z32 = jnp.zeros((DEC_SEQ, 16), F32), jnp.zeros((DEC_SEQ, 32), F32)
    rope_b = pack(jnp.concatenate([one, cb, cb, jnp.ones((DEC_SEQ, 32), F32)], axis=1),
                  jnp.concatenate([zero, -sb, z16, z32], axis=1),
                  jnp.concatenate([zero, z16, sb, z32], axis=1))
    ang_d = angles(DIFF_QK_DIM)
    cd, sd = jnp.cos(ang_d), jnp.sin(ang_d)
    rope_d = pack(jnp.concatenate([cd, cd, cd, cd], axis=1),
                  jnp.concatenate([-sd, z32, -sd, z32], axis=1),
                  jnp.concatenate([z32, sd, z32, sd], axis=1))
    ident = pack(jnp.ones((DEC_SEQ, LANES), F32), jnp.zeros((DEC_SEQ, LANES), F32), jnp.zeros((DEC_SEQ, LANES), F32))
    return rope_b, rope_d, ident


def _na_bias_table(rpb):
    col = jnp.arange(GRID_W)
    dc = jnp.clip(col[None, :] - col[:, None], -(NA_WIN_COLS - 1), NA_WIN_COLS - 1) + NA_WIN_COLS - 1
    rpb_cols = rpb[:, :, dc]
    tabs = [rpb_cols[:, off:off + NA_WIN_ROWS].transpose(0, 2, 1, 3).reshape(NA_HEADS, GRID_W, NA_WIN_ROWS * GRID_W)
            for off in range(NA_WIN_ROWS)]
    return jnp.stack(tabs)


def _pad_cols(w, left, right):
    return jnp.pad(w, ((0, 0), (left, right)))


def kernel(x_prompt, x_sample, cache_mla_ckv, cache_mla_krope, cache_na_k, cache_na_v, cache_diff_k, cache_diff_v, c, c_ctx, w_mod, b_mod, w_in, sgu_norm_g, sgu_w, sgu_b, mla_q_norm_g, mla_w_uq, mla_kv_norm_g, mla_w_ukv, na_rpb, diff_lambda_q1, diff_lambda_k1, diff_lambda_q2, diff_lambda_k2, diff_norm_g, w_branch_a, w_branch_b, w_branch_c, w_branch_d, w_gate, b_gate, w_out, ln1_g, ln1_b, peer_w_q, peer_subkeys, peer_u, peer_v, ln2_g, ln2_b):
    x_ctx = x_prompt.reshape(N_CTX, D_MODEL)
    x_lat = x_sample.reshape(N_LAT, D_MODEL)
    cond = jnp.concatenate([c_ctx[None], c, jnp.zeros((N_COND - 1 - DEC_BATCH, D_MODEL), F32)], axis=0)
    mod_all = _mod_call(cond, w_mod, b_mod)
    rope_b, rope_d, rope_id = _rope_tables()
    cache_kr_pad = jnp.pad(cache_mla_krope, ((0, 0), (0, 0), (0, 0), (MLA_NOPE, LANES - MLA_NOPE - MLA_ROPE)))
    cache_na_k2 = cache_na_k.reshape(DEC_BATCH, DEPTH, PAST_LEN, 256)
    cache_na_v2 = cache_na_v.reshape(DEC_BATCH, DEPTH, PAST_LEN, 256)
    cache_diff_k2 = cache_diff_k.reshape(DEC_BATCH, DEPTH, PAST_LEN, 512)
    cache_diff_v2 = cache_diff_v.reshape(DEC_BATCH, DEPTH, PAST_LEN, 512)

    ctx_out = []
    for l in range(DEPTH):
        lambda_init = 0.8 - 0.6 * math.exp(-0.3 * l)
        mod_ctx = jnp.broadcast_to(mod_all[l, 0], (N_CTX // DEC_SEQ, 1, 6 * D_MODEL))
        mod_lat = mod_all[l, 1:1 + DEC_BATCH].reshape(DEC_BATCH, 1, 6 * D_MODEL)

        wi = w_in[l]
        kr_cols = _pad_cols(wi[:, C_KR:C_KR + MLA_ROPE], MLA_NOPE, LANES - MLA_NOPE - MLA_ROPE)
        w_in_r = jnp.concatenate([wi[:, :C_KR], kr_cols, wi[:, C_KR + MLA_ROPE:]], axis=1).astype(BF16)
        wuq = mla_w_uq[l].reshape(MLA_Q_LORA, MLA_HEADS, MLA_NOPE + MLA_ROPE)
        wuq = jnp.pad(wuq, ((0, 0), (0, 0), (0, LANES - MLA_NOPE - MLA_ROPE))).reshape(MLA_Q_LORA, -1).astype(BF16)
        wukv = mla_w_ukv[l].reshape(MLA_KV_LORA, MLA_HEADS, MLA_NOPE + MLA_V)
        wuk = jnp.pad(wukv[:, :, :MLA_NOPE], ((0, 0), (0, 0), (0, LANES - MLA_NOPE))).reshape(MLA_KV_LORA, -1)
        wuk = wuk.astype(BF16)
        wuv = wukv[:, :, MLA_NOPE:].reshape(MLA_KV_LORA, -1).astype(BF16)
        sgu_bias = jnp.repeat(sgu_b[l].T, SGU_WIDTH // SGU_GROUPS, axis=1)
        lams = [p[l].reshape(1, DIFF_QK_DIM) for p in (diff_lambda_q1, diff_lambda_k1, diff_lambda_q2, diff_lambda_k2)]
        dg = diff_norm_g[l].reshape(1, DIFF_V_DIM)

        inproj_weights = (w_in_r, sgu_norm_g[l].reshape(1, -1), sgu_w[l].astype(BF16), sgu_bias,
                          mla_q_norm_g[l].reshape(1, -1), mla_kv_norm_g[l].reshape(1, -1), wuq)
        merge_weights = (w_gate[l].astype(BF16), b_gate[l].reshape(1, -1),
                         w_branch_a[l].astype(BF16), w_branch_b[l].astype(BF16), w_branch_c[l].astype(BF16),
                         w_branch_d[l].astype(BF16), w_out[l].astype(BF16), ln1_g[l].reshape(1, -1),
                         ln1_b[l].reshape(1, -1))
        keys = peer_subkeys[l].reshape(2 * PEER_HEADS, PEER_N_KEYS, PEER_KEY_DIM // 2).astype(BF16)
        wqt = peer_w_q[l].T.astype(BF16)
        u_bf, v_bf = _table_bf16(peer_u, l), _table_bf16(peer_v, l)
        ln2 = (ln2_g[l].reshape(1, -1), ln2_b[l].reshape(1, -1))

        def channel_mix(x, mod, oa, ob, oc, od):
            x1, h2t = _merge_call(x, mod, oa, ob, oc, od, *merge_weights)
            r2, e2, n1, e1 = _router_call(h2t, wqt, keys)
            return _peer_call(h2t, u_bf, v_bf, r2, e2, n1, e1, x1, mod, *ln2)

        oa, mq, ckv, kr, nq, nk, nv, dq, dk, dv = _inproj_call(x_ctx, mod_ctx, rope_id, rope_id, *inproj_weights)
        ob, oc, od = _ctx_attn_call(lambda_init, (mq, ckv, kr, nq, nk, nv, dq, dk, dv), wuk, wuv, lams, dg)
        x_ctx = channel_mix(x_ctx, mod_ctx, oa, ob, oc, od)
        ctx_out.append((ckv.reshape(BATCH, SEQ, MLA_KV_LORA),
                        kr[:, MLA_NOPE:MLA_NOPE + MLA_ROPE].reshape(BATCH, SEQ, MLA_ROPE),
                        nk.reshape(BATCH, SEQ, NA_HEADS, NA_HEAD_DIM),
                        nv.reshape(BATCH, SEQ, NA_HEADS, NA_HEAD_DIM),
                        dk.reshape(BATCH, SEQ, DIFF_HEADS, 2 * DIFF_QK_DIM),
                        dv.reshape(BATCH, SEQ, DIFF_HEADS, DIFF_V_DIM)))

        oa, mq, ckv, kr, nq, nk, nv, dq, dk, dv = _inproj_call(x_lat, mod_lat, rope_b, rope_d, *inproj_weights)
        ob = _lat_mla_call(l, mq, ckv, kr, cache_mla_ckv, cache_kr_pad, wuk, wuv)
        oc = _lat_na_call(l, nq, nk, nv, cache_na_k2, cache_na_v2, _na_bias_table(na_rpb[l]))
        od = _lat_diff_call(l, lambda_init, dq, dk, dv, cache_diff_k2, cache_diff_v2, lams, dg)
        x_lat = channel_mix(x_lat, mod_lat, oa, ob, oc, od)

    y_prompt = x_ctx.reshape(BATCH, SEQ, D_MODEL)
    y_sample = x_lat.reshape(DEC_BATCH, DEC_SEQ, D_MODEL)
    new = [jnp.stack([t[k] for t in ctx_out], axis=1) for k in range(6)]
    return (y_prompt, y_sample, *new)
```
